```python
import math
import jax, jax.numpy as jnp
from jax import lax
import numpy as np

D_MODEL = 2048
BATCH = 2
SEQ = 8192
DEPTH = 1

GRID_W = 64
CTX_LEN = 256
EPS = 1e-6

SSM_HEAD_DIM = 64
SSM_D_INNER = 2 * D_MODEL
SSM_HEADS = SSM_D_INNER // SSM_HEAD_DIM
SSM_GROUPS = 8
SSM_STATE = 128
SSM_CONV = 7
SSM_CHUNK = 128
XBC_DIM = SSM_D_INNER + 2 * SSM_GROUPS * SSM_STATE

CONV_DIM = D_MODEL
CONV_KERNEL = 31

OFF_DT = XBC_DIM
OFF_Z = OFF_DT + SSM_HEADS
OFF_GLU = OFF_Z + SSM_D_INNER
OFF_GATE = OFF_GLU + 2 * CONV_DIM
IN_PROJ_DIM = OFF_GATE + 2 * D_MODEL

MOE_GROUPS = 8
EXPERTS_PER_GROUP = 8
N_EXPERTS = MOE_GROUPS * EXPERTS_PER_GROUP
TOP_K = 2
D_FF_EXPERT = D_MODEL // 2
MOE_BLOCK = 256

kernel_name = "hybrid_ssd_conformer_hmoe_dit"


def rms_norm(x, w):
    xf = x.astype(jnp.float32)
    y = xf * lax.rsqrt(jnp.mean(xf * xf, axis=-1, keepdims=True) + EPS)
    return (y * w.astype(jnp.float32)).astype(x.dtype)


def layer_norm(x, w, b):
    xf = x.astype(jnp.float32)
    mu = jnp.mean(xf, axis=-1, keepdims=True)
    var = jnp.mean(jnp.square(xf - mu), axis=-1, keepdims=True)
    y = (xf - mu) * lax.rsqrt(var + EPS)
    return (y * w.astype(jnp.float32) + b.astype(jnp.float32)).astype(x.dtype)


def modulate(h, shift, scale):
    return h * (1 + scale) + shift


def conv_1d_centred(x, w, b):
    k = w.shape[0]
    y = lax.conv_general_dilated(x, w[:, None, :].astype(x.dtype), window_strides=(1,),
                                 padding=[(k // 2, k // 2)],
                                 dimension_numbers=("NWC", "WIO", "NWC"),
                                 feature_group_count=x.shape[-1])
    return y + b.astype(x.dtype)


def conv_grid_columns(x, w, b):
    bsz, n_tok, ch = x.shape
    rows = n_tok // GRID_W
    g = x.reshape(bsz, rows, GRID_W, ch)
    k = w.shape[0]
    y = lax.conv_general_dilated(g, w[:, None, None, :].astype(x.dtype), (1, 1),
                                 [(k // 2, k // 2), (0, 0)],
                                 dimension_numbers=("NHWC", "HWIO", "NHWC"),
                                 feature_group_count=ch)
    return (y + b.astype(x.dtype)).reshape(bsz, n_tok, ch)


def ssd_chunked(xs, dt, a, bm, cm, h0, with_output):
    bsz, n_tok, n_heads, p = xs.shape
    g, n = bm.shape[-2:]
    r = n_heads // g
    nc = n_tok // SSM_CHUNK
    f32 = jnp.float32
    x = xs.astype(f32).reshape(bsz, nc, SSM_CHUNK, g, r, p)
    dtc = dt.reshape(bsz, nc, SSM_CHUNK, g, r)
    bc = bm.astype(f32).reshape(bsz, nc, SSM_CHUNK, g, n)
    xdt = x * dtc[..., None]
    cum = jnp.cumsum(dtc * a.reshape(g, r), axis=2)
    to_end = jnp.exp(cum[:, :, -1:] - cum)
    states = jnp.einsum("bcsgn,bcsgrp->bcgrpn", bc, xdt * to_end[..., None])
    chunk_decay = jnp.exp(cum[:, :, -1])

    def step(h, inp):
        s, d = inp
        return h * d[..., None, None] + s, h

    h_last, h_in = lax.scan(step, h0.reshape(bsz, g, r, p, n),
                            (jnp.moveaxis(states, 1, 0), jnp.moveaxis(chunk_decay, 1, 0)))
    h_last = h_last.reshape(bsz, n_heads, p, n)
    if not with_output:
        return None, h_last
    h_in = jnp.moveaxis(h_in, 0, 1)
    cc = cm.astype(f32).reshape(bsz, nc, SSM_CHUNK, g, n)
    idx = jnp.arange(SSM_CHUNK)
    lower = (idx[:, None] >= idx[None, :])[:, :, None, None]
    seg = cum[:, :, :, None] - cum[:, :, None, :]
    decay = jnp.exp(jnp.where(lower, seg, -jnp.inf))
    scores = jnp.einsum("bclgn,bcsgn->bclsg", cc, bc)
    y_diag = jnp.einsum("bclsgr,bcsgrp->bclgrp", scores[..., None] * decay, xdt)
    y_off = jnp.einsum("bclgn,bcgrpn->bclgrp", cc, h_in) * jnp.exp(cum)[..., None]
    return (y_diag + y_off).reshape(bsz, n_tok, n_heads, p), h_last


def ssm_inputs(proj, conv_w, conv_b):
    bsz, n_tok, _ = proj.shape
    gn = SSM_GROUPS * SSM_STATE
    xbc = jax.nn.silu(conv_1d_centred(proj[..., :XBC_DIM], conv_w, conv_b))
    xs = xbc[..., :SSM_D_INNER].reshape(bsz, n_tok, SSM_HEADS, SSM_HEAD_DIM)
    bm = xbc[..., SSM_D_INNER:SSM_D_INNER + gn].reshape(bsz, n_tok, SSM_GROUPS, SSM_STATE)
    cm = xbc[..., SSM_D_INNER + gn:XBC_DIM].reshape(bsz, n_tok, SSM_GROUPS, SSM_STATE)
    dt_raw = proj[..., OFF_DT:OFF_Z].astype(jnp.float32)
    return xs, bm, cm, dt_raw


def bidirectional_ssd(lat, ctx, dt_bias, a_log, d_skip, ctx_out):
    xs_l, b_l, c_l, dtr_l = lat
    xs_c, b_c, c_c, dtr_c = ctx
    bsz = xs_l.shape[0]
    h_zero = jnp.zeros((bsz, SSM_HEADS, SSM_HEAD_DIM, SSM_STATE), jnp.float32)
    ys_lat, ys_ctx = [], []
    for k in range(2):
        rev = (lambda t: jnp.flip(t, axis=1)) if k == 1 else (lambda t: t)
        a = -jnp.exp(a_log[k].astype(jnp.float32))
        bias = dt_bias[k].astype(jnp.float32)
        d = d_skip[k].astype(jnp.float32)[:, None]
        dt_c = jax.nn.softplus(dtr_c + bias)
        dt_l = jax.nn.softplus(dtr_l + bias)
        yc, hc = ssd_chunked(rev(xs_c), rev(dt_c), a, rev(b_c), rev(c_c), h_zero, ctx_out)
        yl, _ = ssd_chunked(rev(xs_l), rev(dt_l), a, rev(b_l), rev(c_l), hc, True)
        ys_lat.append(rev(yl) + d * xs_l.astype(jnp.float32))
        if ctx_out:
            ys_ctx.append(rev(yc) + d * xs_c.astype(jnp.float32))
    y_ctx = ys_ctx[0] + ys_ctx[1] if ctx_out else None
    return ys_lat[0] + ys_lat[1], y_ctx


def branch_merge(proj, y_heads, conv_fn, ssm_norm_w, ssm_out_w, cf_dw_w, cf_dw_b,
                 cf_ln_w, cf_ln_b, cf_out_w, cf_out_b, w_o):
    bsz, n_tok, _ = proj.shape
    z = proj[..., OFF_Z:OFF_GLU]
    y = y_heads.reshape(bsz, n_tok, SSM_D_INNER).astype(proj.dtype)
    y_ssd = rms_norm(y * jax.nn.silu(z), ssm_norm_w) @ ssm_out_w
    glu = proj[..., OFF_GLU:OFF_GATE]
    u = glu[..., :CONV_DIM] * jax.nn.sigmoid(glu[..., CONV_DIM:])
    u = jax.nn.silu(layer_norm(conv_fn(u, cf_dw_w, cf_dw_b), cf_ln_w, cf_ln_b))
    y_cf = u @ cf_out_w + cf_out_b
    gate = jax.nn.sigmoid(proj[..., OFF_GATE:])
    merged = gate[..., :D_MODEL] * y_ssd + gate[..., D_MODEL:] * y_cf
    return merged @ w_o


def hierarchical_moe(h, rg_w, rg_b, re_w, re_b, w_gate, w_up, w_down):
    n_tok, d = h.shape
    f32 = jnp.float32
    g_prob = jax.nn.softmax((h @ rg_w + rg_b).astype(f32), axis=-1)
    g_p, g_idx = lax.top_k(g_prob, 1)
    e_logits = (h @ re_w + re_b).astype(f32).reshape(n_tok, MOE_GROUPS, EXPERTS_PER_GROUP)
    sel = jnp.broadcast_to(g_idx[:, :, None], (n_tok, 1, EXPERTS_PER_GROUP))
    e_in = jnp.take_along_axis(e_logits, sel, axis=1)[:, 0]
    e_l, e_idx = lax.top_k(e_in, TOP_K)
    weight = jax.nn.softmax(e_l, axis=-1) * g_p
    expert = (g_idx * EXPERTS_PER_GROUP + e_idx).reshape(-1)
    n_assign = n_tok * TOP_K
    order = jnp.argsort(expert)
    e_sorted = expert[order]
    tok_sorted = (order // TOP_K).astype(jnp.int32)
    w_sorted = weight.reshape(-1)[order]
    counts = jnp.bincount(expert, length=N_EXPERTS)
    padded = (counts + MOE_BLOCK - 1) // MOE_BLOCK * MOE_BLOCK
    start = jnp.cumsum(counts) - counts
    pad_end = jnp.cumsum(padded)
    pad_start = pad_end - padded
    dest = pad_start[e_sorted] + jnp.arange(n_assign) - start[e_sorted]
    n_blocks = -(-n_assign // MOE_BLOCK) + N_EXPERTS
    buf_tok = jnp.zeros((n_blocks * MOE_BLOCK,), jnp.int32).at[dest].set(tok_sorted)
    block_expert = jnp.minimum(
        jnp.searchsorted(pad_end, jnp.arange(n_blocks) * MOE_BLOCK, side="right"), N_EXPERTS - 1)
    xb = h[buf_tok].reshape(n_blocks, MOE_BLOCK, d)

    def expert_block(args):
        xe, e = args
        hid = jax.nn.silu(xe @ w_gate[e]) * (xe @ w_up[e])
        return hid @ w_down[e]

    yb = lax.map(expert_block, (xb, block_expert)).reshape(n_blocks * MOE_BLOCK, d)
    contrib = yb[dest].astype(f32) * w_sorted[:, None]
    return jnp.zeros((n_tok, d), f32).at[tok_sorted].add(contrib).astype(h.dtype)


def _normal(k, shape, scale):
    return jax.random.normal(k, shape, jnp.float32) * scale


def setup_inputs(seed: int = 0) -> dict:
    key = jax.random.key(seed)
    ks = jax.random.split(key, 40)
    L, D = DEPTH, D_MODEL
    dt0 = jnp.exp(jax.random.uniform(ks[10], (L, 2, SSM_HEADS), jnp.float32,
                                     math.log(1e-3), math.log(1e-1)))
    return {
        "x": _normal(ks[0], (BATCH, SEQ, D), 1.0),
        "c": _normal(ks[1], (BATCH, D), 1.0),
        "ctx": _normal(ks[2], (BATCH, CTX_LEN, D), 1.0),
        "c_ctx": _normal(ks[3], (D,), 1.0),
        "ada_w": _normal(ks[4], (L, D, 6 * D), 0.5 * D ** -0.5),
        "ada_b": _normal(ks[5], (L, 6 * D), 0.02),
        "norm1_w": 1.0 + _normal(ks[6], (L, D), 0.02),
        "w_in": _normal(ks[7], (L, D, IN_PROJ_DIM), D ** -0.5),
        "ssm_conv_w": _normal(ks[8], (L, SSM_CONV, XBC_DIM), SSM_CONV ** -0.5),
        "ssm_conv_b": _normal(ks[9], (L, XBC_DIM), 0.02),
        "dt_bias": dt0 + jnp.log(-jnp.expm1(-dt0)),
        "a_log": jnp.log(jax.random.uniform(ks[11], (L, 2, SSM_HEADS), jnp.float32, 1.0, 16.0)),
        "d_skip": 1.0 + _normal(ks[12], (L, 2, SSM_HEADS), 0.02),
        "ssm_norm_w": 1.0 + _normal(ks[13], (L, SSM_D_INNER), 0.02),
        "ssm_out_w": _normal(ks[14], (L, SSM_D_INNER, D), SSM_D_INNER ** -0.5),
        "cf_dw_w": _normal(ks[15], (L, CONV_KERNEL, CONV_DIM), CONV_KERNEL ** -0.5),
        "cf_dw_b": _normal(ks[16], (L, CONV_DIM), 0.02),
        "cf_ln_w": 1.0 + _normal(ks[17], (L, CONV_DIM), 0.02),
        "cf_ln_b": _normal(ks[18], (L, CONV_DIM), 0.02),
        "cf_out_w": _normal(ks[19], (L, CONV_DIM, D), CONV_DIM ** -0.5),
        "cf_out_b": _normal(ks[20], (L, D), 0.02),
        "w_o": _normal(ks[21], (L, D, D), D ** -0.5),
        "norm2_w": 1.0 + _normal(ks[22], (L, D), 0.02),
        "router_group_w": _normal(ks[23], (L, D, MOE_GROUPS), D ** -0.5),
        "router_group_b": _normal(ks[24], (L, MOE_GROUPS), 0.01),
        "router_expert_w": _normal(ks[25], (L, D, N_EXPERTS), D ** -0.5),
        "router_expert_b": _normal(ks[26], (L, N_EXPERTS), 0.01),
        "expert_w_gate": _normal(ks[27], (L, N_EXPERTS, D, D_FF_EXPERT), D ** -0.5),
        "expert_w_up": _normal(ks[28], (L, N_EXPERTS, D, D_FF_EXPERT), D ** -0.5),
        "expert_w_down": _normal(ks[29], (L, N_EXPERTS, D_FF_EXPERT, D), D_FF_EXPERT ** -0.5),
        "final_norm_w": 1.0 + _normal(ks[30], (D,), 0.02),
    }


def reference(x, c, ctx, c_ctx, ada_w, ada_b, norm1_w, w_in, ssm_conv_w, ssm_conv_b,
              dt_bias, a_log, d_skip, ssm_norm_w, ssm_out_w, cf_dw_w, cf_dw_b, cf_ln_w,
              cf_ln_b, cf_out_w, cf_out_b, w_o, norm2_w, router_group_w, router_group_b,
              router_expert_w, router_expert_b, expert_w_gate, expert_w_up, expert_w_down,
              final_norm_w):
    bsz, n_tok, d = x.shape
    ctx_h = ctx
    for i in range(DEPTH):
        last = i == DEPTH - 1
        mod = jax.nn.silu(c) @ ada_w[i] + ada_b[i]
        sh1, sc1, g1, sh2, sc2, g2 = jnp.split(mod[:, None, :], 6, axis=-1)
        mod_c = jax.nn.silu(c_ctx) @ ada_w[i] + ada_b[i]
        csh1, csc1, cg1, csh2, csc2, cg2 = jnp.split(mod_c, 6, axis=-1)

        h = modulate(rms_norm(x, norm1_w[i]), sh1, sc1)
        hc = modulate(rms_norm(ctx_h, norm1_w[i]), csh1, csc1)
        proj = h @ w_in[i]
        proj_c = hc @ (w_in[i][:, :OFF_Z] if last else w_in[i])
        y_lat, y_ctx = bidirectional_ssd(ssm_inputs(proj, ssm_conv_w[i], ssm_conv_b[i]),
                                         ssm_inputs(proj_c, ssm_conv_w[i], ssm_conv_b[i]),
                                         dt_bias[i], a_log[i], d_skip[i], not last)
        x = x + g1 * branch_merge(proj, y_lat, conv_grid_columns, ssm_norm_w[i], ssm_out_w[i],
                                  cf_dw_w[i], cf_dw_b[i], cf_ln_w[i], cf_ln_b[i],
                                  cf_out_w[i], cf_out_b[i], w_o[i])
        if not last:
            ctx_h = ctx_h + cg1 * branch_merge(proj_c, y_ctx, conv_1d_centred, ssm_norm_w[i],
                                               ssm_out_w[i], cf_dw_w[i], cf_dw_b[i], cf_ln_w[i],
                                               cf_ln_b[i], cf_out_w[i], cf_out_b[i], w_o[i])

        h2 = modulate(rms_norm(x, norm2_w[i]), sh2, sc2).reshape(-1, d)
        if last:
            out = hierarchical_moe(h2, router_group_w[i], router_group_b[i], router_expert_w[i],
                                   router_expert_b[i], expert_w_gate[i], expert_w_up[i],
                                   expert_w_down[i])
            x = x + g2 * out.reshape(x.shape)
        else:
            hc2 = modulate(rms_norm(ctx_h, norm2_w[i]), csh2, csc2).reshape(-1, d)
            out = hierarchical_moe(jnp.concatenate([h2, hc2], axis=0), router_group_w[i],
                                   router_group_b[i], router_expert_w[i], router_expert_b[i],
                                   expert_w_gate[i], expert_w_up[i], expert_w_down[i])
            n_lat = bsz * n_tok
            x = x + g2 * out[:n_lat].reshape(x.shape)
            ctx_h = ctx_h + cg2 * out[n_lat:].reshape(ctx_h.shape)
    return rms_norm(x, final_norm_w)
```

```python
import functools

import jax
import jax.numpy as jnp
from jax import lax
from jax.experimental import pallas as pl
from jax.experimental.pallas import tpu as pltpu

F32 = jnp.float32
BF16 = jnp.bfloat16

EPS = 1e-6
GRID_W = 64
HEAD_DIM = 64
N_HEADS = 64
N_GROUPS = 8
D_STATE = 128
CHUNK = 128
SSM_CONV = 7
CF_KERNEL = 31
MOE_GROUPS = 8
EXPERTS_PER_GROUP = 8
N_EXPERTS = 64
MOE_BLOCK = 256
LANE = 128
VMEM_LIMIT = 56 * 1024 * 1024


def _cparams(sem):
    return pltpu.CompilerParams(dimension_semantics=sem, vmem_limit_bytes=VMEM_LIMIT)


def _silu(v):
    return v * jax.nn.sigmoid(v)


def _ada_kernel(c_ref, w_ref, b_ref, o_ref):
    s = _silu(c_ref[...])
    o_ref[...] = jnp.dot(s.astype(BF16), w_ref[...].astype(BF16),
                         preferred_element_type=F32) + b_ref[...]


def _ada(crows, ada_w, ada_b, tn=1024):
    r, d = crows.shape
    n = ada_w.shape[1]
    return pl.pallas_call(
        _ada_kernel,
        out_shape=jax.ShapeDtypeStruct((r, n), F32),
        grid=(n // tn,),
        in_specs=[pl.BlockSpec((r, d), lambda j: (0, 0)),
                  pl.BlockSpec((d, tn), lambda j: (0, j)),
                  pl.BlockSpec((1, tn), lambda j: (0, j))],
        out_specs=pl.BlockSpec((r, tn), lambda j: (0, j)),
        compiler_params=_cparams(("parallel",)),
        name="ada",
    )(crows, ada_w, ada_b.reshape(1, n))


def _normmod_kernel(rows_ref, x_ref, w_ref, sh_ref, sc_ref, o_ref):
    del rows_ref
    xf = x_ref[0]
    ms = jnp.mean(xf * xf, axis=-1, keepdims=True)
    y = xf * lax.rsqrt(ms + EPS) * w_ref[...]
    o_ref[0] = (y * (1.0 + sc_ref[0]) + sh_ref[0]).astype(o_ref.dtype)


def _normmod(x3, w, mod3, rows, shift_chunk, scale_chunk, out_dtype, tm=256):
    bx, l, d = x3.shape
    grid_spec = pltpu.PrefetchScalarGridSpec(
        num_scalar_prefetch=1,
        grid=(bx, l // tm),
        in_specs=[pl.BlockSpec((1, tm, d), lambda b, i, r: (b, i, 0)),
                  pl.BlockSpec((1, d), lambda b, i, r: (0, 0)),
                  pl.BlockSpec((1, 1, d), lambda b, i, r: (r[b], 0, shift_chunk)),
                  pl.BlockSpec((1, 1, d), lambda b, i, r: (r[b], 0, scale_chunk))],
        out_specs=pl.BlockSpec((1, tm, d), lambda b, i, r: (b, i, 0)),
    )
    return pl.pallas_call(
        _normmod_kernel,
        out_shape=jax.ShapeDtypeStruct((bx, l, d), out_dtype),
        grid_spec=grid_spec,
        compiler_params=_cparams(("parallel", "parallel")),
        name="normmod",
    )(rows, x3, w.reshape(1, d), mod3, mod3)


def _mm_kernel(a_ref, w_ref, *rest, act, has_bias):
    o_ref = rest[-1]
    acc = jnp.dot(a_ref[...], w_ref[...], preferred_element_type=F32)
    if has_bias:
        acc = acc + rest[0][...]
    if act == "silu":
        acc = _silu(acc)
    elif act == "sigmoid":
        acc = jax.nn.sigmoid(acc)
    o_ref[...] = acc.astype(o_ref.dtype)


def _mm(a, w, bias=None, act=None, out_dtype=F32, tm=1024, tn=1024, name="mm"):
    m, k = a.shape
    n = w.shape[1]
    tm, tn = min(tm, m), min(tn, n)
    in_specs = [pl.BlockSpec((tm, k), lambda i, j: (i, 0)),
                pl.BlockSpec((k, tn), lambda i, j: (0, j))]
    args = [a, w]
    if bias is not None:
        in_specs.append(pl.BlockSpec((1, tn), lambda i, j: (0, j)))
        args.append(bias.reshape(1, n))
    return pl.pallas_call(
        functools.partial(_mm_kernel, act=act, has_bias=bias is not None),
        out_shape=jax.ShapeDtypeStruct((m, n), out_dtype),
        grid=(m // tm, n // tn),
        in_specs=in_specs,
        out_specs=pl.BlockSpec((tm, tn), lambda i, j: (i, j)),
        compiler_params=_cparams(("parallel", "parallel")),
        name=name,
    )(*args)


def _mm_glu_kernel(a_ref, wa_ref, wb_ref, o_ref):
    a = a_ref[...]
    va = jnp.dot(a, wa_ref[...], preferred_element_type=F32)
    vb = jnp.dot(a, wb_ref[...], preferred_element_type=F32)
    o_ref[...] = va * jax.nn.sigmoid(vb)


def _mm_glu(a, wa, wb, tm=1024, tn=512):
    m, k = a.shape
    n = wa.shape[1]
    tm = min(tm, m)
    return pl.pallas_call(
        _mm_glu_kernel,
        out_shape=jax.ShapeDtypeStruct((m, n), F32),
        grid=(m // tm, n // tn),
        in_specs=[pl.BlockSpec((tm, k), lambda i, j: (i, 0)),
                  pl.BlockSpec((k, tn), lambda i, j: (0, j)),
                  pl.BlockSpec((k, tn), lambda i, j: (0, j))],
        out_specs=pl.BlockSpec((tm, tn), lambda i, j: (i, j)),
        compiler_params=_cparams(("parallel", "parallel")),
        name="mm_glu",
    )(a, wa, wb)


def _mm_merge_kernel(a_ref, w_ref, b_ref, ga_ref, gb_ref, ys_ref, o_ref):
    ycf = jnp.dot(a_ref[...], w_ref[...], preferred_element_type=F32) + b_ref[...]
    o_ref[...] = (ga_ref[...] * ys_ref[...] + gb_ref[...] * ycf).astype(o_ref.dtype)


def _mm_merge(a, w, bias, gates, y_ssd, tm=1024, tn=512):
    m, k = a.shape
    n = w.shape[1]
    tm = min(tm, m)
    nj = n // tn
    return pl.pallas_call(
        _mm_merge_kernel,
        out_shape=jax.ShapeDtypeStruct((m, n), BF16),
        grid=(m // tm, nj),
        in_specs=[pl.BlockSpec((tm, k), lambda i, j: (i, 0)),
                  pl.BlockSpec((k, tn), lambda i, j: (0, j)),
                  pl.BlockSpec((1, tn), lambda i, j: (0, j)),
                  pl.BlockSpec((tm, tn), lambda i, j: (i, j)),
                  pl.BlockSpec((tm, tn), lambda i, j: (i, j + nj)),
                  pl.BlockSpec((tm, tn), lambda i, j: (i, j))],
        out_specs=pl.BlockSpec((tm, tn), lambda i, j: (i, j)),
        compiler_params=_cparams(("parallel", "parallel")),
        name="mm_merge",
    )(a, w, bias.reshape(1, n), gates, gates, y_ssd)


def _mm_resid_kernel(a_ref, w_ref, x_ref, g_ref, o_ref):
    out = jnp.dot(a_ref[...], w_ref[...], preferred_element_type=F32)
    o_ref[...] = x_ref[...] + g_ref[0] * out


def _mm_resid(a, w, x2, mod3, gate_chunk, rows_per_batch, tm=1024, tn=512):
    m, k = a.shape
    n = w.shape[1]
    tm = min(tm, rows_per_batch)
    nj = n // tn
    tiles_per_batch = rows_per_batch // tm
    return pl.pallas_call(
        _mm_resid_kernel,
        out_shape=jax.ShapeDtypeStruct((m, n), F32),
        grid=(m // tm, nj),
        in_specs=[pl.BlockSpec((tm, k), lambda i, j: (i, 0)),
                  pl.BlockSpec((k, tn), lambda i, j: (0, j)),
                  pl.BlockSpec((tm, tn), lambda i, j: (i, j)),
                  pl.BlockSpec((1, 1, tn),
                               lambda i, j: (i // tiles_per_batch, 0, gate_chunk * nj + j))],
        out_specs=pl.BlockSpec((tm, tn), lambda i, j: (i, j)),
        compiler_params=_cparams(("parallel", "parallel")),
        name="mm_resid",
    )(a, w, x2, mod3)


_CONV_PAD = 8


def _conv7_kernel(ctx_ref, lat_ref, w_ref, b_ref, o_ref, pad_ref, *, l_ctx, l_lat):
    p = _CONV_PAD
    zeros = jnp.zeros((p, LANE), F32)
    off_ctx = p
    off_lat = 2 * p + l_ctx
    pad_ref[0:p, :] = zeros
    pad_ref[off_ctx + l_ctx:off_lat, :] = zeros
    pad_ref[off_lat + l_lat:off_lat + l_lat + p, :] = zeros
    pad_ref[off_ctx:off_ctx + l_ctx, :] = ctx_ref[0]
    pad_ref[off_lat:off_lat + l_lat, :] = lat_ref[0]
    reach = SSM_CONV // 2
    bias = b_ref[...]

    def chunk(pad_base, out_base):
        win = pad_ref[pl.ds(pl.multiple_of(pad_base - p, p), CHUNK + 2 * p), :]
        nwin = CHUNK + 2 * p
        acc = jnp.broadcast_to(bias, (CHUNK, LANE))
        for k in range(SSM_CONV):
            start = p - reach + k
            if start == p:
                tap = win[p:p + CHUNK]
            else:
                tap = pltpu.roll(win, nwin - start, 0)[0:CHUNK]
            acc = acc + tap * w_ref[k:k + 1, :]
        o_ref[0, 0, pl.ds(out_base, CHUNK), :] = _silu(acc)

    def ctx_body(j, c):
        base = pl.multiple_of(j * CHUNK, CHUNK)
        chunk(off_ctx + base, base)
        return c

    def lat_body(j, c):
        base = pl.multiple_of(j * CHUNK, CHUNK)
        chunk(off_lat + base, l_ctx + base)
        return c

    lax.fori_loop(0, l_ctx // CHUNK, ctx_body, 0)
    lax.fori_loop(0, l_lat // CHUNK, lat_body, 0)


def _conv7(ctx_raw, lat_raw, w, b):
    bsz, l_ctx, c = ctx_raw.shape
    l_lat = lat_raw.shape[1]
    ltot = l_ctx + l_lat
    nct = c // LANE
    return pl.pallas_call(
        functools.partial(_conv7_kernel, l_ctx=l_ctx, l_lat=l_lat),
        out_shape=jax.ShapeDtypeStruct((bsz, nct, ltot, LANE), F32),
        grid=(bsz, nct),
        in_specs=[pl.BlockSpec((1, l_ctx, LANE), lambda bi, ci: (bi, 0, ci)),
                  pl.BlockSpec((1, l_lat, LANE), lambda bi, ci: (bi, 0, ci)),
                  pl.BlockSpec((SSM_CONV, LANE), lambda bi, ci: (0, ci)),
                  pl.BlockSpec((1, LANE), lambda bi, ci: (0, ci))],
        out_specs=pl.BlockSpec((1, 1, ltot, LANE), lambda bi, ci: (bi, ci, 0, 0)),
        scratch_shapes=[pltpu.VMEM((ltot + 3 * _CONV_PAD, LANE), F32)],
        compiler_params=_cparams(("parallel", "parallel")),
        name="conv7",
    )(ctx_raw, lat_raw, w, b.reshape(1, c))


def _ssd_kernel(xbc_ref, dtc_ref, dtl_ref, par_ref, dexp_ref, y_ref, st_ref, cumt_ref,
                *, reverse, n_ctx):
    i = pl.program_id(1)

    @pl.when(i == 0)
    def _():
        st_ref[...] = jnp.zeros_like(st_ref)

    dt_raw = jnp.where(i < n_ctx, dtc_ref[0], dtl_ref[0])
    bias = par_ref[0:1, :]
    a = -jnp.exp(par_ref[1:2, :])
    dt = jax.nn.softplus(dt_raw + bias)
    cum = dt * a
    row = lax.broadcasted_iota(jnp.int32, (CHUNK, LANE), 0)
    k = 1
    while k < CHUNK:
        if reverse:
            cum = cum + jnp.where(row < CHUNK - k, pltpu.roll(cum, CHUNK - k, 0), 0.0)
        else:
            cum = cum + jnp.where(row >= k, pltpu.roll(cum, k, 0), 0.0)
        k *= 2
    last = 0 if reverse else CHUNK - 1
    cumt_ref[...] = cum.T
    li = lax.broadcasted_iota(jnp.int32, (CHUNK, CHUNK), 0)
    si = lax.broadcasted_iota(jnp.int32, (CHUNK, CHUNK), 1)
    causal = (li <= si) if reverse else (li >= si)
    lo = lax.broadcasted_iota(jnp.int32, (CHUNK, LANE), 1) < HEAD_DIM
    heads_per_group = N_HEADS // N_GROUPS
    pairs = heads_per_group // 2
    x_tiles = N_HEADS // 2

    def group(g, carry):
        shift = (LANE - heads_per_group * g) & (LANE - 1)
        cum_g = pltpu.roll(cum, shift, 1)
        dt_g = pltpu.roll(dt, shift, 1)
        cum_t = cumt_ref[pl.ds(pl.multiple_of(heads_per_group * g, heads_per_group), heads_per_group), :]
        bb = xbc_ref[0, x_tiles + g].astype(BF16)
        cb = xbc_ref[0, x_tiles + N_GROUPS + g].astype(BF16)
        scores = lax.dot_general(cb, bb, (((1,), (1,)), ((), ())), preferred_element_type=F32)
        h_t = st_ref[g]
        y_off = jnp.dot(cb, h_t.astype(BF16), preferred_element_type=F32)
        xw_parts, dec_parts = [], []
        for p in range(pairs):
            j0, j1 = 2 * p, 2 * p + 1
            x2 = xbc_ref[0, pairs * g + p]
            c0 = cum_g[:, j0:j0 + 1]
            c1 = cum_g[:, j1:j1 + 1]
            l0 = jnp.exp(jnp.where(causal, c0 - cum_t[j0:j0 + 1, :], -jnp.inf))
            l1 = jnp.exp(jnp.where(causal, c1 - cum_t[j1:j1 + 1, :], -jnp.inf))
            m0 = (scores * l0).astype(BF16)
            m1 = (scores * l1).astype(BF16)
            dt2 = jnp.where(lo, dt_g[:, j0:j0 + 1], dt_g[:, j1:j1 + 1])
            c2 = jnp.where(lo, c0, c1)
            xdt = x2 * dt2
            xdt_b = xdt.astype(BF16)
            zero = jnp.zeros_like(xdt_b)
            y_diag = (jnp.dot(m0, jnp.where(lo, xdt_b, zero), preferred_element_type=F32)
                      + jnp.dot(m1, jnp.where(lo, zero, xdt_b), preferred_element_type=F32))
            e2 = jnp.exp(c2)
            y = y_diag + y_off[:, p * LANE:(p + 1) * LANE] * e2
            y_ref[0, pairs * g + p] = y + dexp_ref[pairs * g + p] * x2
            to_end = jnp.exp(c2[last:last + 1, :] - c2)
            xw_parts.append((xdt * to_end).astype(BF16))
            dec_parts.append(e2[last:last + 1, :])
        xw = jnp.concatenate(xw_parts, axis=1)
        dec = jnp.concatenate(dec_parts, axis=1)
        upd = lax.dot_general(bb, xw, (((0,), (0,)), ((), ())), preferred_element_type=F32)
        st_ref[g] = h_t * dec + upd
        return carry

    lax.fori_loop(0, N_GROUPS, group, 0)


def _ssd(xbc_act, dt_ctx, dt_lat, par, dexp, reverse):
    bsz, ntile, ltot, _ = xbc_act.shape
    l_ctx = dt_ctx.shape[1]
    l_lat = dt_lat.shape[1]
    n_ctx = l_ctx // CHUNK
    n_lat = l_lat // CHUNK
    steps = n_ctx + n_lat
    x_tiles = N_HEADS // 2

    if reverse:
        def cat_chunk(i):
            return jnp.where(i < n_ctx, n_ctx - 1 - i, n_ctx + steps - 1 - i)

        def ctx_chunk(i):
            return jnp.maximum(n_ctx - 1 - i, 0)

        def lat_chunk(i):
            return jnp.minimum(steps - 1 - i, n_lat - 1)
    else:
        def cat_chunk(i):
            return i

        def ctx_chunk(i):
            return jnp.minimum(i, n_ctx - 1)

        def lat_chunk(i):
            return jnp.maximum(i - n_ctx, 0)

    return pl.pallas_call(
        functools.partial(_ssd_kernel, reverse=reverse, n_ctx=n_ctx),
        out_shape=jax.ShapeDtypeStruct((bsz, x_tiles, l_lat, LANE), F32),
        grid=(bsz, steps),
        in_specs=[pl.BlockSpec((1, ntile, CHUNK, LANE), lambda b, i: (b, 0, cat_chunk(i), 0)),
                  pl.BlockSpec((1, CHUNK, LANE), lambda b, i: (b, ctx_chunk(i), 0)),
                  pl.BlockSpec((1, CHUNK, LANE), lambda b, i: (b, lat_chunk(i), 0)),
                  pl.BlockSpec((8, LANE), lambda b, i: (0, 0)),
                  pl.BlockSpec((x_tiles, 1, LANE), lambda b, i: (0, 0, 0))],
        out_specs=pl.BlockSpec((1, x_tiles, CHUNK, LANE), lambda b, i: (b, 0, lat_chunk(i), 0)),
        scratch_shapes=[pltpu.VMEM((N_GROUPS, D_STATE, (N_HEADS // N_GROUPS) * HEAD_DIM), F32),
                        pltpu.VMEM((LANE, CHUNK), F32)],
        compiler_params=_cparams(("parallel", "arbitrary")),
        name="ssd_bwd" if reverse else "ssd_fwd",
    )(xbc_act, dt_ctx, dt_lat, par, dexp)


def _gatenorm_kernel(yf_ref, yb_ref, sz_ref, w_ref, o_ref, g_ref):
    nt = yf_ref.shape[1]
    tm = yf_ref.shape[2]
    ss = jnp.zeros((tm, 1), F32)
    for j in range(nt):
        sl = slice(j * LANE, (j + 1) * LANE)
        g = (yf_ref[0, j] + yb_ref[0, j]) * sz_ref[:, sl]
        g_ref[:, sl] = g
        ss = ss + jnp.sum(g * g, axis=-1, keepdims=True)
    r = lax.rsqrt(ss / (nt * LANE) + EPS)
    o_ref[...] = (g_ref[...] * r * w_ref[...]).astype(o_ref.dtype)


def _gatenorm(y_f, y_b, sz, w, tm=256):
    bsz, nt, s, _ = y_f.shape
    dn = nt * LANE
    spb = s // tm
    return pl.pallas_call(
        _gatenorm_kernel,
        out_shape=jax.ShapeDtypeStruct((bsz * s, dn), BF16),
        grid=(bsz, spb),
        in_specs=[pl.BlockSpec((1, nt, tm, LANE), lambda b, i: (b, 0, i, 0)),
                  pl.BlockSpec((1, nt, tm, LANE), lambda b, i: (b, 0, i, 0)),
                  pl.BlockSpec((tm, dn), lambda b, i: (b * spb + i, 0)),
                  pl.BlockSpec((1, dn), lambda b, i: (0, 0))],
        out_specs=pl.BlockSpec((tm, dn), lambda b, i: (b * spb + i, 0)),
        scratch_shapes=[pltpu.VMEM((tm, dn), F32)],
        compiler_params=_cparams(("parallel", "parallel")),
        name="gatenorm",
    )(y_f, y_b, sz, w.reshape(1, dn))


def _conv31_kernel(u_ref, w_ref, b_ref, o_ref, pad_ref, *, seq):
    halo = (CF_KERNEL // 2) * GRID_W
    zeros = jnp.zeros((halo, LANE), F32)
    pad_ref[0:halo, :] = zeros
    pad_ref[halo + seq:halo + seq + halo, :] = zeros
    pad_ref[halo:halo + seq, :] = u_ref[0]
    bias = b_ref[...]

    def body(j, c):
        base = pl.multiple_of(j * CHUNK, CHUNK)
        acc = jnp.broadcast_to(bias, (CHUNK, LANE))
        for k in range(CF_KERNEL):
            tap = pad_ref[pl.ds(pl.multiple_of(base + k * GRID_W, GRID_W), CHUNK), :]
            acc = acc + tap * w_ref[k:k + 1, :]
        o_ref[0, pl.ds(base, CHUNK), :] = acc
        return c

    lax.fori_loop(0, seq // CHUNK, body, 0)


def _conv31(u3, w, b):
    bsz, s, c = u3.shape
    halo = (CF_KERNEL // 2) * GRID_W
    return pl.pallas_call(
        functools.partial(_conv31_kernel, seq=s),
        out_shape=jax.ShapeDtypeStruct((bsz, s, c), F32),
        grid=(bsz, c // LANE),
        in_specs=[pl.BlockSpec((1, s, LANE), lambda bi, ci: (bi, 0, ci)),
                  pl.BlockSpec((CF_KERNEL, LANE), lambda bi, ci: (0, ci)),
                  pl.BlockSpec((1, LANE), lambda bi, ci: (0, ci))],
        out_specs=pl.BlockSpec((1, s, LANE), lambda bi, ci: (bi, 0, ci)),
        scratch_shapes=[pltpu.VMEM((s + 2 * halo, LANE), F32)],
        compiler_params=_cparams(("parallel", "parallel")),
        name="conv31",
    )(u3, w, b.reshape(1, c))


def _lnsilu_kernel(x_ref, w_ref, b_ref, o_ref):
    xf = x_ref[...]
    mu = jnp.mean(xf, axis=-1, keepdims=True)
    xc = xf - mu
    var = jnp.mean(xc * xc, axis=-1, keepdims=True)
    y = xc * lax.rsqrt(var + EPS) * w_ref[...] + b_ref[...]
    o_ref[...] = _silu(y).astype(o_ref.dtype)


def _lnsilu(x2, w, b, tm=512):
    m, d = x2.shape
    tm = min(tm, m)
    return pl.pallas_call(
        _lnsilu_kernel,
        out_shape=jax.ShapeDtypeStruct((m, d), BF16),
        grid=(m // tm,),
        in_specs=[pl.BlockSpec((tm, d), lambda i: (i, 0)),
                  pl.BlockSpec((1, d), lambda i: (0, 0)),
                  pl.BlockSpec((1, d), lambda i: (0, 0))],
        out_specs=pl.BlockSpec((tm, d), lambda i: (i, 0)),
        compiler_params=_cparams(("parallel",)),
        name="lnsilu",
    )(x2, w.reshape(1, d), b.reshape(1, d))


def _route_kernel(x_ref, w_ref, sh_ref, sc_ref, rw_ref, rb_ref, h_ref, eid_ref, ew_ref):
    xf = x_ref[...]
    ms = jnp.mean(xf * xf, axis=-1, keepdims=True)
    h = xf * lax.rsqrt(ms + EPS) * w_ref[...]
    h = h * (1.0 + sc_ref[0]) + sh_ref[0]
    h_ref[...] = h
    logits = jnp.dot(h.astype(BF16), rw_ref[...], preferred_element_type=F32) + rb_ref[...]
    tm = logits.shape[0]
    lane = lax.broadcasted_iota(jnp.int32, (tm, LANE), 1)
    lane_f = lane.astype(F32)
    ninf = -jnp.inf
    gl = jnp.where(lane < MOE_GROUPS, logits, ninf)
    gmax = jnp.max(gl, axis=-1, keepdims=True)
    gidx = jnp.min(jnp.where(gl == gmax, lane_f, float(LANE)), axis=-1, keepdims=True)
    gsum = jnp.sum(jnp.exp(gl - gmax), axis=-1, keepdims=True)
    g_p = 1.0 / gsum
    first = float(MOE_GROUPS) + gidx * float(EXPERTS_PER_GROUP)
    in_group = (lane_f >= first) & (lane_f < first + float(EXPERTS_PER_GROUP))
    el = jnp.where(in_group, logits, ninf)
    m1 = jnp.max(el, axis=-1, keepdims=True)
    i1 = jnp.min(jnp.where(el == m1, lane_f, float(LANE)), axis=-1, keepdims=True)
    el2 = jnp.where(lane_f == i1, ninf, el)
    m2 = jnp.max(el2, axis=-1, keepdims=True)
    i2 = jnp.min(jnp.where(el2 == m2, lane_f, float(LANE)), axis=-1, keepdims=True)
    e21 = jnp.exp(m2 - m1)
    den = 1.0 + e21
    w1 = (1.0 / den) * g_p
    w2 = (e21 / den) * g_p
    e1 = (i1 - float(MOE_GROUPS)).astype(jnp.int32)
    e2 = (i2 - float(MOE_GROUPS)).astype(jnp.int32)
    eid_ref[...] = jnp.where(lane == 0, e1, jnp.where(lane == 1, e2, 0))
    ew_ref[...] = jnp.where(lane == 0, w1, jnp.where(lane == 1, w2, 0.0))


def _route(x2, w, mod3, shift_chunk, scale_chunk, rows_per_batch, rw, rb, tm=256):
    m, d = x2.shape
    tiles_per_batch = rows_per_batch // tm
    return pl.pallas_call(
        _route_kernel,
        out_shape=(jax.ShapeDtypeStruct((m, d), F32),
                   jax.ShapeDtypeStruct((m, LANE), jnp.int32),
                   jax.ShapeDtypeStruct((m, LANE), F32)),
        grid=(m // tm,),
        in_specs=[pl.BlockSpec((tm, d), lambda i: (i, 0)),
                  pl.BlockSpec((1, d), lambda i: (0, 0)),
                  pl.BlockSpec((1, 1, d), lambda i: (i // tiles_per_batch, 0, shift_chunk)),
                  pl.BlockSpec((1, 1, d), lambda i: (i // tiles_per_batch, 0, scale_chunk)),
                  pl.BlockSpec((d, LANE), lambda i: (0, 0)),
                  pl.BlockSpec((1, LANE), lambda i: (0, 0))],
        out_specs=(pl.BlockSpec((tm, d), lambda i: (i, 0)),
                   pl.BlockSpec((tm, LANE), lambda i: (i, 0)),
                   pl.BlockSpec((tm, LANE), lambda i: (i, 0))),
        compiler_params=_cparams(("parallel",)),
        name="route",
    )(x2, w.reshape(1, d), mod3, mod3, rw, rb)


def _row_copy(src_hbm, dst_vmem, sem, src_row, dst_row):
    return pltpu.make_async_copy(src_hbm.at[pl.ds(src_row, 1), :],
                                 dst_vmem.at[pl.ds(dst_row, 1), :], sem)


def _gather_kernel(nused_ref, tok_ref, h_hbm, o_ref, sem):
    b = pl.program_id(0)
    rows = o_ref.shape[0]

    @pl.when(b < nused_ref[0])
    def _():
        def start(r, c):
            _row_copy(h_hbm, o_ref, sem, tok_ref[0, 0, r], r).start()
            return c

        def wait(r, c):
            _row_copy(h_hbm, o_ref, sem, 0, r).wait()
            return c

        lax.fori_loop(0, rows, start, 0)
        lax.fori_loop(0, rows, wait, 0)

    @pl.when(b >= nused_ref[0])
    def _():
        o_ref[...] = jnp.zeros_like(o_ref)


def _gather_rows(h2, buf_tok, n_used):
    t, d = h2.shape
    n_blocks = buf_tok.shape[0]
    grid_spec = pltpu.PrefetchScalarGridSpec(
        num_scalar_prefetch=1,
        grid=(n_blocks,),
        in_specs=[pl.BlockSpec((1, 1, MOE_BLOCK), lambda b, n: (b, 0, 0), memory_space=pltpu.SMEM),
                  pl.BlockSpec(memory_space=pl.ANY)],
        out_specs=pl.BlockSpec((MOE_BLOCK, d), lambda b, n: (b, 0)),
        scratch_shapes=[pltpu.SemaphoreType.DMA(())],
    )
    return pl.pallas_call(
        _gather_kernel,
        out_shape=jax.ShapeDtypeStruct((n_blocks * MOE_BLOCK, d), F32),
        grid_spec=grid_spec,
        compiler_params=_cparams(("arbitrary",)),
        name="moe_gather",
    )(n_used, buf_tok, h2)


def _expert_up_kernel(be_ref, nused_ref, x_ref, wg_ref, wu_ref, o_ref, *, fchunk):
    del be_ref
    b = pl.program_id(0)
    dff = wg_ref.shape[2]

    @pl.when(b < nused_ref[0])
    def _():
        xb = x_ref[...].astype(BF16)
        for f in range(dff // fchunk):
            sl = slice(f * fchunk, (f + 1) * fchunk)
            gate = jnp.dot(xb, wg_ref[0, :, sl].astype(BF16), preferred_element_type=F32)
            up = jnp.dot(xb, wu_ref[0, :, sl].astype(BF16), preferred_element_type=F32)
            o_ref[:, sl] = (_silu(gate) * up).astype(o_ref.dtype)

    @pl.when(b >= nused_ref[0])
    def _():
        o_ref[...] = jnp.zeros_like(o_ref)


def _expert_up(xb, w_gate, w_up, block_expert, n_used, fchunk=256):
    n_rows, d = xb.shape
    n_blocks = n_rows // MOE_BLOCK
    dff = w_gate.shape[2]

    def blk(b, be, n):
        return jnp.minimum(b, n[0] - 1)

    grid_spec = pltpu.PrefetchScalarGridSpec(
        num_scalar_prefetch=2,
        grid=(n_blocks,),
        in_specs=[pl.BlockSpec((MOE_BLOCK, d), lambda b, be, n: (blk(b, be, n), 0)),
                  pl.BlockSpec((1, d, dff), lambda b, be, n: (be[blk(b, be, n)], 0, 0)),
                  pl.BlockSpec((1, d, dff), lambda b, be, n: (be[blk(b, be, n)], 0, 0))],
        out_specs=pl.BlockSpec((MOE_BLOCK, dff), lambda b, be, n: (b, 0)),
    )
    return pl.pallas_call(
        functools.partial(_expert_up_kernel, fchunk=fchunk),
        out_shape=jax.ShapeDtypeStruct((n_rows, dff), BF16),
        grid_spec=grid_spec,
        compiler_params=_cparams(("arbitrary",)),
        name="expert_up",
    )(block_expert, n_used, xb, w_gate, w_up)


def _expert_down_kernel(be_ref, nused_ref, h_ref, wd_ref, o_ref, *, nchunk):
    del be_ref
    b = pl.program_id(0)
    d = wd_ref.shape[2]

    @pl.when(b < nused_ref[0])
    def _():
        hb = h_ref[...]
        for c in range(d // nchunk):
            sl = slice(c * nchunk, (c + 1) * nchunk)
            o_ref[:, sl] = jnp.dot(hb, wd_ref[0, :, sl].astype(BF16), preferred_element_type=F32)

    @pl.when(b >= nused_ref[0])
    def _():
        o_ref[...] = jnp.zeros_like(o_ref)


def _expert_down(hid, w_down, block_expert, n_used, nchunk=512):
    n_rows, dff = hid.shape
    n_blocks = n_rows // MOE_BLOCK
    d = w_down.shape[2]

    def blk(b, be, n):
        return jnp.minimum(b, n[0] - 1)

    grid_spec = pltpu.PrefetchScalarGridSpec(
        num_scalar_prefetch=2,
        grid=(n_blocks,),
        in_specs=[pl.BlockSpec((MOE_BLOCK, dff), lambda b, be, n: (blk(b, be, n), 0)),
                  pl.BlockSpec((1, dff, d), lambda b, be, n: (be[blk(b, be, n)], 0, 0))],
        out_specs=pl.BlockSpec((MOE_BLOCK, d), lambda b, be, n: (b, 0)),
    )
    return pl.pallas_call(
        functools.partial(_expert_down_kernel, nchunk=nchunk),
        out_shape=jax.ShapeDtypeStruct((n_rows, d), F32),
        grid_spec=grid_spec,
        compiler_params=_cparams(("arbitrary",)),
        name="expert_down",
    )(block_expert, n_used, hid, w_down)


def _combine_kernel(dest_ref, yb_hbm, ew_ref, x_ref, g_ref, w_ref, o_ref, rows_ref, sem):
    tm = x_ref.shape[0]

    def start(r, c):
        _row_copy(yb_hbm, rows_ref.at[0], sem, dest_ref[0, 0, 2 * r], r).start()
        _row_copy(yb_hbm, rows_ref.at[1], sem, dest_ref[0, 0, 2 * r + 1], r).start()
        return c

    def wait(r, c):
        _row_copy(yb_hbm, rows_ref.at[0], sem, 0, r).wait()
        _row_copy(yb_hbm, rows_ref.at[1], sem, 0, r).wait()
        return c

    lax.fori_loop(0, tm, start, 0)
    lax.fori_loop(0, tm, wait, 0)
    ew = ew_ref[...]
    moe = rows_ref[0] * ew[:, 0:1] + rows_ref[1] * ew[:, 1:2]
    xo = x_ref[...] + g_ref[0] * moe
    ms = jnp.mean(xo * xo, axis=-1, keepdims=True)
    o_ref[...] = xo * lax.rsqrt(ms + EPS) * w_ref[...]


def _combine(yb, dest, ew, x2, mod3, gate_chunk, rows_per_batch, final_w, tm=256):
    m, d = x2.shape
    tiles_per_batch = rows_per_batch // tm
    return pl.pallas_call(
        _combine_kernel,
        out_shape=jax.ShapeDtypeStruct((m, d), F32),
        grid=(m // tm,),
        in_specs=[pl.BlockSpec((1, 1, 2 * tm), lambda i: (i, 0, 0), memory_space=pltpu.SMEM),
                  pl.BlockSpec(memory_space=pl.ANY),
                  pl.BlockSpec((tm, LANE), lambda i: (i, 0)),
                  pl.BlockSpec((tm, d), lambda i: (i, 0)),
                  pl.BlockSpec((1, 1, d), lambda i: (i // tiles_per_batch, 0, gate_chunk)),
                  pl.BlockSpec((1, d), lambda i: (0, 0))],
        out_specs=pl.BlockSpec((tm, d), lambda i: (i, 0)),
        scratch_shapes=[pltpu.VMEM((2, tm, d), F32), pltpu.SemaphoreType.DMA(())],
        compiler_params=_cparams(("arbitrary",)),
        name="moe_combine",
    )(dest, yb, ew, x2, mod3, final_w.reshape(1, d))


def _dispatch_tables(eid, n_tok):
    top_k = eid.shape[1]
    n_assign = n_tok * top_k
    expert = eid.reshape(-1)
    key = jnp.sort(expert * n_assign + jnp.arange(n_assign, dtype=jnp.int32))
    sorted_assign = key % n_assign
    sorted_tok = sorted_assign // top_k
    onehot = (expert[:, None] == jnp.arange(N_EXPERTS, dtype=jnp.int32)[None, :]).astype(jnp.int32)
    counts = jnp.sum(onehot, axis=0)
    padded = (counts + MOE_BLOCK - 1) // MOE_BLOCK * MOE_BLOCK
    start = jnp.cumsum(counts) - counts
    pad_end = jnp.cumsum(padded)
    pad_start = pad_end - padded
    n_blocks = -(-n_assign // MOE_BLOCK) + N_EXPERTS
    block_expert = jnp.minimum(
        jnp.searchsorted(pad_end, jnp.arange(n_blocks, dtype=jnp.int32) * MOE_BLOCK, side="right"),
        N_EXPERTS - 1).astype(jnp.int32)
    row = jnp.arange(n_blocks * MOE_BLOCK, dtype=jnp.int32)
    row_e = jnp.repeat(block_expert, MOE_BLOCK)
    rank = row - pad_start[row_e]
    valid = rank < counts[row_e]
    src = jnp.clip(start[row_e] + rank, 0, n_assign - 1)
    buf_tok = jnp.where(valid, sorted_tok[src], 0).astype(jnp.int32)
    pos = jnp.zeros((n_assign,), jnp.int32).at[sorted_assign].set(jnp.arange(n_assign, dtype=jnp.int32))
    dest = (pos - start[expert] + pad_start[expert]).astype(jnp.int32)
    n_used = (pad_end[-1] // MOE_BLOCK).astype(jnp.int32).reshape(1)
    return buf_tok.reshape(n_blocks, 1, MOE_BLOCK), block_expert, n_used, dest


def kernel(x, c, ctx, c_ctx, ada_w, ada_b, norm1_w, w_in, ssm_conv_w, ssm_conv_b, dt_bias, a_log, d_skip, ssm_norm_w, ssm_out_w, cf_dw_w, cf_dw_b, cf_ln_w, cf_ln_b, cf_out_w, cf_out_b, w_o, norm2_w, router_group_w, router_group_b, router_expert_w, router_expert_b, expert_w_gate, expert_w_up, expert_w_down, final_norm_w):
    bsz, seq, d = x.shape
    l_ctx = ctx.shape[1]
    n_tok = bsz * seq
    d_inner = ssm_norm_w.shape[1]
    gn = N_GROUPS * D_STATE
    xbc_dim = d_inner + 2 * gn
    off_dt = xbc_dim
    off_z = off_dt + N_HEADS
    off_glu = off_z + d_inner
    off_gate = off_glu + 2 * d

    ctx_row = bsz
    crows = jnp.zeros((8, d), F32).at[:bsz].set(c).at[ctx_row].set(c_ctx)
    mod = _ada(crows, ada_w[0], ada_b[0])
    mod3 = mod.reshape(8, 1, 6 * d)
    lat_rows = jnp.arange(bsz, dtype=jnp.int32)
    ctx_rows = jnp.full((bsz,), ctx_row, jnp.int32)

    h_lat = _normmod(x, norm1_w[0], mod3, lat_rows, 0, 1, BF16).reshape(n_tok, d)
    h_ctx = _normmod(ctx, norm1_w[0], mod3, ctx_rows, 0, 1, BF16).reshape(bsz * l_ctx, d)

    w = w_in[0]
    w_xbc = w[:, :xbc_dim].astype(BF16)
    w_dt = jnp.pad(w[:, off_dt:off_z], ((0, 0), (0, LANE - N_HEADS))).astype(BF16)
    w_z = w[:, off_z:off_glu].astype(BF16)
    w_glu_a = w[:, off_glu:off_glu + d].astype(BF16)
    w_glu_b = w[:, off_glu + d:off_gate].astype(BF16)
    w_gate = w[:, off_gate:].astype(BF16)

    xbc_lat = _mm(h_lat, w_xbc, name="in_xbc").reshape(bsz, seq, xbc_dim)
    xbc_ctx = _mm(h_ctx, w_xbc, tm=512, name="in_xbc_ctx").reshape(bsz, l_ctx, xbc_dim)
    dt_lat = _mm(h_lat, w_dt, name="in_dt").reshape(bsz, seq, LANE)
    dt_ctx = _mm(h_ctx, w_dt, tm=512, name="in_dt_ctx").reshape(bsz, l_ctx, LANE)
    sz = _mm(h_lat, w_z, act="silu", name="in_z")
    u = _mm_glu(h_lat, w_glu_a, w_glu_b)
    gates = _mm(h_lat, w_gate, act="sigmoid", name="in_gate")

    xbc_act = _conv7(xbc_ctx, xbc_lat, ssm_conv_w[0], ssm_conv_b[0])

    ys = []
    for k in range(2):
        par = jnp.zeros((8, LANE), F32).at[0, :N_HEADS].set(dt_bias[0, k]).at[1, :N_HEADS].set(a_log[0, k])
        dexp = jnp.repeat(d_skip[0, k], HEAD_DIM).reshape(N_HEADS // 2, 1, LANE)
        ys.append(_ssd(xbc_act, dt_ctx, dt_lat, par, dexp, reverse=(k == 1)))

    gnorm = _gatenorm(ys[0], ys[1], sz, ssm_norm_w[0])
    y_ssd = _mm(gnorm, ssm_out_w[0].astype(BF16), tn=512, name="ssm_out")

    cv = _conv31(u.reshape(bsz, seq, d), cf_dw_w[0], cf_dw_b[0]).reshape(n_tok, d)
    ua = _lnsilu(cv, cf_ln_w[0], cf_ln_b[0])
    merged = _mm_merge(ua, cf_out_w[0].astype(BF16), cf_out_b[0], gates, y_ssd)
    x1 = _mm_resid(merged, w_o[0].astype(BF16), x.reshape(n_tok, d), mod3, 2, seq)

    n_r = MOE_GROUPS + N_EXPERTS
    rw = jnp.pad(jnp.concatenate([router_group_w[0], router_expert_w[0]], axis=1),
                 ((0, 0), (0, LANE - n_r))).astype(BF16)
    rb = jnp.pad(jnp.concatenate([router_group_b[0], router_expert_b[0]]), (0, LANE - n_r)).reshape(1, LANE)
    h2, eid, ew = _route(x1, norm2_w[0], mod3, 3, 4, seq, rw, rb)

    buf_tok, block_expert, n_used, dest = _dispatch_tables(eid[:, :2], n_tok)
    xb = _gather_rows(h2, buf_tok, n_used)
    hid = _expert_up(xb, expert_w_gate[0], expert_w_up[0], block_expert, n_used)
    yb = _expert_down(hid, expert_w_down[0], block_expert, n_used)

    tmc = 256
    dest3 = dest.reshape(n_tok // tmc, 1, 2 * tmc)
    out = _combine(yb, dest3, ew, x1, mod3, 5, seq, final_norm_w, tm=tmc)
    return out.reshape(bsz, seq, d)
```

```python
import functools

import jax
import jax.numpy as jnp
from jax import lax
from jax.experimental import pallas as pl
from jax.experimental.pallas import tpu as pltpu

F32 = jnp.float32
BF16 = jnp.bfloat16

EPS = 1e-6
GRID_W = 64
HEAD_DIM = 64
N_HEADS = 64
N_GROUPS = 8
D_STATE = 128
CHUNK = 128
SSM_CONV = 7
CF_KERNEL = 31
MOE_GROUPS = 8
EXPERTS_PER_GROUP = 8
N_EXPERTS = 64
MOE_BLOCK = 256
LANE = 128
VMEM_LIMIT = 56 * 1024 * 1024


def _cparams(sem):
    return pltpu.CompilerParams(dimension_semantics=sem, vmem_limit_bytes=VMEM_LIMIT)


def _silu(v):
    return v * jax.nn.sigmoid(v)


def _ada_kernel(c_ref, w_ref, b_ref, o_ref):
    s = _silu(c_ref[...])
    o_ref[...] = jnp.dot(s.astype(BF16), w_ref[...].astype(BF16),
                         preferred_element_type=F32) + b_ref[...]


def _ada(crows, ada_w, ada_b, tn=1024):
    r, d = crows.shape
    n = ada_w.shape[1]
    return pl.pallas_call(
        _ada_kernel,
        out_shape=jax.ShapeDtypeStruct((r, n), F32),
        grid=(n // tn,),
        in_specs=[pl.BlockSpec((r, d), lambda j: (0, 0)),
                  pl.BlockSpec((d, tn), lambda j: (0, j)),
                  pl.BlockSpec((1, tn), lambda j: (0, j))],
        out_specs=pl.BlockSpec((r, tn), lambda j: (0, j)),
        compiler_params=_cparams(("parallel",)),
        name="ada",
    )(crows, ada_w, ada_b.reshape(1, n))


def _normmod_kernel(rows_ref, x_ref, w_ref, sh_ref, sc_ref, o_ref):
    del rows_ref
    xf = x_ref[0]
    ms = jnp.mean(xf * xf, axis=-1, keepdims=True)
    y = xf * lax.rsqrt(ms + EPS) * w_ref[...]
    o_ref[0] = (y * (1.0 + sc_ref[0]) + sh_ref[0]).astype(o_ref.dtype)


def _normmod(x3, w, mod3, rows, shift_chunk, scale_chunk, out_dtype, tm=256):
    bx, l, d = x3.shape
    grid_spec = pltpu.PrefetchScalarGridSpec(
        num_scalar_prefetch=1,
        grid=(bx, l // tm),
        in_specs=[pl.BlockSpec((1, tm, d), lambda b, i, r: (b, i, 0)),
                  pl.BlockSpec((1, d), lambda b, i, r: (0, 0)),
                  pl.BlockSpec((1, 1, d), lambda b, i, r: (r[b], 0, shift_chunk)),
                  pl.BlockSpec((1, 1, d), lambda b, i, r: (r[b], 0, scale_chunk))],
        out_specs=pl.BlockSpec((1, tm, d), lambda b, i, r: (b, i, 0)),
    )
    return pl.pallas_call(
        _normmod_kernel,
        out_shape=jax.ShapeDtypeStruct((bx, l, d), out_dtype),
        grid_spec=grid_spec,
        compiler_params=_cparams(("parallel", "parallel")),
        name="normmod",
    )(rows, x3, w.reshape(1, d), mod3, mod3)


def _mm_kernel(a_ref, w_ref, *rest, act, has_bias):
    o_ref = rest[-1]
    acc = jnp.dot(a_ref[...], w_ref[...], preferred_element_type=F32)
    if has_bias:
        acc = acc + rest[0][...]
    if act == "silu":
        acc = _silu(acc)
    elif act == "sigmoid":
        acc = jax.nn.sigmoid(acc)
    o_ref[...] = acc.astype(o_ref.dtype)


def _mm(a, w, bias=None, act=None, out_dtype=F32, tm=1024, tn=1024, name="mm"):
    m, k = a.shape
    n = w.shape[1]
    tm, tn = min(tm, m), min(tn, n)
    in_specs = [pl.BlockSpec((tm, k), lambda i, j: (i, 0)),
                pl.BlockSpec((k, tn), lambda i, j: (0, j))]
    args = [a, w]
    if bias is not None:
        in_specs.append(pl.BlockSpec((1, tn), lambda i, j: (0, j)))
        args.append(bias.reshape(1, n))
    return pl.pallas_call(
        functools.partial(_mm_kernel, act=act, has_bias=bias is not None),
        out_shape=jax.ShapeDtypeStruct((m, n), out_dtype),
        grid=(m // tm, n // tn),
        in_specs=in_specs,
        out_specs=pl.BlockSpec((tm, tn), lambda i, j: (i, j)),
        compiler_params=_cparams(("parallel", "parallel")),
        name=name,
    )(*args)


def _mm_glu_kernel(a_ref, wa_ref, wb_ref, o_ref):
    a = a_ref[...]
    va = jnp.dot(a, wa_ref[...], preferred_element_type=F32)
    vb = jnp.dot(a, wb_ref[...], preferred_element_type=F32)
    o_ref[...] = va * jax.nn.sigmoid(vb)


def _mm_glu(a, wa, wb, tm=1024, tn=512):
    m, k = a.shape
    n = wa.shape[1]
    tm = min(tm, m)
    return pl.pallas_call(
        _mm_glu_kernel,
        out_shape=jax.ShapeDtypeStruct((m, n), F32),
        grid=(m // tm, n // tn),
        in_specs=[pl.BlockSpec((tm, k), lambda i, j: (i, 0)),
                  pl.BlockSpec((k, tn), lambda i, j: (0, j)),
                  pl.BlockSpec((k, tn), lambda i, j: (0, j))],
        out_specs=pl.BlockSpec((tm, tn), lambda i, j: (i, j)),
        compiler_params=_cparams(("parallel", "parallel")),
        name="mm_glu",
    )(a, wa, wb)


def _mm_merge_kernel(a_ref, w_ref, b_ref, ga_ref, gb_ref, ys_ref, o_ref):
    ycf = jnp.dot(a_ref[...], w_ref[...], preferred_element_type=F32) + b_ref[...]
    o_ref[...] = (ga_ref[...] * ys_ref[...] + gb_ref[...] * ycf).astype(o_ref.dtype)


def _mm_merge(a, w, bias, gates, y_ssd, tm=1024, tn=512):
    m, k = a.shape
    n = w.shape[1]
    tm = min(tm, m)
    nj = n // tn
    return pl.pallas_call(
        _mm_merge_kernel,
        out_shape=jax.ShapeDtypeStruct((m, n), BF16),
        grid=(m // tm, nj),
        in_specs=[pl.BlockSpec((tm, k), lambda i, j: (i, 0)),
                  pl.BlockSpec((k, tn), lambda i, j: (0, j)),
                  pl.BlockSpec((1, tn), lambda i, j: (0, j)),
                  pl.BlockSpec((tm, tn), lambda i, j: (i, j)),
                  pl.BlockSpec((tm, tn), lambda i, j: (i, j + nj)),
                  pl.BlockSpec((tm, tn), lambda i, j: (i, j))],
        out_specs=pl.BlockSpec((tm, tn), lambda i, j: (i, j)),
        compiler_params=_cparams(("parallel", "parallel")),
        name="mm_merge",
    )(a, w, bias.reshape(1, n), gates, gates, y_ssd)


def _mm_resid_kernel(a_ref, w_ref, x_ref, g_ref, o_ref):
    out = jnp.dot(a_ref[...], w_ref[...], preferred_element_type=F32)
    o_ref[...] = x_ref[...] + g_ref[0] * out


def _mm_resid(a, w, x2, mod3, gate_chunk, rows_per_batch, tm=1024, tn=512):
    m, k = a.shape
    n = w.shape[1]
    tm = min(tm, rows_per_batch)
    nj = n // tn
    tiles_per_batch = rows_per_batch // tm
    return pl.pallas_call(
        _mm_resid_kernel,
        out_shape=jax.ShapeDtypeStruct((m, n), F32),
        grid=(m // tm, nj),
        in_specs=[pl.BlockSpec((tm, k), lambda i, j: (i, 0)),
                  pl.BlockSpec((k, tn), lambda i, j: (0, j)),
                  pl.BlockSpec((tm, tn), lambda i, j: (i, j)),
                  pl.BlockSpec((1, 1, tn),
                               lambda i, j: (i // tiles_per_batch, 0, gate_chunk * nj + j))],
        out_specs=pl.BlockSpec((tm, tn), lambda i, j: (i, j)),
        compiler_params=_cparams(("parallel", "parallel")),
        name="mm_resid",
    )(a, w, x2, mod3)


_CONV_PAD = 8


def _conv7_kernel(ctx_ref, lat_ref, w_ref, b_ref, o_ref, pad_ref, *, l_ctx, l_lat):
    p = _CONV_PAD
    zeros = jnp.zeros((p, LANE), F32)
    off_ctx = p
    off_lat = 2 * p + l_ctx
    pad_ref[0:p, :] = zeros
    pad_ref[off_ctx + l_ctx:off_lat, :] = zeros
    pad_ref[off_lat + l_lat:off_lat + l_lat + p, :] = zeros
    pad_ref[off_ctx:off_ctx + l_ctx, :] = ctx_ref[0]
    pad_ref[off_lat:off_lat + l_lat, :] = lat_ref[0]
    reach = SSM_CONV // 2
    bias = b_ref[...]

    def chunk(pad_base, out_base):
        win = pad_ref[pl.ds(pl.multiple_of(pad_base - p, p), CHUNK + 2 * p), :]
        nwin = CHUNK + 2 * p
        acc = jnp.broadcast_to(bias, (CHUNK, LANE))
        for k in range(SSM_CONV):
            start = p - reach + k
            if start == p:
                tap = win[p:p + CHUNK]
            else:
                tap = pltpu.roll(win, nwin - start, 0)[0:CHUNK]
            acc = acc + tap * w_ref[k:k + 1, :]
        o_ref[0, 0, pl.ds(out_base, CHUNK), :] = _silu(acc)

    def ctx_body(j, c):
        base = pl.multiple_of(j * CHUNK, CHUNK)
        chunk(off_ctx + base, base)
        return c

    def lat_body(j, c):
        base = pl.multiple_of(j * CHUNK, CHUNK)
        chunk(off_lat + base, l_ctx + base)
        return c

    lax.fori_loop(0, l_ctx // CHUNK, ctx_body, 0)
    lax.fori_loop(0, l_lat // CHUNK, lat_body, 0)


def _conv7(ctx_raw, lat_raw, w, b):
    bsz, l_ctx, c = ctx_raw.shape
    l_lat = lat_raw.shape[1]
    ltot = l_ctx + l_lat
    nct = c // LANE
    return pl.pallas_call(
        functools.partial(_conv7_kernel, l_ctx=l_ctx, l_lat=l_lat),
        out_shape=jax.ShapeDtypeStruct((bsz, nct, ltot, LANE), F32),
        grid=(bsz, nct),
        in_specs=[pl.BlockSpec((1, l_ctx, LANE), lambda bi, ci: (bi, 0, ci)),
                  pl.BlockSpec((1, l_lat, LANE), lambda bi, ci: (bi, 0, ci)),
                  pl.BlockSpec((SSM_CONV, LANE), lambda bi, ci: (0, ci)),
                  pl.BlockSpec((1, LANE), lambda bi, ci: (0, ci))],
        out_specs=pl.BlockSpec((1, 1, ltot, LANE), lambda bi, ci: (bi, ci, 0, 0)),
        scratch_shapes=[pltpu.VMEM((ltot + 3 * _CONV_PAD, LANE), F32)],
        compiler_params=_cparams(("parallel", "parallel")),
        name="conv7",
    )(ctx_raw, lat_raw, w, b.reshape(1, c))


def _ssd_kernel(xbc_ref, dtc_ref, dtl_ref, par_ref, dexp_ref, y_ref, st_ref, cumt_ref,
                *, reverse, n_ctx):
    i = pl.program_id(1)

    @pl.when(i == 0)
    def _():
        st_ref[...] = jnp.zeros_like(st_ref)

    dt_raw = jnp.where(i < n_ctx, dtc_ref[0], dtl_ref[0])
    bias = par_ref[0:1, :]
    a = -jnp.exp(par_ref[1:2, :])
    dt = jax.nn.softplus(dt_raw + bias)
    cum = dt * a
    row = lax.broadcasted_iota(jnp.int32, (CHUNK, LANE), 0)
    k = 1
    while k < CHUNK:
        if reverse:
            cum = cum + jnp.where(row < CHUNK - k, pltpu.roll(cum, CHUNK - k, 0), 0.0)
        else:
            cum = cum + jnp.where(row >= k, pltpu.roll(cum, k, 0), 0.0)
        k *= 2
    last = 0 if reverse else CHUNK - 1
    cumt_ref[...] = cum.T
    li = lax.broadcasted_iota(jnp.int32, (CHUNK, CHUNK), 0)
    si = lax.broadcasted_iota(jnp.int32, (CHUNK, CHUNK), 1)
    causal = (li <= si) if reverse else (li >= si)
    lo = lax.broadcasted_iota(jnp.int32, (CHUNK, LANE), 1) < HEAD_DIM
    heads_per_group = N_HEADS // N_GROUPS
    pairs = heads_per_group // 2
    x_tiles = N_HEADS // 2

    def group(g, carry):
        shift = (LANE - heads_per_group * g) & (LANE - 1)
        cum_g = pltpu.roll(cum, shift, 1)
        dt_g = pltpu.roll(dt, shift, 1)
        cum_t = cumt_ref[pl.ds(pl.multiple_of(heads_per_group * g, heads_per_group), heads_per_group), :]
        bb = xbc_ref[0, x_tiles + g].astype(BF16)
        cb = xbc_ref[0, x_tiles + N_GROUPS + g].astype(BF16)
        scores = lax.dot_general(cb, bb, (((1,), (1,)), ((), ())), preferred_element_type=F32)
        h_t = st_ref[g]
        y_off = jnp.dot(cb, h_t.astype(BF16), preferred_element_type=F32)
        xw_parts, dec_parts = [], []
        for p in range(pairs):
            j0, j1 = 2 * p, 2 * p + 1
            x2 = xbc_ref[0, pairs * g + p]
            c0 = cum_g[:, j0:j0 + 1]
            c1 = cum_g[:, j1:j1 + 1]
            l0 = jnp.exp(jnp.where(causal, c0 - cum_t[j0:j0 + 1, :], -jnp.inf))
            l1 = jnp.exp(jnp.where(causal, c1 - cum_t[j1:j1 + 1, :], -jnp.inf))
            m0 = (scores * l0).astype(BF16)
            m1 = (scores * l1).astype(BF16)
            dt2 = jnp.where(lo, dt_g[:, j0:j0 + 1], dt_g[:, j1:j1 + 1])
            c2 = jnp.where(lo, c0, c1)
            xdt = x2 * dt2
            xdt_b = xdt.astype(BF16)
            zero = jnp.zeros_like(xdt_b)
            y_diag = (jnp.dot(m0, jnp.where(lo, xdt_b, zero), preferred_element_type=F32)
                      + jnp.dot(m1, jnp.where(lo, zero, xdt_b), preferred_element_type=F32))
            e2 = jnp.exp(c2)
            y = y_diag + y_off[:, p * LANE:(p + 1) * LANE] * e2
            y_ref[0, pairs * g + p] = y + dexp_ref[pairs * g + p] * x2
            to_end = jnp.exp(c2[last:last + 1, :] - c2)
            xw_parts.append((xdt * to_end).astype(BF16))
            dec_parts.append(e2[last:last + 1, :])
        xw = jnp.concatenate(xw_parts, axis=1)
        dec = jnp.concatenate(dec_parts, axis=1)
        upd = lax.dot_general(bb, xw, (((0,), (0,)), ((), ())), preferred_element_type=F32)
        st_ref[g] = h_t * dec + upd
        return carry

    lax.fori_loop(0, N_GROUPS, group, 0)


def _ssd(xbc_act, dt_ctx, dt_lat, par, dexp, reverse):
    bsz, ntile, ltot, _ = xbc_act.shape
    l_ctx = dt_ctx.shape[1]
    l_lat = dt_lat.shape[1]
    n_ctx = l_ctx // CHUNK
    n_lat = l_lat // CHUNK
    steps = n_ctx + n_lat
    x_tiles = N_HEADS // 2

    if reverse:
        def cat_chunk(i):
            return jnp.where(i < n_ctx, n_ctx - 1 - i, n_ctx + steps - 1 - i)

        def ctx_chunk(i):
            return jnp.maximum(n_ctx - 1 - i, 0)

        def lat_chunk(i):
            return jnp.minimum(steps - 1 - i, n_lat - 1)
    else:
        def cat_chunk(i):
            return i

        def ctx_chunk(i):
            return jnp.minimum(i, n_ctx - 1)

        def lat_chunk(i):
            return jnp.maximum(i - n_ctx, 0)

    return pl.pallas_call(
        functools.partial(_ssd_kernel, reverse=reverse, n_ctx=n_ctx),
        out_shape=jax.ShapeDtypeStruct((bsz, x_tiles, l_lat, LANE), F32),
        grid=(bsz, steps),
        in_specs=[pl.BlockSpec((1, ntile, CHUNK, LANE), lambda b, i: (b, 0, cat_chunk(i), 0)),
                  pl.BlockSpec((1, CHUNK, LANE), lambda b, i: (b, ctx_chunk(i), 0)),
                  pl.BlockSpec((1, CHUNK, LANE), lambda b, i: (b, lat_chunk(i), 0)),
                  pl.BlockSpec((8, LANE), lambda b, i: (0, 0)),
                  pl.BlockSpec((x_tiles, 1, LANE), lambda b, i: (0, 0, 0))],
        out_specs=pl.BlockSpec((1, x_tiles, CHUNK, LANE), lambda b, i: (b, 0, lat_chunk(i), 0)),
        scratch_shapes=[pltpu.VMEM((N_GROUPS, D_STATE, (N_HEADS // N_GROUPS) * HEAD_DIM), F32),
                        pltpu.VMEM((LANE, CHUNK), F32)],
        compiler_params=_cparams(("parallel", "arbitrary")),
        name="ssd_bwd" if reverse else "ssd_fwd",
    )(xbc_act, dt_ctx, dt_lat, par, dexp)


def _gatenorm_kernel(yf_ref, yb_ref, sz_ref, w_ref, o_ref, g_ref):
    nt = yf_ref.shape[1]
    tm = yf_ref.shape[2]
    ss = jnp.zeros((tm, 1), F32)
    for j in range(nt):
        sl = slice(j * LANE, (j + 1) * LANE)
        g = (yf_ref[0, j] + yb_ref[0, j]) * sz_ref[:, sl]
        g_ref[:, sl] = g
        ss = ss + jnp.sum(g * g, axis=-1, keepdims=True)
    r = lax.rsqrt(ss / (nt * LANE) + EPS)
    o_ref[...] = (g_ref[...] * r * w_ref[...]).astype(o_ref.dtype)


def _gatenorm(y_f, y_b, sz, w, tm=256):
    bsz, nt, s, _ = y_f.shape
    dn = nt * LANE
    spb = s // tm
    return pl.pallas_call(
        _gatenorm_kernel,
        out_shape=jax.ShapeDtypeStruct((bsz * s, dn), BF16),
        grid=(bsz, spb),
        in_specs=[pl.BlockSpec((1, nt, tm, LANE), lambda b, i: (b, 0, i, 0)),
                  pl.BlockSpec((1, nt, tm, LANE), lambda b, i: (b, 0, i, 0)),
                  pl.BlockSpec((tm, dn), lambda b, i: (b * spb + i, 0)),
                  pl.BlockSpec((1, dn), lambda b, i: (0, 0))],
        out_specs=pl.BlockSpec((tm, dn), lambda b, i: (b * spb + i, 0)),
        scratch_shapes=[pltpu.VMEM((tm, dn), F32)],
        compiler_params=_cparams(("parallel", "parallel")),
        name="gatenorm",
    )(y_f, y_b, sz, w.reshape(1, dn))


def _conv31_kernel(u_ref, w_ref, b_ref, o_ref, pad_ref, *, seq):
    halo = (CF_KERNEL // 2) * GRID_W
    zeros = jnp.zeros((halo, LANE), F32)
    pad_ref[0:halo, :] = zeros
    pad_ref[halo + seq:halo + seq + halo, :] = zeros
    pad_ref[halo:halo + seq, :] = u_ref[0]
    bias = b_ref[...]

    def body(j, c):
        base = pl.multiple_of(j * CHUNK, CHUNK)
        acc = jnp.broadcast_to(bias, (CHUNK, LANE))
        for k in range(CF_KERNEL):
            tap = pad_ref[pl.ds(pl.multiple_of(base + k * GRID_W, GRID_W), CHUNK), :]
            acc = acc + tap * w_ref[k:k + 1, :]
        o_ref[0, pl.ds(base, CHUNK), :] = acc
        return c

    lax.fori_loop(0, seq // CHUNK, body, 0)


def _conv31(u3, w, b):
    bsz, s, c = u3.shape
    halo = (CF_KERNEL // 2) * GRID_W
    return pl.pallas_call(
        functools.partial(_conv31_kernel, seq=s),
        out_shape=jax.ShapeDtypeStruct((bsz, s, c), F32),
        grid=(bsz, c // LANE),
        in_specs=[pl.BlockSpec((1, s, LANE), lambda bi, ci: (bi, 0, ci)),
                  pl.BlockSpec((CF_KERNEL, LANE), lambda bi, ci: (0, ci)),
                  pl.BlockSpec((1, LANE), lambda bi, ci: (0, ci))],
        out_specs=pl.BlockSpec((1, s, LANE), lambda bi, ci: (bi, 0, ci)),
        scratch_shapes=[pltpu.VMEM((s + 2 * halo, LANE), F32)],
        compiler_params=_cparams(("parallel", "parallel")),
        name="conv31",
    )(u3, w, b.reshape(1, c))


def _lnsilu_kernel(x_ref, w_ref, b_ref, o_ref):
    xf = x_ref[...]
    mu = jnp.mean(xf, axis=-1, keepdims=True)
    xc = xf - mu
    var = jnp.mean(xc * xc, axis=-1, keepdims=True)
    y = xc * lax.rsqrt(var + EPS) * w_ref[...] + b_ref[...]
    o_ref[...] = _silu(y).astype(o_ref.dtype)


def _lnsilu(x2, w, b, tm=512):
    m, d = x2.shape
    tm = min(tm, m)
    return pl.pallas_call(
        _lnsilu_kernel,
        out_shape=jax.ShapeDtypeStruct((m, d), BF16),
        grid=(m // tm,),
        in_specs=[pl.BlockSpec((tm, d), lambda i: (i, 0)),
                  pl.BlockSpec((1, d), lambda i: (0, 0)),
                  pl.BlockSpec((1, d), lambda i: (0, 0))],
        out_specs=pl.BlockSpec((tm, d), lambda i: (i, 0)),
        compiler_params=_cparams(("parallel",)),
        name="lnsilu",
    )(x2, w.reshape(1, d), b.reshape(1, d))


def _route_kernel(x_ref, w_ref, sh_ref, sc_ref, rw_ref, rb_ref, h_ref, eid_ref, ew_ref):
    xf = x_ref[...]
    ms = jnp.mean(xf * xf, axis=-1, keepdims=True)
    h = xf * lax.rsqrt(ms + EPS) * w_ref[...]
    h = h * (1.0 + sc_ref[0]) + sh_ref[0]
    tm = xf.shape[0]
    nt = xf.shape[1] // LANE
    for j in range(nt):
        h_ref[pl.ds(j, tm, stride=nt), :] = h[:, j * LANE:(j + 1) * LANE]
    logits = jnp.dot(h.astype(BF16), rw_ref[...], preferred_element_type=F32) + rb_ref[...]
    lane = lax.broadcasted_iota(jnp.int32, (tm, LANE), 1)
    lane_f = lane.astype(F32)
    ninf = -jnp.inf
    gl = jnp.where(lane < MOE_GROUPS, logits, ninf)
    gmax = jnp.max(gl, axis=-1, keepdims=True)
    gidx = jnp.min(jnp.where(gl == gmax, lane_f, float(LANE)), axis=-1, keepdims=True)
    gsum = jnp.sum(jnp.exp(gl - gmax), axis=-1, keepdims=True)
    g_p = 1.0 / gsum
    first = float(MOE_GROUPS) + gidx * float(EXPERTS_PER_GROUP)
    in_group = (lane_f >= first) & (lane_f < first + float(EXPERTS_PER_GROUP))
    el = jnp.where(in_group, logits, ninf)
    m1 = jnp.max(el, axis=-1, keepdims=True)
    i1 = jnp.min(jnp.where(el == m1, lane_f, float(LANE)), axis=-1, keepdims=True)
    el2 = jnp.where(lane_f == i1, ninf, el)
    m2 = jnp.max(el2, axis=-1, keepdims=True)
    i2 = jnp.min(jnp.where(el2 == m2, lane_f, float(LANE)), axis=-1, keepdims=True)
    e21 = jnp.exp(m2 - m1)
    den = 1.0 + e21
    w1 = (1.0 / den) * g_p
    w2 = (e21 / den) * g_p
    e1 = (i1 - float(MOE_GROUPS)).astype(jnp.int32)
    e2 = (i2 - float(MOE_GROUPS)).astype(jnp.int32)
    eid_ref[...] = jnp.where(lane == 0, e1, jnp.where(lane == 1, e2, 0))
    ew_ref[...] = jnp.where(lane == 0, w1, jnp.where(lane == 1, w2, 0.0))


def _route(x2, w, mod3, shift_chunk, scale_chunk, rows_per_batch, rw, rb, tm=256):
    m, d = x2.shape
    nt = d // LANE
    tiles_per_batch = rows_per_batch // tm
    return pl.pallas_call(
        _route_kernel,
        out_shape=(jax.ShapeDtypeStruct((m * nt, LANE), F32),
                   jax.ShapeDtypeStruct((m, LANE), jnp.int32),
                   jax.ShapeDtypeStruct((m, LANE), F32)),
        grid=(m // tm,),
        in_specs=[pl.BlockSpec((tm, d), lambda i: (i, 0)),
                  pl.BlockSpec((1, d), lambda i: (0, 0)),
                  pl.BlockSpec((1, 1, d), lambda i: (i // tiles_per_batch, 0, shift_chunk)),
                  pl.BlockSpec((1, 1, d), lambda i: (i // tiles_per_batch, 0, scale_chunk)),
                  pl.BlockSpec((d, LANE), lambda i: (0, 0)),
                  pl.BlockSpec((1, LANE), lambda i: (0, 0))],
        out_specs=(pl.BlockSpec((tm * nt, LANE), lambda i: (i, 0)),
                   pl.BlockSpec((tm, LANE), lambda i: (i, 0)),
                   pl.BlockSpec((tm, LANE), lambda i: (i, 0))),
        compiler_params=_cparams(("parallel",)),
        name="route",
    )(x2, w.reshape(1, d), mod3, mod3, rw, rb)


def _rows_to_matrix(ref, tm, nt):
    return jnp.concatenate([ref[pl.ds(j, tm, stride=nt), :] for j in range(nt)], axis=1)


def _token_copy(src, dst, sem, nt, src_tok, dst_tok):
    return pltpu.make_async_copy(src.at[pl.ds(src_tok * nt, nt), :],
                                 dst.at[pl.ds(dst_tok * nt, nt), :], sem)


def _expert_up_kernel(be_ref, nv_ref, nused_ref, tokc_ref, tokn_ref, h_hbm, wg_ref, wu_ref, o_ref,
                      xs0_ref, xs1_ref, wgb_ref, wub_ref, sem, *, fchunk, nt):
    b = pl.program_id(0)
    n_used = nused_ref[0]
    dff = wg_ref.shape[2]
    slots = (xs0_ref, xs1_ref)

    def start_gather(tok_ref, blk, slot):
        def body(r, c):
            _token_copy(h_hbm, slots[slot], sem.at[slot], nt, tok_ref[0, 0, r], r).start()
            return c
        lax.fori_loop(0, nv_ref[blk], body, 0)

    def wait_gather(blk, slot):
        def body(r, c):
            _token_copy(h_hbm, slots[slot], sem.at[slot], nt, 0, r).wait()
            return c
        lax.fori_loop(0, nv_ref[blk], body, 0)

    @pl.when(b == 0)
    def _():
        xs0_ref[...] = jnp.zeros_like(xs0_ref)
        xs1_ref[...] = jnp.zeros_like(xs1_ref)
        start_gather(tokc_ref, 0, 0)

    for slot in range(2):
        @pl.when((b + 1 < n_used) & (lax.rem(b, 2) == slot))
        def _(slot=slot):
            start_gather(tokn_ref, b + 1, 1 - slot)

    @pl.when(b < n_used)
    def _():
        prev = jnp.maximum(b - 1, 0)

        @pl.when((b == 0) | (be_ref[b] != be_ref[prev]))
        def _():
            wgb_ref[...] = wg_ref[0].astype(BF16)
            wub_ref[...] = wu_ref[0].astype(BF16)

        for slot in range(2):
            @pl.when(lax.rem(b, 2) == slot)
            def _(slot=slot):
                wait_gather(b, slot)
                xb = _rows_to_matrix(slots[slot], MOE_BLOCK, nt).astype(BF16)
                for f in range(dff // fchunk):
                    sl = slice(f * fchunk, (f + 1) * fchunk)
                    gate = jnp.dot(xb, wgb_ref[:, sl], preferred_element_type=F32)
                    up = jnp.dot(xb, wub_ref[:, sl], preferred_element_type=F32)
                    o_ref[:, sl] = (_silu(gate) * up).astype(o_ref.dtype)

    @pl.when(b >= n_used)
    def _():
        o_ref[...] = jnp.zeros_like(o_ref)


def _expert_up(h2t, buf_tok, w_gate, w_up, block_expert, n_valid, n_used, fchunk=256):
    n_blocks = buf_tok.shape[0]
    _, d, dff = w_gate.shape
    nt = d // LANE

    def blk(b, n):
        return jnp.minimum(b, n[0] - 1)

    grid_spec = pltpu.PrefetchScalarGridSpec(
        num_scalar_prefetch=3,
        grid=(n_blocks,),
        in_specs=[pl.BlockSpec((1, 1, MOE_BLOCK), lambda b, be, nv, n: (b, 0, 0), memory_space=pltpu.SMEM),
                  pl.BlockSpec((1, 1, MOE_BLOCK), lambda b, be, nv, n: (jnp.minimum(b + 1, n_blocks - 1), 0, 0),
                               memory_space=pltpu.SMEM),
                  pl.BlockSpec(memory_space=pl.ANY),
                  pl.BlockSpec((1, d, dff), lambda b, be, nv, n: (be[blk(b, n)], 0, 0)),
                  pl.BlockSpec((1, d, dff), lambda b, be, nv, n: (be[blk(b, n)], 0, 0))],
        out_specs=pl.BlockSpec((MOE_BLOCK, dff), lambda b, be, nv, n: (b, 0)),
        scratch_shapes=[pltpu.VMEM((MOE_BLOCK * nt, LANE), F32),
                        pltpu.VMEM((MOE_BLOCK * nt, LANE), F32),
                        pltpu.VMEM((d, dff), BF16),
                        pltpu.VMEM((d, dff), BF16),
                        pltpu.SemaphoreType.DMA((2,))],
    )
    return pl.pallas_call(
        functools.partial(_expert_up_kernel, fchunk=fchunk, nt=nt),
        out_shape=jax.ShapeDtypeStruct((n_blocks * MOE_BLOCK, dff), BF16),
        grid_spec=grid_spec,
        compiler_params=_cparams(("arbitrary",)),
        name="expert_up",
    )(block_expert, n_valid, n_used, buf_tok, buf_tok, h2t, w_gate, w_up)


def _expert_down_kernel(be_ref, nv_ref, nused_ref, asg_ref, h_ref, wd_ref, y_hbm,
                        ys0_ref, ys1_ref, wdb_ref, sem, *, nchunk, nt, n_tok):
    b = pl.program_id(0)
    n_used = nused_ref[0]
    d = wd_ref.shape[2]
    slots = (ys0_ref, ys1_ref)

    def row_copy(slot, r, assign):
        k = lax.rem(assign, 2)
        tok = lax.div(assign, 2)
        return _token_copy(slots[slot], y_hbm, sem.at[slot], nt, r, k * n_tok + tok)

    def start_scatter(blk, slot):
        def body(r, c):
            row_copy(slot, r, asg_ref[0, 0, r]).start()
            return c
        lax.fori_loop(0, nv_ref[blk], body, 0)

    def wait_scatter(blk, slot):
        def body(r, c):
            row_copy(slot, r, 0).wait()
            return c
        lax.fori_loop(0, nv_ref[blk], body, 0)

    @pl.when(b < n_used)
    def _():
        prev = jnp.maximum(b - 1, 0)

        @pl.when((b == 0) | (be_ref[b] != be_ref[prev]))
        def _():
            wdb_ref[...] = wd_ref[0].astype(BF16)

        hb = h_ref[...]
        for slot in range(2):
            @pl.when(lax.rem(b, 2) == slot)
            def _(slot=slot):
                for c in range(d // nchunk):
                    out = jnp.dot(hb, wdb_ref[:, c * nchunk:(c + 1) * nchunk], preferred_element_type=F32)
                    for j in range(nchunk // LANE):
                        slots[slot][pl.ds(c * (nchunk // LANE) + j, MOE_BLOCK, stride=nt), :] = (
                            out[:, j * LANE:(j + 1) * LANE])

    for slot in range(2):
        @pl.when((b >= 1) & (b - 1 < n_used) & (lax.rem(b, 2) == slot))
        def _(slot=slot):
            wait_scatter(b - 1, 1 - slot)

        @pl.when((b < n_used) & (lax.rem(b, 2) == slot))
        def _(slot=slot):
            start_scatter(b, slot)

        @pl.when((b == pl.num_programs(0) - 1) & (b < n_used) & (lax.rem(b, 2) == slot))
        def _(slot=slot):
            wait_scatter(b, slot)


def _expert_down(hid, buf_assign, w_down, block_expert, n_valid, n_used, n_tok, nchunk=512):
    n_rows, dff = hid.shape
    n_blocks = n_rows // MOE_BLOCK
    d = w_down.shape[2]
    nt = d // LANE

    def blk(b, n):
        return jnp.minimum(b, n[0] - 1)

    grid_spec = pltpu.PrefetchScalarGridSpec(
        num_scalar_prefetch=3,
        grid=(n_blocks,),
        in_specs=[pl.BlockSpec((1, 1, MOE_BLOCK), lambda b, be, nv, n: (b, 0, 0), memory_space=pltpu.SMEM),
                  pl.BlockSpec((MOE_BLOCK, dff), lambda b, be, nv, n: (blk(b, n), 0)),
                  pl.BlockSpec((1, dff, d), lambda b, be, nv, n: (be[blk(b, n)], 0, 0))],
        out_specs=pl.BlockSpec(memory_space=pl.ANY),
        scratch_shapes=[pltpu.VMEM((MOE_BLOCK * nt, LANE), F32),
                        pltpu.VMEM((MOE_BLOCK * nt, LANE), F32),
                        pltpu.VMEM((dff, d), BF16),
                        pltpu.SemaphoreType.DMA((2,))],
    )
    return pl.pallas_call(
        functools.partial(_expert_down_kernel, nchunk=nchunk, nt=nt, n_tok=n_tok),
        out_shape=jax.ShapeDtypeStruct((2 * n_tok * nt, LANE), F32),
        grid_spec=grid_spec,
        compiler_params=_cparams(("arbitrary",)),
        name="expert_down",
    )(block_expert, n_valid, n_used, buf_assign, hid, w_down)


def _combine_kernel(y0_ref, y1_ref, ew_ref, x_ref, g_ref, w_ref, o_ref):
    tm, d = x_ref.shape
    nt = d // LANE
    ew = ew_ref[...]
    moe = (_rows_to_matrix(y0_ref, tm, nt) * ew[:, 0:1]
           + _rows_to_matrix(y1_ref, tm, nt) * ew[:, 1:2])
    xo = x_ref[...] + g_ref[0] * moe
    ms = jnp.mean(xo * xo, axis=-1, keepdims=True)
    o_ref[...] = xo * lax.rsqrt(ms + EPS) * w_ref[...]


def _combine(y, ew, x2, mod3, gate_chunk, rows_per_batch, final_w, tm=256):
    m, d = x2.shape
    nt = d // LANE
    tiles = m // tm
    tiles_per_batch = rows_per_batch // tm
    return pl.pallas_call(
        _combine_kernel,
        out_shape=jax.ShapeDtypeStruct((m, d), F32),
        grid=(tiles,),
        in_specs=[pl.BlockSpec((tm * nt, LANE), lambda i: (i, 0)),
                  pl.BlockSpec((tm * nt, LANE), lambda i: (tiles + i, 0)),
                  pl.BlockSpec((tm, LANE), lambda i: (i, 0)),
                  pl.BlockSpec((tm, d), lambda i: (i, 0)),
                  pl.BlockSpec((1, 1, d), lambda i: (i // tiles_per_batch, 0, gate_chunk)),
                  pl.BlockSpec((1, d), lambda i: (0, 0))],
        out_specs=pl.BlockSpec((tm, d), lambda i: (i, 0)),
        compiler_params=_cparams(("parallel",)),
        name="moe_combine",
    )(y, y, ew, x2, mod3, final_w.reshape(1, d))


def _dispatch_tables(eid, n_tok):
    top_k = eid.shape[1]
    n_assign = n_tok * top_k
    expert = eid.reshape(-1)
    key = jnp.sort(expert * n_assign + jnp.arange(n_assign, dtype=jnp.int32))
    sorted_assign = key % n_assign
    bounds = jnp.arange(N_EXPERTS + 1, dtype=jnp.int32) * n_assign
    start = jnp.searchsorted(key, bounds, side="left").astype(jnp.int32)
    counts = start[1:] - start[:-1]
    nblk = (counts + MOE_BLOCK - 1) // MOE_BLOCK
    blk_end = jnp.cumsum(nblk)
    blk_start = blk_end - nblk
    n_blocks = -(-n_assign // MOE_BLOCK) + N_EXPERTS
    bidx = jnp.arange(n_blocks, dtype=jnp.int32)
    block_expert = jnp.minimum(jnp.searchsorted(blk_end, bidx, side="right"), N_EXPERTS - 1).astype(jnp.int32)
    in_expert = (bidx - blk_start[block_expert]) * MOE_BLOCK
    n_valid = jnp.clip(counts[block_expert] - in_expert, 0, MOE_BLOCK).astype(jnp.int32)
    src = start[block_expert][:, None] + in_expert[:, None] + jnp.arange(MOE_BLOCK, dtype=jnp.int32)[None, :]
    valid = jnp.arange(MOE_BLOCK, dtype=jnp.int32)[None, :] < n_valid[:, None]
    buf_assign = jnp.where(valid, sorted_assign[jnp.clip(src, 0, n_assign - 1)], 0).astype(jnp.int32)
    n_used = blk_end[-1].astype(jnp.int32).reshape(1)
    return buf_assign.reshape(n_blocks, 1, MOE_BLOCK), block_expert, n_valid, n_used


def kernel(x, c, ctx, c_ctx, ada_w, ada_b, norm1_w, w_in, ssm_conv_w, ssm_conv_b, dt_bias, a_log, d_skip, ssm_norm_w, ssm_out_w, cf_dw_w, cf_dw_b, cf_ln_w, cf_ln_b, cf_out_w, cf_out_b, w_o, norm2_w, router_group_w, router_group_b, router_expert_w, router_expert_b, expert_w_gate, expert_w_up, expert_w_down, final_norm_w):
    bsz, seq, d = x.shape
    l_ctx = ctx.shape[1]
    n_tok = bsz * seq
    d_inner = ssm_norm_w.shape[1]
    gn = N_GROUPS * D_STATE
    xbc_dim = d_inner + 2 * gn
    off_dt = xbc_dim
    off_z = off_dt + N_HEADS
    off_glu = off_z + d_inner
    off_gate = off_glu + 2 * d

    ctx_row = bsz
    crows = jnp.zeros((8, d), F32).at[:bsz].set(c).at[ctx_row].set(c_ctx)
    mod = _ada(crows, ada_w[0], ada_b[0])
    mod3 = mod.reshape(8, 1, 6 * d)
    lat_rows = jnp.arange(bsz, dtype=jnp.int32)
    ctx_rows = jnp.full((bsz,), ctx_row, jnp.int32)

    h_lat = _normmod(x, norm1_w[0], mod3, lat_rows, 0, 1, BF16).reshape(n_tok, d)
    h_ctx = _normmod(ctx, norm1_w[0], mod3, ctx_rows, 0, 1, BF16).reshape(bsz * l_ctx, d)

    w = w_in[0]
    w_xbc = w[:, :xbc_dim].astype(BF16)
    w_dt = jnp.pad(w[:, off_dt:off_z], ((0, 0), (0, LANE - N_HEADS))).astype(BF16)
    w_z = w[:, off_z:off_glu].astype(BF16)
    w_glu_a = w[:, off_glu:off_glu + d].astype(BF16)
    w_glu_b = w[:, off_glu + d:off_gate].astype(BF16)
    w_gate = w[:, off_gate:].astype(BF16)

    xbc_lat = _mm(h_lat, w_xbc, name="in_xbc").reshape(bsz, seq, xbc_dim)
    xbc_ctx = _mm(h_ctx, w_xbc, tm=512, name="in_xbc_ctx").reshape(bsz, l_ctx, xbc_dim)
    dt_lat = _mm(h_lat, w_dt, name="in_dt").reshape(bsz, seq, LANE)
    dt_ctx = _mm(h_ctx, w_dt, tm=512, name="in_dt_ctx").reshape(bsz, l_ctx, LANE)
    sz = _mm(h_lat, w_z, act="silu", name="in_z")
    u = _mm_glu(h_lat, w_glu_a, w_glu_b)
    gates = _mm(h_lat, w_gate, act="sigmoid", name="in_gate")

    xbc_act = _conv7(xbc_ctx, xbc_lat, ssm_conv_w[0], ssm_conv_b[0])

    ys = []
    for k in range(2):
        par = jnp.zeros((8, LANE), F32).at[0, :N_HEADS].set(dt_bias[0, k]).at[1, :N_HEADS].set(a_log[0, k])
        dexp = jnp.repeat(d_skip[0, k], HEAD_DIM).reshape(N_HEADS // 2, 1, LANE)
        ys.append(_ssd(xbc_act, dt_ctx, dt_lat, par, dexp, reverse=(k == 1)))

    gnorm = _gatenorm(ys[0], ys[1], sz, ssm_norm_w[0])
    y_ssd = _mm(gnorm, ssm_out_w[0].astype(BF16), tn=512, name="ssm_out")

    cv = _conv31(u.reshape(bsz, seq, d), cf_dw_w[0], cf_dw_b[0]).reshape(n_tok, d)
    ua = _lnsilu(cv, cf_ln_w[0], cf_ln_b[0])
    merged = _mm_merge(ua, cf_out_w[0].astype(BF16), cf_out_b[0], gates, y_ssd)
    x1 = _mm_resid(merged, w_o[0].astype(BF16), x.reshape(n_tok, d), mod3, 2, seq)

    n_r = MOE_GROUPS + N_EXPERTS
    rw = jnp.pad(jnp.concatenate([router_group_w[0], router_expert_w[0]], axis=1),
                 ((0, 0), (0, LANE - n_r))).astype(BF16)
    rb = jnp.pad(jnp.concatenate([router_group_b[0], router_expert_b[0]]), (0, LANE - n_r)).reshape(1, LANE)
    h2t, eid, ew = _route(x1, norm2_w[0], mod3, 3, 4, seq, rw, rb)

    buf_assign, block_expert, n_valid, n_used = _dispatch_tables(eid[:, :2], n_tok)
    hid = _expert_up(h2t, buf_assign // 2, expert_w_gate[0], expert_w_up[0], block_expert, n_valid, n_used)
    y = _expert_down(hid, buf_assign, expert_w_down[0], block_expert, n_valid, n_used, n_tok)
    out = _combine(y, ew, x1, mod3, 5, seq, final_norm_w)
    return out.reshape(bsz, seq, d)
```

```python
import functools

import jax
import jax.numpy as jnp
from jax import lax
from jax.experimental import pallas as pl
from jax.experimental.pallas import tpu as pltpu

F32 = jnp.float32
BF16 = jnp.bfloat16

EPS = 1e-6
GRID_W = 64
HEAD_DIM = 64
N_HEADS = 64
N_GROUPS = 8
D_STATE = 128
CHUNK = 128
SSM_CONV = 7
CF_KERNEL = 31
MOE_GROUPS = 8
EXPERTS_PER_GROUP = 8
N_EXPERTS = 64
MOE_BLOCK = 256
LANE = 128
VMEM_LIMIT = 56 * 1024 * 1024


def _cparams(sem):
    return pltpu.CompilerParams(dimension_semantics=sem, vmem_limit_bytes=VMEM_LIMIT)


def _silu(v):
    return v * jax.nn.sigmoid(v)


def _pitch(nt):
    return nt + 1


def _ada_kernel(c_ref, w_ref, b_ref, o_ref):
    s = _silu(c_ref[...])
    o_ref[...] = jnp.dot(s.astype(BF16), w_ref[...].astype(BF16),
                         preferred_element_type=F32) + b_ref[...]


def _ada(crows, ada_w, ada_b, tn=1024):
    r, d = crows.shape
    n = ada_w.shape[1]
    return pl.pallas_call(
        _ada_kernel,
        out_shape=jax.ShapeDtypeStruct((r, n), F32),
        grid=(n // tn,),
        in_specs=[pl.BlockSpec((r, d), lambda j: (0, 0)),
                  pl.BlockSpec((d, tn), lambda j: (0, j)),
                  pl.BlockSpec((1, tn), lambda j: (0, j))],
        out_specs=pl.BlockSpec((r, tn), lambda j: (0, j)),
        compiler_params=_cparams(("parallel",)),
        name="ada",
    )(crows, ada_w, ada_b.reshape(1, n))


def _normmod_kernel(rows_ref, x_ref, w_ref, sh_ref, sc_ref, o_ref):
    del rows_ref
    xf = x_ref[0]
    ms = jnp.mean(xf * xf, axis=-1, keepdims=True)
    y = xf * lax.rsqrt(ms + EPS) * w_ref[...]
    o_ref[0] = (y * (1.0 + sc_ref[0]) + sh_ref[0]).astype(o_ref.dtype)


def _normmod(x3, w, mod3, rows, shift_chunk, scale_chunk, out_dtype, tm=256):
    bx, l, d = x3.shape
    grid_spec = pltpu.PrefetchScalarGridSpec(
        num_scalar_prefetch=1,
        grid=(bx, l // tm),
        in_specs=[pl.BlockSpec((1, tm, d), lambda b, i, r: (b, i, 0)),
                  pl.BlockSpec((1, d), lambda b, i, r: (0, 0)),
                  pl.BlockSpec((1, 1, d), lambda b, i, r: (r[b], 0, shift_chunk)),
                  pl.BlockSpec((1, 1, d), lambda b, i, r: (r[b], 0, scale_chunk))],
        out_specs=pl.BlockSpec((1, tm, d), lambda b, i, r: (b, i, 0)),
    )
    return pl.pallas_call(
        _normmod_kernel,
        out_shape=jax.ShapeDtypeStruct((bx, l, d), out_dtype),
        grid_spec=grid_spec,
        compiler_params=_cparams(("parallel", "parallel")),
        name="normmod",
    )(rows, x3, w.reshape(1, d), mod3, mod3)


def _mm_kernel(a_ref, w_ref, *rest, act, has_bias):
    o_ref = rest[-1]
    acc = jnp.dot(a_ref[...], w_ref[...], preferred_element_type=F32)
    if has_bias:
        acc = acc + rest[0][...]
    if act == "silu":
        acc = _silu(acc)
    elif act == "sigmoid":
        acc = jax.nn.sigmoid(acc)
    o_ref[...] = acc.astype(o_ref.dtype)


def _mm(a, w, bias=None, act=None, out_dtype=F32, tm=1024, tn=1024, name="mm"):
    m, k = a.shape
    n = w.shape[1]
    tm, tn = min(tm, m), min(tn, n)
    in_specs = [pl.BlockSpec((tm, k), lambda i, j: (i, 0)),
                pl.BlockSpec((k, tn), lambda i, j: (0, j))]
    args = [a, w]
    if bias is not None:
        in_specs.append(pl.BlockSpec((1, tn), lambda i, j: (0, j)))
        args.append(bias.reshape(1, n))
    return pl.pallas_call(
        functools.partial(_mm_kernel, act=act, has_bias=bias is not None),
        out_shape=jax.ShapeDtypeStruct((m, n), out_dtype),
        grid=(m // tm, n // tn),
        in_specs=in_specs,
        out_specs=pl.BlockSpec((tm, tn), lambda i, j: (i, j)),
        compiler_params=_cparams(("parallel", "parallel")),
        name=name,
    )(*args)


def _mm_glu_kernel(a_ref, wa_ref, wb_ref, o_ref):
    a = a_ref[...]
    va = jnp.dot(a, wa_ref[...], preferred_element_type=F32)
    vb = jnp.dot(a, wb_ref[...], preferred_element_type=F32)
    o_ref[...] = va * jax.nn.sigmoid(vb)


def _mm_glu(a, wa, wb, tm=1024, tn=512):
    m, k = a.shape
    n = wa.shape[1]
    tm = min(tm, m)
    return pl.pallas_call(
        _mm_glu_kernel,
        out_shape=jax.ShapeDtypeStruct((m, n), F32),
        grid=(m // tm, n // tn),
        in_specs=[pl.BlockSpec((tm, k), lambda i, j: (i, 0)),
                  pl.BlockSpec((k, tn), lambda i, j: (0, j)),
                  pl.BlockSpec((k, tn), lambda i, j: (0, j))],
        out_specs=pl.BlockSpec((tm, tn), lambda i, j: (i, j)),
        compiler_params=_cparams(("parallel", "parallel")),
        name="mm_glu",
    )(a, wa, wb)


def _mm_merge_kernel(a_ref, w_ref, b_ref, ga_ref, gb_ref, ys_ref, o_ref):
    ycf = jnp.dot(a_ref[...], w_ref[...], preferred_element_type=F32) + b_ref[...]
    o_ref[...] = (ga_ref[...] * ys_ref[...] + gb_ref[...] * ycf).astype(o_ref.dtype)


def _mm_merge(a, w, bias, gates, y_ssd, tm=1024, tn=512):
    m, k = a.shape
    n = w.shape[1]
    tm = min(tm, m)
    nj = n // tn
    return pl.pallas_call(
        _mm_merge_kernel,
        out_shape=jax.ShapeDtypeStruct((m, n), BF16),
        grid=(m // tm, nj),
        in_specs=[pl.BlockSpec((tm, k), lambda i, j: (i, 0)),
                  pl.BlockSpec((k, tn), lambda i, j: (0, j)),
                  pl.BlockSpec((1, tn), lambda i, j: (0, j)),
                  pl.BlockSpec((tm, tn), lambda i, j: (i, j)),
                  pl.BlockSpec((tm, tn), lambda i, j: (i, j + nj)),
                  pl.BlockSpec((tm, tn), lambda i, j: (i, j))],
        out_specs=pl.BlockSpec((tm, tn), lambda i, j: (i, j)),
        compiler_params=_cparams(("parallel", "parallel")),
        name="mm_merge",
    )(a, w, bias.reshape(1, n), gates, gates, y_ssd)


def _mm_resid_kernel(a_ref, w_ref, x_ref, g_ref, o_ref):
    out = jnp.dot(a_ref[...], w_ref[...], preferred_element_type=F32)
    o_ref[...] = x_ref[...] + g_ref[0] * out


def _mm_resid(a, w, x2, mod3, gate_chunk, rows_per_batch, tm=1024, tn=512):
    m, k = a.shape
    n = w.shape[1]
    tm = min(tm, rows_per_batch)
    nj = n // tn
    tiles_per_batch = rows_per_batch // tm
    return pl.pallas_call(
        _mm_resid_kernel,
        out_shape=jax.ShapeDtypeStruct((m, n), F32),
        grid=(m // tm, nj),
        in_specs=[pl.BlockSpec((tm, k), lambda i, j: (i, 0)),
                  pl.BlockSpec((k, tn), lambda i, j: (0, j)),
                  pl.BlockSpec((tm, tn), lambda i, j: (i, j)),
                  pl.BlockSpec((1, 1, tn),
                               lambda i, j: (i // tiles_per_batch, 0, gate_chunk * nj + j))],
        out_specs=pl.BlockSpec((tm, tn), lambda i, j: (i, j)),
        compiler_params=_cparams(("parallel", "parallel")),
        name="mm_resid",
    )(a, w, x2, mod3)


_CONV_PAD = 8


def _conv7_kernel(ctx_ref, lat_ref, w_ref, b_ref, o_ref, pad_ref, *, l_ctx, l_lat):
    p = _CONV_PAD
    zeros = jnp.zeros((p, LANE), F32)
    off_ctx = p
    off_lat = 2 * p + l_ctx
    pad_ref[0:p, :] = zeros
    pad_ref[off_ctx + l_ctx:off_lat, :] = zeros
    pad_ref[off_lat + l_lat:off_lat + l_lat + p, :] = zeros
    pad_ref[off_ctx:off_ctx + l_ctx, :] = ctx_ref[0]
    pad_ref[off_lat:off_lat + l_lat, :] = lat_ref[0]
    reach = SSM_CONV // 2
    bias = b_ref[...]

    def chunk(pad_base, out_base):
        acc = jnp.broadcast_to(bias, (CHUNK, LANE))
        for k in range(SSM_CONV):
            tap = pad_ref[pl.ds(pad_base - reach + k, CHUNK), :]
            acc = acc + tap * w_ref[k:k + 1, :]
        o_ref[0, 0, pl.ds(out_base, CHUNK), :] = _silu(acc)

    def ctx_body(j, c):
        base = pl.multiple_of(j * CHUNK, CHUNK)
        chunk(off_ctx + base, base)
        return c

    def lat_body(j, c):
        base = pl.multiple_of(j * CHUNK, CHUNK)
        chunk(off_lat + base, l_ctx + base)
        return c

    lax.fori_loop(0, l_ctx // CHUNK, ctx_body, 0)
    lax.fori_loop(0, l_lat // CHUNK, lat_body, 0)


def _conv7(ctx_raw, lat_raw, w, b):
    bsz, l_ctx, c = ctx_raw.shape
    l_lat = lat_raw.shape[1]
    ltot = l_ctx + l_lat
    nct = c // LANE
    return pl.pallas_call(
        functools.partial(_conv7_kernel, l_ctx=l_ctx, l_lat=l_lat),
        out_shape=jax.ShapeDtypeStruct((bsz, nct, ltot, LANE), F32),
        grid=(bsz, nct),
        in_specs=[pl.BlockSpec((1, l_ctx, LANE), lambda bi, ci: (bi, 0, ci)),
                  pl.BlockSpec((1, l_lat, LANE), lambda bi, ci: (bi, 0, ci)),
                  pl.BlockSpec((SSM_CONV, LANE), lambda bi, ci: (0, ci)),
                  pl.BlockSpec((1, LANE), lambda bi, ci: (0, ci))],
        out_specs=pl.BlockSpec((1, 1, ltot, LANE), lambda bi, ci: (bi, ci, 0, 0)),
        scratch_shapes=[pltpu.VMEM((ltot + 3 * _CONV_PAD, LANE), F32)],
        compiler_params=_cparams(("parallel", "parallel")),
        name="conv7",
    )(ctx_raw, lat_raw, w, b.reshape(1, c))


def _ssd_kernel(xbc_ref, dtc_ref, dtl_ref, par_ref, dexp_ref, y_ref, st_ref, cumt_ref,
                *, reverse, n_ctx):
    i = pl.program_id(1)

    @pl.when(i == 0)
    def _():
        st_ref[...] = jnp.zeros_like(st_ref)

    dt_raw = jnp.where(i < n_ctx, dtc_ref[0], dtl_ref[0])
    bias = par_ref[0:1, :]
    a = -jnp.exp(par_ref[1:2, :])
    dt = jax.nn.softplus(dt_raw + bias)
    cum = dt * a
    row = lax.broadcasted_iota(jnp.int32, (CHUNK, LANE), 0)
    k = 1
    while k < CHUNK:
        if reverse:
            cum = cum + jnp.where(row < CHUNK - k, pltpu.roll(cum, CHUNK - k, 0), 0.0)
        else:
            cum = cum + jnp.where(row >= k, pltpu.roll(cum, k, 0), 0.0)
        k *= 2
    last = 0 if reverse else CHUNK - 1
    cumt_ref[...] = cum.T
    li = lax.broadcasted_iota(jnp.int32, (CHUNK, CHUNK), 0)
    si = lax.broadcasted_iota(jnp.int32, (CHUNK, CHUNK), 1)
    causal = (li <= si) if reverse else (li >= si)
    lo = lax.broadcasted_iota(jnp.int32, (CHUNK, LANE), 1) < HEAD_DIM
    heads_per_group = N_HEADS // N_GROUPS
    pairs = heads_per_group // 2
    x_tiles = N_HEADS // 2

    def group(g, carry):
        shift = (LANE - heads_per_group * g) & (LANE - 1)
        cum_g = pltpu.roll(cum, shift, 1)
        dt_g = pltpu.roll(dt, shift, 1)
        cum_t = cumt_ref[pl.ds(pl.multiple_of(heads_per_group * g, heads_per_group), heads_per_group), :]
        bb = xbc_ref[0, x_tiles + g].astype(BF16)
        cb = xbc_ref[0, x_tiles + N_GROUPS + g].astype(BF16)
        scores = lax.dot_general(cb, bb, (((1,), (1,)), ((), ())), preferred_element_type=F32)
        h_t = st_ref[g]
        y_off = jnp.dot(cb, h_t.astype(BF16), preferred_element_type=F32)
        xw_parts, dec_parts = [], []
        for p in range(pairs):
            j0, j1 = 2 * p, 2 * p + 1
            x2 = xbc_ref[0, pairs * g + p]
            c0 = cum_g[:, j0:j0 + 1]
            c1 = cum_g[:, j1:j1 + 1]
            l0 = jnp.exp(jnp.where(causal, c0 - cum_t[j0:j0 + 1, :], -jnp.inf))
            l1 = jnp.exp(jnp.where(causal, c1 - cum_t[j1:j1 + 1, :], -jnp.inf))
            m0 = (scores * l0).astype(BF16)
            m1 = (scores * l1).astype(BF16)
            dt2 = jnp.where(lo, dt_g[:, j0:j0 + 1], dt_g[:, j1:j1 + 1])
            c2 = jnp.where(lo, c0, c1)
            xdt = x2 * dt2
            xdt_b = xdt.astype(BF16)
            zero = jnp.zeros_like(xdt_b)
            y_diag = (jnp.dot(m0, jnp.where(lo, xdt_b, zero), preferred_element_type=F32)
                      + jnp.dot(m1, jnp.where(lo, zero, xdt_b), preferred_element_type=F32))
            e2 = jnp.exp(c2)
            y = y_diag + y_off[:, p * LANE:(p + 1) * LANE] * e2
            y_ref[0, pairs * g + p] = y + dexp_ref[pairs * g + p] * x2
            to_end = jnp.exp(c2[last:last + 1, :] - c2)
            xw_parts.append((xdt * to_end).astype(BF16))
            dec_parts.append(e2[last:last + 1, :])
        xw = jnp.concatenate(xw_parts, axis=1)
        dec = jnp.concatenate(dec_parts, axis=1)
        upd = lax.dot_general(bb, xw, (((0,), (0,)), ((), ())), preferred_element_type=F32)
        st_ref[g] = h_t * dec + upd
        return carry

    lax.fori_loop(0, N_GROUPS, group, 0)


def _ssd(xbc_act, dt_ctx, dt_lat, par, dexp, reverse):
    bsz, ntile, ltot, _ = xbc_act.shape
    l_ctx = dt_ctx.shape[1]
    l_lat = dt_lat.shape[1]
    n_ctx = l_ctx // CHUNK
    n_lat = l_lat // CHUNK
    steps = n_ctx + n_lat
    x_tiles = N_HEADS // 2

    if reverse:
        def cat_chunk(i):
            return jnp.where(i < n_ctx, n_ctx - 1 - i, n_ctx + steps - 1 - i)

        def ctx_chunk(i):
            return jnp.maximum(n_ctx - 1 - i, 0)

        def lat_chunk(i):
            return jnp.minimum(steps - 1 - i, n_lat - 1)
    else:
        def cat_chunk(i):
            return i

        def ctx_chunk(i):
            return jnp.minimum(i, n_ctx - 1)

        def lat_chunk(i):
            return jnp.maximum(i - n_ctx, 0)

    return pl.pallas_call(
        functools.partial(_ssd_kernel, reverse=reverse, n_ctx=n_ctx),
        out_shape=jax.ShapeDtypeStruct((bsz, x_tiles, l_lat, LANE), F32),
        grid=(bsz, steps),
        in_specs=[pl.BlockSpec((1, ntile, CHUNK, LANE), lambda b, i: (b, 0, cat_chunk(i), 0)),
                  pl.BlockSpec((1, CHUNK, LANE), lambda b, i: (b, ctx_chunk(i), 0)),
                  pl.BlockSpec((1, CHUNK, LANE), lambda b, i: (b, lat_chunk(i), 0)),
                  pl.BlockSpec((8, LANE), lambda b, i: (0, 0)),
                  pl.BlockSpec((x_tiles, 1, LANE), lambda b, i: (0, 0, 0))],
        out_specs=pl.BlockSpec((1, x_tiles, CHUNK, LANE), lambda b, i: (b, 0, lat_chunk(i), 0)),
        scratch_shapes=[pltpu.VMEM((N_GROUPS, D_STATE, (N_HEADS // N_GROUPS) * HEAD_DIM), F32),
                        pltpu.VMEM((LANE, CHUNK), F32)],
        compiler_params=_cparams(("parallel", "arbitrary")),
        name="ssd_bwd" if reverse else "ssd_fwd",
    )(xbc_act, dt_ctx, dt_lat, par, dexp)


def _gatenorm_kernel(yf_ref, yb_ref, sz_ref, w_ref, o_ref, g_ref):
    nt = yf_ref.shape[1]
    tm = yf_ref.shape[2]
    ss = jnp.zeros((tm, 1), F32)
    for j in range(nt):
        sl = slice(j * LANE, (j + 1) * LANE)
        g = (yf_ref[0, j] + yb_ref[0, j]) * sz_ref[:, sl]
        g_ref[:, sl] = g
        ss = ss + jnp.sum(g * g, axis=-1, keepdims=True)
    r = lax.rsqrt(ss / (nt * LANE) + EPS)
    o_ref[...] = (g_ref[...] * r * w_ref[...]).astype(o_ref.dtype)


def _gatenorm(y_f, y_b, sz, w, tm=256):
    bsz, nt, s, _ = y_f.shape
    dn = nt * LANE
    spb = s // tm
    return pl.pallas_call(
        _gatenorm_kernel,
        out_shape=jax.ShapeDtypeStruct((bsz * s, dn), BF16),
        grid=(bsz, spb),
        in_specs=[pl.BlockSpec((1, nt, tm, LANE), lambda b, i: (b, 0, i, 0)),
                  pl.BlockSpec((1, nt, tm, LANE), lambda b, i: (b, 0, i, 0)),
                  pl.BlockSpec((tm, dn), lambda b, i: (b * spb + i, 0)),
                  pl.BlockSpec((1, dn), lambda b, i: (0, 0))],
        out_specs=pl.BlockSpec((tm, dn), lambda b, i: (b * spb + i, 0)),
        scratch_shapes=[pltpu.VMEM((tm, dn), F32)],
        compiler_params=_cparams(("parallel", "parallel")),
        name="gatenorm",
    )(y_f, y_b, sz, w.reshape(1, dn))


def _conv31_kernel(u_ref, w_ref, b_ref, o_ref, pad_ref, *, seq):
    halo = (CF_KERNEL // 2) * GRID_W
    zeros = jnp.zeros((halo, LANE), F32)
    pad_ref[0:halo, :] = zeros
    pad_ref[halo + seq:halo + seq + halo, :] = zeros
    pad_ref[halo:halo + seq, :] = u_ref[0]
    bias = b_ref[...]

    def body(j, c):
        base = pl.multiple_of(j * CHUNK, CHUNK)
        acc = jnp.broadcast_to(bias, (CHUNK, LANE))
        for k in range(CF_KERNEL):
            tap = pad_ref[pl.ds(pl.multiple_of(base + k * GRID_W, GRID_W), CHUNK), :]
            acc = acc + tap * w_ref[k:k + 1, :]
        o_ref[0, pl.ds(base, CHUNK), :] = acc
        return c

    lax.fori_loop(0, seq // CHUNK, body, 0)


def _conv31(u3, w, b):
    bsz, s, c = u3.shape
    halo = (CF_KERNEL // 2) * GRID_W
    return pl.pallas_call(
        functools.partial(_conv31_kernel, seq=s),
        out_shape=jax.ShapeDtypeStruct((bsz, s, c), F32),
        grid=(bsz, c // LANE),
        in_specs=[pl.BlockSpec((1, s, LANE), lambda bi, ci: (bi, 0, ci)),
                  pl.BlockSpec((CF_KERNEL, LANE), lambda bi, ci: (0, ci)),
                  pl.BlockSpec((1, LANE), lambda bi, ci: (0, ci))],
        out_specs=pl.BlockSpec((1, s, LANE), lambda bi, ci: (bi, 0, ci)),
        scratch_shapes=[pltpu.VMEM((s + 2 * halo, LANE), F32)],
        compiler_params=_cparams(("parallel", "parallel")),
        name="conv31",
    )(u3, w, b.reshape(1, c))


def _lnsilu_kernel(x_ref, w_ref, b_ref, o_ref):
    xf = x_ref[...]
    mu = jnp.mean(xf, axis=-1, keepdims=True)
    xc = xf - mu
    var = jnp.mean(xc * xc, axis=-1, keepdims=True)
    y = xc * lax.rsqrt(var + EPS) * w_ref[...] + b_ref[...]
    o_ref[...] = _silu(y).astype(o_ref.dtype)


def _lnsilu(x2, w, b, tm=512):
    m, d = x2.shape
    tm = min(tm, m)
    return pl.pallas_call(
        _lnsilu_kernel,
        out_shape=jax.ShapeDtypeStruct((m, d), BF16),
        grid=(m // tm,),
        in_specs=[pl.BlockSpec((tm, d), lambda i: (i, 0)),
                  pl.BlockSpec((1, d), lambda i: (0, 0)),
                  pl.BlockSpec((1, d), lambda i: (0, 0))],
        out_specs=pl.BlockSpec((tm, d), lambda i: (i, 0)),
        compiler_params=_cparams(("parallel",)),
        name="lnsilu",
    )(x2, w.reshape(1, d), b.reshape(1, d))


def _route_kernel(x_ref, w_ref, sh_ref, sc_ref, rw_ref, rb_ref, h_ref, eid_ref, ew_ref):
    xf = x_ref[...]
    ms = jnp.mean(xf * xf, axis=-1, keepdims=True)
    h = xf * lax.rsqrt(ms + EPS) * w_ref[...]
    h = h * (1.0 + sc_ref[0]) + sh_ref[0]
    tm = xf.shape[0]
    nt = xf.shape[1] // LANE
    pitch = _pitch(nt)
    for j in range(nt):
        h_ref[pl.ds(j, tm, stride=pitch), :] = h[:, j * LANE:(j + 1) * LANE]
    for j in range(nt, pitch):
        h_ref[pl.ds(j, tm, stride=pitch), :] = jnp.zeros((tm, LANE), F32)
    logits = jnp.dot(h.astype(BF16), rw_ref[...], preferred_element_type=F32) + rb_ref[...]
    lane = lax.broadcasted_iota(jnp.int32, (tm, LANE), 1)
    lane_f = lane.astype(F32)
    ninf = -jnp.inf
    gl = jnp.where(lane < MOE_GROUPS, logits, ninf)
    gmax = jnp.max(gl, axis=-1, keepdims=True)
    gidx = jnp.min(jnp.where(gl == gmax, lane_f, float(LANE)), axis=-1, keepdims=True)
    gsum = jnp.sum(jnp.exp(gl - gmax), axis=-1, keepdims=True)
    g_p = 1.0 / gsum
    first = float(MOE_GROUPS) + gidx * float(EXPERTS_PER_GROUP)
    in_group = (lane_f >= first) & (lane_f < first + float(EXPERTS_PER_GROUP))
    el = jnp.where(in_group, logits, ninf)
    m1 = jnp.max(el, axis=-1, keepdims=True)
    i1 = jnp.min(jnp.where(el == m1, lane_f, float(LANE)), axis=-1, keepdims=True)
    el2 = jnp.where(lane_f == i1, ninf, el)
    m2 = jnp.max(el2, axis=-1, keepdims=True)
    i2 = jnp.min(jnp.where(el2 == m2, lane_f, float(LANE)), axis=-1, keepdims=True)
    e21 = jnp.exp(m2 - m1)
    den = 1.0 + e21
    w1 = (1.0 / den) * g_p
    w2 = (e21 / den) * g_p
    e1 = (i1 - float(MOE_GROUPS)).astype(jnp.int32)
    e2 = (i2 - float(MOE_GROUPS)).astype(jnp.int32)
    eid_ref[...] = jnp.where(lane == 0, e1, jnp.where(lane == 1, e2, 0))
    ew_ref[...] = jnp.where(lane == 0, w1, jnp.where(lane == 1, w2, 0.0))


def _route(x2, w, mod3, shift_chunk, scale_chunk, rows_per_batch, rw, rb, tm=256):
    m, d = x2.shape
    pitch = _pitch(d // LANE)
    tiles_per_batch = rows_per_batch // tm
    return pl.pallas_call(
        _route_kernel,
        out_shape=(jax.ShapeDtypeStruct((m * pitch, LANE), F32),
                   jax.ShapeDtypeStruct((m, LANE), jnp.int32),
                   jax.ShapeDtypeStruct((m, LANE), F32)),
        grid=(m // tm,),
        in_specs=[pl.BlockSpec((tm, d), lambda i: (i, 0)),
                  pl.BlockSpec((1, d), lambda i: (0, 0)),
                  pl.BlockSpec((1, 1, d), lambda i: (i // tiles_per_batch, 0, shift_chunk)),
                  pl.BlockSpec((1, 1, d), lambda i: (i // tiles_per_batch, 0, scale_chunk)),
                  pl.BlockSpec((d, LANE), lambda i: (0, 0)),
                  pl.BlockSpec((1, LANE), lambda i: (0, 0))],
        out_specs=(pl.BlockSpec((tm * pitch, LANE), lambda i: (i, 0)),
                   pl.BlockSpec((tm, LANE), lambda i: (i, 0)),
                   pl.BlockSpec((tm, LANE), lambda i: (i, 0))),
        compiler_params=_cparams(("parallel",)),
        name="route",
    )(x2, w.reshape(1, d), mod3, mod3, rw, rb)


_DMA_UNROLL = 8


def _rows_to_matrix(ref, tm, nt):
    return jnp.concatenate([ref[pl.ds(j, tm, stride=_pitch(nt)), :] for j in range(nt)], axis=1)


def _token_copy(src, dst, sem, nt, rows, src_tok, dst_tok):
    pitch = _pitch(nt)
    return pltpu.make_async_copy(src.at[pl.ds(src_tok * pitch, rows), :],
                                 dst.at[pl.ds(dst_tok * pitch, rows), :], sem)


def _bulk_wait(src, dst, sem, total_rows):
    pltpu.make_async_copy(src.at[pl.ds(0, total_rows), :], dst.at[pl.ds(0, total_rows), :], sem).wait()


def _for_rows(n, body):
    groups = lax.shift_right_logical(n, _DMA_UNROLL.bit_length() - 1)

    def group(g, c):
        for u in range(_DMA_UNROLL):
            body(g * _DMA_UNROLL + u)
        return c

    def tail(r, c):
        body(r)
        return c

    lax.fori_loop(0, groups, group, 0)
    lax.fori_loop(groups * _DMA_UNROLL, n, tail, 0)


def _wait_rows(n, wait_tokens):
    p = MOE_BLOCK
    while p >= 1:
        @pl.when((n & p) != 0)
        def _(p=p):
            wait_tokens(p)
        p //= 2


def _expert_up_kernel(be_ref, nv_ref, nused_ref, tokc_ref, tokn_ref, h_hbm, wg_ref, wu_ref, o_ref,
                      xs0_ref, xs1_ref, wgb_ref, wub_ref, sem, *, fchunk, nt):
    b = pl.program_id(0)
    n_used = nused_ref[0]
    dff = wg_ref.shape[2]
    slots = (xs0_ref, xs1_ref)

    def start_gather(tok_ref, blk, slot):
        _for_rows(nv_ref[blk], lambda r: _token_copy(
            h_hbm, slots[slot], sem.at[slot], nt, nt, tok_ref[0, 0, r], r).start())

    def wait_gather(blk, slot):
        _wait_rows(nv_ref[blk], lambda p: _bulk_wait(h_hbm, slots[slot], sem.at[slot], p * nt))

    @pl.when(b == 0)
    def _():
        xs0_ref[...] = jnp.zeros_like(xs0_ref)
        xs1_ref[...] = jnp.zeros_like(xs1_ref)
        start_gather(tokc_ref, 0, 0)

    for slot in range(2):
        @pl.when((b + 1 < n_used) & (lax.rem(b, 2) == slot))
        def _(slot=slot):
            start_gather(tokn_ref, b + 1, 1 - slot)

    @pl.when(b < n_used)
    def _():
        prev = jnp.maximum(b - 1, 0)

        @pl.when((b == 0) | (be_ref[b] != be_ref[prev]))
        def _():
            wgb_ref[...] = wg_ref[0].astype(BF16)
            wub_ref[...] = wu_ref[0].astype(BF16)

        for slot in range(2):
            @pl.when(lax.rem(b, 2) == slot)
            def _(slot=slot):
                wait_gather(b, slot)
                xb = _rows_to_matrix(slots[slot], MOE_BLOCK, nt).astype(BF16)
                for f in range(dff // fchunk):
                    sl = slice(f * fchunk, (f + 1) * fchunk)
                    gate = jnp.dot(xb, wgb_ref[:, sl], preferred_element_type=F32)
                    up = jnp.dot(xb, wub_ref[:, sl], preferred_element_type=F32)
                    o_ref[:, sl] = (_silu(gate) * up).astype(o_ref.dtype)

    @pl.when(b >= n_used)
    def _():
        o_ref[...] = jnp.zeros_like(o_ref)


def _expert_up(h2t, buf_tok, w_gate, w_up, block_expert, n_valid, n_used, fchunk=256):
    n_blocks = buf_tok.shape[0]
    _, d, dff = w_gate.shape
    nt = d // LANE
    slot_rows = MOE_BLOCK * _pitch(nt)

    def blk(b, n):
        return jnp.minimum(b, n[0] - 1)

    grid_spec = pltpu.PrefetchScalarGridSpec(
        num_scalar_prefetch=3,
        grid=(n_blocks,),
        in_specs=[pl.BlockSpec((1, 1, MOE_BLOCK), lambda b, be, nv, n: (b, 0, 0), memory_space=pltpu.SMEM),
                  pl.BlockSpec((1, 1, MOE_BLOCK), lambda b, be, nv, n: (jnp.minimum(b + 1, n_blocks - 1), 0, 0),
                               memory_space=pltpu.SMEM),
                  pl.BlockSpec(memory_space=pl.ANY),
                  pl.BlockSpec((1, d, dff), lambda b, be, nv, n: (be[blk(b, n)], 0, 0)),
                  pl.BlockSpec((1, d, dff), lambda b, be, nv, n: (be[blk(b, n)], 0, 0))],
        out_specs=pl.BlockSpec((MOE_BLOCK, dff), lambda b, be, nv, n: (b, 0)),
        scratch_shapes=[pltpu.VMEM((slot_rows, LANE), F32),
                        pltpu.VMEM((slot_rows, LANE), F32),
                        pltpu.VMEM((d, dff), BF16),
                        pltpu.VMEM((d, dff), BF16),
                        pltpu.SemaphoreType.DMA((2,))],
    )
    return pl.pallas_call(
        functools.partial(_expert_up_kernel, fchunk=fchunk, nt=nt),
        out_shape=jax.ShapeDtypeStruct((n_blocks * MOE_BLOCK, dff), BF16),
        grid_spec=grid_spec,
        compiler_params=_cparams(("arbitrary",)),
        name="expert_up",
    )(block_expert, n_valid, n_used, buf_tok, buf_tok, h2t, w_gate, w_up)


def _expert_down_kernel(be_ref, nv_ref, nused_ref, asg_ref, h_ref, wd_ref, y_hbm,
                        ys0_ref, ys1_ref, wdb_ref, sem, *, nchunk, nt, n_tok):
    b = pl.program_id(0)
    n_used = nused_ref[0]
    d = wd_ref.shape[2]
    slots = (ys0_ref, ys1_ref)
    pitch = _pitch(nt)

    def row_copy(slot, r, assign):
        k = assign & 1
        tok = lax.shift_right_logical(assign, 1)
        return _token_copy(slots[slot], y_hbm, sem.at[slot], nt, pitch, r, k * n_tok + tok)

    def start_scatter(blk, slot):
        _for_rows(nv_ref[blk], lambda r: row_copy(slot, r, asg_ref[0, 0, r]).start())

    def wait_scatter(blk, slot):
        _wait_rows(nv_ref[blk], lambda p: _bulk_wait(slots[slot], y_hbm, sem.at[slot], p * pitch))

    @pl.when(b == 0)
    def _():
        ys0_ref[...] = jnp.zeros_like(ys0_ref)
        ys1_ref[...] = jnp.zeros_like(ys1_ref)

    @pl.when(b < n_used)
    def _():
        prev = jnp.maximum(b - 1, 0)

        @pl.when((b == 0) | (be_ref[b] != be_ref[prev]))
        def _():
            wdb_ref[...] = wd_ref[0].astype(BF16)

        hb = h_ref[...]
        for slot in range(2):
            @pl.when(lax.rem(b, 2) == slot)
            def _(slot=slot):
                for c in range(d // nchunk):
                    out = jnp.dot(hb, wdb_ref[:, c * nchunk:(c + 1) * nchunk], preferred_element_type=F32)
                    for j in range(nchunk // LANE):
                        slots[slot][pl.ds(c * (nchunk // LANE) + j, MOE_BLOCK, stride=pitch), :] = (
                            out[:, j * LANE:(j + 1) * LANE])

    for slot in range(2):
        @pl.when((b >= 1) & (b - 1 < n_used) & (lax.rem(b, 2) == slot))
        def _(slot=slot):
            wait_scatter(b - 1, 1 - slot)

        @pl.when((b < n_used) & (lax.rem(b, 2) == slot))
        def _(slot=slot):
            start_scatter(b, slot)

        @pl.when((b == pl.num_programs(0) - 1) & (b < n_used) & (lax.rem(b, 2) == slot))
        def _(slot=slot):
            wait_scatter(b, slot)


def _expert_down(hid, buf_assign, w_down, block_expert, n_valid, n_used, n_tok, nchunk=512):
    n_rows, dff = hid.shape
    n_blocks = n_rows // MOE_BLOCK
    d = w_down.shape[2]
    nt = d // LANE
    pitch = _pitch(nt)

    def blk(b, n):
        return jnp.minimum(b, n[0] - 1)

    grid_spec = pltpu.PrefetchScalarGridSpec(
        num_scalar_prefetch=3,
        grid=(n_blocks,),
        in_specs=[pl.BlockSpec((1, 1, MOE_BLOCK), lambda b, be, nv, n: (b, 0, 0), memory_space=pltpu.SMEM),
                  pl.BlockSpec((MOE_BLOCK, dff), lambda b, be, nv, n: (blk(b, n), 0)),
                  pl.BlockSpec((1, dff, d), lambda b, be, nv, n: (be[blk(b, n)], 0, 0))],
        out_specs=pl.BlockSpec(memory_space=pl.ANY),
        scratch_shapes=[pltpu.VMEM((MOE_BLOCK * pitch, LANE), F32),
                        pltpu.VMEM((MOE_BLOCK * pitch, LANE), F32),
                        pltpu.VMEM((dff, d), BF16),
                        pltpu.SemaphoreType.DMA((2,))],
    )
    return pl.pallas_call(
        functools.partial(_expert_down_kernel, nchunk=nchunk, nt=nt, n_tok=n_tok),
        out_shape=jax.ShapeDtypeStruct((2 * n_tok * pitch, LANE), F32),
        grid_spec=grid_spec,
        compiler_params=_cparams(("arbitrary",)),
        name="expert_down",
    )(block_expert, n_valid, n_used, buf_assign, hid, w_down)


def _combine_kernel(y0_ref, y1_ref, ew_ref, x_ref, g_ref, w_ref, o_ref):
    tm, d = x_ref.shape
    nt = d // LANE
    ew = ew_ref[...]
    moe = (_rows_to_matrix(y0_ref, tm, nt) * ew[:, 0:1]
           + _rows_to_matrix(y1_ref, tm, nt) * ew[:, 1:2])
    xo = x_ref[...] + g_ref[0] * moe
    ms = jnp.mean(xo * xo, axis=-1, keepdims=True)
    o_ref[...] = xo * lax.rsqrt(ms + EPS) * w_ref[...]


def _combine(y, ew, x2, mod3, gate_chunk, rows_per_batch, final_w, tm=256):
    m, d = x2.shape
    pitch = _pitch(d // LANE)
    tiles = m // tm
    tiles_per_batch = rows_per_batch // tm
    return pl.pallas_call(
        _combine_kernel,
        out_shape=jax.ShapeDtypeStruct((m, d), F32),
        grid=(tiles,),
        in_specs=[pl.BlockSpec((tm * pitch, LANE), lambda i: (i, 0)),
                  pl.BlockSpec((tm * pitch, LANE), lambda i: (tiles + i, 0)),
                  pl.BlockSpec((tm, LANE), lambda i: (i, 0)),
                  pl.BlockSpec((tm, d), lambda i: (i, 0)),
                  pl.BlockSpec((1, 1, d), lambda i: (i // tiles_per_batch, 0, gate_chunk)),
                  pl.BlockSpec((1, d), lambda i: (0, 0))],
        out_specs=pl.BlockSpec((tm, d), lambda i: (i, 0)),
        compiler_params=_cparams(("parallel",)),
        name="moe_combine",
    )(y, y, ew, x2, mod3, final_w.reshape(1, d))


def _dispatch_tables(eid, n_tok):
    top_k = eid.shape[1]
    n_assign = n_tok * top_k
    expert = eid.reshape(-1)
    key = jnp.sort(expert * n_assign + jnp.arange(n_assign, dtype=jnp.int32))
    sorted_assign = key % n_assign
    bounds = jnp.arange(N_EXPERTS + 1, dtype=jnp.int32) * n_assign
    start = jnp.searchsorted(key, bounds, side="left").astype(jnp.int32)
    counts = start[1:] - start[:-1]
    nblk = (counts + MOE_BLOCK - 1) // MOE_BLOCK
    blk_end = jnp.cumsum(nblk)
    blk_start = blk_end - nblk
    n_blocks = -(-n_assign // MOE_BLOCK) + N_EXPERTS
    bidx = jnp.arange(n_blocks, dtype=jnp.int32)
    block_expert = jnp.minimum(jnp.searchsorted(blk_end, bidx, side="right"), N_EXPERTS - 1).astype(jnp.int32)
    in_expert = (bidx - blk_start[block_expert]) * MOE_BLOCK
    n_valid = jnp.clip(counts[block_expert] - in_expert, 0, MOE_BLOCK).astype(jnp.int32)
    src = start[block_expert][:, None] + in_expert[:, None] + jnp.arange(MOE_BLOCK, dtype=jnp.int32)[None, :]
    valid = jnp.arange(MOE_BLOCK, dtype=jnp.int32)[None, :] < n_valid[:, None]
    buf_assign = jnp.where(valid, sorted_assign[jnp.clip(src, 0, n_assign - 1)], 0).astype(jnp.int32)
    n_used = blk_end[-1].astype(jnp.int32).reshape(1)
    return buf_assign.reshape(n_blocks, 1, MOE_BLOCK), block_expert, n_valid, n_used


def kernel(x, c, ctx, c_ctx, ada_w, ada_b, norm1_w, w_in, ssm_conv_w, ssm_conv_b, dt_bias, a_log, d_skip, ssm_norm_w, ssm_out_w, cf_dw_w, cf_dw_b, cf_ln_w, cf_ln_b, cf_out_w, cf_out_b, w_o, norm2_w, router_group_w, router_group_b, router_expert_w, router_expert_b, expert_w_gate, expert_w_up, expert_w_down, final_norm_w):
    bsz, seq, d = x.shape
    l_ctx = ctx.shape[1]
    n_tok = bsz * seq
    d_inner = ssm_norm_w.shape[1]
    gn = N_GROUPS * D_STATE
    xbc_dim = d_inner + 2 * gn
    off_dt = xbc_dim
    off_z = off_dt + N_HEADS
    off_glu = off_z + d_inner
    off_gate = off_glu + 2 * d

    ctx_row = bsz
    crows = jnp.zeros((8, d), F32).at[:bsz].set(c).at[ctx_row].set(c_ctx)
    mod = _ada(crows, ada_w[0], ada_b[0])
    mod3 = mod.reshape(8, 1, 6 * d)
    lat_rows = jnp.arange(bsz, dtype=jnp.int32)
    ctx_rows = jnp.full((bsz,), ctx_row, jnp.int32)

    h_lat = _normmod(x, norm1_w[0], mod3, lat_rows, 0, 1, BF16).reshape(n_tok, d)
    h_ctx = _normmod(ctx, norm1_w[0], mod3, ctx_rows, 0, 1, BF16).reshape(bsz * l_ctx, d)

    w = w_in[0]
    w_xbc = w[:, :xbc_dim].astype(BF16)
    w_dt = jnp.pad(w[:, off_dt:off_z], ((0, 0), (0, LANE - N_HEADS))).astype(BF16)
    w_z = w[:, off_z:off_glu].astype(BF16)
    w_glu_a = w[:, off_glu:off_glu + d].astype(BF16)
    w_glu_b = w[:, off_glu + d:off_gate].astype(BF16)
    w_gate = w[:, off_gate:].astype(BF16)

    xbc_lat = _mm(h_lat, w_xbc, name="in_xbc").reshape(bsz, seq, xbc_dim)
    xbc_ctx = _mm(h_ctx, w_xbc, tm=512, name="in_xbc_ctx").reshape(bsz, l_ctx, xbc_dim)
    dt_lat = _mm(h_lat, w_dt, name="in_dt").reshape(bsz, seq, LANE)
    dt_ctx = _mm(h_ctx, w_dt, tm=512, name="in_dt_ctx").reshape(bsz, l_ctx, LANE)
    sz = _mm(h_lat, w_z, act="silu", name="in_z")
    u = _mm_glu(h_lat, w_glu_a, w_glu_b)
    gates = _mm(h_lat, w_gate, act="sigmoid", name="in_gate")

    xbc_act = _conv7(xbc_ctx, xbc_lat, ssm_conv_w[0], ssm_conv_b[0])

    ys = []
    for k in range(2):
        par = jnp.zeros((8, LANE), F32).at[0, :N_HEADS].set(dt_bias[0, k]).at[1, :N_HEADS].set(a_log[0, k])
        dexp = jnp.repeat(d_skip[0, k], HEAD_DIM).reshape(N_HEADS // 2, 1, LANE)
        ys.append(_ssd(xbc_act, dt_ctx, dt_lat, par, dexp, reverse=(k == 1)))

    gnorm = _gatenorm(ys[0], ys[1], sz, ssm_norm_w[0])
    y_ssd = _mm(gnorm, ssm_out_w[0].astype(BF16), tn=512, name="ssm_out")

    cv = _conv31(u.reshape(bsz, seq, d), cf_dw_w[0], cf_dw_b[0]).reshape(n_tok, d)
    ua = _lnsilu(cv, cf_ln_w[0], cf_ln_b[0])
    merged = _mm_merge(ua, cf_out_w[0].astype(BF16), cf_out_b[0], gates, y_ssd)
    x1 = _mm_resid(merged, w_o[0].astype(BF16), x.reshape(n_tok, d), mod3, 2, seq)

    n_r = MOE_GROUPS + N_EXPERTS
    rw = jnp.pad(jnp.concatenate([router_group_w[0], router_expert_w[0]], axis=1),
                 ((0, 0), (0, LANE - n_r))).astype(BF16)
    rb = jnp.pad(jnp.concatenate([router_group_b[0], router_expert_b[0]]), (0, LANE - n_r)).reshape(1, LANE)
    h2t, eid, ew = _route(x1, norm2_w[0], mod3, 3, 4, seq, rw, rb)

    buf_assign, block_expert, n_valid, n_used = _dispatch_tables(eid[:, :2], n_tok)
    hid = _expert_up(h2t, buf_assign // 2, expert_w_gate[0], expert_w_up[0], block_expert, n_valid, n_used)
    y = _expert_down(hid, buf_assign, expert_w_down[0], block_expert, n_valid, n_used, n_tok)
    out = _combine(y, ew, x1, mod3, 5, seq, final_norm_w)
    return out.reshape(bsz, seq, d)
```

```python
import functools

import jax
import jax.numpy as jnp
from jax import lax
from jax.experimental import pallas as pl
from jax.experimental.pallas import tpu as pltpu

F32 = jnp.float32
BF16 = jnp.bfloat16

EPS = 1e-6
GRID_W = 64
HEAD_DIM = 64
N_HEADS = 64
N_GROUPS = 8
D_STATE = 128
CHUNK = 128
SSM_CONV = 7
CF_KERNEL = 31
MOE_GROUPS = 8
EXPERTS_PER_GROUP = 8
N_EXPERTS = 64
MOE_BLOCK = 256
LANE = 128
VMEM_LIMIT = 56 * 1024 * 1024


def _cparams(sem):
    return pltpu.CompilerParams(dimension_semantics=sem, vmem_limit_bytes=VMEM_LIMIT)


def _silu(v):
    return v * jax.nn.sigmoid(v)


def _pitch(nt):
    return nt + 1


def _ada_kernel(c_ref, w_ref, b_ref, o_ref):
    s = _silu(c_ref[...])
    o_ref[...] = jnp.dot(s.astype(BF16), w_ref[...].astype(BF16),
                         preferred_element_type=F32) + b_ref[...]


def _ada(crows, ada_w, ada_b, tn=1024):
    r, d = crows.shape
    n = ada_w.shape[1]
    return pl.pallas_call(
        _ada_kernel,
        out_shape=jax.ShapeDtypeStruct((r, n), F32),
        grid=(n // tn,),
        in_specs=[pl.BlockSpec((r, d), lambda j: (0, 0)),
                  pl.BlockSpec((d, tn), lambda j: (0, j)),
                  pl.BlockSpec((1, tn), lambda j: (0, j))],
        out_specs=pl.BlockSpec((r, tn), lambda j: (0, j)),
        compiler_params=_cparams(("parallel",)),
        name="ada",
    )(crows, ada_w, ada_b.reshape(1, n))


def _normmod_kernel(rows_ref, x_ref, w_ref, sh_ref, sc_ref, o_ref):
    del rows_ref
    xf = x_ref[0]
    ms = jnp.mean(xf * xf, axis=-1, keepdims=True)
    y = xf * lax.rsqrt(ms + EPS) * w_ref[...]
    o_ref[0] = (y * (1.0 + sc_ref[0]) + sh_ref[0]).astype(o_ref.dtype)


def _normmod(x3, w, mod3, rows, shift_chunk, scale_chunk, out_dtype, tm=256):
    bx, l, d = x3.shape
    grid_spec = pltpu.PrefetchScalarGridSpec(
        num_scalar_prefetch=1,
        grid=(bx, l // tm),
        in_specs=[pl.BlockSpec((1, tm, d), lambda b, i, r: (b, i, 0)),
                  pl.BlockSpec((1, d), lambda b, i, r: (0, 0)),
                  pl.BlockSpec((1, 1, d), lambda b, i, r: (r[b], 0, shift_chunk)),
                  pl.BlockSpec((1, 1, d), lambda b, i, r: (r[b], 0, scale_chunk))],
        out_specs=pl.BlockSpec((1, tm, d), lambda b, i, r: (b, i, 0)),
    )
    return pl.pallas_call(
        _normmod_kernel,
        out_shape=jax.ShapeDtypeStruct((bx, l, d), out_dtype),
        grid_spec=grid_spec,
        compiler_params=_cparams(("parallel", "parallel")),
        name="normmod",
    )(rows, x3, w.reshape(1, d), mod3, mod3)


def _mm_kernel(a_ref, w_ref, *rest, act, has_bias):
    o_ref = rest[-1]
    acc = jnp.dot(a_ref[...], w_ref[...], preferred_element_type=F32)
    if has_bias:
        acc = acc + rest[0][...]
    if act == "silu":
        acc = _silu(acc)
    elif act == "sigmoid":
        acc = jax.nn.sigmoid(acc)
    o_ref[...] = acc.astype(o_ref.dtype)


def _mm(a, w, bias=None, act=None, out_dtype=F32, tm=1024, tn=1024, name="mm"):
    m, k = a.shape
    n = w.shape[1]
    tm, tn = min(tm, m), min(tn, n)
    in_specs = [pl.BlockSpec((tm, k), lambda i, j: (i, 0)),
                pl.BlockSpec((k, tn), lambda i, j: (0, j))]
    args = [a, w]
    if bias is not None:
        in_specs.append(pl.BlockSpec((1, tn), lambda i, j: (0, j)))
        args.append(bias.reshape(1, n))
    return pl.pallas_call(
        functools.partial(_mm_kernel, act=act, has_bias=bias is not None),
        out_shape=jax.ShapeDtypeStruct((m, n), out_dtype),
        grid=(m // tm, n // tn),
        in_specs=in_specs,
        out_specs=pl.BlockSpec((tm, tn), lambda i, j: (i, j)),
        compiler_params=_cparams(("parallel", "parallel")),
        name=name,
    )(*args)


def _mm_glu_kernel(a_ref, wa_ref, wb_ref, o_ref):
    a = a_ref[...]
    va = jnp.dot(a, wa_ref[...], preferred_element_type=F32)
    vb = jnp.dot(a, wb_ref[...], preferred_element_type=F32)
    o_ref[...] = va * jax.nn.sigmoid(vb)


def _mm_glu(a, wa, wb, tm=1024, tn=512):
    m, k = a.shape
    n = wa.shape[1]
    tm = min(tm, m)
    return pl.pallas_call(
        _mm_glu_kernel,
        out_shape=jax.ShapeDtypeStruct((m, n), F32),
        grid=(m // tm, n // tn),
        in_specs=[pl.BlockSpec((tm, k), lambda i, j: (i, 0)),
                  pl.BlockSpec((k, tn), lambda i, j: (0, j)),
                  pl.BlockSpec((k, tn), lambda i, j: (0, j))],
        out_specs=pl.BlockSpec((tm, tn), lambda i, j: (i, j)),
        compiler_params=_cparams(("parallel", "parallel")),
        name="mm_glu",
    )(a, wa, wb)


def _mm_merge_kernel(a_ref, w_ref, b_ref, ga_ref, gb_ref, ys_ref, o_ref):
    ycf = jnp.dot(a_ref[...], w_ref[...], preferred_element_type=F32) + b_ref[...]
    o_ref[...] = (ga_ref[...] * ys_ref[...] + gb_ref[...] * ycf).astype(o_ref.dtype)


def _mm_merge(a, w, bias, gates, y_ssd, tm=1024, tn=512):
    m, k = a.shape
    n = w.shape[1]
    tm = min(tm, m)
    nj = n // tn
    return pl.pallas_call(
        _mm_merge_kernel,
        out_shape=jax.ShapeDtypeStruct((m, n), BF16),
        grid=(m // tm, nj),
        in_specs=[pl.BlockSpec((tm, k), lambda i, j: (i, 0)),
                  pl.BlockSpec((k, tn), lambda i, j: (0, j)),
                  pl.BlockSpec((1, tn), lambda i, j: (0, j)),
                  pl.BlockSpec((tm, tn), lambda i, j: (i, j)),
                  pl.BlockSpec((tm, tn), lambda i, j: (i, j + nj)),
                  pl.BlockSpec((tm, tn), lambda i, j: (i, j))],
        out_specs=pl.BlockSpec((tm, tn), lambda i, j: (i, j)),
        compiler_params=_cparams(("parallel", "parallel")),
        name="mm_merge",
    )(a, w, bias.reshape(1, n), gates, gates, y_ssd)


def _mm_resid_kernel(a_ref, w_ref, x_ref, g_ref, o_ref):
    out = jnp.dot(a_ref[...], w_ref[...], preferred_element_type=F32)
    o_ref[...] = x_ref[...] + g_ref[0] * out


def _mm_resid(a, w, x2, mod3, gate_chunk, rows_per_batch, tm=1024, tn=512):
    m, k = a.shape
    n = w.shape[1]
    tm = min(tm, rows_per_batch)
    nj = n // tn
    tiles_per_batch = rows_per_batch // tm
    return pl.pallas_call(
        _mm_resid_kernel,
        out_shape=jax.ShapeDtypeStruct((m, n), F32),
        grid=(m // tm, nj),
        in_specs=[pl.BlockSpec((tm, k), lambda i, j: (i, 0)),
                  pl.BlockSpec((k, tn), lambda i, j: (0, j)),
                  pl.BlockSpec((tm, tn), lambda i, j: (i, j)),
                  pl.BlockSpec((1, 1, tn),
                               lambda i, j: (i // tiles_per_batch, 0, gate_chunk * nj + j))],
        out_specs=pl.BlockSpec((tm, tn), lambda i, j: (i, j)),
        compiler_params=_cparams(("parallel", "parallel")),
        name="mm_resid",
    )(a, w, x2, mod3)


_CONV_PAD = 8


def _conv7_kernel(ctx_ref, lat_ref, w_ref, b_ref, o_ref, pad_ref, *, l_ctx, l_lat):
    p = _CONV_PAD
    zeros = jnp.zeros((p, LANE), F32)
    off_ctx = p
    off_lat = 2 * p + l_ctx
    pad_ref[0:p, :] = zeros
    pad_ref[off_ctx + l_ctx:off_lat, :] = zeros
    pad_ref[off_lat + l_lat:off_lat + l_lat + p, :] = zeros
    pad_ref[off_ctx:off_ctx + l_ctx, :] = ctx_ref[0]
    pad_ref[off_lat:off_lat + l_lat, :] = lat_ref[0]
    reach = SSM_CONV // 2
    bias = b_ref[...]

    def chunk(pad_base, out_base):
        acc = jnp.broadcast_to(bias, (CHUNK, LANE))
        for k in range(SSM_CONV):
            tap = pad_ref[pl.ds(pad_base - reach + k, CHUNK), :]
            acc = acc + tap * w_ref[k:k + 1, :]
        o_ref[0, 0, pl.ds(out_base, CHUNK), :] = _silu(acc)

    def ctx_body(j, c):
        base = pl.multiple_of(j * CHUNK, CHUNK)
        chunk(off_ctx + base, base)
        return c

    def lat_body(j, c):
        base = pl.multiple_of(j * CHUNK, CHUNK)
        chunk(off_lat + base, l_ctx + base)
        return c

    lax.fori_loop(0, l_ctx // CHUNK, ctx_body, 0)
    lax.fori_loop(0, l_lat // CHUNK, lat_body, 0)


def _conv7(ctx_raw, lat_raw, w, b):
    bsz, l_ctx, c = ctx_raw.shape
    l_lat = lat_raw.shape[1]
    ltot = l_ctx + l_lat
    nct = c // LANE
    return pl.pallas_call(
        functools.partial(_conv7_kernel, l_ctx=l_ctx, l_lat=l_lat),
        out_shape=jax.ShapeDtypeStruct((bsz, nct, ltot, LANE), F32),
        grid=(bsz, nct),
        in_specs=[pl.BlockSpec((1, l_ctx, LANE), lambda bi, ci: (bi, 0, ci)),
                  pl.BlockSpec((1, l_lat, LANE), lambda bi, ci: (bi, 0, ci)),
                  pl.BlockSpec((SSM_CONV, LANE), lambda bi, ci: (0, ci)),
                  pl.BlockSpec((1, LANE), lambda bi, ci: (0, ci))],
        out_specs=pl.BlockSpec((1, 1, ltot, LANE), lambda bi, ci: (bi, ci, 0, 0)),
        scratch_shapes=[pltpu.VMEM((ltot + 3 * _CONV_PAD, LANE), F32)],
        compiler_params=_cparams(("parallel", "parallel")),
        name="conv7",
    )(ctx_raw, lat_raw, w, b.reshape(1, c))


def _ssd_kernel(xbc_ref, dtc_ref, dtl_ref, par_ref, dexp_ref, ex_ref, *rest, reverse, n_ctx, fuse_norm):
    if fuse_norm:
        yo_ref, sz_ref, nw_ref, o_ref, st_ref, cumt_ref, y_ref = rest
    else:
        y_ref, st_ref, cumt_ref = rest
    i = pl.program_id(1)

    @pl.when(i == 0)
    def _():
        st_ref[...] = jnp.zeros_like(st_ref)

    dt_raw = jnp.where(i < n_ctx, dtc_ref[0], dtl_ref[0])
    bias = par_ref[0:1, :]
    a = -jnp.exp(par_ref[1:2, :])
    dt = jax.nn.softplus(dt_raw + bias)
    cum = dt * a
    row = lax.broadcasted_iota(jnp.int32, (CHUNK, LANE), 0)
    k = 1
    while k < CHUNK:
        if reverse:
            cum = cum + jnp.where(row < CHUNK - k, pltpu.roll(cum, CHUNK - k, 0), 0.0)
        else:
            cum = cum + jnp.where(row >= k, pltpu.roll(cum, k, 0), 0.0)
        k *= 2
    last = 0 if reverse else CHUNK - 1
    cumt_ref[...] = cum.T
    li = lax.broadcasted_iota(jnp.int32, (CHUNK, CHUNK), 0)
    si = lax.broadcasted_iota(jnp.int32, (CHUNK, CHUNK), 1)
    causal = (li <= si) if reverse else (li >= si)
    lo = lax.broadcasted_iota(jnp.int32, (CHUNK, LANE), 1) < HEAD_DIM
    heads_per_group = N_HEADS // N_GROUPS
    pairs = heads_per_group // 2
    x_tiles = N_HEADS // 2

    def group(g, carry):
        shift = (LANE - heads_per_group * g) & (LANE - 1)
        cum_g = pltpu.roll(cum, shift, 1)
        dt_g = pltpu.roll(dt, shift, 1)
        cum_t = cumt_ref[pl.ds(pl.multiple_of(heads_per_group * g, heads_per_group), heads_per_group), :]
        bb = xbc_ref[0, x_tiles + g].astype(BF16)
        cb = xbc_ref[0, x_tiles + N_GROUPS + g].astype(BF16)
        scores = lax.dot_general(cb, bb, (((1,), (1,)), ((), ())), preferred_element_type=F32)
        h_t = st_ref[g]
        y_off = jnp.dot(cb, h_t.astype(BF16), preferred_element_type=F32)
        d_hi = dt_g.astype(BF16)
        r_hi = dt_g - d_hi.astype(F32)
        d_mid = r_hi.astype(BF16)
        d_lo = (r_hi - d_mid.astype(F32)).astype(BF16)
        dt_x = (jnp.dot(jnp.concatenate([d_hi, d_mid], axis=1), ex_ref[...], preferred_element_type=F32)
                + jnp.dot(d_lo, ex_ref[0:LANE, :], preferred_element_type=F32))
        xw_parts, dec_parts = [], []
        for p in range(pairs):
            j0, j1 = 2 * p, 2 * p + 1
            x2 = xbc_ref[0, pairs * g + p]
            c0 = cum_g[:, j0:j0 + 1]
            c1 = cum_g[:, j1:j1 + 1]
            l0 = jnp.exp(jnp.where(causal, c0 - cum_t[j0:j0 + 1, :], -jnp.inf))
            l1 = jnp.exp(jnp.where(causal, c1 - cum_t[j1:j1 + 1, :], -jnp.inf))
            m0 = (scores * l0).astype(BF16)
            m1 = (scores * l1).astype(BF16)
            dt2 = dt_x[:, p * LANE:(p + 1) * LANE]
            c2 = jnp.where(lo, c0, c1)
            xdt = x2 * dt2
            xdt_b = xdt.astype(BF16)
            zero = jnp.zeros_like(xdt_b)
            y_diag = (jnp.dot(m0, jnp.where(lo, xdt_b, zero), preferred_element_type=F32)
                      + jnp.dot(m1, jnp.where(lo, zero, xdt_b), preferred_element_type=F32))
            e2 = jnp.exp(c2)
            y = y_diag + y_off[:, p * LANE:(p + 1) * LANE] * e2
            y_ref[0, pairs * g + p] = y + dexp_ref[pairs * g + p] * x2
            to_end = jnp.exp(c2[last:last + 1, :] - c2)
            xw_parts.append((xdt * to_end).astype(BF16))
            dec_parts.append(e2[last:last + 1, :])
        xw = jnp.concatenate(xw_parts, axis=1)
        dec = jnp.concatenate(dec_parts, axis=1)
        upd = lax.dot_general(bb, xw, (((0,), (0,)), ((), ())), preferred_element_type=F32)
        st_ref[g] = h_t * dec + upd
        return carry

    lax.fori_loop(0, N_GROUPS, group, 0, unroll=2)

    if fuse_norm:
        sq = jnp.zeros((CHUNK, LANE), F32)
        for j in range(x_tiles):
            gj = (y_ref[0, j] + yo_ref[0, j]) * sz_ref[:, j * LANE:(j + 1) * LANE]
            y_ref[0, j] = gj
            sq = sq + gj * gj
        r = lax.rsqrt(jnp.sum(sq, axis=-1, keepdims=True) / (x_tiles * LANE) + EPS)
        for j in range(x_tiles):
            sl = slice(j * LANE, (j + 1) * LANE)
            o_ref[:, sl] = (y_ref[0, j] * r * nw_ref[:, sl]).astype(o_ref.dtype)


def _ssd(xbc_act, dt_ctx, dt_lat, par, dexp, reverse, norm_with=None):
    bsz, ntile, ltot, _ = xbc_act.shape
    l_ctx = dt_ctx.shape[1]
    l_lat = dt_lat.shape[1]
    n_ctx = l_ctx // CHUNK
    n_lat = l_lat // CHUNK
    steps = n_ctx + n_lat
    x_tiles = N_HEADS // 2
    gw = (N_HEADS // N_GROUPS) * HEAD_DIM
    e1 = (jnp.arange(gw)[None, :] // HEAD_DIM == jnp.arange(LANE)[:, None]).astype(BF16)
    expand = jnp.concatenate([e1, e1], axis=0)

    if reverse:
        def cat_chunk(i):
            return jnp.where(i < n_ctx, n_ctx - 1 - i, n_ctx + steps - 1 - i)

        def ctx_chunk(i):
            return jnp.maximum(n_ctx - 1 - i, 0)

        def lat_chunk(i):
            return jnp.minimum(steps - 1 - i, n_lat - 1)
    else:
        def cat_chunk(i):
            return i

        def ctx_chunk(i):
            return jnp.minimum(i, n_ctx - 1)

        def lat_chunk(i):
            return jnp.maximum(i - n_ctx, 0)

    y_spec = pl.BlockSpec((1, x_tiles, CHUNK, LANE), lambda b, i: (b, 0, lat_chunk(i), 0))
    in_specs = [pl.BlockSpec((1, ntile, CHUNK, LANE), lambda b, i: (b, 0, cat_chunk(i), 0)),
                pl.BlockSpec((1, CHUNK, LANE), lambda b, i: (b, ctx_chunk(i), 0)),
                pl.BlockSpec((1, CHUNK, LANE), lambda b, i: (b, lat_chunk(i), 0)),
                pl.BlockSpec((8, LANE), lambda b, i: (0, 0)),
                pl.BlockSpec((x_tiles, 1, LANE), lambda b, i: (0, 0, 0)),
                pl.BlockSpec((2 * LANE, gw), lambda b, i: (0, 0))]
    args = [xbc_act, dt_ctx, dt_lat, par, dexp, expand]
    scratch = [pltpu.VMEM((N_GROUPS, D_STATE, gw), F32), pltpu.VMEM((LANE, CHUNK), F32)]
    if norm_with is None:
        out_shape = jax.ShapeDtypeStruct((bsz, x_tiles, l_lat, LANE), F32)
        out_spec = y_spec
    else:
        y_other, silu_z, norm_w = norm_with
        dn = x_tiles * LANE
        row_spec = pl.BlockSpec((CHUNK, dn), lambda b, i: (b * n_lat + lat_chunk(i), 0))
        in_specs += [y_spec, row_spec, pl.BlockSpec((1, dn), lambda b, i: (0, 0))]
        args += [y_other, silu_z, norm_w.reshape(1, dn)]
        out_shape = jax.ShapeDtypeStruct((bsz * l_lat, dn), BF16)
        out_spec = row_spec
        scratch.append(pltpu.VMEM((1, x_tiles, CHUNK, LANE), F32))
    return pl.pallas_call(
        functools.partial(_ssd_kernel, reverse=reverse, n_ctx=n_ctx, fuse_norm=norm_with is not None),
        out_shape=out_shape,
        grid=(bsz, steps),
        in_specs=in_specs,
        out_specs=out_spec,
        scratch_shapes=scratch,
        compiler_params=_cparams(("parallel", "arbitrary")),
        name="ssd_bwd" if reverse else "ssd_fwd",
    )(*args)


def _conv31_kernel(u_ref, w_ref, b_ref, o_ref, pad_ref, *, seq):
    halo = (CF_KERNEL // 2) * GRID_W
    zeros = jnp.zeros((halo, LANE), F32)
    pad_ref[0:halo, :] = zeros
    pad_ref[halo + seq:halo + seq + halo, :] = zeros
    pad_ref[halo:halo + seq, :] = u_ref[0]
    bias = b_ref[...]

    def body(j, c):
        base = pl.multiple_of(j * CHUNK, CHUNK)
        acc = jnp.broadcast_to(bias, (CHUNK, LANE))
        for k in range(CF_KERNEL):
            tap = pad_ref[pl.ds(pl.multiple_of(base + k * GRID_W, GRID_W), CHUNK), :]
            acc = acc + tap * w_ref[k:k + 1, :]
        o_ref[0, pl.ds(base, CHUNK), :] = acc
        return c

    lax.fori_loop(0, seq // CHUNK, body, 0)


def _conv31(u3, w, b):
    bsz, s, c = u3.shape
    halo = (CF_KERNEL // 2) * GRID_W
    return pl.pallas_call(
        functools.partial(_conv31_kernel, seq=s),
        out_shape=jax.ShapeDtypeStruct((bsz, s, c), F32),
        grid=(bsz, c // LANE),
        in_specs=[pl.BlockSpec((1, s, LANE), lambda bi, ci: (bi, 0, ci)),
                  pl.BlockSpec((CF_KERNEL, LANE), lambda bi, ci: (0, ci)),
                  pl.BlockSpec((1, LANE), lambda bi, ci: (0, ci))],
        out_specs=pl.BlockSpec((1, s, LANE), lambda bi, ci: (bi, 0, ci)),
        scratch_shapes=[pltpu.VMEM((s + 2 * halo, LANE), F32)],
        compiler_params=_cparams(("parallel", "parallel")),
        name="conv31",
    )(u3, w, b.reshape(1, c))


def _lnsilu_kernel(x_ref, w_ref, b_ref, o_ref):
    xf = x_ref[...]
    mu = jnp.mean(xf, axis=-1, keepdims=True)
    xc = xf - mu
    var = jnp.mean(xc * xc, axis=-1, keepdims=True)
    y = xc * lax.rsqrt(var + EPS) * w_ref[...] + b_ref[...]
    o_ref[...] = _silu(y).astype(o_ref.dtype)


def _lnsilu(x2, w, b, tm=512):
    m, d = x2.shape
    tm = min(tm, m)
    return pl.pallas_call(
        _lnsilu_kernel,
        out_shape=jax.ShapeDtypeStruct((m, d), BF16),
        grid=(m // tm,),
        in_specs=[pl.BlockSpec((tm, d), lambda i: (i, 0)),
                  pl.BlockSpec((1, d), lambda i: (0, 0)),
                  pl.BlockSpec((1, d), lambda i: (0, 0))],
        out_specs=pl.BlockSpec((tm, d), lambda i: (i, 0)),
        compiler_params=_cparams(("parallel",)),
        name="lnsilu",
    )(x2, w.reshape(1, d), b.reshape(1, d))


def _route_kernel(x_ref, w_ref, sh_ref, sc_ref, rw_ref, rb_ref, h_ref, eid_ref, ew_ref):
    xf = x_ref[...]
    ms = jnp.mean(xf * xf, axis=-1, keepdims=True)
    h = xf * lax.rsqrt(ms + EPS) * w_ref[...]
    h = h * (1.0 + sc_ref[0]) + sh_ref[0]
    tm = xf.shape[0]
    nt = xf.shape[1] // LANE
    pitch = _pitch(nt)
    for j in range(nt):
        h_ref[pl.ds(j, tm, stride=pitch), :] = h[:, j * LANE:(j + 1) * LANE]
    for j in range(nt, pitch):
        h_ref[pl.ds(j, tm, stride=pitch), :] = jnp.zeros((tm, LANE), F32)
    logits = jnp.dot(h.astype(BF16), rw_ref[...], preferred_element_type=F32) + rb_ref[...]
    lane = lax.broadcasted_iota(jnp.int32, (tm, LANE), 1)
    lane_f = lane.astype(F32)
    ninf = -jnp.inf
    gl = jnp.where(lane < MOE_GROUPS, logits, ninf)
    gmax = jnp.max(gl, axis=-1, keepdims=True)
    gidx = jnp.min(jnp.where(gl == gmax, lane_f, float(LANE)), axis=-1, keepdims=True)
    gsum = jnp.sum(jnp.exp(gl - gmax), axis=-1, keepdims=True)
    g_p = 1.0 / gsum
    first = float(MOE_GROUPS) + gidx * float(EXPERTS_PER_GROUP)
    in_group = (lane_f >= first) & (lane_f < first + float(EXPERTS_PER_GROUP))
    el = jnp.where(in_group, logits, ninf)
    m1 = jnp.max(el, axis=-1, keepdims=True)
    i1 = jnp.min(jnp.where(el == m1, lane_f, float(LANE)), axis=-1, keepdims=True)
    el2 = jnp.where(lane_f == i1, ninf, el)
    m2 = jnp.max(el2, axis=-1, keepdims=True)
    i2 = jnp.min(jnp.where(el2 == m2, lane_f, float(LANE)), axis=-1, keepdims=True)
    e21 = jnp.exp(m2 - m1)
    den = 1.0 + e21
    w1 = (1.0 / den) * g_p
    w2 = (e21 / den) * g_p
    e1 = (i1 - float(MOE_GROUPS)).astype(jnp.int32)
    e2 = (i2 - float(MOE_GROUPS)).astype(jnp.int32)
    eid_ref[...] = jnp.where(lane == 0, e1, jnp.where(lane == 1, e2, 0))
    ew_ref[...] = jnp.where(lane == 0, w1, jnp.where(lane == 1, w2, 0.0))


def _route(x2, w, mod3, shift_chunk, scale_chunk, rows_per_batch, rw, rb, tm=256):
    m, d = x2.shape
    pitch = _pitch(d // LANE)
    tiles_per_batch = rows_per_batch // tm
    return pl.pallas_call(
        _route_kernel,
        out_shape=(jax.ShapeDtypeStruct((m * pitch, LANE), F32),
                   jax.ShapeDtypeStruct((m, LANE), jnp.int32),
                   jax.ShapeDtypeStruct((m, LANE), F32)),
        grid=(m // tm,),
        in_specs=[pl.BlockSpec((tm, d), lambda i: (i, 0)),
                  pl.BlockSpec((1, d), lambda i: (0, 0)),
                  pl.BlockSpec((1, 1, d), lambda i: (i // tiles_per_batch, 0, shift_chunk)),
                  pl.BlockSpec((1, 1, d), lambda i: (i // tiles_per_batch, 0, scale_chunk)),
                  pl.BlockSpec((d, LANE), lambda i: (0, 0)),
                  pl.BlockSpec((1, LANE), lambda i: (0, 0))],
        out_specs=(pl.BlockSpec((tm * pitch, LANE), lambda i: (i, 0)),
                   pl.BlockSpec((tm, LANE), lambda i: (i, 0)),
                   pl.BlockSpec((tm, LANE), lambda i: (i, 0))),
        compiler_params=_cparams(("parallel",)),
        name="route",
    )(x2, w.reshape(1, d), mod3, mod3, rw, rb)


_DMA_UNROLL = 8


def _rows_to_matrix(ref, tm, nt):
    return jnp.concatenate([ref[pl.ds(j, tm, stride=_pitch(nt)), :] for j in range(nt)], axis=1)


def _token_copy(src, dst, sem, nt, rows, src_tok, dst_tok):
    pitch = _pitch(nt)
    return pltpu.make_async_copy(src.at[pl.ds(src_tok * pitch, rows), :],
                                 dst.at[pl.ds(dst_tok * pitch, rows), :], sem)


def _bulk_wait(src, dst, sem, total_rows):
    pltpu.make_async_copy(src.at[pl.ds(0, total_rows), :], dst.at[pl.ds(0, total_rows), :], sem).wait()


def _for_rows(n, body):
    groups = lax.shift_right_logical(n, _DMA_UNROLL.bit_length() - 1)

    def group(g, c):
        for u in range(_DMA_UNROLL):
            body(g * _DMA_UNROLL + u)
        return c

    def tail(r, c):
        body(r)
        return c

    lax.fori_loop(0, groups, group, 0)
    lax.fori_loop(groups * _DMA_UNROLL, n, tail, 0)


def _wait_rows(n, wait_tokens):
    p = MOE_BLOCK
    while p >= 1:
        @pl.when((n & p) != 0)
        def _(p=p):
            wait_tokens(p)
        p //= 2


def _stream_expert_weights(b, be_ref, eord_ref, enext_ref, w_hbms, w_bufs, w_caches, wsem):
    prev = jnp.maximum(b - 1, 0)

    def copies(e, slot):
        return [pltpu.make_async_copy(w.at[e], buf.at[slot], wsem.at[slot]) for w, buf in zip(w_hbms, w_bufs)]

    @pl.when(b == 0)
    def _():
        for cp in copies(be_ref[0], 0):
            cp.start()

    @pl.when((b == 0) | (be_ref[b] != be_ref[prev]))
    def _():
        for s in range(2):
            @pl.when((eord_ref[b] & 1) == s)
            def _(s=s):
                for cp in copies(be_ref[b], s):
                    cp.wait()

                @pl.when(enext_ref[b] >= 0)
                def _():
                    for cp in copies(enext_ref[b], 1 - s):
                        cp.start()

                for buf, cache in zip(w_bufs, w_caches):
                    cache[...] = buf[s].astype(BF16)


def _expert_up_kernel(be_ref, nv_ref, nused_ref, eord_ref, enext_ref, tokc_ref, tokn_ref, h_hbm, wg_hbm, wu_hbm,
                      o_ref, xs0_ref, xs1_ref, wgs_ref, wus_ref, wgb_ref, wub_ref, sem, wsem, *, fchunk, nt):
    b = pl.program_id(0)
    n_used = nused_ref[0]
    dff = wgb_ref.shape[1]
    slots = (xs0_ref, xs1_ref)

    def start_gather(tok_ref, blk, slot):
        _for_rows(nv_ref[blk], lambda r: _token_copy(
            h_hbm, slots[slot], sem.at[slot], nt, nt, tok_ref[0, 0, r], r).start())

    def wait_gather(blk, slot):
        _wait_rows(nv_ref[blk], lambda p: _bulk_wait(h_hbm, slots[slot], sem.at[slot], p * nt))

    @pl.when(b == 0)
    def _():
        xs0_ref[...] = jnp.zeros_like(xs0_ref)
        xs1_ref[...] = jnp.zeros_like(xs1_ref)
        start_gather(tokc_ref, 0, 0)

    for slot in range(2):
        @pl.when((b + 1 < n_used) & (lax.rem(b, 2) == slot))
        def _(slot=slot):
            start_gather(tokn_ref, b + 1, 1 - slot)

    @pl.when(b < n_used)
    def _():
        _stream_expert_weights(b, be_ref, eord_ref, enext_ref, (wg_hbm, wu_hbm), (wgs_ref, wus_ref),
                               (wgb_ref, wub_ref), wsem)

        for slot in range(2):
            @pl.when(lax.rem(b, 2) == slot)
            def _(slot=slot):
                wait_gather(b, slot)
                xb = _rows_to_matrix(slots[slot], MOE_BLOCK, nt).astype(BF16)
                for f in range(dff // fchunk):
                    sl = slice(f * fchunk, (f + 1) * fchunk)
                    gate = jnp.dot(xb, wgb_ref[:, sl], preferred_element_type=F32)
                    up = jnp.dot(xb, wub_ref[:, sl], preferred_element_type=F32)
                    o_ref[:, sl] = (_silu(gate) * up).astype(o_ref.dtype)

    @pl.when(b >= n_used)
    def _():
        o_ref[...] = jnp.zeros_like(o_ref)


def _expert_up(h2t, buf_tok, w_gate, w_up, tables, fchunk=256):
    n_blocks = buf_tok.shape[0]
    _, d, dff = w_gate.shape
    nt = d // LANE
    slot_rows = MOE_BLOCK * _pitch(nt)
    grid_spec = pltpu.PrefetchScalarGridSpec(
        num_scalar_prefetch=len(tables),
        grid=(n_blocks,),
        in_specs=[pl.BlockSpec((1, 1, MOE_BLOCK), lambda b, *_: (b, 0, 0), memory_space=pltpu.SMEM),
                  pl.BlockSpec((1, 1, MOE_BLOCK), lambda b, *_: (jnp.minimum(b + 1, n_blocks - 1), 0, 0),
                               memory_space=pltpu.SMEM),
                  pl.BlockSpec(memory_space=pl.ANY),
                  pl.BlockSpec(memory_space=pl.ANY),
                  pl.BlockSpec(memory_space=pl.ANY)],
        out_specs=pl.BlockSpec((MOE_BLOCK, dff), lambda b, *_: (b, 0)),
        scratch_shapes=[pltpu.VMEM((slot_rows, LANE), F32),
                        pltpu.VMEM((slot_rows, LANE), F32),
                        pltpu.VMEM((2, d, dff), F32),
                        pltpu.VMEM((2, d, dff), F32),
                        pltpu.VMEM((d, dff), BF16),
                        pltpu.VMEM((d, dff), BF16),
                        pltpu.SemaphoreType.DMA((2,)),
                        pltpu.SemaphoreType.DMA((2,))],
    )
    return pl.pallas_call(
        functools.partial(_expert_up_kernel, fchunk=fchunk, nt=nt),
        out_shape=jax.ShapeDtypeStruct((n_blocks * MOE_BLOCK, dff), BF16),
        grid_spec=grid_spec,
        compiler_params=_cparams(("arbitrary",)),
        name="expert_up",
    )(*tables, buf_tok, buf_tok, h2t, w_gate, w_up)


def _expert_down_kernel(be_ref, nv_ref, nused_ref, eord_ref, enext_ref, asg_ref, h_ref, wd_hbm, y_hbm,
                        ys0_ref, ys1_ref, wds_ref, wdb_ref, sem, wsem, *, nchunk, nt, n_tok):
    b = pl.program_id(0)
    n_used = nused_ref[0]
    d = wdb_ref.shape[1]
    slots = (ys0_ref, ys1_ref)
    pitch = _pitch(nt)

    def row_copy(slot, r, assign):
        k = assign & 1
        tok = lax.shift_right_logical(assign, 1)
        return _token_copy(slots[slot], y_hbm, sem.at[slot], nt, pitch, r, k * n_tok + tok)

    def start_scatter(blk, slot):
        _for_rows(nv_ref[blk], lambda r: row_copy(slot, r, asg_ref[0, 0, r]).start())

    def wait_scatter(blk, slot):
        _wait_rows(nv_ref[blk], lambda p: _bulk_wait(slots[slot], y_hbm, sem.at[slot], p * pitch))

    @pl.when(b == 0)
    def _():
        ys0_ref[...] = jnp.zeros_like(ys0_ref)
        ys1_ref[...] = jnp.zeros_like(ys1_ref)

    @pl.when(b < n_used)
    def _():
        _stream_expert_weights(b, be_ref, eord_ref, enext_ref, (wd_hbm,), (wds_ref,), (wdb_ref,), wsem)
        hb = h_ref[...]
        for slot in range(2):
            @pl.when(lax.rem(b, 2) == slot)
            def _(slot=slot):
                for c in range(d // nchunk):
                    out = jnp.dot(hb, wdb_ref[:, c * nchunk:(c + 1) * nchunk], preferred_element_type=F32)
                    for j in range(nchunk // LANE):
                        slots[slot][pl.ds(c * (nchunk // LANE) + j, MOE_BLOCK, stride=pitch), :] = (
                            out[:, j * LANE:(j + 1) * LANE])

    for slot in range(2):
        @pl.when((b >= 1) & (b - 1 < n_used) & (lax.rem(b, 2) == slot))
        def _(slot=slot):
            wait_scatter(b - 1, 1 - slot)

        @pl.when((b < n_used) & (lax.rem(b, 2) == slot))
        def _(slot=slot):
            start_scatter(b, slot)

        @pl.when((b == pl.num_programs(0) - 1) & (b < n_used) & (lax.rem(b, 2) == slot))
        def _(slot=slot):
            wait_scatter(b, slot)


def _expert_down(hid, buf_assign, w_down, tables, n_tok, nchunk=512):
    n_rows, dff = hid.shape
    n_blocks = n_rows // MOE_BLOCK
    d = w_down.shape[2]
    nt = d // LANE
    pitch = _pitch(nt)
    grid_spec = pltpu.PrefetchScalarGridSpec(
        num_scalar_prefetch=len(tables),
        grid=(n_blocks,),
        in_specs=[pl.BlockSpec((1, 1, MOE_BLOCK), lambda b, *_: (b, 0, 0), memory_space=pltpu.SMEM),
                  pl.BlockSpec((MOE_BLOCK, dff), lambda b, be, nv, n, *_: (jnp.minimum(b, n[0] - 1), 0)),
                  pl.BlockSpec(memory_space=pl.ANY)],
        out_specs=pl.BlockSpec(memory_space=pl.ANY),
        scratch_shapes=[pltpu.VMEM((MOE_BLOCK * pitch, LANE), F32),
                        pltpu.VMEM((MOE_BLOCK * pitch, LANE), F32),
                        pltpu.VMEM((2, dff, d), F32),
                        pltpu.VMEM((dff, d), BF16),
                        pltpu.SemaphoreType.DMA((2,)),
                        pltpu.SemaphoreType.DMA((2,))],
    )
    return pl.pallas_call(
        functools.partial(_expert_down_kernel, nchunk=nchunk, nt=nt, n_tok=n_tok),
        out_shape=jax.ShapeDtypeStruct((2 * n_tok * pitch, LANE), F32),
        grid_spec=grid_spec,
        compiler_params=_cparams(("arbitrary",)),
        name="expert_down",
    )(*tables, buf_assign, hid, w_down)


def _combine_kernel(y0_ref, y1_ref, ew_ref, x_ref, g_ref, w_ref, o_ref):
    tm, d = x_ref.shape
    nt = d // LANE
    ew = ew_ref[...]
    moe = (_rows_to_matrix(y0_ref, tm, nt) * ew[:, 0:1]
           + _rows_to_matrix(y1_ref, tm, nt) * ew[:, 1:2])
    xo = x_ref[...] + g_ref[0] * moe
    ms = jnp.mean(xo * xo, axis=-1, keepdims=True)
    o_ref[...] = xo * lax.rsqrt(ms + EPS) * w_ref[...]


def _combine(y, ew, x2, mod3, gate_chunk, rows_per_batch, final_w, tm=256):
    m, d = x2.shape
    pitch = _pitch(d // LANE)
    tiles = m // tm
    tiles_per_batch = rows_per_batch // tm
    return pl.pallas_call(
        _combine_kernel,
        out_shape=jax.ShapeDtypeStruct((m, d), F32),
        grid=(tiles,),
        in_specs=[pl.BlockSpec((tm * pitch, LANE), lambda i: (i, 0)),
                  pl.BlockSpec((tm * pitch, LANE), lambda i: (tiles + i, 0)),
                  pl.BlockSpec((tm, LANE), lambda i: (i, 0)),
                  pl.BlockSpec((tm, d), lambda i: (i, 0)),
                  pl.BlockSpec((1, 1, d), lambda i: (i // tiles_per_batch, 0, gate_chunk)),
                  pl.BlockSpec((1, d), lambda i: (0, 0))],
        out_specs=pl.BlockSpec((tm, d), lambda i: (i, 0)),
        compiler_params=_cparams(("parallel",)),
        name="moe_combine",
    )(y, y, ew, x2, mod3, final_w.reshape(1, d))


def _dispatch_tables(eid, n_tok):
    top_k = eid.shape[1]
    n_assign = n_tok * top_k
    expert = eid.reshape(-1)
    key = jnp.sort(expert * n_assign + jnp.arange(n_assign, dtype=jnp.int32))
    sorted_assign = key % n_assign
    bounds = jnp.arange(N_EXPERTS + 1, dtype=jnp.int32) * n_assign
    start = jnp.searchsorted(key, bounds, side="left").astype(jnp.int32)
    counts = start[1:] - start[:-1]
    nblk = (counts + MOE_BLOCK - 1) // MOE_BLOCK
    blk_end = jnp.cumsum(nblk)
    blk_start = blk_end - nblk
    n_blocks = -(-n_assign // MOE_BLOCK) + N_EXPERTS
    bidx = jnp.arange(n_blocks, dtype=jnp.int32)
    block_expert = jnp.minimum(jnp.searchsorted(blk_end, bidx, side="right"), N_EXPERTS - 1).astype(jnp.int32)
    in_expert = (bidx - blk_start[block_expert]) * MOE_BLOCK
    n_valid = jnp.clip(counts[block_expert] - in_expert, 0, MOE_BLOCK).astype(jnp.int32)
    src = start[block_expert][:, None] + in_expert[:, None] + jnp.arange(MOE_BLOCK, dtype=jnp.int32)[None, :]
    valid = jnp.arange(MOE_BLOCK, dtype=jnp.int32)[None, :] < n_valid[:, None]
    buf_assign = jnp.where(valid, sorted_assign[jnp.clip(src, 0, n_assign - 1)], 0).astype(jnp.int32)
    n_used = blk_end[-1].astype(jnp.int32)
    first = jnp.concatenate([jnp.ones((1,), jnp.int32),
                             (block_expert[1:] != block_expert[:-1]).astype(jnp.int32)])
    expert_ordinal = (jnp.cumsum(first) - 1).astype(jnp.int32)
    next_blk = blk_end[block_expert]
    next_expert = jnp.where(next_blk < n_used, block_expert[jnp.minimum(next_blk, n_blocks - 1)], -1).astype(jnp.int32)
    tables = (block_expert, n_valid, n_used.reshape(1), expert_ordinal, next_expert)
    return buf_assign.reshape(n_blocks, 1, MOE_BLOCK), tables


def kernel(x, c, ctx, c_ctx, ada_w, ada_b, norm1_w, w_in, ssm_conv_w, ssm_conv_b, dt_bias, a_log, d_skip, ssm_norm_w, ssm_out_w, cf_dw_w, cf_dw_b, cf_ln_w, cf_ln_b, cf_out_w, cf_out_b, w_o, norm2_w, router_group_w, router_group_b, router_expert_w, router_expert_b, expert_w_gate, expert_w_up, expert_w_down, final_norm_w):
    bsz, seq, d = x.shape
    l_ctx = ctx.shape[1]
    n_tok = bsz * seq
    d_inner = ssm_norm_w.shape[1]
    gn = N_GROUPS * D_STATE
    xbc_dim = d_inner + 2 * gn
    off_dt = xbc_dim
    off_z = off_dt + N_HEADS
    off_glu = off_z + d_inner
    off_gate = off_glu + 2 * d

    ctx_row = bsz
    crows = jnp.zeros((8, d), F32).at[:bsz].set(c).at[ctx_row].set(c_ctx)
    mod = _ada(crows, ada_w[0], ada_b[0])
    mod3 = mod.reshape(8, 1, 6 * d)
    lat_rows = jnp.arange(bsz, dtype=jnp.int32)
    ctx_rows = jnp.full((bsz,), ctx_row, jnp.int32)

    h_lat = _normmod(x, norm1_w[0], mod3, lat_rows, 0, 1, BF16).reshape(n_tok, d)
    h_ctx = _normmod(ctx, norm1_w[0], mod3, ctx_rows, 0, 1, BF16).reshape(bsz * l_ctx, d)

    w = w_in[0]
    w_xbc = w[:, :xbc_dim].astype(BF16)
    w_dt = jnp.pad(w[:, off_dt:off_z], ((0, 0), (0, LANE - N_HEADS))).astype(BF16)
    w_z = w[:, off_z:off_glu].astype(BF16)
    w_glu_a = w[:, off_glu:off_glu + d].astype(BF16)
    w_glu_b = w[:, off_glu + d:off_gate].astype(BF16)
    w_gate = w[:, off_gate:].astype(BF16)

    xbc_lat = _mm(h_lat, w_xbc, name="in_xbc").reshape(bsz, seq, xbc_dim)
    xbc_ctx = _mm(h_ctx, w_xbc, tm=512, name="in_xbc_ctx").reshape(bsz, l_ctx, xbc_dim)
    dt_lat = _mm(h_lat, w_dt, name="in_dt").reshape(bsz, seq, LANE)
    dt_ctx = _mm(h_ctx, w_dt, tm=512, name="in_dt_ctx").reshape(bsz, l_ctx, LANE)
    sz = _mm(h_lat, w_z, act="silu", name="in_z")
    u = _mm_glu(h_lat, w_glu_a, w_glu_b)
    gates = _mm(h_lat, w_gate, act="sigmoid", name="in_gate")

    xbc_act = _conv7(xbc_ctx, xbc_lat, ssm_conv_w[0], ssm_conv_b[0])

    def ssd_params(k):
        par = jnp.zeros((8, LANE), F32).at[0, :N_HEADS].set(dt_bias[0, k]).at[1, :N_HEADS].set(a_log[0, k])
        return par, jnp.repeat(d_skip[0, k], HEAD_DIM).reshape(N_HEADS // 2, 1, LANE)

    y_bwd = _ssd(xbc_act, dt_ctx, dt_lat, *ssd_params(1), reverse=True)
    gnorm = _ssd(xbc_act, dt_ctx, dt_lat, *ssd_params(0), reverse=False,
                 norm_with=(y_bwd, sz, ssm_norm_w[0]))
    y_ssd = _mm(gnorm, ssm_out_w[0].astype(BF16), tn=512, name="ssm_out")

    cv = _conv31(u.reshape(bsz, seq, d), cf_dw_w[0], cf_dw_b[0]).reshape(n_tok, d)
    ua = _lnsilu(cv, cf_ln_w[0], cf_ln_b[0])
    merged = _mm_merge(ua, cf_out_w[0].astype(BF16), cf_out_b[0], gates, y_ssd)
    x1 = _mm_resid(merged, w_o[0].astype(BF16), x.reshape(n_tok, d), mod3, 2, seq)

    n_r = MOE_GROUPS + N_EXPERTS
    rw = jnp.pad(jnp.concatenate([router_group_w[0], router_expert_w[0]], axis=1),
                 ((0, 0), (0, LANE - n_r))).astype(BF16)
    rb = jnp.pad(jnp.concatenate([router_group_b[0], router_expert_b[0]]), (0, LANE - n_r)).reshape(1, LANE)
    h2t, eid, ew = _route(x1, norm2_w[0], mod3, 3, 4, seq, rw, rb)

    buf_assign, tables = _dispatch_tables(eid[:, :2], n_tok)
    hid = _expert_up(h2t, buf_assign // 2, expert_w_gate[0], expert_w_up[0], tables)
    y = _expert_down(hid, buf_assign, expert_w_down[0], tables, n_tok)
    out = _combine(y, ew, x1, mod3, 5, seq, final_norm_w)
    return out.reshape(bsz, seq, d)
```

```python
import functools

import jax
import jax.numpy as jnp
from jax import lax
from jax.experimental import pallas as pl
from jax.experimental.pallas import tpu as pltpu

F32 = jnp.float32
BF16 = jnp.bfloat16

EPS = 1e-6
GRID_W = 64
HEAD_DIM = 64
N_HEADS = 64
N_GROUPS = 8
D_STATE = 128
CHUNK = 128
SSM_CONV = 7
CF_KERNEL = 31
MOE_GROUPS = 8
EXPERTS_PER_GROUP = 8
N_EXPERTS = 64
MOE_BLOCK = 256
LANE = 128
LOG2E = 1.4426950408889634
VMEM_LIMIT = 56 * 1024 * 1024


def _cparams(sem):
    return pltpu.CompilerParams(dimension_semantics=sem, vmem_limit_bytes=VMEM_LIMIT)


def _silu(v):
    return v * jax.nn.sigmoid(v)


def _pitch(nt):
    return nt + 1


def _ada_kernel(c_ref, w_ref, b_ref, o_ref):
    s = _silu(c_ref[...])
    o_ref[...] = jnp.dot(s.astype(BF16), w_ref[...].astype(BF16),
                         preferred_element_type=F32) + b_ref[...]


def _ada(crows, ada_w, ada_b, tn=1024):
    r, d = crows.shape
    n = ada_w.shape[1]
    return pl.pallas_call(
        _ada_kernel,
        out_shape=jax.ShapeDtypeStruct((r, n), F32),
        grid=(n // tn,),
        in_specs=[pl.BlockSpec((r, d), lambda j: (0, 0)),
                  pl.BlockSpec((d, tn), lambda j: (0, j)),
                  pl.BlockSpec((1, tn), lambda j: (0, j))],
        out_specs=pl.BlockSpec((r, tn), lambda j: (0, j)),
        compiler_params=_cparams(("parallel",)),
        name="ada",
    )(crows, ada_w, ada_b.reshape(1, n))


def _normmod_kernel(rows_ref, x_ref, w_ref, sh_ref, sc_ref, o_ref):
    del rows_ref
    xf = x_ref[0]
    ms = jnp.mean(xf * xf, axis=-1, keepdims=True)
    y = xf * lax.rsqrt(ms + EPS) * w_ref[...]
    o_ref[0] = (y * (1.0 + sc_ref[0]) + sh_ref[0]).astype(o_ref.dtype)


def _normmod(x3, w, mod3, rows, shift_chunk, scale_chunk, out_dtype, tm=256):
    bx, l, d = x3.shape
    grid_spec = pltpu.PrefetchScalarGridSpec(
        num_scalar_prefetch=1,
        grid=(bx, l // tm),
        in_specs=[pl.BlockSpec((1, tm, d), lambda b, i, r: (b, i, 0)),
                  pl.BlockSpec((1, d), lambda b, i, r: (0, 0)),
                  pl.BlockSpec((1, 1, d), lambda b, i, r: (r[b], 0, shift_chunk)),
                  pl.BlockSpec((1, 1, d), lambda b, i, r: (r[b], 0, scale_chunk))],
        out_specs=pl.BlockSpec((1, tm, d), lambda b, i, r: (b, i, 0)),
    )
    return pl.pallas_call(
        _normmod_kernel,
        out_shape=jax.ShapeDtypeStruct((bx, l, d), out_dtype),
        grid_spec=grid_spec,
        compiler_params=_cparams(("parallel", "parallel")),
        name="normmod",
    )(rows, x3, w.reshape(1, d), mod3, mod3)


def _mm_kernel(a_ref, w_ref, *rest, act, has_bias):
    o_ref = rest[-1]
    acc = jnp.dot(a_ref[...], w_ref[...], preferred_element_type=F32)
    if has_bias:
        acc = acc + rest[0][...]
    if act == "silu":
        acc = _silu(acc)
    elif act == "sigmoid":
        acc = jax.nn.sigmoid(acc)
    o_ref[...] = acc.astype(o_ref.dtype)


def _mm(a, w, bias=None, act=None, out_dtype=F32, tm=1024, tn=1024, name="mm"):
    m, k = a.shape
    n = w.shape[1]
    tm, tn = min(tm, m), min(tn, n)
    in_specs = [pl.BlockSpec((tm, k), lambda i, j: (i, 0)),
                pl.BlockSpec((k, tn), lambda i, j: (0, j))]
    args = [a, w]
    if bias is not None:
        in_specs.append(pl.BlockSpec((1, tn), lambda i, j: (0, j)))
        args.append(bias.reshape(1, n))
    return pl.pallas_call(
        functools.partial(_mm_kernel, act=act, has_bias=bias is not None),
        out_shape=jax.ShapeDtypeStruct((m, n), out_dtype),
        grid=(m // tm, n // tn),
        in_specs=in_specs,
        out_specs=pl.BlockSpec((tm, tn), lambda i, j: (i, j)),
        compiler_params=_cparams(("parallel", "parallel")),
        name=name,
    )(*args)


def _mm_glu_kernel(a_ref, wa_ref, wb_ref, o_ref):
    a = a_ref[...]
    va = jnp.dot(a, wa_ref[...], preferred_element_type=F32)
    vb = jnp.dot(a, wb_ref[...], preferred_element_type=F32)
    o_ref[...] = va * jax.nn.sigmoid(vb)


def _mm_glu(a, wa, wb, tm=1024, tn=512):
    m, k = a.shape
    n = wa.shape[1]
    tm = min(tm, m)
    return pl.pallas_call(
        _mm_glu_kernel,
        out_shape=jax.ShapeDtypeStruct((m, n), F32),
        grid=(m // tm, n // tn),
        in_specs=[pl.BlockSpec((tm, k), lambda i, j: (i, 0)),
                  pl.BlockSpec((k, tn), lambda i, j: (0, j)),
                  pl.BlockSpec((k, tn), lambda i, j: (0, j))],
        out_specs=pl.BlockSpec((tm, tn), lambda i, j: (i, j)),
        compiler_params=_cparams(("parallel", "parallel")),
        name="mm_glu",
    )(a, wa, wb)


def _mm_merge_kernel(cv_ref, lw_ref, lb_ref, w_ref, b_ref, ga_ref, gb_ref, ys_ref, o_ref, u_ref):
    @pl.when(pl.program_id(1) == 0)
    def _():
        xf = cv_ref[...]
        mu = jnp.mean(xf, axis=-1, keepdims=True)
        xc = xf - mu
        var = jnp.mean(xc * xc, axis=-1, keepdims=True)
        y = xc * lax.rsqrt(var + EPS) * lw_ref[...] + lb_ref[...]
        u_ref[...] = _silu(y).astype(u_ref.dtype)

    ycf = jnp.dot(u_ref[...], w_ref[...], preferred_element_type=F32) + b_ref[...]
    o_ref[...] = (ga_ref[...] * ys_ref[...] + gb_ref[...] * ycf).astype(o_ref.dtype)


def _mm_merge(cv, ln_w, ln_b, w, bias, gates, y_ssd, tm=1024, tn=512):
    m, k = cv.shape
    n = w.shape[1]
    tm = min(tm, m)
    nj = n // tn
    return pl.pallas_call(
        _mm_merge_kernel,
        out_shape=jax.ShapeDtypeStruct((m, n), BF16),
        grid=(m // tm, nj),
        in_specs=[pl.BlockSpec((tm, k), lambda i, j: (i, 0)),
                  pl.BlockSpec((1, k), lambda i, j: (0, 0)),
                  pl.BlockSpec((1, k), lambda i, j: (0, 0)),
                  pl.BlockSpec((k, tn), lambda i, j: (0, j)),
                  pl.BlockSpec((1, tn), lambda i, j: (0, j)),
                  pl.BlockSpec((tm, tn), lambda i, j: (i, j)),
                  pl.BlockSpec((tm, tn), lambda i, j: (i, j + nj)),
                  pl.BlockSpec((tm, tn), lambda i, j: (i, j))],
        out_specs=pl.BlockSpec((tm, tn), lambda i, j: (i, j)),
        scratch_shapes=[pltpu.VMEM((tm, k), BF16)],
        compiler_params=_cparams(("parallel", "arbitrary")),
        name="mm_merge",
    )(cv, ln_w.reshape(1, k), ln_b.reshape(1, k), w, bias.reshape(1, n), gates, gates, y_ssd)


def _mm_resid_kernel(a_ref, w_ref, x_ref, g_ref, o_ref):
    out = jnp.dot(a_ref[...], w_ref[...], preferred_element_type=F32)
    o_ref[...] = x_ref[...] + g_ref[0] * out


def _mm_resid(a, w, x2, mod3, gate_chunk, rows_per_batch, tm=1024, tn=512):
    m, k = a.shape
    n = w.shape[1]
    tm = min(tm, rows_per_batch)
    nj = n // tn
    tiles_per_batch = rows_per_batch // tm
    return pl.pallas_call(
        _mm_resid_kernel,
        out_shape=jax.ShapeDtypeStruct((m, n), F32),
        grid=(m // tm, nj),
        in_specs=[pl.BlockSpec((tm, k), lambda i, j: (i, 0)),
                  pl.BlockSpec((k, tn), lambda i, j: (0, j)),
                  pl.BlockSpec((tm, tn), lambda i, j: (i, j)),
                  pl.BlockSpec((1, 1, tn),
                               lambda i, j: (i // tiles_per_batch, 0, gate_chunk * nj + j))],
        out_specs=pl.BlockSpec((tm, tn), lambda i, j: (i, j)),
        compiler_params=_cparams(("parallel", "parallel")),
        name="mm_resid",
    )(a, w, x2, mod3)


def _split_w_kernel(w_ref, *o_refs, bounds):
    for o_ref, (lo, hi) in zip(o_refs, bounds):
        width = o_ref.shape[1]
        if hi - lo == width:
            o_ref[...] = w_ref[:, lo:hi].astype(o_ref.dtype)
        else:
            lane = lax.broadcasted_iota(jnp.int32, o_ref.shape, 1)
            o_ref[...] = jnp.where(lane < hi - lo, w_ref[:, lo:lo + width], 0.0).astype(o_ref.dtype)


def _split_w(w, bounds, tr=128):
    k, n = w.shape
    widths = [max(hi - lo, LANE) for lo, hi in bounds]
    return pl.pallas_call(
        functools.partial(_split_w_kernel, bounds=tuple(bounds)),
        out_shape=tuple(jax.ShapeDtypeStruct((k, wd), BF16) for wd in widths),
        grid=(k // tr,),
        in_specs=[pl.BlockSpec((tr, n), lambda i: (i, 0))],
        out_specs=tuple(pl.BlockSpec((tr, wd), lambda i: (i, 0)) for wd in widths),
        compiler_params=_cparams(("parallel",)),
        name="split_w",
    )(w)


_CONV_PAD = 8


def _conv7_kernel(ctx_ref, lat_ref, w_ref, b_ref, o_ref, pad_ref, *, l_ctx, l_lat):
    p = _CONV_PAD
    zeros = jnp.zeros((p, LANE), F32)
    off_ctx = p
    off_lat = 2 * p + l_ctx
    pad_ref[0:p, :] = zeros
    pad_ref[off_ctx + l_ctx:off_lat, :] = zeros
    pad_ref[off_lat + l_lat:off_lat + l_lat + p, :] = zeros
    pad_ref[off_ctx:off_ctx + l_ctx, :] = ctx_ref[0]
    pad_ref[off_lat:off_lat + l_lat, :] = lat_ref[0]
    reach = SSM_CONV // 2
    bias = b_ref[...]

    def chunk(pad_base, out_base):
        acc = jnp.broadcast_to(bias, (CHUNK, LANE))
        for k in range(SSM_CONV):
            tap = pad_ref[pl.ds(pad_base - reach + k, CHUNK), :]
            acc = acc + tap * w_ref[k:k + 1, :]
        o_ref[0, 0, pl.ds(out_base, CHUNK), :] = _silu(acc)

    def ctx_body(j, c):
        base = pl.multiple_of(j * CHUNK, CHUNK)
        chunk(off_ctx + base, base)
        return c

    def lat_body(j, c):
        base = pl.multiple_of(j * CHUNK, CHUNK)
        chunk(off_lat + base, l_ctx + base)
        return c

    lax.fori_loop(0, l_ctx // CHUNK, ctx_body, 0)
    lax.fori_loop(0, l_lat // CHUNK, lat_body, 0, unroll=2)


def _conv7(ctx_raw, lat_raw, w, b):
    bsz, l_ctx, c = ctx_raw.shape
    l_lat = lat_raw.shape[1]
    ltot = l_ctx + l_lat
    nct = c // LANE
    return pl.pallas_call(
        functools.partial(_conv7_kernel, l_ctx=l_ctx, l_lat=l_lat),
        out_shape=jax.ShapeDtypeStruct((bsz, nct, ltot, LANE), F32),
        grid=(bsz, nct),
        in_specs=[pl.BlockSpec((1, l_ctx, LANE), lambda bi, ci: (bi, 0, ci)),
                  pl.BlockSpec((1, l_lat, LANE), lambda bi, ci: (bi, 0, ci)),
                  pl.BlockSpec((SSM_CONV, LANE), lambda bi, ci: (0, ci)),
                  pl.BlockSpec((1, LANE), lambda bi, ci: (0, ci))],
        out_specs=pl.BlockSpec((1, 1, ltot, LANE), lambda bi, ci: (bi, ci, 0, 0)),
        scratch_shapes=[pltpu.VMEM((ltot + 3 * _CONV_PAD, LANE), F32)],
        compiler_params=_cparams(("parallel", "parallel")),
        name="conv7",
    )(ctx_raw, lat_raw, w, b.reshape(1, c))


def _ssd_kernel(xbc_ref, dtc_ref, dtl_ref, par_ref, dexp_ref, ex_ref, *rest, reverse, n_ctx, fuse_norm):
    if fuse_norm:
        yo_ref, sz_ref, nw_ref, o_ref, st_ref, cumt_ref, y_ref = rest
    else:
        y_ref, st_ref, cumt_ref = rest
    i = pl.program_id(1)

    @pl.when(i == 0)
    def _():
        st_ref[...] = jnp.zeros_like(st_ref)

    dt_raw = jnp.where(i < n_ctx, dtc_ref[0], dtl_ref[0])
    bias = par_ref[0:1, :]
    a = -jnp.exp(par_ref[1:2, :])
    dt = jax.nn.softplus(dt_raw + bias)
    cum = dt * a
    row = lax.broadcasted_iota(jnp.int32, (CHUNK, LANE), 0)
    k = 1
    while k < CHUNK:
        if reverse:
            cum = cum + jnp.where(row < CHUNK - k, pltpu.roll(cum, CHUNK - k, 0), 0.0)
        else:
            cum = cum + jnp.where(row >= k, pltpu.roll(cum, k, 0), 0.0)
        k *= 2
    last = 0 if reverse else CHUNK - 1
    cum = cum * LOG2E
    cumt_ref[...] = cum.T
    li = lax.broadcasted_iota(jnp.int32, (CHUNK, CHUNK), 0)
    si = lax.broadcasted_iota(jnp.int32, (CHUNK, CHUNK), 1)
    causal = (li <= si) if reverse else (li >= si)
    lo = lax.broadcasted_iota(jnp.int32, (CHUNK, LANE), 1) < HEAD_DIM
    heads_per_group = N_HEADS // N_GROUPS
    pairs = heads_per_group // 2
    x_tiles = N_HEADS // 2

    def group(g, carry):
        shift = (LANE - heads_per_group * g) & (LANE - 1)
        cum_g = pltpu.roll(cum, shift, 1)
        dt_g = pltpu.roll(dt, shift, 1)
        cum_t = cumt_ref[pl.ds(pl.multiple_of(heads_per_group * g, heads_per_group), heads_per_group), :]
        bb = xbc_ref[0, x_tiles + g].astype(BF16)
        cb = xbc_ref[0, x_tiles + N_GROUPS + g].astype(BF16)
        scores = lax.dot_general(cb, bb, (((1,), (1,)), ((), ())), preferred_element_type=F32)
        h_t = st_ref[g]
        y_off = jnp.dot(cb, h_t.astype(BF16), preferred_element_type=F32)
        d_hi = dt_g.astype(BF16)
        r_hi = dt_g - d_hi.astype(F32)
        d_mid = r_hi.astype(BF16)
        d_lo = (r_hi - d_mid.astype(F32)).astype(BF16)
        dt_x = (jnp.dot(jnp.concatenate([d_hi, d_mid], axis=1), ex_ref[...], preferred_element_type=F32)
                + jnp.dot(d_lo, ex_ref[0:LANE, :], preferred_element_type=F32))
        xw_parts, dec_parts = [], []
        for p in range(pairs):
            j0, j1 = 2 * p, 2 * p + 1
            x2 = xbc_ref[0, pairs * g + p]
            c0 = cum_g[:, j0:j0 + 1]
            c1 = cum_g[:, j1:j1 + 1]
            l0 = jnp.exp2(jnp.where(causal, c0 - cum_t[j0:j0 + 1, :], -jnp.inf))
            l1 = jnp.exp2(jnp.where(causal, c1 - cum_t[j1:j1 + 1, :], -jnp.inf))
            m0 = (scores * l0).astype(BF16)
            m1 = (scores * l1).astype(BF16)
            dt2 = dt_x[:, p * LANE:(p + 1) * LANE]
            c2 = jnp.where(lo, c0, c1)
            xdt = x2 * dt2
            xdt_b = xdt.astype(BF16)
            zero = jnp.zeros_like(xdt_b)
            y_diag = (jnp.dot(m0, jnp.where(lo, xdt_b, zero), preferred_element_type=F32)
                      + jnp.dot(m1, jnp.where(lo, zero, xdt_b), preferred_element_type=F32))
            e2 = jnp.exp2(c2)
            y = y_diag + y_off[:, p * LANE:(p + 1) * LANE] * e2
            y_ref[0, pairs * g + p] = y + dexp_ref[pairs * g + p] * x2
            to_end = jnp.exp2(c2[last:last + 1, :] - c2)
            xw_parts.append((xdt * to_end).astype(BF16))
            dec_parts.append(e2[last:last + 1, :])
        xw = jnp.concatenate(xw_parts, axis=1)
        dec = jnp.concatenate(dec_parts, axis=1)
        upd = lax.dot_general(bb, xw, (((0,), (0,)), ((), ())), preferred_element_type=F32)
        st_ref[g] = h_t * dec + upd
        return carry

    lax.fori_loop(0, N_GROUPS, group, 0, unroll=2)

    if fuse_norm:
        sq = jnp.zeros((CHUNK, LANE), F32)
        for j in range(x_tiles):
            gj = (y_ref[0, j] + yo_ref[0, j]) * sz_ref[:, j * LANE:(j + 1) * LANE]
            y_ref[0, j] = gj
            sq = sq + gj * gj
        r = lax.rsqrt(jnp.sum(sq, axis=-1, keepdims=True) / (x_tiles * LANE) + EPS)
        for j in range(x_tiles):
            sl = slice(j * LANE, (j + 1) * LANE)
            o_ref[:, sl] = (y_ref[0, j] * r * nw_ref[:, sl]).astype(o_ref.dtype)


def _ssd(xbc_act, dt_ctx, dt_lat, par, dexp, reverse, norm_with=None):
    bsz, ntile, ltot, _ = xbc_act.shape
    l_ctx = dt_ctx.shape[1]
    l_lat = dt_lat.shape[1]
    n_ctx = l_ctx // CHUNK
    n_lat = l_lat // CHUNK
    steps = n_ctx + n_lat
    x_tiles = N_HEADS // 2
    gw = (N_HEADS // N_GROUPS) * HEAD_DIM
    e1 = (jnp.arange(gw)[None, :] // HEAD_DIM == jnp.arange(LANE)[:, None]).astype(BF16)
    expand = jnp.concatenate([e1, e1], axis=0)

    if reverse:
        def cat_chunk(i):
            return jnp.where(i < n_ctx, n_ctx - 1 - i, n_ctx + steps - 1 - i)

        def ctx_chunk(i):
            return jnp.maximum(n_ctx - 1 - i, 0)

        def lat_chunk(i):
            return jnp.minimum(steps - 1 - i, n_lat - 1)
    else:
        def cat_chunk(i):
            return i

        def ctx_chunk(i):
            return jnp.minimum(i, n_ctx - 1)

        def lat_chunk(i):
            return jnp.maximum(i - n_ctx, 0)

    y_spec = pl.BlockSpec((1, x_tiles, CHUNK, LANE), lambda b, i: (b, 0, lat_chunk(i), 0))
    in_specs = [pl.BlockSpec((1, ntile, CHUNK, LANE), lambda b, i: (b, 0, cat_chunk(i), 0)),
                pl.BlockSpec((1, CHUNK, LANE), lambda b, i: (b, ctx_chunk(i), 0)),
                pl.BlockSpec((1, CHUNK, LANE), lambda b, i: (b, lat_chunk(i), 0)),
                pl.BlockSpec((8, LANE), lambda b, i: (0, 0)),
                pl.BlockSpec((x_tiles, 1, LANE), lambda b, i: (0, 0, 0)),
                pl.BlockSpec((2 * LANE, gw), lambda b, i: (0, 0))]
    args = [xbc_act, dt_ctx, dt_lat, par, dexp, expand]
    scratch = [pltpu.VMEM((N_GROUPS, D_STATE, gw), F32), pltpu.VMEM((LANE, CHUNK), F32)]
    if norm_with is None:
        out_shape = jax.ShapeDtypeStruct((bsz, x_tiles, l_lat, LANE), F32)
        out_spec = y_spec
    else:
        y_other, silu_z, norm_w = norm_with
        dn = x_tiles * LANE
        row_spec = pl.BlockSpec((CHUNK, dn), lambda b, i: (b * n_lat + lat_chunk(i), 0))
        in_specs += [y_spec, row_spec, pl.BlockSpec((1, dn), lambda b, i: (0, 0))]
        args += [y_other, silu_z, norm_w.reshape(1, dn)]
        out_shape = jax.ShapeDtypeStruct((bsz * l_lat, dn), BF16)
        out_spec = row_spec
        scratch.append(pltpu.VMEM((1, x_tiles, CHUNK, LANE), F32))
    return pl.pallas_call(
        functools.partial(_ssd_kernel, reverse=reverse, n_ctx=n_ctx, fuse_norm=norm_with is not None),
        out_shape=out_shape,
        grid=(bsz, steps),
        in_specs=in_specs,
        out_specs=out_spec,
        scratch_shapes=scratch,
        compiler_params=_cparams(("parallel", "arbitrary")),
        name="ssd_bwd" if reverse else "ssd_fwd",
    )(*args)


def _conv31_kernel(u_ref, w_ref, b_ref, o_ref, pad_ref, *, seq):
    halo = (CF_KERNEL // 2) * GRID_W
    zeros = jnp.zeros((halo, LANE), F32)
    pad_ref[0:halo, :] = zeros
    pad_ref[halo + seq:halo + seq + halo, :] = zeros
    pad_ref[halo:halo + seq, :] = u_ref[0]
    bias = b_ref[...]

    def body(j, c):
        base = pl.multiple_of(j * CHUNK, CHUNK)
        acc = jnp.broadcast_to(bias, (CHUNK, LANE))
        for k in range(CF_KERNEL):
            tap = pad_ref[pl.ds(pl.multiple_of(base + k * GRID_W, GRID_W), CHUNK), :]
            acc = acc + tap * w_ref[k:k + 1, :]
        o_ref[0, pl.ds(base, CHUNK), :] = acc
        return c

    lax.fori_loop(0, seq // CHUNK, body, 0, unroll=2)


def _conv31(u3, w, b):
    bsz, s, c = u3.shape
    halo = (CF_KERNEL // 2) * GRID_W
    return pl.pallas_call(
        functools.partial(_conv31_kernel, seq=s),
        out_shape=jax.ShapeDtypeStruct((bsz, s, c), F32),
        grid=(bsz, c // LANE),
        in_specs=[pl.BlockSpec((1, s, LANE), lambda bi, ci: (bi, 0, ci)),
                  pl.BlockSpec((CF_KERNEL, LANE), lambda bi, ci: (0, ci)),
                  pl.BlockSpec((1, LANE), lambda bi, ci: (0, ci))],
        out_specs=pl.BlockSpec((1, s, LANE), lambda bi, ci: (bi, 0, ci)),
        scratch_shapes=[pltpu.VMEM((s + 2 * halo, LANE), F32)],
        compiler_params=_cparams(("parallel", "parallel")),
        name="conv31",
    )(u3, w, b.reshape(1, c))


def _route_kernel(x_ref, w_ref, sh_ref, sc_ref, rw_ref, rb_ref, h_ref, eid_ref, ew_ref):
    xf = x_ref[...]
    ms = jnp.mean(xf * xf, axis=-1, keepdims=True)
    h = xf * lax.rsqrt(ms + EPS) * w_ref[...]
    h = h * (1.0 + sc_ref[0]) + sh_ref[0]
    tm = xf.shape[0]
    nt = xf.shape[1] // LANE
    pitch = _pitch(nt)
    for j in range(nt):
        h_ref[pl.ds(j, tm, stride=pitch), :] = h[:, j * LANE:(j + 1) * LANE]
    for j in range(nt, pitch):
        h_ref[pl.ds(j, tm, stride=pitch), :] = jnp.zeros((tm, LANE), F32)
    logits = jnp.dot(h.astype(BF16), rw_ref[...], preferred_element_type=F32) + rb_ref[...]
    lane = lax.broadcasted_iota(jnp.int32, (tm, LANE), 1)
    lane_f = lane.astype(F32)
    ninf = -jnp.inf
    gl = jnp.where(lane < MOE_GROUPS, logits, ninf)
    gmax = jnp.max(gl, axis=-1, keepdims=True)
    gidx = jnp.min(jnp.where(gl == gmax, lane_f, float(LANE)), axis=-1, keepdims=True)
    gsum = jnp.sum(jnp.exp(gl - gmax), axis=-1, keepdims=True)
    g_p = 1.0 / gsum
    first = float(MOE_GROUPS) + gidx * float(EXPERTS_PER_GROUP)
    in_group = (lane_f >= first) & (lane_f < first + float(EXPERTS_PER_GROUP))
    el = jnp.where(in_group, logits, ninf)
    m1 = jnp.max(el, axis=-1, keepdims=True)
    i1 = jnp.min(jnp.where(el == m1, lane_f, float(LANE)), axis=-1, keepdims=True)
    el2 = jnp.where(lane_f == i1, ninf, el)
    m2 = jnp.max(el2, axis=-1, keepdims=True)
    i2 = jnp.min(jnp.where(el2 == m2, lane_f, float(LANE)), axis=-1, keepdims=True)
    e21 = jnp.exp(m2 - m1)
    den = 1.0 + e21
    w1 = (1.0 / den) * g_p
    w2 = (e21 / den) * g_p
    e1 = (i1 - float(MOE_GROUPS)).astype(jnp.int32)
    e2 = (i2 - float(MOE_GROUPS)).astype(jnp.int32)
    eid_ref[...] = jnp.where(lane == 0, e1, jnp.where(lane == 1, e2, 0))
    ew_ref[...] = jnp.where(lane == 0, w1, jnp.where(lane == 1, w2, 0.0))


def _route(x2, w, mod3, shift_chunk, scale_chunk, rows_per_batch, rw, rb, tm=256):
    m, d = x2.shape
    pitch = _pitch(d // LANE)
    tiles_per_batch = rows_per_batch // tm
    return pl.pallas_call(
        _route_kernel,
        out_shape=(jax.ShapeDtypeStruct((m * pitch, LANE), F32),
                   jax.ShapeDtypeStruct((m, LANE), jnp.int32),
                   jax.ShapeDtypeStruct((m, LANE), F32)),
        grid=(m // tm,),
        in_specs=[pl.BlockSpec((tm, d), lambda i: (i, 0)),
                  pl.BlockSpec((1, d), lambda i: (0, 0)),
                  pl.BlockSpec((1, 1, d), lambda i: (i // tiles_per_batch, 0, shift_chunk)),
                  pl.BlockSpec((1, 1, d), lambda i: (i // tiles_per_batch, 0, scale_chunk)),
                  pl.BlockSpec((d, LANE), lambda i: (0, 0)),
                  pl.BlockSpec((1, LANE), lambda i: (0, 0))],
        out_specs=(pl.BlockSpec((tm * pitch, LANE), lambda i: (i, 0)),
                   pl.BlockSpec((tm, LANE), lambda i: (i, 0)),
                   pl.BlockSpec((tm, LANE), lambda i: (i, 0))),
        compiler_params=_cparams(("parallel",)),
        name="route",
    )(x2, w.reshape(1, d), mod3, mod3, rw, rb)


_DMA_UNROLL = 8


def _rows_to_matrix(ref, tm, nt):
    return jnp.concatenate([ref[pl.ds(j, tm, stride=_pitch(nt)), :] for j in range(nt)], axis=1)


def _token_copy(src, dst, sem, nt, rows, src_tok, dst_tok):
    pitch = _pitch(nt)
    return pltpu.make_async_copy(src.at[pl.ds(src_tok * pitch, rows), :],
                                 dst.at[pl.ds(dst_tok * pitch, rows), :], sem)


def _bulk_wait(src, dst, sem, total_rows):
    pltpu.make_async_copy(src.at[pl.ds(0, total_rows), :], dst.at[pl.ds(0, total_rows), :], sem).wait()


def _for_rows(n, body):
    groups = lax.shift_right_logical(n, _DMA_UNROLL.bit_length() - 1)

    def group(g, c):
        for u in range(_DMA_UNROLL):
            body(g * _DMA_UNROLL + u)
        return c

    def tail(r, c):
        body(r)
        return c

    lax.fori_loop(0, groups, group, 0)
    lax.fori_loop(groups * _DMA_UNROLL, n, tail, 0)


def _wait_rows(n, wait_tokens):
    p = MOE_BLOCK
    while p >= 1:
        @pl.when((n & p) != 0)
        def _(p=p):
            wait_tokens(p)
        p //= 2


def _stream_expert_weights(b, be_ref, eord_ref, enext_ref, w_hbms, w_bufs, w_caches, wsem):
    prev = jnp.maximum(b - 1, 0)

    def copies(e, slot):
        return [pltpu.make_async_copy(w.at[e], buf.at[slot], wsem.at[slot]) for w, buf in zip(w_hbms, w_bufs)]

    @pl.when(b == 0)
    def _():
        for cp in copies(be_ref[0], 0):
            cp.start()

    @pl.when((b == 0) | (be_ref[b] != be_ref[prev]))
    def _():
        for s in range(2):
            @pl.when((eord_ref[b] & 1) == s)
            def _(s=s):
                for cp in copies(be_ref[b], s):
                    cp.wait()

                @pl.when(enext_ref[b] >= 0)
                def _():
                    for cp in copies(enext_ref[b], 1 - s):
                        cp.start()

                for buf, cache in zip(w_bufs, w_caches):
                    cache[...] = buf[s].astype(BF16)


def _expert_up_kernel(be_ref, nv_ref, nused_ref, eord_ref, enext_ref, tokc_ref, tokn_ref, h_hbm, wg_hbm, wu_hbm,
                      o_ref, xs0_ref, xs1_ref, wgs_ref, wus_ref, wgb_ref, wub_ref, sem, wsem, *, fchunk, nt):
    b = pl.program_id(0)
    n_used = nused_ref[0]
    dff = wgb_ref.shape[1]
    slots = (xs0_ref, xs1_ref)

    def start_gather(tok_ref, blk, slot):
        _for_rows(nv_ref[blk], lambda r: _token_copy(
            h_hbm, slots[slot], sem.at[slot], nt, nt, tok_ref[0, 0, r], r).start())

    def wait_gather(blk, slot):
        _wait_rows(nv_ref[blk], lambda p: _bulk_wait(h_hbm, slots[slot], sem.at[slot], p * nt))

    @pl.when(b == 0)
    def _():
        xs0_ref[...] = jnp.zeros_like(xs0_ref)
        xs1_ref[...] = jnp.zeros_like(xs1_ref)
        start_gather(tokc_ref, 0, 0)

    for slot in range(2):
        @pl.when((b + 1 < n_used) & (lax.rem(b, 2) == slot))
        def _(slot=slot):
            start_gather(tokn_ref, b + 1, 1 - slot)

    @pl.when(b < n_used)
    def _():
        _stream_expert_weights(b, be_ref, eord_ref, enext_ref, (wg_hbm, wu_hbm), (wgs_ref, wus_ref),
                               (wgb_ref, wub_ref), wsem)

        for slot in range(2):
            @pl.when(lax.rem(b, 2) == slot)
            def _(slot=slot):
                wait_gather(b, slot)
                xb = _rows_to_matrix(slots[slot], MOE_BLOCK, nt).astype(BF16)
                for f in range(dff // fchunk):
                    sl = slice(f * fchunk, (f + 1) * fchunk)
                    gate = jnp.dot(xb, wgb_ref[:, sl], preferred_element_type=F32)
                    up = jnp.dot(xb, wub_ref[:, sl], preferred_element_type=F32)
                    o_ref[:, sl] = (_silu(gate) * up).astype(o_ref.dtype)

    @pl.when(b >= n_used)
    def _():
        o_ref[...] = jnp.zeros_like(o_ref)


def _expert_up(h2t, buf_tok, w_gate, w_up, tables, fchunk=256):
    n_blocks = buf_tok.shape[0]
    _, d, dff = w_gate.shape
    nt = d // LANE
    slot_rows = MOE_BLOCK * _pitch(nt)
    grid_spec = pltpu.PrefetchScalarGridSpec(
        num_scalar_prefetch=len(tables),
        grid=(n_blocks,),
        in_specs=[pl.BlockSpec((1, 1, MOE_BLOCK), lambda b, *_: (b, 0, 0), memory_space=pltpu.SMEM),
                  pl.BlockSpec((1, 1, MOE_BLOCK), lambda b, *_: (jnp.minimum(b + 1, n_blocks - 1), 0, 0),
                               memory_space=pltpu.SMEM),
                  pl.BlockSpec(memory_space=pl.ANY),
                  pl.BlockSpec(memory_space=pl.ANY),
                  pl.BlockSpec(memory_space=pl.ANY)],
        out_specs=pl.BlockSpec((MOE_BLOCK, dff), lambda b, *_: (b, 0)),
        scratch_shapes=[pltpu.VMEM((slot_rows, LANE), F32),
                        pltpu.VMEM((slot_rows, LANE), F32),
                        pltpu.VMEM((2, d, dff), F32),
                        pltpu.VMEM((2, d, dff), F32),
                        pltpu.VMEM((d, dff), BF16),
                        pltpu.VMEM((d, dff), BF16),
                        pltpu.SemaphoreType.DMA((2,)),
                        pltpu.SemaphoreType.DMA((2,))],
    )
    return pl.pallas_call(
        functools.partial(_expert_up_kernel, fchunk=fchunk, nt=nt),
        out_shape=jax.ShapeDtypeStruct((n_blocks * MOE_BLOCK, dff), BF16),
        grid_spec=grid_spec,
        compiler_params=_cparams(("arbitrary",)),
        name="expert_up",
    )(*tables, buf_tok, buf_tok, h2t, w_gate, w_up)


def _expert_down_kernel(be_ref, nv_ref, nused_ref, eord_ref, enext_ref, asg_ref, h_ref, wd_hbm, y_hbm,
                        ys0_ref, ys1_ref, wds_ref, wdb_ref, sem, wsem, *, nchunk, nt, n_tok):
    b = pl.program_id(0)
    n_used = nused_ref[0]
    d = wdb_ref.shape[1]
    slots = (ys0_ref, ys1_ref)
    pitch = _pitch(nt)

    def row_copy(slot, r, assign):
        k = assign & 1
        tok = lax.shift_right_logical(assign, 1)
        return _token_copy(slots[slot], y_hbm, sem.at[slot], nt, pitch, r, k * n_tok + tok)

    def start_scatter(blk, slot):
        _for_rows(nv_ref[blk], lambda r: row_copy(slot, r, asg_ref[0, 0, r]).start())

    def wait_scatter(blk, slot):
        _wait_rows(nv_ref[blk], lambda p: _bulk_wait(slots[slot], y_hbm, sem.at[slot], p * pitch))

    @pl.when(b == 0)
    def _():
        ys0_ref[...] = jnp.zeros_like(ys0_ref)
        ys1_ref[...] = jnp.zeros_like(ys1_ref)

    @pl.when(b < n_used)
    def _():
        _stream_expert_weights(b, be_ref, eord_ref, enext_ref, (wd_hbm,), (wds_ref,), (wdb_ref,), wsem)
        hb = h_ref[...]
        for slot in range(2):
            @pl.when(lax.rem(b, 2) == slot)
            def _(slot=slot):
                for c in range(d // nchunk):
                    out = jnp.dot(hb, wdb_ref[:, c * nchunk:(c + 1) * nchunk], preferred_element_type=F32)
                    for j in range(nchunk // LANE):
                        slots[slot][pl.ds(c * (nchunk // LANE) + j, MOE_BLOCK, stride=pitch), :] = (
                            out[:, j * LANE:(j + 1) * LANE])

    for slot in range(2):
        @pl.when((b >= 1) & (b - 1 < n_used) & (lax.rem(b, 2) == slot))
        def _(slot=slot):
            wait_scatter(b - 1, 1 - slot)

        @pl.when((b < n_used) & (lax.rem(b, 2) == slot))
        def _(slot=slot):
            start_scatter(b, slot)

        @pl.when((b == pl.num_programs(0) - 1) & (b < n_used) & (lax.rem(b, 2) == slot))
        def _(slot=slot):
            wait_scatter(b, slot)


def _expert_down(hid, buf_assign, w_down, tables, n_tok, nchunk=512):
    n_rows, dff = hid.shape
    n_blocks = n_rows // MOE_BLOCK
    d = w_down.shape[2]
    nt = d // LANE
    pitch = _pitch(nt)
    grid_spec = pltpu.PrefetchScalarGridSpec(
        num_scalar_prefetch=len(tables),
        grid=(n_blocks,),
        in_specs=[pl.BlockSpec((1, 1, MOE_BLOCK), lambda b, *_: (b, 0, 0), memory_space=pltpu.SMEM),
                  pl.BlockSpec((MOE_BLOCK, dff), lambda b, be, nv, n, *_: (jnp.minimum(b, n[0] - 1), 0)),
                  pl.BlockSpec(memory_space=pl.ANY)],
        out_specs=pl.BlockSpec(memory_space=pl.ANY),
        scratch_shapes=[pltpu.VMEM((MOE_BLOCK * pitch, LANE), F32),
                        pltpu.VMEM((MOE_BLOCK * pitch, LANE), F32),
                        pltpu.VMEM((2, dff, d), F32),
                        pltpu.VMEM((dff, d), BF16),
                        pltpu.SemaphoreType.DMA((2,)),
                        pltpu.SemaphoreType.DMA((2,))],
    )
    return pl.pallas_call(
        functools.partial(_expert_down_kernel, nchunk=nchunk, nt=nt, n_tok=n_tok),
        out_shape=jax.ShapeDtypeStruct((2 * n_tok * pitch, LANE), F32),
        grid_spec=grid_spec,
        compiler_params=_cparams(("arbitrary",)),
        name="expert_down",
    )(*tables, buf_assign, hid, w_down)


def _combine_kernel(y0_ref, y1_ref, ew_ref, x_ref, g_ref, w_ref, o_ref):
    tm, d = x_ref.shape
    nt = d // LANE
    ew = ew_ref[...]
    moe = (_rows_to_matrix(y0_ref, tm, nt) * ew[:, 0:1]
           + _rows_to_matrix(y1_ref, tm, nt) * ew[:, 1:2])
    xo = x_ref[...] + g_ref[0] * moe
    ms = jnp.mean(xo * xo, axis=-1, keepdims=True)
    o_ref[...] = xo * lax.rsqrt(ms + EPS) * w_ref[...]


def _combine(y, ew, x2, mod3, gate_chunk, rows_per_batch, final_w, tm=256):
    m, d = x2.shape
    pitch = _pitch(d // LANE)
    tiles = m // tm
    tiles_per_batch = rows_per_batch // tm
    return pl.pallas_call(
        _combine_kernel,
        out_shape=jax.ShapeDtypeStruct((m, d), F32),
        grid=(tiles,),
        in_specs=[pl.BlockSpec((tm * pitch, LANE), lambda i: (i, 0)),
                  pl.BlockSpec((tm * pitch, LANE), lambda i: (tiles + i, 0)),
                  pl.BlockSpec((tm, LANE), lambda i: (i, 0)),
                  pl.BlockSpec((tm, d), lambda i: (i, 0)),
                  pl.BlockSpec((1, 1, d), lambda i: (i // tiles_per_batch, 0, gate_chunk)),
                  pl.BlockSpec((1, d), lambda i: (0, 0))],
        out_specs=pl.BlockSpec((tm, d), lambda i: (i, 0)),
        compiler_params=_cparams(("parallel",)),
        name="moe_combine",
    )(y, y, ew, x2, mod3, final_w.reshape(1, d))


def _dispatch_tables(eid, n_tok):
    top_k = eid.shape[1]
    n_assign = n_tok * top_k
    expert = eid.reshape(-1)
    key = jnp.sort(expert * n_assign + jnp.arange(n_assign, dtype=jnp.int32))
    sorted_assign = key % n_assign
    bounds = jnp.arange(N_EXPERTS + 1, dtype=jnp.int32) * n_assign
    start = jnp.searchsorted(key, bounds, side="left").astype(jnp.int32)
    counts = start[1:] - start[:-1]
    nblk = (counts + MOE_BLOCK - 1) // MOE_BLOCK
    blk_end = jnp.cumsum(nblk)
    blk_start = blk_end - nblk
    n_blocks = -(-n_assign // MOE_BLOCK) + N_EXPERTS
    bidx = jnp.arange(n_blocks, dtype=jnp.int32)
    block_expert = jnp.minimum(jnp.searchsorted(blk_end, bidx, side="right"), N_EXPERTS - 1).astype(jnp.int32)
    in_expert = (bidx - blk_start[block_expert]) * MOE_BLOCK
    n_valid = jnp.clip(counts[block_expert] - in_expert, 0, MOE_BLOCK).astype(jnp.int32)
    src = start[block_expert][:, None] + in_expert[:, None] + jnp.arange(MOE_BLOCK, dtype=jnp.int32)[None, :]
    valid = jnp.arange(MOE_BLOCK, dtype=jnp.int32)[None, :] < n_valid[:, None]
    buf_assign = jnp.where(valid, sorted_assign[jnp.clip(src, 0, n_assign - 1)], 0).astype(jnp.int32)
    n_used = blk_end[-1].astype(jnp.int32)
    first = jnp.concatenate([jnp.ones((1,), jnp.int32),
                             (block_expert[1:] != block_expert[:-1]).astype(jnp.int32)])
    expert_ordinal = (jnp.cumsum(first) - 1).astype(jnp.int32)
    next_blk = blk_end[block_expert]
    next_expert = jnp.where(next_blk < n_used, block_expert[jnp.minimum(next_blk, n_blocks - 1)], -1).astype(jnp.int32)
    tables = (block_expert, n_valid, n_used.reshape(1), expert_ordinal, next_expert)
    return buf_assign.reshape(n_blocks, 1, MOE_BLOCK), tables


def kernel(x, c, ctx, c_ctx, ada_w, ada_b, norm1_w, w_in, ssm_conv_w, ssm_conv_b, dt_bias, a_log, d_skip, ssm_norm_w, ssm_out_w, cf_dw_w, cf_dw_b, cf_ln_w, cf_ln_b, cf_out_w, cf_out_b, w_o, norm2_w, router_group_w, router_group_b, router_expert_w, router_expert_b, expert_w_gate, expert_w_up, expert_w_down, final_norm_w):
    bsz, seq, d = x.shape
    l_ctx = ctx.shape[1]
    n_tok = bsz * seq
    d_inner = ssm_norm_w.shape[1]
    gn = N_GROUPS * D_STATE
    xbc_dim = d_inner + 2 * gn
    off_dt = xbc_dim
    off_z = off_dt + N_HEADS
    off_glu = off_z + d_inner
    off_gate = off_glu + 2 * d

    ctx_row = bsz
    crows = jnp.zeros((8, d), F32).at[:bsz].set(c).at[ctx_row].set(c_ctx)
    mod = _ada(crows, ada_w[0], ada_b[0])
    mod3 = mod.reshape(8, 1, 6 * d)
    lat_rows = jnp.arange(bsz, dtype=jnp.int32)
    ctx_rows = jnp.full((bsz,), ctx_row, jnp.int32)

    h_lat = _normmod(x, norm1_w[0], mod3, lat_rows, 0, 1, BF16).reshape(n_tok, d)
    h_ctx = _normmod(ctx, norm1_w[0], mod3, ctx_rows, 0, 1, BF16).reshape(bsz * l_ctx, d)

    w_xbc, w_dt, w_z, w_glu_a, w_glu_b, w_gate = _split_w(
        w_in[0], [(0, xbc_dim), (off_dt, off_z), (off_z, off_glu), (off_glu, off_glu + d),
                  (off_glu + d, off_gate), (off_gate, off_gate + 2 * d)])

    xbc_lat = _mm(h_lat, w_xbc, tn=2048, name="in_xbc").reshape(bsz, seq, xbc_dim)
    xbc_ctx = _mm(h_ctx, w_xbc, tm=512, name="in_xbc_ctx").reshape(bsz, l_ctx, xbc_dim)
    dt_lat = _mm(h_lat, w_dt, name="in_dt").reshape(bsz, seq, LANE)
    dt_ctx = _mm(h_ctx, w_dt, tm=512, name="in_dt_ctx").reshape(bsz, l_ctx, LANE)
    sz = _mm(h_lat, w_z, act="silu", tn=2048, name="in_z")
    u = _mm_glu(h_lat, w_glu_a, w_glu_b, tn=1024)
    gates = _mm(h_lat, w_gate, act="sigmoid", tn=2048, name="in_gate")

    xbc_act = _conv7(xbc_ctx, xbc_lat, ssm_conv_w[0], ssm_conv_b[0])

    def ssd_params(k):
        par = jnp.zeros((8, LANE), F32).at[0, :N_HEADS].set(dt_bias[0, k]).at[1, :N_HEADS].set(a_log[0, k])
        return par, jnp.repeat(d_skip[0, k], HEAD_DIM).reshape(N_HEADS // 2, 1, LANE)

    y_bwd = _ssd(xbc_act, dt_ctx, dt_lat, *ssd_params(1), reverse=True)
    gnorm = _ssd(xbc_act, dt_ctx, dt_lat, *ssd_params(0), reverse=False,
                 norm_with=(y_bwd, sz, ssm_norm_w[0]))
    y_ssd = _mm(gnorm, ssm_out_w[0].astype(BF16), tn=512, name="ssm_out")

    cv = _conv31(u.reshape(bsz, seq, d), cf_dw_w[0], cf_dw_b[0]).reshape(n_tok, d)
    merged = _mm_merge(cv, cf_ln_w[0], cf_ln_b[0], cf_out_w[0].astype(BF16), cf_out_b[0], gates, y_ssd)
    x1 = _mm_resid(merged, w_o[0].astype(BF16), x.reshape(n_tok, d), mod3, 2, seq)

    n_r = MOE_GROUPS + N_EXPERTS
    rw = jnp.pad(jnp.concatenate([router_group_w[0], router_expert_w[0]], axis=1),
                 ((0, 0), (0, LANE - n_r))).astype(BF16)
    rb = jnp.pad(jnp.concatenate([router_group_b[0], router_expert_b[0]]), (0, LANE - n_r)).reshape(1, LANE)
    h2t, eid, ew = _route(x1, norm2_w[0], mod3, 3, 4, seq, rw, rb)

    buf_assign, tables = _dispatch_tables(eid[:, :2], n_tok)
    hid = _expert_up(h2t, buf_assign // 2, expert_w_gate[0], expert_w_up[0], tables)
    y = _expert_down(hid, buf_assign, expert_w_down[0], tables, n_tok)
    out = _combine(y, ew, x1, mod3, 5, seq, final_norm_w)
    return out.reshape(bsz, seq, d)
```

```python
import functools

import jax
import jax.numpy as jnp
from jax import lax
from jax.experimental import pallas as pl
from jax.experimental.pallas import tpu as pltpu

F32 = jnp.float32
BF16 = jnp.bfloat16

EPS = 1e-6
GRID_W = 64
HEAD_DIM = 64
N_HEADS = 64
N_GROUPS = 8
D_STATE = 128
CHUNK = 128
SSM_CONV = 7
CF_KERNEL = 31
MOE_GROUPS = 8
EXPERTS_PER_GROUP = 8
N_EXPERTS = 64
MOE_BLOCK = 256
LANE = 128
LOG2E = 1.4426950408889634
VMEM_LIMIT = 56 * 1024 * 1024


def _cparams(sem):
    return pltpu.CompilerParams(dimension_semantics=sem, vmem_limit_bytes=VMEM_LIMIT)


def _silu(v):
    return v * jax.nn.sigmoid(v)


def _pitch(nt):
    return nt + 1


def _ada_kernel(c_ref, w_ref, b_ref, o_ref):
    s = _silu(c_ref[...])
    o_ref[...] = jnp.dot(s.astype(BF16), w_ref[...].astype(BF16),
                         preferred_element_type=F32) + b_ref[...]


def _ada(crows, ada_w, ada_b, tn=1024):
    r, d = crows.shape
    n = ada_w.shape[1]
    return pl.pallas_call(
        _ada_kernel,
        out_shape=jax.ShapeDtypeStruct((r, n), F32),
        grid=(n // tn,),
        in_specs=[pl.BlockSpec((r, d), lambda j: (0, 0)),
                  pl.BlockSpec((d, tn), lambda j: (0, j)),
                  pl.BlockSpec((1, tn), lambda j: (0, j))],
        out_specs=pl.BlockSpec((r, tn), lambda j: (0, j)),
        compiler_params=_cparams(("parallel",)),
        name="ada",
    )(crows, ada_w, ada_b.reshape(1, n))


def _normmod_kernel(rows_ref, x_ref, w_ref, sh_ref, sc_ref, o_ref):
    del rows_ref
    xf = x_ref[0]
    ms = jnp.mean(xf * xf, axis=-1, keepdims=True)
    y = xf * lax.rsqrt(ms + EPS) * w_ref[...]
    o_ref[0] = (y * (1.0 + sc_ref[0]) + sh_ref[0]).astype(o_ref.dtype)


def _normmod(x3, w, mod3, rows, shift_chunk, scale_chunk, out_dtype, tm=256):
    bx, l, d = x3.shape
    grid_spec = pltpu.PrefetchScalarGridSpec(
        num_scalar_prefetch=1,
        grid=(bx, l // tm),
        in_specs=[pl.BlockSpec((1, tm, d), lambda b, i, r: (b, i, 0)),
                  pl.BlockSpec((1, d), lambda b, i, r: (0, 0)),
                  pl.BlockSpec((1, 1, d), lambda b, i, r: (r[b], 0, shift_chunk)),
                  pl.BlockSpec((1, 1, d), lambda b, i, r: (r[b], 0, scale_chunk))],
        out_specs=pl.BlockSpec((1, tm, d), lambda b, i, r: (b, i, 0)),
    )
    return pl.pallas_call(
        _normmod_kernel,
        out_shape=jax.ShapeDtypeStruct((bx, l, d), out_dtype),
        grid_spec=grid_spec,
        compiler_params=_cparams(("parallel", "parallel")),
        name="normmod",
    )(rows, x3, w.reshape(1, d), mod3, mod3)


def _dot_nt(a, wt):
    return lax.dot_general(a, wt, (((1,), (1,)), ((), ())), preferred_element_type=F32)


def _mm_kernel(a_ref, w_ref, *rest, act, has_bias, w_rows):
    o_ref = rest[-1]
    a = a_ref[...]
    acc = _dot_nt(a, w_ref[...]) if w_rows else jnp.dot(a, w_ref[...], preferred_element_type=F32)
    if has_bias:
        acc = acc + rest[0][...]
    if act == "silu":
        acc = _silu(acc)
    elif act == "sigmoid":
        acc = jax.nn.sigmoid(acc)
    o_ref[...] = acc.astype(o_ref.dtype)


def _mm(a, w, bias=None, act=None, out_dtype=F32, tm=1024, tn=1024, name="mm", rows=None):
    m, k = a.shape
    start, n = (0, w.shape[1]) if rows is None else rows
    tm, tn = min(tm, m), min(tn, n)
    j0 = start // tn
    w_spec = (pl.BlockSpec((k, tn), lambda i, j: (0, j)) if rows is None
              else pl.BlockSpec((tn, k), lambda i, j: (j0 + j, 0)))
    in_specs = [pl.BlockSpec((tm, k), lambda i, j: (i, 0)), w_spec]
    args = [a, w]
    if bias is not None:
        in_specs.append(pl.BlockSpec((1, tn), lambda i, j: (0, j)))
        args.append(bias.reshape(1, n))
    return pl.pallas_call(
        functools.partial(_mm_kernel, act=act, has_bias=bias is not None, w_rows=rows is not None),
        out_shape=jax.ShapeDtypeStruct((m, n), out_dtype),
        grid=(m // tm, n // tn),
        in_specs=in_specs,
        out_specs=pl.BlockSpec((tm, tn), lambda i, j: (i, j)),
        compiler_params=_cparams(("parallel", "parallel")),
        name=name,
    )(*args)


def _mm_glu_kernel(a_ref, wa_ref, wb_ref, o_ref):
    a = a_ref[...]
    va = _dot_nt(a, wa_ref[...])
    vb = _dot_nt(a, wb_ref[...])
    o_ref[...] = va * jax.nn.sigmoid(vb)


def _mm_glu(a, wt, start, n, tm=1024, tn=512):
    m, k = a.shape
    tm = min(tm, m)
    ja, jb = start // tn, (start + n) // tn
    return pl.pallas_call(
        _mm_glu_kernel,
        out_shape=jax.ShapeDtypeStruct((m, n), F32),
        grid=(m // tm, n // tn),
        in_specs=[pl.BlockSpec((tm, k), lambda i, j: (i, 0)),
                  pl.BlockSpec((tn, k), lambda i, j: (ja + j, 0)),
                  pl.BlockSpec((tn, k), lambda i, j: (jb + j, 0))],
        out_specs=pl.BlockSpec((tm, tn), lambda i, j: (i, j)),
        compiler_params=_cparams(("parallel", "parallel")),
        name="mm_glu",
    )(a, wt, wt)


def _mm_merge_kernel(cv_ref, lw_ref, lb_ref, w_ref, b_ref, ga_ref, gb_ref, ys_ref, o_ref, u_ref):
    @pl.when(pl.program_id(1) == 0)
    def _():
        xf = cv_ref[...]
        mu = jnp.mean(xf, axis=-1, keepdims=True)
        xc = xf - mu
        var = jnp.mean(xc * xc, axis=-1, keepdims=True)
        y = xc * lax.rsqrt(var + EPS) * lw_ref[...] + lb_ref[...]
        u_ref[...] = _silu(y).astype(u_ref.dtype)

    ycf = jnp.dot(u_ref[...], w_ref[...], preferred_element_type=F32) + b_ref[...]
    o_ref[...] = (ga_ref[...] * ys_ref[...] + gb_ref[...] * ycf).astype(o_ref.dtype)


def _mm_merge(cv, ln_w, ln_b, w, bias, gates, y_ssd, tm=1024, tn=512):
    m, k = cv.shape
    n = w.shape[1]
    tm = min(tm, m)
    nj = n // tn
    return pl.pallas_call(
        _mm_merge_kernel,
        out_shape=jax.ShapeDtypeStruct((m, n), BF16),
        grid=(m // tm, nj),
        in_specs=[pl.BlockSpec((tm, k), lambda i, j: (i, 0)),
                  pl.BlockSpec((1, k), lambda i, j: (0, 0)),
                  pl.BlockSpec((1, k), lambda i, j: (0, 0)),
                  pl.BlockSpec((k, tn), lambda i, j: (0, j)),
                  pl.BlockSpec((1, tn), lambda i, j: (0, j)),
                  pl.BlockSpec((tm, tn), lambda i, j: (i, j)),
                  pl.BlockSpec((tm, tn), lambda i, j: (i, j + nj)),
                  pl.BlockSpec((tm, tn), lambda i, j: (i, j))],
        out_specs=pl.BlockSpec((tm, tn), lambda i, j: (i, j)),
        scratch_shapes=[pltpu.VMEM((tm, k), BF16)],
        compiler_params=_cparams(("parallel", "arbitrary")),
        name="mm_merge",
    )(cv, ln_w.reshape(1, k), ln_b.reshape(1, k), w, bias.reshape(1, n), gates, gates, y_ssd)


def _mm_resid_kernel(a_ref, w_ref, x_ref, g_ref, o_ref):
    out = jnp.dot(a_ref[...], w_ref[...], preferred_element_type=F32)
    o_ref[...] = x_ref[...] + g_ref[0] * out


def _mm_resid(a, w, x2, mod3, gate_chunk, rows_per_batch, tm=1024, tn=512):
    m, k = a.shape
    n = w.shape[1]
    tm = min(tm, rows_per_batch)
    nj = n // tn
    tiles_per_batch = rows_per_batch // tm
    return pl.pallas_call(
        _mm_resid_kernel,
        out_shape=jax.ShapeDtypeStruct((m, n), F32),
        grid=(m // tm, nj),
        in_specs=[pl.BlockSpec((tm, k), lambda i, j: (i, 0)),
                  pl.BlockSpec((k, tn), lambda i, j: (0, j)),
                  pl.BlockSpec((tm, tn), lambda i, j: (i, j)),
                  pl.BlockSpec((1, 1, tn),
                               lambda i, j: (i // tiles_per_batch, 0, gate_chunk * nj + j))],
        out_specs=pl.BlockSpec((tm, tn), lambda i, j: (i, j)),
        compiler_params=_cparams(("parallel", "parallel")),
        name="mm_resid",
    )(a, w, x2, mod3)


W_ALIGN = 2048


def _pack_wt_kernel(valid_ref, off_ref, w_ref, o_ref):
    del off_ref
    nrow = valid_ref[pl.program_id(0)]
    row = lax.broadcasted_iota(jnp.int32, o_ref.shape, 0)
    o_ref[...] = jnp.where(row < nrow, w_ref[...], 0.0).astype(o_ref.dtype)


def _pack_wt(wt, segments, tr=128):
    n, k = wt.shape
    starts, src_off, valid = [], [], []
    pos = 0
    for lo, hi in segments:
        pos = -(-pos // W_ALIGN) * W_ALIGN
        starts.append(pos)
        while len(src_off) < pos // tr:
            src_off.append(0)
            valid.append(0)
        for r in range(lo, hi, tr):
            src_off.append(min(r, n - tr))
            valid.append(min(tr, hi - r))
            assert r <= n - tr or hi - r == tr
        pos += -(-(hi - lo) // tr) * tr
    total = -(-pos // W_ALIGN) * W_ALIGN
    while len(src_off) < total // tr:
        src_off.append(0)
        valid.append(0)
    grid_spec = pltpu.PrefetchScalarGridSpec(
        num_scalar_prefetch=2,
        grid=(total // tr,),
        in_specs=[pl.BlockSpec((pl.Element(tr), pl.Element(k)), lambda t, v, off: (off[t] * 8, 0))],
        out_specs=pl.BlockSpec((tr, k), lambda t, v, off: (t, 0)),
    )
    packed = pl.pallas_call(
        _pack_wt_kernel,
        out_shape=jax.ShapeDtypeStruct((total, k), BF16),
        grid_spec=grid_spec,
        compiler_params=_cparams(("parallel",)),
        name="pack_wt",
    )(jnp.asarray(valid, jnp.int32), jnp.asarray(src_off, jnp.int32) // 8, wt)
    return packed, starts


_CONV_PAD = 8


def _conv7_kernel(ctx_ref, lat_ref, w_ref, b_ref, o_ref, pad_ref, *, l_ctx, l_lat):
    p = _CONV_PAD
    zeros = jnp.zeros((p, LANE), F32)
    off_ctx = p
    off_lat = 2 * p + l_ctx
    pad_ref[0:p, :] = zeros
    pad_ref[off_ctx + l_ctx:off_lat, :] = zeros
    pad_ref[off_lat + l_lat:off_lat + l_lat + p, :] = zeros
    pad_ref[off_ctx:off_ctx + l_ctx, :] = ctx_ref[0]
    pad_ref[off_lat:off_lat + l_lat, :] = lat_ref[0]
    reach = SSM_CONV // 2
    bias = b_ref[...]

    def chunk(pad_base, out_base):
        acc = jnp.broadcast_to(bias, (CHUNK, LANE))
        for k in range(SSM_CONV):
            tap = pad_ref[pl.ds(pad_base - reach + k, CHUNK), :]
            acc = acc + tap * w_ref[k:k + 1, :]
        o_ref[0, 0, pl.ds(out_base, CHUNK), :] = _silu(acc)

    def ctx_body(j, c):
        base = pl.multiple_of(j * CHUNK, CHUNK)
        chunk(off_ctx + base, base)
        return c

    def lat_body(j, c):
        base = pl.multiple_of(j * CHUNK, CHUNK)
        chunk(off_lat + base, l_ctx + base)
        return c

    lax.fori_loop(0, l_ctx // CHUNK, ctx_body, 0)
    lax.fori_loop(0, l_lat // CHUNK, lat_body, 0, unroll=2)


def _conv7(ctx_raw, lat_raw, w, b):
    bsz, l_ctx, c = ctx_raw.shape
    l_lat = lat_raw.shape[1]
    ltot = l_ctx + l_lat
    nct = c // LANE
    return pl.pallas_call(
        functools.partial(_conv7_kernel, l_ctx=l_ctx, l_lat=l_lat),
        out_shape=jax.ShapeDtypeStruct((bsz, nct, ltot, LANE), F32),
        grid=(bsz, nct),
        in_specs=[pl.BlockSpec((1, l_ctx, LANE), lambda bi, ci: (bi, 0, ci)),
                  pl.BlockSpec((1, l_lat, LANE), lambda bi, ci: (bi, 0, ci)),
                  pl.BlockSpec((SSM_CONV, LANE), lambda bi, ci: (0, ci)),
                  pl.BlockSpec((1, LANE), lambda bi, ci: (0, ci))],
        out_specs=pl.BlockSpec((1, 1, ltot, LANE), lambda bi, ci: (bi, ci, 0, 0)),
        scratch_shapes=[pltpu.VMEM((ltot + 3 * _CONV_PAD, LANE), F32)],
        compiler_params=_cparams(("parallel", "parallel")),
        name="conv7",
    )(ctx_raw, lat_raw, w, b.reshape(1, c))


def _ssd_kernel(xbc_ref, dtc_ref, dtl_ref, par_ref, dexp_ref, ex_ref, *rest, reverse, n_ctx, fuse_norm):
    if fuse_norm:
        yo_ref, sz_ref, nw_ref, o_ref, st_ref, cumt_ref, y_ref = rest
    else:
        y_ref, st_ref, cumt_ref = rest
    i = pl.program_id(1)

    @pl.when(i == 0)
    def _():
        st_ref[...] = jnp.zeros_like(st_ref)

    dt_raw = jnp.where(i < n_ctx, dtc_ref[0], dtl_ref[0])
    bias = par_ref[0:1, :]
    a = -jnp.exp(par_ref[1:2, :])
    dt = jax.nn.softplus(dt_raw + bias)
    cum = dt * a
    row = lax.broadcasted_iota(jnp.int32, (CHUNK, LANE), 0)
    k = 1
    while k < CHUNK:
        if reverse:
            cum = cum + jnp.where(row < CHUNK - k, pltpu.roll(cum, CHUNK - k, 0), 0.0)
        else:
            cum = cum + jnp.where(row >= k, pltpu.roll(cum, k, 0), 0.0)
        k *= 2
    last = 0 if reverse else CHUNK - 1
    cum = cum * LOG2E
    cumt_ref[...] = cum.T
    li = lax.broadcasted_iota(jnp.int32, (CHUNK, CHUNK), 0)
    si = lax.broadcasted_iota(jnp.int32, (CHUNK, CHUNK), 1)
    causal = (li <= si) if reverse else (li >= si)
    lo = lax.broadcasted_iota(jnp.int32, (CHUNK, LANE), 1) < HEAD_DIM
    heads_per_group = N_HEADS // N_GROUPS
    pairs = heads_per_group // 2
    x_tiles = N_HEADS // 2

    def group(g, carry):
        shift = (LANE - heads_per_group * g) & (LANE - 1)
        cum_g = pltpu.roll(cum, shift, 1)
        dt_g = pltpu.roll(dt, shift, 1)
        cum_t = cumt_ref[pl.ds(pl.multiple_of(heads_per_group * g, heads_per_group), heads_per_group), :]
        bb = xbc_ref[0, x_tiles + g].astype(BF16)
        cb = xbc_ref[0, x_tiles + N_GROUPS + g].astype(BF16)
        scores = lax.dot_general(cb, bb, (((1,), (1,)), ((), ())), preferred_element_type=F32)
        h_t = st_ref[g]
        y_off = jnp.dot(cb, h_t.astype(BF16), preferred_element_type=F32)
        d_hi = dt_g.astype(BF16)
        r_hi = dt_g - d_hi.astype(F32)
        d_mid = r_hi.astype(BF16)
        d_lo = (r_hi - d_mid.astype(F32)).astype(BF16)
        dt_x = (jnp.dot(jnp.concatenate([d_hi, d_mid], axis=1), ex_ref[...], preferred_element_type=F32)
                + jnp.dot(d_lo, ex_ref[0:LANE, :], preferred_element_type=F32))
        xw_parts, dec_parts = [], []
        for p in range(pairs):
            j0, j1 = 2 * p, 2 * p + 1
            x2 = xbc_ref[0, pairs * g + p]
            c0 = cum_g[:, j0:j0 + 1]
            c1 = cum_g[:, j1:j1 + 1]
            l0 = jnp.exp2(jnp.where(causal, c0 - cum_t[j0:j0 + 1, :], -jnp.inf))
            l1 = jnp.exp2(jnp.where(causal, c1 - cum_t[j1:j1 + 1, :], -jnp.inf))
            m0 = (scores * l0).astype(BF16)
            m1 = (scores * l1).astype(BF16)
            dt2 = dt_x[:, p * LANE:(p + 1) * LANE]
            c2 = jnp.where(lo, c0, c1)
            xdt = x2 * dt2
            xdt_b = xdt.astype(BF16)
            zero = jnp.zeros_like(xdt_b)
            y_diag = (jnp.dot(m0, jnp.where(lo, xdt_b, zero), preferred_element_type=F32)
                      + jnp.dot(m1, jnp.where(lo, zero, xdt_b), preferred_element_type=F32))
            e2 = jnp.exp2(c2)
            y = y_diag + y_off[:, p * LANE:(p + 1) * LANE] * e2
            y_ref[0, pairs * g + p] = y + dexp_ref[pairs * g + p] * x2
            to_end = jnp.exp2(c2[last:last + 1, :] - c2)
            xw_parts.append((xdt * to_end).astype(BF16))
            dec_parts.append(e2[last:last + 1, :])
        xw = jnp.concatenate(xw_parts, axis=1)
        dec = jnp.concatenate(dec_parts, axis=1)
        upd = lax.dot_general(bb, xw, (((0,), (0,)), ((), ())), preferred_element_type=F32)
        st_ref[g] = h_t * dec + upd
        return carry

    lax.fori_loop(0, N_GROUPS, group, 0, unroll=2)

    if fuse_norm:
        sq = jnp.zeros((CHUNK, LANE), F32)
        for j in range(x_tiles):
            gj = (y_ref[0, j] + yo_ref[0, j]) * sz_ref[:, j * LANE:(j + 1) * LANE]
            y_ref[0, j] = gj
            sq = sq + gj * gj
        r = lax.rsqrt(jnp.sum(sq, axis=-1, keepdims=True) / (x_tiles * LANE) + EPS)
        for j in range(x_tiles):
            sl = slice(j * LANE, (j + 1) * LANE)
            o_ref[:, sl] = (y_ref[0, j] * r * nw_ref[:, sl]).astype(o_ref.dtype)


def _ssd(xbc_act, dt_ctx, dt_lat, par, dexp, reverse, norm_with=None):
    bsz, ntile, ltot, _ = xbc_act.shape
    l_ctx = dt_ctx.shape[1]
    l_lat = dt_lat.shape[1]
    n_ctx = l_ctx // CHUNK
    n_lat = l_lat // CHUNK
    steps = n_ctx + n_lat
    x_tiles = N_HEADS // 2
    gw = (N_HEADS // N_GROUPS) * HEAD_DIM
    e1 = (jnp.arange(gw)[None, :] // HEAD_DIM == jnp.arange(LANE)[:, None]).astype(BF16)
    expand = jnp.concatenate([e1, e1], axis=0)

    if reverse:
        def cat_chunk(i):
            return jnp.where(i < n_ctx, n_ctx - 1 - i, n_ctx + steps - 1 - i)

        def ctx_chunk(i):
            return jnp.maximum(n_ctx - 1 - i, 0)

        def lat_chunk(i):
            return jnp.minimum(steps - 1 - i, n_lat - 1)
    else:
        def cat_chunk(i):
            return i

        def ctx_chunk(i):
            return jnp.minimum(i, n_ctx - 1)

        def lat_chunk(i):
            return jnp.maximum(i - n_ctx, 0)

    y_spec = pl.BlockSpec((1, x_tiles, CHUNK, LANE), lambda b, i: (b, 0, lat_chunk(i), 0))
    in_specs = [pl.BlockSpec((1, ntile, CHUNK, LANE), lambda b, i: (b, 0, cat_chunk(i), 0)),
                pl.BlockSpec((1, CHUNK, LANE), lambda b, i: (b, ctx_chunk(i), 0)),
                pl.BlockSpec((1, CHUNK, LANE), lambda b, i: (b, lat_chunk(i), 0)),
                pl.BlockSpec((8, LANE), lambda b, i: (0, 0)),
                pl.BlockSpec((x_tiles, 1, LANE), lambda b, i: (0, 0, 0)),
                pl.BlockSpec((2 * LANE, gw), lambda b, i: (0, 0))]
    args = [xbc_act, dt_ctx, dt_lat, par, dexp, expand]
    scratch = [pltpu.VMEM((N_GROUPS, D_STATE, gw), F32), pltpu.VMEM((LANE, CHUNK), F32)]
    if norm_with is None:
        out_shape = jax.ShapeDtypeStruct((bsz, x_tiles, l_lat, LANE), F32)
        out_spec = y_spec
    else:
        y_other, silu_z, norm_w = norm_with
        dn = x_tiles * LANE
        row_spec = pl.BlockSpec((CHUNK, dn), lambda b, i: (b * n_lat + lat_chunk(i), 0))
        in_specs += [y_spec, row_spec, pl.BlockSpec((1, dn), lambda b, i: (0, 0))]
        args += [y_other, silu_z, norm_w.reshape(1, dn)]
        out_shape = jax.ShapeDtypeStruct((bsz * l_lat, dn), BF16)
        out_spec = row_spec
        scratch.append(pltpu.VMEM((1, x_tiles, CHUNK, LANE), F32))
    return pl.pallas_call(
        functools.partial(_ssd_kernel, reverse=reverse, n_ctx=n_ctx, fuse_norm=norm_with is not None),
        out_shape=out_shape,
        grid=(bsz, steps),
        in_specs=in_specs,
        out_specs=out_spec,
        scratch_shapes=scratch,
        compiler_params=_cparams(("parallel", "arbitrary")),
        name="ssd_bwd" if reverse else "ssd_fwd",
    )(*args)


def _conv31_kernel(u_ref, w_ref, b_ref, o_ref, pad_ref, *, seq):
    halo = (CF_KERNEL // 2) * GRID_W
    zeros = jnp.zeros((halo, LANE), F32)
    pad_ref[0:halo, :] = zeros
    pad_ref[halo + seq:halo + seq + halo, :] = zeros
    pad_ref[halo:halo + seq, :] = u_ref[0]
    bias = b_ref[...]

    def body(j, c):
        base = pl.multiple_of(j * CHUNK, CHUNK)
        acc = jnp.broadcast_to(bias, (CHUNK, LANE))
        for k in range(CF_KERNEL):
            tap = pad_ref[pl.ds(pl.multiple_of(base + k * GRID_W, GRID_W), CHUNK), :]
            acc = acc + tap * w_ref[k:k + 1, :]
        o_ref[0, pl.ds(base, CHUNK), :] = acc
        return c

    lax.fori_loop(0, seq // CHUNK, body, 0, unroll=2)


def _conv31(u3, w, b):
    bsz, s, c = u3.shape
    halo = (CF_KERNEL // 2) * GRID_W
    return pl.pallas_call(
        functools.partial(_conv31_kernel, seq=s),
        out_shape=jax.ShapeDtypeStruct((bsz, s, c), F32),
        grid=(bsz, c // LANE),
        in_specs=[pl.BlockSpec((1, s, LANE), lambda bi, ci: (bi, 0, ci)),
                  pl.BlockSpec((CF_KERNEL, LANE), lambda bi, ci: (0, ci)),
                  pl.BlockSpec((1, LANE), lambda bi, ci: (0, ci))],
        out_specs=pl.BlockSpec((1, s, LANE), lambda bi, ci: (bi, 0, ci)),
        scratch_shapes=[pltpu.VMEM((s + 2 * halo, LANE), F32)],
        compiler_params=_cparams(("parallel", "parallel")),
        name="conv31",
    )(u3, w, b.reshape(1, c))


def _route_kernel(x_ref, w_ref, sh_ref, sc_ref, rw_ref, rb_ref, h_ref, eid_ref, ew_ref):
    xf = x_ref[...]
    ms = jnp.mean(xf * xf, axis=-1, keepdims=True)
    h = xf * lax.rsqrt(ms + EPS) * w_ref[...]
    h = h * (1.0 + sc_ref[0]) + sh_ref[0]
    tm = xf.shape[0]
    nt = xf.shape[1] // LANE
    pitch = _pitch(nt)
    for j in range(nt):
        h_ref[pl.ds(j, tm, stride=pitch), :] = h[:, j * LANE:(j + 1) * LANE]
    for j in range(nt, pitch):
        h_ref[pl.ds(j, tm, stride=pitch), :] = jnp.zeros((tm, LANE), F32)
    logits = jnp.dot(h.astype(BF16), rw_ref[...], preferred_element_type=F32) + rb_ref[...]
    lane = lax.broadcasted_iota(jnp.int32, (tm, LANE), 1)
    lane_f = lane.astype(F32)
    ninf = -jnp.inf
    gl = jnp.where(lane < MOE_GROUPS, logits, ninf)
    gmax = jnp.max(gl, axis=-1, keepdims=True)
    gidx = jnp.min(jnp.where(gl == gmax, lane_f, float(LANE)), axis=-1, keepdims=True)
    gsum = jnp.sum(jnp.exp(gl - gmax), axis=-1, keepdims=True)
    g_p = 1.0 / gsum
    first = float(MOE_GROUPS) + gidx * float(EXPERTS_PER_GROUP)
    in_group = (lane_f >= first) & (lane_f < first + float(EXPERTS_PER_GROUP))
    el = jnp.where(in_group, logits, ninf)
    m1 = jnp.max(el, axis=-1, keepdims=True)
    i1 = jnp.min(jnp.where(el == m1, lane_f, float(LANE)), axis=-1, keepdims=True)
    el2 = jnp.where(lane_f == i1, ninf, el)
    m2 = jnp.max(el2, axis=-1, keepdims=True)
    i2 = jnp.min(jnp.where(el2 == m2, lane_f, float(LANE)), axis=-1, keepdims=True)
    e21 = jnp.exp(m2 - m1)
    den = 1.0 + e21
    w1 = (1.0 / den) * g_p
    w2 = (e21 / den) * g_p
    e1 = (i1 - float(MOE_GROUPS)).astype(jnp.int32)
    e2 = (i2 - float(MOE_GROUPS)).astype(jnp.int32)
    eid_ref[...] = jnp.where(lane == 0, e1, jnp.where(lane == 1, e2, 0))
    ew_ref[...] = jnp.where(lane == 0, w1, jnp.where(lane == 1, w2, 0.0))


def _route(x2, w, mod3, shift_chunk, scale_chunk, rows_per_batch, rw, rb, tm=256):
    m, d = x2.shape
    pitch = _pitch(d // LANE)
    tiles_per_batch = rows_per_batch // tm
    return pl.pallas_call(
        _route_kernel,
        out_shape=(jax.ShapeDtypeStruct((m * pitch, LANE), F32),
                   jax.ShapeDtypeStruct((m, LANE), jnp.int32),
                   jax.ShapeDtypeStruct((m, LANE), F32)),
        grid=(m // tm,),
        in_specs=[pl.BlockSpec((tm, d), lambda i: (i, 0)),
                  pl.BlockSpec((1, d), lambda i: (0, 0)),
                  pl.BlockSpec((1, 1, d), lambda i: (i // tiles_per_batch, 0, shift_chunk)),
                  pl.BlockSpec((1, 1, d), lambda i: (i // tiles_per_batch, 0, scale_chunk)),
                  pl.BlockSpec((d, LANE), lambda i: (0, 0)),
                  pl.BlockSpec((1, LANE), lambda i: (0, 0))],
        out_specs=(pl.BlockSpec((tm * pitch, LANE), lambda i: (i, 0)),
                   pl.BlockSpec((tm, LANE), lambda i: (i, 0)),
                   pl.BlockSpec((tm, LANE), lambda i: (i, 0))),
        compiler_params=_cparams(("parallel",)),
        name="route",
    )(x2, w.reshape(1, d), mod3, mod3, rw, rb)


_DMA_UNROLL = 8


def _rows_to_matrix(ref, tm, nt):
    return jnp.concatenate([ref[pl.ds(j, tm, stride=_pitch(nt)), :] for j in range(nt)], axis=1)


def _token_copy(src, dst, sem, nt, rows, src_tok, dst_tok):
    pitch = _pitch(nt)
    return pltpu.make_async_copy(src.at[pl.ds(src_tok * pitch, rows), :],
                                 dst.at[pl.ds(dst_tok * pitch, rows), :], sem)


def _bulk_wait(src, dst, sem, total_rows):
    pltpu.make_async_copy(src.at[pl.ds(0, total_rows), :], dst.at[pl.ds(0, total_rows), :], sem).wait()


def _for_rows(n, body):
    groups = lax.shift_right_logical(n, _DMA_UNROLL.bit_length() - 1)

    def group(g, c):
        for u in range(_DMA_UNROLL):
            body(g * _DMA_UNROLL + u)
        return c

    def tail(r, c):
        body(r)
        return c

    lax.fori_loop(0, groups, group, 0)
    lax.fori_loop(groups * _DMA_UNROLL, n, tail, 0)


def _wait_rows(n, wait_tokens):
    p = MOE_BLOCK
    while p >= 1:
        @pl.when((n & p) != 0)
        def _(p=p):
            wait_tokens(p)
        p //= 2


def _stream_expert_weights(b, be_ref, eord_ref, enext_ref, w_hbms, w_bufs, w_caches, wsem):
    prev = jnp.maximum(b - 1, 0)

    def copies(e, slot):
        return [pltpu.make_async_copy(w.at[e], buf.at[slot], wsem.at[slot]) for w, buf in zip(w_hbms, w_bufs)]

    @pl.when(b == 0)
    def _():
        for cp in copies(be_ref[0], 0):
            cp.start()

    @pl.when((b == 0) | (be_ref[b] != be_ref[prev]))
    def _():
        for s in range(2):
            @pl.when((eord_ref[b] & 1) == s)
            def _(s=s):
                for cp in copies(be_ref[b], s):
                    cp.wait()

                @pl.when(enext_ref[b] >= 0)
                def _():
                    for cp in copies(enext_ref[b], 1 - s):
                        cp.start()

                for buf, cache in zip(w_bufs, w_caches):
                    cache[...] = buf[s].astype(BF16)


def _expert_up_kernel(be_ref, nv_ref, nused_ref, eord_ref, enext_ref, tokc_ref, tokn_ref, h_hbm, wg_hbm, wu_hbm,
                      o_ref, xs0_ref, xs1_ref, wgs_ref, wus_ref, wgb_ref, wub_ref, sem, wsem, *, fchunk, nt):
    b = pl.program_id(0)
    n_used = nused_ref[0]
    dff = wgb_ref.shape[1]
    slots = (xs0_ref, xs1_ref)

    def start_gather(tok_ref, blk, slot):
        _for_rows(nv_ref[blk], lambda r: _token_copy(
            h_hbm, slots[slot], sem.at[slot], nt, nt, tok_ref[0, 0, r], r).start())

    def wait_gather(blk, slot):
        _wait_rows(nv_ref[blk], lambda p: _bulk_wait(h_hbm, slots[slot], sem.at[slot], p * nt))

    @pl.when(b == 0)
    def _():
        xs0_ref[...] = jnp.zeros_like(xs0_ref)
        xs1_ref[...] = jnp.zeros_like(xs1_ref)
        start_gather(tokc_ref, 0, 0)

    for slot in range(2):
        @pl.when((b + 1 < n_used) & (lax.rem(b, 2) == slot))
        def _(slot=slot):
            start_gather(tokn_ref, b + 1, 1 - slot)

    @pl.when(b < n_used)
    def _():
        _stream_expert_weights(b, be_ref, eord_ref, enext_ref, (wg_hbm, wu_hbm), (wgs_ref, wus_ref),
                               (wgb_ref, wub_ref), wsem)

        for slot in range(2):
            @pl.when(lax.rem(b, 2) == slot)
            def _(slot=slot):
                wait_gather(b, slot)
                xb = _rows_to_matrix(slots[slot], MOE_BLOCK, nt).astype(BF16)
                for f in range(dff // fchunk):
                    sl = slice(f * fchunk, (f + 1) * fchunk)
                    gate = jnp.dot(xb, wgb_ref[:, sl], preferred_element_type=F32)
                    up = jnp.dot(xb, wub_ref[:, sl], preferred_element_type=F32)
                    o_ref[:, sl] = (_silu(gate) * up).astype(o_ref.dtype)

    @pl.when(b >= n_used)
    def _():
        o_ref[...] = jnp.zeros_like(o_ref)


def _expert_up(h2t, buf_tok, w_gate, w_up, tables, fchunk=256):
    n_blocks = buf_tok.shape[0]
    _, d, dff = w_gate.shape
    nt = d // LANE
    slot_rows = MOE_BLOCK * _pitch(nt)
    grid_spec = pltpu.PrefetchScalarGridSpec(
        num_scalar_prefetch=len(tables),
        grid=(n_blocks,),
        in_specs=[pl.BlockSpec((1, 1, MOE_BLOCK), lambda b, *_: (b, 0, 0), memory_space=pltpu.SMEM),
                  pl.BlockSpec((1, 1, MOE_BLOCK), lambda b, *_: (jnp.minimum(b + 1, n_blocks - 1), 0, 0),
                               memory_space=pltpu.SMEM),
                  pl.BlockSpec(memory_space=pl.ANY),
                  pl.BlockSpec(memory_space=pl.ANY),
                  pl.BlockSpec(memory_space=pl.ANY)],
        out_specs=pl.BlockSpec((MOE_BLOCK, dff), lambda b, *_: (b, 0)),
        scratch_shapes=[pltpu.VMEM((slot_rows, LANE), F32),
                        pltpu.VMEM((slot_rows, LANE), F32),
                        pltpu.VMEM((2, d, dff), F32),
                        pltpu.VMEM((2, d, dff), F32),
                        pltpu.VMEM((d, dff), BF16),
                        pltpu.VMEM((d, dff), BF16),
                        pltpu.SemaphoreType.DMA((2,)),
                        pltpu.SemaphoreType.DMA((2,))],
    )
    return pl.pallas_call(
        functools.partial(_expert_up_kernel, fchunk=fchunk, nt=nt),
        out_shape=jax.ShapeDtypeStruct((n_blocks * MOE_BLOCK, dff), BF16),
        grid_spec=grid_spec,
        compiler_params=_cparams(("arbitrary",)),
        name="expert_up",
    )(*tables, buf_tok, buf_tok, h2t, w_gate, w_up)


def _expert_down_kernel(be_ref, nv_ref, nused_ref, eord_ref, enext_ref, asg_ref, h_ref, wd_hbm, y_hbm,
                        ys0_ref, ys1_ref, wds_ref, wdb_ref, sem, wsem, *, nchunk, nt, n_tok):
    b = pl.program_id(0)
    n_used = nused_ref[0]
    d = wdb_ref.shape[1]
    slots = (ys0_ref, ys1_ref)
    pitch = _pitch(nt)

    def row_copy(slot, r, assign):
        k = assign & 1
        tok = lax.shift_right_logical(assign, 1)
        return _token_copy(slots[slot], y_hbm, sem.at[slot], nt, pitch, r, k * n_tok + tok)

    def start_scatter(blk, slot):
        _for_rows(nv_ref[blk], lambda r: row_copy(slot, r, asg_ref[0, 0, r]).start())

    def wait_scatter(blk, slot):
        _wait_rows(nv_ref[blk], lambda p: _bulk_wait(slots[slot], y_hbm, sem.at[slot], p * pitch))

    @pl.when(b == 0)
    def _():
        ys0_ref[...] = jnp.zeros_like(ys0_ref)
        ys1_ref[...] = jnp.zeros_like(ys1_ref)

    @pl.when(b < n_used)
    def _():
        _stream_expert_weights(b, be_ref, eord_ref, enext_ref, (wd_hbm,), (wds_ref,), (wdb_ref,), wsem)
        hb = h_ref[...]
        for slot in range(2):
            @pl.when(lax.rem(b, 2) == slot)
            def _(slot=slot):
                for c in range(d // nchunk):
                    out = jnp.dot(hb, wdb_ref[:, c * nchunk:(c + 1) * nchunk], preferred_element_type=F32)
                    for j in range(nchunk // LANE):
                        slots[slot][pl.ds(c * (nchunk // LANE) + j, MOE_BLOCK, stride=pitch), :] = (
                            out[:, j * LANE:(j + 1) * LANE])

    for slot in range(2):
        @pl.when((b >= 1) & (b - 1 < n_used) & (lax.rem(b, 2) == slot))
        def _(slot=slot):
            wait_scatter(b - 1, 1 - slot)

        @pl.when((b < n_used) & (lax.rem(b, 2) == slot))
        def _(slot=slot):
            start_scatter(b, slot)

        @pl.when((b == pl.num_programs(0) - 1) & (b < n_used) & (lax.rem(b, 2) == slot))
        def _(slot=slot):
            wait_scatter(b, slot)


def _expert_down(hid, buf_assign, w_down, tables, n_tok, nchunk=512):
    n_rows, dff = hid.shape
    n_blocks = n_rows // MOE_BLOCK
    d = w_down.shape[2]
    nt = d // LANE
    pitch = _pitch(nt)
    grid_spec = pltpu.PrefetchScalarGridSpec(
        num_scalar_prefetch=len(tables),
        grid=(n_blocks,),
        in_specs=[pl.BlockSpec((1, 1, MOE_BLOCK), lambda b, *_: (b, 0, 0), memory_space=pltpu.SMEM),
                  pl.BlockSpec((MOE_BLOCK, dff), lambda b, be, nv, n, *_: (jnp.minimum(b, n[0] - 1), 0)),
                  pl.BlockSpec(memory_space=pl.ANY)],
        out_specs=pl.BlockSpec(memory_space=pl.ANY),
        scratch_shapes=[pltpu.VMEM((MOE_BLOCK * pitch, LANE), F32),
                        pltpu.VMEM((MOE_BLOCK * pitch, LANE), F32),
                        pltpu.VMEM((2, dff, d), F32),
                        pltpu.VMEM((dff, d), BF16),
                        pltpu.SemaphoreType.DMA((2,)),
                        pltpu.SemaphoreType.DMA((2,))],
    )
    return pl.pallas_call(
        functools.partial(_expert_down_kernel, nchunk=nchunk, nt=nt, n_tok=n_tok),
        out_shape=jax.ShapeDtypeStruct((2 * n_tok * pitch, LANE), F32),
        grid_spec=grid_spec,
        compiler_params=_cparams(("arbitrary",)),
        name="expert_down",
    )(*tables, buf_assign, hid, w_down)


def _combine_kernel(y0_ref, y1_ref, ew_ref, x_ref, g_ref, w_ref, o_ref):
    tm, d = x_ref.shape
    nt = d // LANE
    ew = ew_ref[...]
    moe = (_rows_to_matrix(y0_ref, tm, nt) * ew[:, 0:1]
           + _rows_to_matrix(y1_ref, tm, nt) * ew[:, 1:2])
    xo = x_ref[...] + g_ref[0] * moe
    ms = jnp.mean(xo * xo, axis=-1, keepdims=True)
    o_ref[...] = xo * lax.rsqrt(ms + EPS) * w_ref[...]


def _combine(y, ew, x2, mod3, gate_chunk, rows_per_batch, final_w, tm=256):
    m, d = x2.shape
    pitch = _pitch(d // LANE)
    tiles = m // tm
    tiles_per_batch = rows_per_batch // tm
    return pl.pallas_call(
        _combine_kernel,
        out_shape=jax.ShapeDtypeStruct((m, d), F32),
        grid=(tiles,),
        in_specs=[pl.BlockSpec((tm * pitch, LANE), lambda i: (i, 0)),
                  pl.BlockSpec((tm * pitch, LANE), lambda i: (tiles + i, 0)),
                  pl.BlockSpec((tm, LANE), lambda i: (i, 0)),
                  pl.BlockSpec((tm, d), lambda i: (i, 0)),
                  pl.BlockSpec((1, 1, d), lambda i: (i // tiles_per_batch, 0, gate_chunk)),
                  pl.BlockSpec((1, d), lambda i: (0, 0))],
        out_specs=pl.BlockSpec((tm, d), lambda i: (i, 0)),
        compiler_params=_cparams(("parallel",)),
        name="moe_combine",
    )(y, y, ew, x2, mod3, final_w.reshape(1, d))


def _dispatch_tables(eid, n_tok):
    top_k = eid.shape[1]
    n_assign = n_tok * top_k
    expert = eid.reshape(-1)
    key = jnp.sort(expert * n_assign + jnp.arange(n_assign, dtype=jnp.int32))
    sorted_assign = key % n_assign
    bounds = jnp.arange(N_EXPERTS + 1, dtype=jnp.int32) * n_assign
    start = jnp.searchsorted(key, bounds, side="left").astype(jnp.int32)
    counts = start[1:] - start[:-1]
    nblk = (counts + MOE_BLOCK - 1) // MOE_BLOCK
    blk_end = jnp.cumsum(nblk)
    blk_start = blk_end - nblk
    n_blocks = -(-n_assign // MOE_BLOCK) + N_EXPERTS
    bidx = jnp.arange(n_blocks, dtype=jnp.int32)
    block_expert = jnp.minimum(jnp.searchsorted(blk_end, bidx, side="right"), N_EXPERTS - 1).astype(jnp.int32)
    in_expert = (bidx - blk_start[block_expert]) * MOE_BLOCK
    n_valid = jnp.clip(counts[block_expert] - in_expert, 0, MOE_BLOCK).astype(jnp.int32)
    src = start[block_expert][:, None] + in_expert[:, None] + jnp.arange(MOE_BLOCK, dtype=jnp.int32)[None, :]
    valid = jnp.arange(MOE_BLOCK, dtype=jnp.int32)[None, :] < n_valid[:, None]
    buf_assign = jnp.where(valid, sorted_assign[jnp.clip(src, 0, n_assign - 1)], 0).astype(jnp.int32)
    n_used = blk_end[-1].astype(jnp.int32)
    first = jnp.concatenate([jnp.ones((1,), jnp.int32),
                             (block_expert[1:] != block_expert[:-1]).astype(jnp.int32)])
    expert_ordinal = (jnp.cumsum(first) - 1).astype(jnp.int32)
    next_blk = blk_end[block_expert]
    next_expert = jnp.where(next_blk < n_used, block_expert[jnp.minimum(next_blk, n_blocks - 1)], -1).astype(jnp.int32)
    tables = (block_expert, n_valid, n_used.reshape(1), expert_ordinal, next_expert)
    return buf_assign.reshape(n_blocks, 1, MOE_BLOCK), tables


def kernel(x, c, ctx, c_ctx, ada_w, ada_b, norm1_w, w_in, ssm_conv_w, ssm_conv_b, dt_bias, a_log, d_skip, ssm_norm_w, ssm_out_w, cf_dw_w, cf_dw_b, cf_ln_w, cf_ln_b, cf_out_w, cf_out_b, w_o, norm2_w, router_group_w, router_group_b, router_expert_w, router_expert_b, expert_w_gate, expert_w_up, expert_w_down, final_norm_w):
    bsz, seq, d = x.shape
    l_ctx = ctx.shape[1]
    n_tok = bsz * seq
    d_inner = ssm_norm_w.shape[1]
    gn = N_GROUPS * D_STATE
    xbc_dim = d_inner + 2 * gn
    off_dt = xbc_dim
    off_z = off_dt + N_HEADS
    off_glu = off_z + d_inner
    off_gate = off_glu + 2 * d

    ctx_row = bsz
    crows = jnp.zeros((8, d), F32).at[:bsz].set(c).at[ctx_row].set(c_ctx)
    mod = _ada(crows, ada_w[0], ada_b[0])
    mod3 = mod.reshape(8, 1, 6 * d)
    lat_rows = jnp.arange(bsz, dtype=jnp.int32)
    ctx_rows = jnp.full((bsz,), ctx_row, jnp.int32)

    h_lat = _normmod(x, norm1_w[0], mod3, lat_rows, 0, 1, BF16).reshape(n_tok, d)
    h_ctx = _normmod(ctx, norm1_w[0], mod3, ctx_rows, 0, 1, BF16).reshape(bsz * l_ctx, d)

    wt, (r_xbc, r_dt, r_z, r_glu, r_gate) = _pack_wt(
        jnp.transpose(w_in[0]),
        [(0, xbc_dim), (off_dt, off_z), (off_z, off_glu), (off_glu, off_gate), (off_gate, off_gate + 2 * d)])

    xbc_lat = _mm(h_lat, wt, tn=2048, name="in_xbc", rows=(r_xbc, xbc_dim)).reshape(bsz, seq, xbc_dim)
    xbc_ctx = _mm(h_ctx, wt, tm=512, name="in_xbc_ctx", rows=(r_xbc, xbc_dim)).reshape(bsz, l_ctx, xbc_dim)
    dt_lat = _mm(h_lat, wt, name="in_dt", rows=(r_dt, LANE)).reshape(bsz, seq, LANE)
    dt_ctx = _mm(h_ctx, wt, tm=512, name="in_dt_ctx", rows=(r_dt, LANE)).reshape(bsz, l_ctx, LANE)
    sz = _mm(h_lat, wt, act="silu", tn=2048, name="in_z", rows=(r_z, d_inner))
    u = _mm_glu(h_lat, wt, r_glu, d, tn=1024)
    gates = _mm(h_lat, wt, act="sigmoid", tn=2048, name="in_gate", rows=(r_gate, 2 * d))

    xbc_act = _conv7(xbc_ctx, xbc_lat, ssm_conv_w[0], ssm_conv_b[0])

    def ssd_params(k):
        par = jnp.zeros((8, LANE), F32).at[0, :N_HEADS].set(dt_bias[0, k]).at[1, :N_HEADS].set(a_log[0, k])
        return par, jnp.repeat(d_skip[0, k], HEAD_DIM).reshape(N_HEADS // 2, 1, LANE)

    y_bwd = _ssd(xbc_act, dt_ctx, dt_lat, *ssd_params(1), reverse=True)
    gnorm = _ssd(xbc_act, dt_ctx, dt_lat, *ssd_params(0), reverse=False,
                 norm_with=(y_bwd, sz, ssm_norm_w[0]))
    y_ssd = _mm(gnorm, ssm_out_w[0].astype(BF16), tn=512, name="ssm_out")

    cv = _conv31(u.reshape(bsz, seq, d), cf_dw_w[0], cf_dw_b[0]).reshape(n_tok, d)
    merged = _mm_merge(cv, cf_ln_w[0], cf_ln_b[0], cf_out_w[0].astype(BF16), cf_out_b[0], gates, y_ssd)
    x1 = _mm_resid(merged, w_o[0].astype(BF16), x.reshape(n_tok, d), mod3, 2, seq)

    n_r = MOE_GROUPS + N_EXPERTS
    rw = jnp.pad(jnp.concatenate([router_group_w[0], router_expert_w[0]], axis=1),
                 ((0, 0), (0, LANE - n_r))).astype(BF16)
    rb = jnp.pad(jnp.concatenate([router_group_b[0], router_expert_b[0]]), (0, LANE - n_r)).reshape(1, LANE)
    h2t, eid, ew = _route(x1, norm2_w[0], mod3, 3, 4, seq, rw, rb)

    buf_assign, tables = _dispatch_tables(eid[:, :2], n_tok)
    hid = _expert_up(h2t, buf_assign // 2, expert_w_gate[0], expert_w_up[0], tables)
    y = _expert_down(hid, buf_assign, expert_w_down[0], tables, n_tok)
    out = _combine(y, ew, x1, mod3, 5, seq, final_norm_w)
    return out.reshape(bsz, seq, d)
```

```python
import functools

import jax
import jax.numpy as jnp
from jax import lax
from jax.experimental import pallas as pl
from jax.experimental.pallas import tpu as pltpu

F32 = jnp.float32
BF16 = jnp.bfloat16

EPS = 1e-6
GRID_W = 64
HEAD_DIM = 64
N_HEADS = 64
N_GROUPS = 8
D_STATE = 128
CHUNK = 128
SSM_CONV = 7
CF_KERNEL = 31
MOE_GROUPS = 8
EXPERTS_PER_GROUP = 8
N_EXPERTS = 64
MOE_BLOCK = 256
LANE = 128
LOG2E = 1.4426950408889634
VMEM_LIMIT = 56 * 1024 * 1024


def _cparams(sem):
    return pltpu.CompilerParams(dimension_semantics=sem, vmem_limit_bytes=VMEM_LIMIT)


def _silu(v):
    return v * jax.nn.sigmoid(v)


def _pitch(nt):
    return nt + 1


def _ada_kernel(c_ref, w_ref, b_ref, o_ref):
    s = _silu(c_ref[...])
    o_ref[...] = jnp.dot(s.astype(BF16), w_ref[...].astype(BF16),
                         preferred_element_type=F32) + b_ref[...]


def _ada(crows, ada_w, ada_b, tn=1024):
    r, d = crows.shape
    n = ada_w.shape[1]
    return pl.pallas_call(
        _ada_kernel,
        out_shape=jax.ShapeDtypeStruct((r, n), F32),
        grid=(n // tn,),
        in_specs=[pl.BlockSpec((r, d), lambda j: (0, 0)),
                  pl.BlockSpec((d, tn), lambda j: (0, j)),
                  pl.BlockSpec((1, tn), lambda j: (0, j))],
        out_specs=pl.BlockSpec((r, tn), lambda j: (0, j)),
        compiler_params=_cparams(("parallel",)),
        name="ada",
    )(crows, ada_w, ada_b.reshape(1, n))


def _normmod_kernel(rows_ref, x_ref, w_ref, sh_ref, sc_ref, o_ref):
    del rows_ref
    xf = x_ref[0]
    ms = jnp.mean(xf * xf, axis=-1, keepdims=True)
    y = xf * lax.rsqrt(ms + EPS) * w_ref[...]
    o_ref[0] = (y * (1.0 + sc_ref[0]) + sh_ref[0]).astype(o_ref.dtype)


def _normmod(x3, w, mod3, rows, shift_chunk, scale_chunk, out_dtype, tm=256):
    bx, l, d = x3.shape
    grid_spec = pltpu.PrefetchScalarGridSpec(
        num_scalar_prefetch=1,
        grid=(bx, l // tm),
        in_specs=[pl.BlockSpec((1, tm, d), lambda b, i, r: (b, i, 0)),
                  pl.BlockSpec((1, d), lambda b, i, r: (0, 0)),
                  pl.BlockSpec((1, 1, d), lambda b, i, r: (r[b], 0, shift_chunk)),
                  pl.BlockSpec((1, 1, d), lambda b, i, r: (r[b], 0, scale_chunk))],
        out_specs=pl.BlockSpec((1, tm, d), lambda b, i, r: (b, i, 0)),
    )
    return pl.pallas_call(
        _normmod_kernel,
        out_shape=jax.ShapeDtypeStruct((bx, l, d), out_dtype),
        grid_spec=grid_spec,
        compiler_params=_cparams(("parallel", "parallel")),
        name="normmod",
    )(rows, x3, w.reshape(1, d), mod3, mod3)


def _dot_nt(a, wt):
    return lax.dot_general(a, wt, (((1,), (1,)), ((), ())), preferred_element_type=F32)


def _mm_kernel(a_ref, w_ref, *rest, act, has_bias, w_rows):
    o_ref = rest[-1]
    a = a_ref[...]
    acc = _dot_nt(a, w_ref[...]) if w_rows else jnp.dot(a, w_ref[...], preferred_element_type=F32)
    if has_bias:
        acc = acc + rest[0][...]
    if act == "silu":
        acc = _silu(acc)
    elif act == "sigmoid":
        acc = jax.nn.sigmoid(acc)
    o_ref[...] = acc.astype(o_ref.dtype)


def _mm(a, w, bias=None, act=None, out_dtype=F32, tm=1024, tn=1024, name="mm", rows=None):
    m, k = a.shape
    start, n = (0, w.shape[1]) if rows is None else rows
    tm, tn = min(tm, m), min(tn, n)
    j0 = start // tn
    w_spec = (pl.BlockSpec((k, tn), lambda i, j: (0, j)) if rows is None
              else pl.BlockSpec((tn, k), lambda i, j: (j0 + j, 0)))
    in_specs = [pl.BlockSpec((tm, k), lambda i, j: (i, 0)), w_spec]
    args = [a, w]
    if bias is not None:
        in_specs.append(pl.BlockSpec((1, tn), lambda i, j: (0, j)))
        args.append(bias.reshape(1, n))
    return pl.pallas_call(
        functools.partial(_mm_kernel, act=act, has_bias=bias is not None, w_rows=rows is not None),
        out_shape=jax.ShapeDtypeStruct((m, n), out_dtype),
        grid=(m // tm, n // tn),
        in_specs=in_specs,
        out_specs=pl.BlockSpec((tm, tn), lambda i, j: (i, j)),
        compiler_params=_cparams(("parallel", "parallel")),
        name=name,
    )(*args)


def _mm_glu_kernel(a_ref, wa_ref, wb_ref, o_ref):
    a = a_ref[...]
    va = _dot_nt(a, wa_ref[...])
    vb = _dot_nt(a, wb_ref[...])
    o_ref[...] = va * jax.nn.sigmoid(vb)


def _mm_glu(a, wt, start, n, tm=1024, tn=512):
    m, k = a.shape
    tm = min(tm, m)
    ja, jb = start // tn, (start + n) // tn
    return pl.pallas_call(
        _mm_glu_kernel,
        out_shape=jax.ShapeDtypeStruct((m, n), F32),
        grid=(m // tm, n // tn),
        in_specs=[pl.BlockSpec((tm, k), lambda i, j: (i, 0)),
                  pl.BlockSpec((tn, k), lambda i, j: (ja + j, 0)),
                  pl.BlockSpec((tn, k), lambda i, j: (jb + j, 0))],
        out_specs=pl.BlockSpec((tm, tn), lambda i, j: (i, j)),
        compiler_params=_cparams(("parallel", "parallel")),
        name="mm_glu",
    )(a, wt, wt)


def _mm_merge_kernel(cv_ref, lw_ref, lb_ref, w_ref, b_ref, h_ref, wga_ref, wgb_ref, ys_ref, o_ref, u_ref):
    @pl.when(pl.program_id(1) == 0)
    def _():
        xf = cv_ref[...]
        mu = jnp.mean(xf, axis=-1, keepdims=True)
        xc = xf - mu
        var = jnp.mean(xc * xc, axis=-1, keepdims=True)
        y = xc * lax.rsqrt(var + EPS) * lw_ref[...] + lb_ref[...]
        u_ref[...] = _silu(y).astype(u_ref.dtype)

    ycf = jnp.dot(u_ref[...], w_ref[...], preferred_element_type=F32) + b_ref[...]
    h = h_ref[...]
    gate_a = jax.nn.sigmoid(_dot_nt(h, wga_ref[...]))
    gate_b = jax.nn.sigmoid(_dot_nt(h, wgb_ref[...]))
    o_ref[...] = (gate_a * ys_ref[...] + gate_b * ycf).astype(o_ref.dtype)


def _mm_merge(cv, ln_w, ln_b, w, bias, h, wt, gate_start, y_ssd, tm=512, tn=512):
    m, k = cv.shape
    n = w.shape[1]
    tm = min(tm, m)
    ja, jb = gate_start // tn, (gate_start + n) // tn
    return pl.pallas_call(
        _mm_merge_kernel,
        out_shape=jax.ShapeDtypeStruct((m, n), BF16),
        grid=(m // tm, n // tn),
        in_specs=[pl.BlockSpec((tm, k), lambda i, j: (i, 0)),
                  pl.BlockSpec((1, k), lambda i, j: (0, 0)),
                  pl.BlockSpec((1, k), lambda i, j: (0, 0)),
                  pl.BlockSpec((k, tn), lambda i, j: (0, j)),
                  pl.BlockSpec((1, tn), lambda i, j: (0, j)),
                  pl.BlockSpec((tm, k), lambda i, j: (i, 0)),
                  pl.BlockSpec((tn, k), lambda i, j: (ja + j, 0)),
                  pl.BlockSpec((tn, k), lambda i, j: (jb + j, 0)),
                  pl.BlockSpec((tm, tn), lambda i, j: (i, j))],
        out_specs=pl.BlockSpec((tm, tn), lambda i, j: (i, j)),
        scratch_shapes=[pltpu.VMEM((tm, k), BF16)],
        compiler_params=_cparams(("parallel", "arbitrary")),
        name="mm_merge",
    )(cv, ln_w.reshape(1, k), ln_b.reshape(1, k), w, bias.reshape(1, n), h, wt, wt, y_ssd)


def _mm_resid_kernel(a_ref, w_ref, x_ref, g_ref, o_ref):
    out = jnp.dot(a_ref[...], w_ref[...], preferred_element_type=F32)
    o_ref[...] = x_ref[...] + g_ref[0] * out


def _mm_resid(a, w, x2, mod3, gate_chunk, rows_per_batch, tm=1024, tn=512):
    m, k = a.shape
    n = w.shape[1]
    tm = min(tm, rows_per_batch)
    nj = n // tn
    tiles_per_batch = rows_per_batch // tm
    return pl.pallas_call(
        _mm_resid_kernel,
        out_shape=jax.ShapeDtypeStruct((m, n), F32),
        grid=(m // tm, nj),
        in_specs=[pl.BlockSpec((tm, k), lambda i, j: (i, 0)),
                  pl.BlockSpec((k, tn), lambda i, j: (0, j)),
                  pl.BlockSpec((tm, tn), lambda i, j: (i, j)),
                  pl.BlockSpec((1, 1, tn),
                               lambda i, j: (i // tiles_per_batch, 0, gate_chunk * nj + j))],
        out_specs=pl.BlockSpec((tm, tn), lambda i, j: (i, j)),
        compiler_params=_cparams(("parallel", "parallel")),
        name="mm_resid",
    )(a, w, x2, mod3)


W_ALIGN = 2048


def _pack_wt_kernel(valid_ref, off_ref, w_ref, o_ref):
    del off_ref
    nrow = valid_ref[pl.program_id(0)]
    row = lax.broadcasted_iota(jnp.int32, o_ref.shape, 0)
    o_ref[...] = jnp.where(row < nrow, w_ref[...], 0.0).astype(o_ref.dtype)


def _pack_wt(wt, segments, tr=512):
    n, k = wt.shape
    starts, src_off, valid = [], [], []
    pos = 0
    for lo, hi in segments:
        pos = -(-pos // W_ALIGN) * W_ALIGN
        starts.append(pos)
        while len(src_off) < pos // tr:
            src_off.append(0)
            valid.append(0)
        for r in range(lo, hi, tr):
            src_off.append(min(r, n - tr))
            valid.append(min(tr, hi - r))
            assert r <= n - tr or hi - r == tr
        pos += -(-(hi - lo) // tr) * tr
    total = -(-pos // W_ALIGN) * W_ALIGN
    while len(src_off) < total // tr:
        src_off.append(0)
        valid.append(0)
    grid_spec = pltpu.PrefetchScalarGridSpec(
        num_scalar_prefetch=2,
        grid=(total // tr,),
        in_specs=[pl.BlockSpec((pl.Element(tr), pl.Element(k)), lambda t, v, off: (off[t] * 8, 0))],
        out_specs=pl.BlockSpec((tr, k), lambda t, v, off: (t, 0)),
    )
    packed = pl.pallas_call(
        _pack_wt_kernel,
        out_shape=jax.ShapeDtypeStruct((total, k), BF16),
        grid_spec=grid_spec,
        compiler_params=_cparams(("parallel",)),
        name="pack_wt",
    )(jnp.asarray(valid, jnp.int32), jnp.asarray(src_off, jnp.int32) // 8, wt)
    return packed, starts


_CONV_PAD = 8


def _conv7_kernel(ctx_ref, lat_ref, w_ref, b_ref, o_ref, pad_ref, *, l_ctx, l_lat):
    p = _CONV_PAD
    zeros = jnp.zeros((p, LANE), F32)
    off_ctx = p
    off_lat = 2 * p + l_ctx
    pad_ref[0:p, :] = zeros
    pad_ref[off_ctx + l_ctx:off_lat, :] = zeros
    pad_ref[off_lat + l_lat:off_lat + l_lat + p, :] = zeros
    pad_ref[off_ctx:off_ctx + l_ctx, :] = ctx_ref[0]
    pad_ref[off_lat:off_lat + l_lat, :] = lat_ref[0]
    reach = SSM_CONV // 2
    bias = b_ref[...]

    def chunk(pad_base, out_base):
        acc = jnp.broadcast_to(bias, (CHUNK, LANE))
        for k in range(SSM_CONV):
            tap = pad_ref[pl.ds(pad_base - reach + k, CHUNK), :]
            acc = acc + tap * w_ref[k:k + 1, :]
        o_ref[0, 0, pl.ds(out_base, CHUNK), :] = _silu(acc)

    def ctx_body(j, c):
        base = pl.multiple_of(j * CHUNK, CHUNK)
        chunk(off_ctx + base, base)
        return c

    def lat_body(j, c):
        base = pl.multiple_of(j * CHUNK, CHUNK)
        chunk(off_lat + base, l_ctx + base)
        return c

    lax.fori_loop(0, l_ctx // CHUNK, ctx_body, 0)
    lax.fori_loop(0, l_lat // CHUNK, lat_body, 0, unroll=2)


def _conv7(ctx_raw, lat_raw, w, b):
    bsz, l_ctx, c = ctx_raw.shape
    l_lat = lat_raw.shape[1]
    ltot = l_ctx + l_lat
    nct = c // LANE
    return pl.pallas_call(
        functools.partial(_conv7_kernel, l_ctx=l_ctx, l_lat=l_lat),
        out_shape=jax.ShapeDtypeStruct((bsz, nct, ltot, LANE), F32),
        grid=(bsz, nct),
        in_specs=[pl.BlockSpec((1, l_ctx, LANE), lambda bi, ci: (bi, 0, ci)),
                  pl.BlockSpec((1, l_lat, LANE), lambda bi, ci: (bi, 0, ci)),
                  pl.BlockSpec((SSM_CONV, LANE), lambda bi, ci: (0, ci)),
                  pl.BlockSpec((1, LANE), lambda bi, ci: (0, ci))],
        out_specs=pl.BlockSpec((1, 1, ltot, LANE), lambda bi, ci: (bi, ci, 0, 0)),
        scratch_shapes=[pltpu.VMEM((ltot + 3 * _CONV_PAD, LANE), F32)],
        compiler_params=_cparams(("parallel", "parallel")),
        name="conv7",
    )(ctx_raw, lat_raw, w, b.reshape(1, c))


def _ssd_kernel(xbc_ref, dtc_ref, dtl_ref, par_ref, dexp_ref, ex_ref, *rest, reverse, n_ctx, fuse_norm):
    if fuse_norm:
        yo_ref, sz_ref, nw_ref, o_ref, st_ref, cumt_ref, y_ref = rest
    else:
        y_ref, st_ref, cumt_ref = rest
    i = pl.program_id(1)

    @pl.when(i == 0)
    def _():
        st_ref[...] = jnp.zeros_like(st_ref)

    dt_raw = jnp.where(i < n_ctx, dtc_ref[0], dtl_ref[0])
    bias = par_ref[0:1, :]
    a = -jnp.exp(par_ref[1:2, :])
    dt = jax.nn.softplus(dt_raw + bias)
    cum = dt * a
    row = lax.broadcasted_iota(jnp.int32, (CHUNK, LANE), 0)
    k = 1
    while k < CHUNK:
        if reverse:
            cum = cum + jnp.where(row < CHUNK - k, pltpu.roll(cum, CHUNK - k, 0), 0.0)
        else:
            cum = cum + jnp.where(row >= k, pltpu.roll(cum, k, 0), 0.0)
        k *= 2
    last = 0 if reverse else CHUNK - 1
    cum = cum * LOG2E
    cumt_ref[...] = cum.T
    li = lax.broadcasted_iota(jnp.int32, (CHUNK, CHUNK), 0)
    si = lax.broadcasted_iota(jnp.int32, (CHUNK, CHUNK), 1)
    causal = (li <= si) if reverse else (li >= si)
    lo = lax.broadcasted_iota(jnp.int32, (CHUNK, LANE), 1) < HEAD_DIM
    heads_per_group = N_HEADS // N_GROUPS
    pairs = heads_per_group // 2
    x_tiles = N_HEADS // 2

    def group(g, carry):
        shift = (LANE - heads_per_group * g) & (LANE - 1)
        cum_g = pltpu.roll(cum, shift, 1)
        dt_g = pltpu.roll(dt, shift, 1)
        cum_t = cumt_ref[pl.ds(pl.multiple_of(heads_per_group * g, heads_per_group), heads_per_group), :]
        bb = xbc_ref[0, x_tiles + g].astype(BF16)
        cb = xbc_ref[0, x_tiles + N_GROUPS + g].astype(BF16)
        scores = lax.dot_general(cb, bb, (((1,), (1,)), ((), ())), preferred_element_type=F32)
        h_t = st_ref[g]
        y_off = jnp.dot(cb, h_t.astype(BF16), preferred_element_type=F32)
        d_hi = dt_g.astype(BF16)
        r_hi = dt_g - d_hi.astype(F32)
        d_mid = r_hi.astype(BF16)
        d_lo = (r_hi - d_mid.astype(F32)).astype(BF16)
        dt_x = (jnp.dot(jnp.concatenate([d_hi, d_mid], axis=1), ex_ref[...], preferred_element_type=F32)
                + jnp.dot(d_lo, ex_ref[0:LANE, :], preferred_element_type=F32))
        xw_parts, dec_parts = [], []
        for p in range(pairs):
            j0, j1 = 2 * p, 2 * p + 1
            x2 = xbc_ref[0, pairs * g + p]
            c0 = cum_g[:, j0:j0 + 1]
            c1 = cum_g[:, j1:j1 + 1]
            l0 = jnp.exp2(jnp.where(causal, c0 - cum_t[j0:j0 + 1, :], -jnp.inf))
            l1 = jnp.exp2(jnp.where(causal, c1 - cum_t[j1:j1 + 1, :], -jnp.inf))
            m0 = (scores * l0).astype(BF16)
            m1 = (scores * l1).astype(BF16)
            dt2 = dt_x[:, p * LANE:(p + 1) * LANE]
            c2 = jnp.where(lo, c0, c1)
            xdt = x2 * dt2
            xdt_b = xdt.astype(BF16)
            zero = jnp.zeros_like(xdt_b)
            y_diag = (jnp.dot(m0, jnp.where(lo, xdt_b, zero), preferred_element_type=F32)
                      + jnp.dot(m1, jnp.where(lo, zero, xdt_b), preferred_element_type=F32))
            e2 = jnp.exp2(c2)
            y = y_diag + y_off[:, p * LANE:(p + 1) * LANE] * e2
            y_ref[0, pairs * g + p] = y + dexp_ref[pairs * g + p] * x2
            to_end = jnp.exp2(c2[last:last + 1, :] - c2)
            xw_parts.append((xdt * to_end).astype(BF16))
            dec_parts.append(e2[last:last + 1, :])
        xw = jnp.concatenate(xw_parts, axis=1)
        dec = jnp.concatenate(dec_parts, axis=1)
        upd = lax.dot_general(bb, xw, (((0,), (0,)), ((), ())), preferred_element_type=F32)
        st_ref[g] = h_t * dec + upd
        return carry

    lax.fori_loop(0, N_GROUPS, group, 0, unroll=2)

    if fuse_norm:
        sq = jnp.zeros((CHUNK, LANE), F32)
        for j in range(x_tiles):
            gj = (y_ref[0, j] + yo_ref[0, j]) * sz_ref[:, j * LANE:(j + 1) * LANE]
            y_ref[0, j] = gj
            sq = sq + gj * gj
        r = lax.rsqrt(jnp.sum(sq, axis=-1, keepdims=True) / (x_tiles * LANE) + EPS)
        for j in range(x_tiles):
            sl = slice(j * LANE, (j + 1) * LANE)
            o_ref[:, sl] = (y_ref[0, j] * r * nw_ref[:, sl]).astype(o_ref.dtype)


def _ssd(xbc_act, dt_ctx, dt_lat, par, dexp, reverse, norm_with=None):
    bsz, ntile, ltot, _ = xbc_act.shape
    l_ctx = dt_ctx.shape[1]
    l_lat = dt_lat.shape[1]
    n_ctx = l_ctx // CHUNK
    n_lat = l_lat // CHUNK
    steps = n_ctx + n_lat
    x_tiles = N_HEADS // 2
    gw = (N_HEADS // N_GROUPS) * HEAD_DIM
    e1 = (jnp.arange(gw)[None, :] // HEAD_DIM == jnp.arange(LANE)[:, None]).astype(BF16)
    expand = jnp.concatenate([e1, e1], axis=0)

    if reverse:
        def cat_chunk(i):
            return jnp.where(i < n_ctx, n_ctx - 1 - i, n_ctx + steps - 1 - i)

        def ctx_chunk(i):
            return jnp.maximum(n_ctx - 1 - i, 0)

        def lat_chunk(i):
            return jnp.minimum(steps - 1 - i, n_lat - 1)
    else:
        def cat_chunk(i):
            return i

        def ctx_chunk(i):
            return jnp.minimum(i, n_ctx - 1)

        def lat_chunk(i):
            return jnp.maximum(i - n_ctx, 0)

    y_spec = pl.BlockSpec((1, x_tiles, CHUNK, LANE), lambda b, i: (b, 0, lat_chunk(i), 0))
    in_specs = [pl.BlockSpec((1, ntile, CHUNK, LANE), lambda b, i: (b, 0, cat_chunk(i), 0)),
                pl.BlockSpec((1, CHUNK, LANE), lambda b, i: (b, ctx_chunk(i), 0)),
                pl.BlockSpec((1, CHUNK, LANE), lambda b, i: (b, lat_chunk(i), 0)),
                pl.BlockSpec((8, LANE), lambda b, i: (0, 0)),
                pl.BlockSpec((x_tiles, 1, LANE), lambda b, i: (0, 0, 0)),
                pl.BlockSpec((2 * LANE, gw), lambda b, i: (0, 0))]
    args = [xbc_act, dt_ctx, dt_lat, par, dexp, expand]
    scratch = [pltpu.VMEM((N_GROUPS, D_STATE, gw), F32), pltpu.VMEM((LANE, CHUNK), F32)]
    if norm_with is None:
        out_shape = jax.ShapeDtypeStruct((bsz, x_tiles, l_lat, LANE), F32)
        out_spec = y_spec
    else:
        y_other, silu_z, norm_w = norm_with
        dn = x_tiles * LANE
        row_spec = pl.BlockSpec((CHUNK, dn), lambda b, i: (b * n_lat + lat_chunk(i), 0))
        in_specs += [y_spec, row_spec, pl.BlockSpec((1, dn), lambda b, i: (0, 0))]
        args += [y_other, silu_z, norm_w.reshape(1, dn)]
        out_shape = jax.ShapeDtypeStruct((bsz * l_lat, dn), BF16)
        out_spec = row_spec
        scratch.append(pltpu.VMEM((1, x_tiles, CHUNK, LANE), F32))
    return pl.pallas_call(
        functools.partial(_ssd_kernel, reverse=reverse, n_ctx=n_ctx, fuse_norm=norm_with is not None),
        out_shape=out_shape,
        grid=(bsz, steps),
        in_specs=in_specs,
        out_specs=out_spec,
        scratch_shapes=scratch,
        compiler_params=_cparams(("parallel", "arbitrary")),
        name="ssd_bwd" if reverse else "ssd_fwd",
    )(*args)


def _conv31_kernel(u_ref, w_ref, b_ref, o_ref, pad_ref, *, seq):
    halo = (CF_KERNEL // 2) * GRID_W
    zeros = jnp.zeros((halo, LANE), F32)
    pad_ref[0:halo, :] = zeros
    pad_ref[halo + seq:halo + seq + halo, :] = zeros
    pad_ref[halo:halo + seq, :] = u_ref[0]
    bias = b_ref[...]

    def body(j, c):
        base = pl.multiple_of(j * CHUNK, CHUNK)
        acc = jnp.broadcast_to(bias, (CHUNK, LANE))
        for k in range(CF_KERNEL):
            tap = pad_ref[pl.ds(pl.multiple_of(base + k * GRID_W, GRID_W), CHUNK), :]
            acc = acc + tap * w_ref[k:k + 1, :]
        o_ref[0, pl.ds(base, CHUNK), :] = acc
        return c

    lax.fori_loop(0, seq // CHUNK, body, 0, unroll=2)


def _conv31(u3, w, b):
    bsz, s, c = u3.shape
    halo = (CF_KERNEL // 2) * GRID_W
    return pl.pallas_call(
        functools.partial(_conv31_kernel, seq=s),
        out_shape=jax.ShapeDtypeStruct((bsz, s, c), F32),
        grid=(bsz, c // LANE),
        in_specs=[pl.BlockSpec((1, s, LANE), lambda bi, ci: (bi, 0, ci)),
                  pl.BlockSpec((CF_KERNEL, LANE), lambda bi, ci: (0, ci)),
                  pl.BlockSpec((1, LANE), lambda bi, ci: (0, ci))],
        out_specs=pl.BlockSpec((1, s, LANE), lambda bi, ci: (bi, 0, ci)),
        scratch_shapes=[pltpu.VMEM((s + 2 * halo, LANE), F32)],
        compiler_params=_cparams(("parallel", "parallel")),
        name="conv31",
    )(u3, w, b.reshape(1, c))


def _route_kernel(x_ref, w_ref, sh_ref, sc_ref, rw_ref, rb_ref, h_ref, eid_ref, ew_ref):
    xf = x_ref[...]
    ms = jnp.mean(xf * xf, axis=-1, keepdims=True)
    h = xf * lax.rsqrt(ms + EPS) * w_ref[...]
    h = h * (1.0 + sc_ref[0]) + sh_ref[0]
    tm = xf.shape[0]
    nt = xf.shape[1] // LANE
    pitch = _pitch(nt)
    for j in range(nt):
        h_ref[pl.ds(j, tm, stride=pitch), :] = h[:, j * LANE:(j + 1) * LANE]
    for j in range(nt, pitch):
        h_ref[pl.ds(j, tm, stride=pitch), :] = jnp.zeros((tm, LANE), F32)
    logits = jnp.dot(h.astype(BF16), rw_ref[...], preferred_element_type=F32) + rb_ref[...]
    lane = lax.broadcasted_iota(jnp.int32, (tm, LANE), 1)
    lane_f = lane.astype(F32)
    ninf = -jnp.inf
    gl = jnp.where(lane < MOE_GROUPS, logits, ninf)
    gmax = jnp.max(gl, axis=-1, keepdims=True)
    gidx = jnp.min(jnp.where(gl == gmax, lane_f, float(LANE)), axis=-1, keepdims=True)
    gsum = jnp.sum(jnp.exp(gl - gmax), axis=-1, keepdims=True)
    g_p = 1.0 / gsum
    first = float(MOE_GROUPS) + gidx * float(EXPERTS_PER_GROUP)
    in_group = (lane_f >= first) & (lane_f < first + float(EXPERTS_PER_GROUP))
    el = jnp.where(in_group, logits, ninf)
    m1 = jnp.max(el, axis=-1, keepdims=True)
    i1 = jnp.min(jnp.where(el == m1, lane_f, float(LANE)), axis=-1, keepdims=True)
    el2 = jnp.where(lane_f == i1, ninf, el)
    m2 = jnp.max(el2, axis=-1, keepdims=True)
    i2 = jnp.min(jnp.where(el2 == m2, lane_f, float(LANE)), axis=-1, keepdims=True)
    e21 = jnp.exp(m2 - m1)
    den = 1.0 + e21
    w1 = (1.0 / den) * g_p
    w2 = (e21 / den) * g_p
    e1 = (i1 - float(MOE_GROUPS)).astype(jnp.int32)
    e2 = (i2 - float(MOE_GROUPS)).astype(jnp.int32)
    eid_ref[...] = jnp.where(lane == 0, e1, jnp.where(lane == 1, e2, 0))
    ew_ref[...] = jnp.where(lane == 0, w1, jnp.where(lane == 1, w2, 0.0))


def _route(x2, w, mod3, shift_chunk, scale_chunk, rows_per_batch, rw, rb, tm=256):
    m, d = x2.shape
    pitch = _pitch(d // LANE)
    tiles_per_batch = rows_per_batch // tm
    return pl.pallas_call(
        _route_kernel,
        out_shape=(jax.ShapeDtypeStruct((m * pitch, LANE), F32),
                   jax.ShapeDtypeStruct((m, LANE), jnp.int32),
                   jax.ShapeDtypeStruct((m, LANE), F32)),
        grid=(m // tm,),
        in_specs=[pl.BlockSpec((tm, d), lambda i: (i, 0)),
                  pl.BlockSpec((1, d), lambda i: (0, 0)),
                  pl.BlockSpec((1, 1, d), lambda i: (i // tiles_per_batch, 0, shift_chunk)),
                  pl.BlockSpec((1, 1, d), lambda i: (i // tiles_per_batch, 0, scale_chunk)),
                  pl.BlockSpec((d, LANE), lambda i: (0, 0)),
                  pl.BlockSpec((1, LANE), lambda i: (0, 0))],
        out_specs=(pl.BlockSpec((tm * pitch, LANE), lambda i: (i, 0)),
                   pl.BlockSpec((tm, LANE), lambda i: (i, 0)),
                   pl.BlockSpec((tm, LANE), lambda i: (i, 0))),
        compiler_params=_cparams(("parallel",)),
        name="route",
    )(x2, w.reshape(1, d), mod3, mod3, rw, rb)


_DMA_UNROLL = 8


def _rows_to_matrix(ref, tm, nt):
    return jnp.concatenate([ref[pl.ds(j, tm, stride=_pitch(nt)), :] for j in range(nt)], axis=1)


def _token_copy(src, dst, sem, nt, rows, src_tok, dst_tok):
    pitch = _pitch(nt)
    return pltpu.make_async_copy(src.at[pl.ds(src_tok * pitch, rows), :],
                                 dst.at[pl.ds(dst_tok * pitch, rows), :], sem)


def _bulk_wait(src, dst, sem, total_rows):
    pltpu.make_async_copy(src.at[pl.ds(0, total_rows), :], dst.at[pl.ds(0, total_rows), :], sem).wait()


def _for_rows(n, body):
    groups = lax.shift_right_logical(n, _DMA_UNROLL.bit_length() - 1)

    def group(g, c):
        for u in range(_DMA_UNROLL):
            body(g * _DMA_UNROLL + u)
        return c

    def tail(r, c):
        body(r)
        return c

    lax.fori_loop(0, groups, group, 0)
    lax.fori_loop(groups * _DMA_UNROLL, n, tail, 0)


def _wait_rows(n, wait_tokens):
    p = MOE_BLOCK
    while p >= 1:
        @pl.when((n & p) != 0)
        def _(p=p):
            wait_tokens(p)
        p //= 2


def _stream_expert_weights(b, be_ref, eord_ref, enext_ref, w_hbms, w_bufs, w_caches, wsem):
    prev = jnp.maximum(b - 1, 0)

    def copies(e, slot):
        return [pltpu.make_async_copy(w.at[e], buf.at[slot], wsem.at[slot]) for w, buf in zip(w_hbms, w_bufs)]

    @pl.when(b == 0)
    def _():
        for cp in copies(be_ref[0], 0):
            cp.start()

    @pl.when((b == 0) | (be_ref[b] != be_ref[prev]))
    def _():
        for s in range(2):
            @pl.when((eord_ref[b] & 1) == s)
            def _(s=s):
                for cp in copies(be_ref[b], s):
                    cp.wait()

                @pl.when(enext_ref[b] >= 0)
                def _():
                    for cp in copies(enext_ref[b], 1 - s):
                        cp.start()

                for buf, cache in zip(w_bufs, w_caches):
                    cache[...] = buf[s].astype(BF16)


def _expert_up_kernel(be_ref, nv_ref, nused_ref, eord_ref, enext_ref, tokc_ref, tokn_ref, h_hbm, wg_hbm, wu_hbm,
                      o_ref, xs0_ref, xs1_ref, wgs_ref, wus_ref, wgb_ref, wub_ref, sem, wsem, *, fchunk, nt):
    b = pl.program_id(0)
    n_used = nused_ref[0]
    dff = wgb_ref.shape[1]
    slots = (xs0_ref, xs1_ref)

    def start_gather(tok_ref, blk, slot):
        _for_rows(nv_ref[blk], lambda r: _token_copy(
            h_hbm, slots[slot], sem.at[slot], nt, nt, tok_ref[0, 0, r], r).start())

    def wait_gather(blk, slot):
        _wait_rows(nv_ref[blk], lambda p: _bulk_wait(h_hbm, slots[slot], sem.at[slot], p * nt))

    @pl.when(b == 0)
    def _():
        xs0_ref[...] = jnp.zeros_like(xs0_ref)
        xs1_ref[...] = jnp.zeros_like(xs1_ref)
        start_gather(tokc_ref, 0, 0)

    for slot in range(2):
        @pl.when((b + 1 < n_used) & (lax.rem(b, 2) == slot))
        def _(slot=slot):
            start_gather(tokn_ref, b + 1, 1 - slot)

    @pl.when(b < n_used)
    def _():
        _stream_expert_weights(b, be_ref, eord_ref, enext_ref, (wg_hbm, wu_hbm), (wgs_ref, wus_ref),
                               (wgb_ref, wub_ref), wsem)

        for slot in range(2):
            @pl.when(lax.rem(b, 2) == slot)
            def _(slot=slot):
                wait_gather(b, slot)
                xb = _rows_to_matrix(slots[slot], MOE_BLOCK, nt).astype(BF16)
                for f in range(dff // fchunk):
                    sl = slice(f * fchunk, (f + 1) * fchunk)
                    gate = jnp.dot(xb, wgb_ref[:, sl], preferred_element_type=F32)
                    up = jnp.dot(xb, wub_ref[:, sl], preferred_element_type=F32)
                    o_ref[:, sl] = (_silu(gate) * up).astype(o_ref.dtype)

    @pl.when(b >= n_used)
    def _():
        o_ref[...] = jnp.zeros_like(o_ref)


def _expert_up(h2t, buf_tok, w_gate, w_up, tables, fchunk=256):
    n_blocks = buf_tok.shape[0]
    _, d, dff = w_gate.shape
    nt = d // LANE
    slot_rows = MOE_BLOCK * _pitch(nt)
    grid_spec = pltpu.PrefetchScalarGridSpec(
        num_scalar_prefetch=len(tables),
        grid=(n_blocks,),
        in_specs=[pl.BlockSpec((1, 1, MOE_BLOCK), lambda b, *_: (b, 0, 0), memory_space=pltpu.SMEM),
                  pl.BlockSpec((1, 1, MOE_BLOCK), lambda b, *_: (jnp.minimum(b + 1, n_blocks - 1), 0, 0),
                               memory_space=pltpu.SMEM),
                  pl.BlockSpec(memory_space=pl.ANY),
                  pl.BlockSpec(memory_space=pl.ANY),
                  pl.BlockSpec(memory_space=pl.ANY)],
        out_specs=pl.BlockSpec((MOE_BLOCK, dff), lambda b, *_: (b, 0)),
        scratch_shapes=[pltpu.VMEM((slot_rows, LANE), F32),
                        pltpu.VMEM((slot_rows, LANE), F32),
                        pltpu.VMEM((2, d, dff), F32),
                        pltpu.VMEM((2, d, dff), F32),
                        pltpu.VMEM((d, dff), BF16),
                        pltpu.VMEM((d, dff), BF16),
                        pltpu.SemaphoreType.DMA((2,)),
                        pltpu.SemaphoreType.DMA((2,))],
    )
    return pl.pallas_call(
        functools.partial(_expert_up_kernel, fchunk=fchunk, nt=nt),
        out_shape=jax.ShapeDtypeStruct((n_blocks * MOE_BLOCK, dff), BF16),
        grid_spec=grid_spec,
        compiler_params=_cparams(("arbitrary",)),
        name="expert_up",
    )(*tables, buf_tok, buf_tok, h2t, w_gate, w_up)


def _expert_down_kernel(be_ref, nv_ref, nused_ref, eord_ref, enext_ref, asg_ref, h_ref, wd_hbm, y_hbm,
                        ys0_ref, ys1_ref, wds_ref, wdb_ref, sem, wsem, *, nchunk, nt, n_tok):
    b = pl.program_id(0)
    n_used = nused_ref[0]
    d = wdb_ref.shape[1]
    slots = (ys0_ref, ys1_ref)
    pitch = _pitch(nt)

    def row_copy(slot, r, assign):
        k = assign & 1
        tok = lax.shift_right_logical(assign, 1)
        return _token_copy(slots[slot], y_hbm, sem.at[slot], nt, pitch, r, k * n_tok + tok)

    def start_scatter(blk, slot):
        _for_rows(nv_ref[blk], lambda r: row_copy(slot, r, asg_ref[0, 0, r]).start())

    def wait_scatter(blk, slot):
        _wait_rows(nv_ref[blk], lambda p: _bulk_wait(slots[slot], y_hbm, sem.at[slot], p * pitch))

    @pl.when(b == 0)
    def _():
        ys0_ref[...] = jnp.zeros_like(ys0_ref)
        ys1_ref[...] = jnp.zeros_like(ys1_ref)

    @pl.when(b < n_used)
    def _():
        _stream_expert_weights(b, be_ref, eord_ref, enext_ref, (wd_hbm,), (wds_ref,), (wdb_ref,), wsem)
        hb = h_ref[...]
        for slot in range(2):
            @pl.when(lax.rem(b, 2) == slot)
            def _(slot=slot):
                for c in range(d // nchunk):
                    out = jnp.dot(hb, wdb_ref[:, c * nchunk:(c + 1) * nchunk], preferred_element_type=F32)
                    for j in range(nchunk // LANE):
                        slots[slot][pl.ds(c * (nchunk // LANE) + j, MOE_BLOCK, stride=pitch), :] = (
                            out[:, j * LANE:(j + 1) * LANE])

    for slot in range(2):
        @pl.when((b >= 1) & (b - 1 < n_used) & (lax.rem(b, 2) == slot))
        def _(slot=slot):
            wait_scatter(b - 1, 1 - slot)

        @pl.when((b < n_used) & (lax.rem(b, 2) == slot))
        def _(slot=slot):
            start_scatter(b, slot)

        @pl.when((b == pl.num_programs(0) - 1) & (b < n_used) & (lax.rem(b, 2) == slot))
        def _(slot=slot):
            wait_scatter(b, slot)


def _expert_down(hid, buf_assign, w_down, tables, n_tok, nchunk=512):
    n_rows, dff = hid.shape
    n_blocks = n_rows // MOE_BLOCK
    d = w_down.shape[2]
    nt = d // LANE
    pitch = _pitch(nt)
    grid_spec = pltpu.PrefetchScalarGridSpec(
        num_scalar_prefetch=len(tables),
        grid=(n_blocks,),
        in_specs=[pl.BlockSpec((1, 1, MOE_BLOCK), lambda b, *_: (b, 0, 0), memory_space=pltpu.SMEM),
                  pl.BlockSpec((MOE_BLOCK, dff), lambda b, be, nv, n, *_: (jnp.minimum(b, n[0] - 1), 0)),
                  pl.BlockSpec(memory_space=pl.ANY)],
        out_specs=pl.BlockSpec(memory_space=pl.ANY),
        scratch_shapes=[pltpu.VMEM((MOE_BLOCK * pitch, LANE), F32),
                        pltpu.VMEM((MOE_BLOCK * pitch, LANE), F32),
                        pltpu.VMEM((2, dff, d), F32),
                        pltpu.VMEM((dff, d), BF16),
                        pltpu.SemaphoreType.DMA((2,)),
                        pltpu.SemaphoreType.DMA((2,))],
    )
    return pl.pallas_call(
        functools.partial(_expert_down_kernel, nchunk=nchunk, nt=nt, n_tok=n_tok),
        out_shape=jax.ShapeDtypeStruct((2 * n_tok * pitch, LANE), F32),
        grid_spec=grid_spec,
        compiler_params=_cparams(("arbitrary",)),
        name="expert_down",
    )(*tables, buf_assign, hid, w_down)


def _combine_kernel(y0_ref, y1_ref, ew_ref, x_ref, g_ref, w_ref, o_ref):
    tm, d = x_ref.shape
    nt = d // LANE
    ew = ew_ref[...]
    moe = (_rows_to_matrix(y0_ref, tm, nt) * ew[:, 0:1]
           + _rows_to_matrix(y1_ref, tm, nt) * ew[:, 1:2])
    xo = x_ref[...] + g_ref[0] * moe
    ms = jnp.mean(xo * xo, axis=-1, keepdims=True)
    o_ref[...] = xo * lax.rsqrt(ms + EPS) * w_ref[...]


def _combine(y, ew, x2, mod3, gate_chunk, rows_per_batch, final_w, tm=256):
    m, d = x2.shape
    pitch = _pitch(d // LANE)
    tiles = m // tm
    tiles_per_batch = rows_per_batch // tm
    return pl.pallas_call(
        _combine_kernel,
        out_shape=jax.ShapeDtypeStruct((m, d), F32),
        grid=(tiles,),
        in_specs=[pl.BlockSpec((tm * pitch, LANE), lambda i: (i, 0)),
                  pl.BlockSpec((tm * pitch, LANE), lambda i: (tiles + i, 0)),
                  pl.BlockSpec((tm, LANE), lambda i: (i, 0)),
                  pl.BlockSpec((tm, d), lambda i: (i, 0)),
                  pl.BlockSpec((1, 1, d), lambda i: (i // tiles_per_batch, 0, gate_chunk)),
                  pl.BlockSpec((1, d), lambda i: (0, 0))],
        out_specs=pl.BlockSpec((tm, d), lambda i: (i, 0)),
        compiler_params=_cparams(("parallel",)),
        name="moe_combine",
    )(y, y, ew, x2, mod3, final_w.reshape(1, d))


def _dispatch_tables(eid, n_tok):
    top_k = eid.shape[1]
    n_assign = n_tok * top_k
    expert = eid.reshape(-1)
    key = jnp.sort(expert * n_assign + jnp.arange(n_assign, dtype=jnp.int32))
    sorted_assign = key % n_assign
    bounds = jnp.arange(N_EXPERTS + 1, dtype=jnp.int32) * n_assign
    start = jnp.searchsorted(key, bounds, side="left").astype(jnp.int32)
    counts = start[1:] - start[:-1]
    nblk = (counts + MOE_BLOCK - 1) // MOE_BLOCK
    blk_end = jnp.cumsum(nblk)
    blk_start = blk_end - nblk
    n_blocks = -(-n_assign // MOE_BLOCK) + N_EXPERTS
    bidx = jnp.arange(n_blocks, dtype=jnp.int32)
    block_expert = jnp.minimum(jnp.searchsorted(blk_end, bidx, side="right"), N_EXPERTS - 1).astype(jnp.int32)
    in_expert = (bidx - blk_start[block_expert]) * MOE_BLOCK
    n_valid = jnp.clip(counts[block_expert] - in_expert, 0, MOE_BLOCK).astype(jnp.int32)
    src = start[block_expert][:, None] + in_expert[:, None] + jnp.arange(MOE_BLOCK, dtype=jnp.int32)[None, :]
    valid = jnp.arange(MOE_BLOCK, dtype=jnp.int32)[None, :] < n_valid[:, None]
    buf_assign = jnp.where(valid, sorted_assign[jnp.clip(src, 0, n_assign - 1)], 0).astype(jnp.int32)
    n_used = blk_end[-1].astype(jnp.int32)
    first = jnp.concatenate([jnp.ones((1,), jnp.int32),
                             (block_expert[1:] != block_expert[:-1]).astype(jnp.int32)])
    expert_ordinal = (jnp.cumsum(first) - 1).astype(jnp.int32)
    next_blk = blk_end[block_expert]
    next_expert = jnp.where(next_blk < n_used, block_expert[jnp.minimum(next_blk, n_blocks - 1)], -1).astype(jnp.int32)
    tables = (block_expert, n_valid, n_used.reshape(1), expert_ordinal, next_expert)
    return buf_assign.reshape(n_blocks, 1, MOE_BLOCK), tables


def kernel(x, c, ctx, c_ctx, ada_w, ada_b, norm1_w, w_in, ssm_conv_w, ssm_conv_b, dt_bias, a_log, d_skip, ssm_norm_w, ssm_out_w, cf_dw_w, cf_dw_b, cf_ln_w, cf_ln_b, cf_out_w, cf_out_b, w_o, norm2_w, router_group_w, router_group_b, router_expert_w, router_expert_b, expert_w_gate, expert_w_up, expert_w_down, final_norm_w):
    bsz, seq, d = x.shape
    l_ctx = ctx.shape[1]
    n_tok = bsz * seq
    d_inner = ssm_norm_w.shape[1]
    gn = N_GROUPS * D_STATE
    xbc_dim = d_inner + 2 * gn
    off_dt = xbc_dim
    off_z = off_dt + N_HEADS
    off_glu = off_z + d_inner
    off_gate = off_glu + 2 * d

    ctx_row = bsz
    crows = jnp.zeros((8, d), F32).at[:bsz].set(c).at[ctx_row].set(c_ctx)
    mod = _ada(crows, ada_w[0], ada_b[0])
    mod3 = mod.reshape(8, 1, 6 * d)
    lat_rows = jnp.arange(bsz, dtype=jnp.int32)
    ctx_rows = jnp.full((bsz,), ctx_row, jnp.int32)

    h_lat = _normmod(x, norm1_w[0], mod3, lat_rows, 0, 1, BF16).reshape(n_tok, d)
    h_ctx = _normmod(ctx, norm1_w[0], mod3, ctx_rows, 0, 1, BF16).reshape(bsz * l_ctx, d)

    wt, (r_xbc, r_dt, r_z, r_glu, r_gate) = _pack_wt(
        jnp.transpose(w_in[0]),
        [(0, xbc_dim), (off_dt, off_z), (off_z, off_glu), (off_glu, off_gate), (off_gate, off_gate + 2 * d)])

    xbc_lat = _mm(h_lat, wt, tn=2048, name="in_xbc", rows=(r_xbc, xbc_dim)).reshape(bsz, seq, xbc_dim)
    xbc_ctx = _mm(h_ctx, wt, tm=512, name="in_xbc_ctx", rows=(r_xbc, xbc_dim)).reshape(bsz, l_ctx, xbc_dim)
    dt_lat = _mm(h_lat, wt, name="in_dt", rows=(r_dt, LANE)).reshape(bsz, seq, LANE)
    dt_ctx = _mm(h_ctx, wt, tm=512, name="in_dt_ctx", rows=(r_dt, LANE)).reshape(bsz, l_ctx, LANE)
    sz = _mm(h_lat, wt, act="silu", tn=2048, name="in_z", rows=(r_z, d_inner))
    u = _mm_glu(h_lat, wt, r_glu, d, tn=1024)

    xbc_act = _conv7(xbc_ctx, xbc_lat, ssm_conv_w[0], ssm_conv_b[0])

    def ssd_params(k):
        par = jnp.zeros((8, LANE), F32).at[0, :N_HEADS].set(dt_bias[0, k]).at[1, :N_HEADS].set(a_log[0, k])
        return par, jnp.repeat(d_skip[0, k], HEAD_DIM).reshape(N_HEADS // 2, 1, LANE)

    y_bwd = _ssd(xbc_act, dt_ctx, dt_lat, *ssd_params(1), reverse=True)
    gnorm = _ssd(xbc_act, dt_ctx, dt_lat, *ssd_params(0), reverse=False,
                 norm_with=(y_bwd, sz, ssm_norm_w[0]))
    y_ssd = _mm(gnorm, ssm_out_w[0].astype(BF16), tn=512, name="ssm_out")

    cv = _conv31(u.reshape(bsz, seq, d), cf_dw_w[0], cf_dw_b[0]).reshape(n_tok, d)
    merged = _mm_merge(cv, cf_ln_w[0], cf_ln_b[0], cf_out_w[0].astype(BF16), cf_out_b[0],
                       h_lat, wt, r_gate, y_ssd)
    x1 = _mm_resid(merged, w_o[0].astype(BF16), x.reshape(n_tok, d), mod3, 2, seq, tm=2048)

    n_r = MOE_GROUPS + N_EXPERTS
    rw = jnp.pad(jnp.concatenate([router_group_w[0], router_expert_w[0]], axis=1),
                 ((0, 0), (0, LANE - n_r))).astype(BF16)
    rb = jnp.pad(jnp.concatenate([router_group_b[0], router_expert_b[0]]), (0, LANE - n_r)).reshape(1, LANE)
    h2t, eid, ew = _route(x1, norm2_w[0], mod3, 3, 4, seq, rw, rb)

    buf_assign, tables = _dispatch_tables(eid[:, :2], n_tok)
    hid = _expert_up(h2t, buf_assign // 2, expert_w_gate[0], expert_w_up[0], tables)
    y = _expert_down(hid, buf_assign, expert_w_down[0], tables, n_tok)
    out = _combine(y, ew, x1, mod3, 5, seq, final_norm_w)
    return out.reshape(bsz, seq, d)
```

```python
import functools

import jax
import jax.numpy as jnp
from jax import lax
from jax.experimental import pallas as pl
from jax.experimental.pallas import tpu as pltpu

F32 = jnp.float32
BF16 = jnp.bfloat16

EPS = 1e-6
GRID_W = 64
HEAD_DIM = 64
N_HEADS = 64
N_GROUPS = 8
D_STATE = 128
CHUNK = 128
SSM_CONV = 7
CF_KERNEL = 31
MOE_GROUPS = 8
EXPERTS_PER_GROUP = 8
N_EXPERTS = 64
MOE_BLOCK = 256
LANE = 128
LOG2E = 1.4426950408889634
VMEM_LIMIT = 56 * 1024 * 1024


def _cparams(sem):
    return pltpu.CompilerParams(dimension_semantics=sem, vmem_limit_bytes=VMEM_LIMIT)


def _silu(v):
    return v * jax.nn.sigmoid(v)


def _pitch(nt):
    return nt + 1


def _ada_kernel(c_ref, w_ref, b_ref, o_ref):
    s = _silu(c_ref[...])
    o_ref[...] = jnp.dot(s.astype(BF16), w_ref[...].astype(BF16),
                         preferred_element_type=F32) + b_ref[...]


def _ada(crows, ada_w, ada_b, tn=1024):
    r, d = crows.shape
    n = ada_w.shape[1]
    return pl.pallas_call(
        _ada_kernel,
        out_shape=jax.ShapeDtypeStruct((r, n), F32),
        grid=(n // tn,),
        in_specs=[pl.BlockSpec((r, d), lambda j: (0, 0)),
                  pl.BlockSpec((d, tn), lambda j: (0, j)),
                  pl.BlockSpec((1, tn), lambda j: (0, j))],
        out_specs=pl.BlockSpec((r, tn), lambda j: (0, j)),
        compiler_params=_cparams(("parallel",)),
        name="ada",
    )(crows, ada_w, ada_b.reshape(1, n))


def _normmod_kernel(rows_ref, x_ref, w_ref, sh_ref, sc_ref, o_ref):
    del rows_ref
    xf = x_ref[0]
    ms = jnp.mean(xf * xf, axis=-1, keepdims=True)
    y = xf * lax.rsqrt(ms + EPS) * w_ref[...]
    o_ref[0] = (y * (1.0 + sc_ref[0]) + sh_ref[0]).astype(o_ref.dtype)


def _normmod(x3, w, mod3, rows, shift_chunk, scale_chunk, out_dtype, tm=256):
    bx, l, d = x3.shape
    grid_spec = pltpu.PrefetchScalarGridSpec(
        num_scalar_prefetch=1,
        grid=(bx, l // tm),
        in_specs=[pl.BlockSpec((1, tm, d), lambda b, i, r: (b, i, 0)),
                  pl.BlockSpec((1, d), lambda b, i, r: (0, 0)),
                  pl.BlockSpec((1, 1, d), lambda b, i, r: (r[b], 0, shift_chunk)),
                  pl.BlockSpec((1, 1, d), lambda b, i, r: (r[b], 0, scale_chunk))],
        out_specs=pl.BlockSpec((1, tm, d), lambda b, i, r: (b, i, 0)),
    )
    return pl.pallas_call(
        _normmod_kernel,
        out_shape=jax.ShapeDtypeStruct((bx, l, d), out_dtype),
        grid_spec=grid_spec,
        compiler_params=_cparams(("parallel", "parallel")),
        name="normmod",
    )(rows, x3, w.reshape(1, d), mod3, mod3)


def _dot_nt(a, wt):
    return lax.dot_general(a, wt, (((1,), (1,)), ((), ())), preferred_element_type=F32)


def _mm_kernel(a_ref, w_ref, *rest, act, has_bias, w_rows):
    o_ref = rest[-1]
    a = a_ref[...]
    acc = _dot_nt(a, w_ref[...]) if w_rows else jnp.dot(a, w_ref[...], preferred_element_type=F32)
    if has_bias:
        acc = acc + rest[0][...]
    if act == "silu":
        acc = _silu(acc)
    elif act == "sigmoid":
        acc = jax.nn.sigmoid(acc)
    o_ref[...] = acc.astype(o_ref.dtype)


def _mm(a, w, bias=None, act=None, out_dtype=F32, tm=1024, tn=1024, name="mm", rows=None):
    m, k = a.shape
    start, n = (0, w.shape[1]) if rows is None else rows
    tm, tn = min(tm, m), min(tn, n)
    j0 = start // tn
    w_spec = (pl.BlockSpec((k, tn), lambda i, j: (0, j)) if rows is None
              else pl.BlockSpec((tn, k), lambda i, j: (j0 + j, 0)))
    in_specs = [pl.BlockSpec((tm, k), lambda i, j: (i, 0)), w_spec]
    args = [a, w]
    if bias is not None:
        in_specs.append(pl.BlockSpec((1, tn), lambda i, j: (0, j)))
        args.append(bias.reshape(1, n))
    return pl.pallas_call(
        functools.partial(_mm_kernel, act=act, has_bias=bias is not None, w_rows=rows is not None),
        out_shape=jax.ShapeDtypeStruct((m, n), out_dtype),
        grid=(m // tm, n // tn),
        in_specs=in_specs,
        out_specs=pl.BlockSpec((tm, tn), lambda i, j: (i, j)),
        compiler_params=_cparams(("parallel", "parallel")),
        name=name,
    )(*args)


def _mm_glu_kernel(a_ref, wa_ref, wb_ref, o_ref):
    a = a_ref[...]
    va = _dot_nt(a, wa_ref[...])
    vb = _dot_nt(a, wb_ref[...])
    o_ref[...] = va * jax.nn.sigmoid(vb)


def _mm_glu(a, wt, start, n, tm=1024, tn=512):
    m, k = a.shape
    tm = min(tm, m)
    ja, jb = start // tn, (start + n) // tn
    return pl.pallas_call(
        _mm_glu_kernel,
        out_shape=jax.ShapeDtypeStruct((m, n), F32),
        grid=(m // tm, n // tn),
        in_specs=[pl.BlockSpec((tm, k), lambda i, j: (i, 0)),
                  pl.BlockSpec((tn, k), lambda i, j: (ja + j, 0)),
                  pl.BlockSpec((tn, k), lambda i, j: (jb + j, 0))],
        out_specs=pl.BlockSpec((tm, tn), lambda i, j: (i, j)),
        compiler_params=_cparams(("parallel", "parallel")),
        name="mm_glu",
    )(a, wt, wt)


def _mm_merge_kernel(cv_ref, lw_ref, lb_ref, w_ref, b_ref, h_ref, wga_ref, wgb_ref, ys_ref, o_ref, u_ref):
    @pl.when(pl.program_id(1) == 0)
    def _():
        xf = cv_ref[...]
        mu = jnp.mean(xf, axis=-1, keepdims=True)
        xc = xf - mu
        var = jnp.mean(xc * xc, axis=-1, keepdims=True)
        y = xc * lax.rsqrt(var + EPS) * lw_ref[...] + lb_ref[...]
        u_ref[...] = _silu(y).astype(u_ref.dtype)

    ycf = jnp.dot(u_ref[...], w_ref[...], preferred_element_type=F32) + b_ref[...]
    h = h_ref[...]
    gate_a = jax.nn.sigmoid(_dot_nt(h, wga_ref[...]))
    gate_b = jax.nn.sigmoid(_dot_nt(h, wgb_ref[...]))
    o_ref[...] = (gate_a * ys_ref[...] + gate_b * ycf).astype(o_ref.dtype)


def _mm_merge(cv, ln_w, ln_b, w, bias, h, wt, gate_start, y_ssd, tm=512, tn=512):
    m, k = cv.shape
    n = w.shape[1]
    tm = min(tm, m)
    ja, jb = gate_start // tn, (gate_start + n) // tn
    return pl.pallas_call(
        _mm_merge_kernel,
        out_shape=jax.ShapeDtypeStruct((m, n), BF16),
        grid=(m // tm, n // tn),
        in_specs=[pl.BlockSpec((tm, k), lambda i, j: (i, 0)),
                  pl.BlockSpec((1, k), lambda i, j: (0, 0)),
                  pl.BlockSpec((1, k), lambda i, j: (0, 0)),
                  pl.BlockSpec((k, tn), lambda i, j: (0, j)),
                  pl.BlockSpec((1, tn), lambda i, j: (0, j)),
                  pl.BlockSpec((tm, k), lambda i, j: (i, 0)),
                  pl.BlockSpec((tn, k), lambda i, j: (ja + j, 0)),
                  pl.BlockSpec((tn, k), lambda i, j: (jb + j, 0)),
                  pl.BlockSpec((tm, tn), lambda i, j: (i, j))],
        out_specs=pl.BlockSpec((tm, tn), lambda i, j: (i, j)),
        scratch_shapes=[pltpu.VMEM((tm, k), BF16)],
        compiler_params=_cparams(("parallel", "arbitrary")),
        name="mm_merge",
    )(cv, ln_w.reshape(1, k), ln_b.reshape(1, k), w, bias.reshape(1, n), h, wt, wt, y_ssd)


def _mm_resid_kernel(a_ref, w_ref, x_ref, g_ref, o_ref):
    out = jnp.dot(a_ref[...], w_ref[...], preferred_element_type=F32)
    o_ref[...] = x_ref[...] + g_ref[0] * out


def _mm_resid(a, w, x2, mod3, gate_chunk, rows_per_batch, tm=1024, tn=512):
    m, k = a.shape
    n = w.shape[1]
    tm = min(tm, rows_per_batch)
    nj = n // tn
    tiles_per_batch = rows_per_batch // tm
    return pl.pallas_call(
        _mm_resid_kernel,
        out_shape=jax.ShapeDtypeStruct((m, n), F32),
        grid=(m // tm, nj),
        in_specs=[pl.BlockSpec((tm, k), lambda i, j: (i, 0)),
                  pl.BlockSpec((k, tn), lambda i, j: (0, j)),
                  pl.BlockSpec((tm, tn), lambda i, j: (i, j)),
                  pl.BlockSpec((1, 1, tn),
                               lambda i, j: (i // tiles_per_batch, 0, gate_chunk * nj + j))],
        out_specs=pl.BlockSpec((tm, tn), lambda i, j: (i, j)),
        compiler_params=_cparams(("parallel", "parallel")),
        name="mm_resid",
    )(a, w, x2, mod3)


W_ALIGN = 2048


def _pack_wt_kernel(valid_ref, off_ref, w_ref, o_ref):
    del off_ref
    nrow = valid_ref[pl.program_id(0)]
    row = lax.broadcasted_iota(jnp.int32, o_ref.shape, 0)
    o_ref[...] = jnp.where(row < nrow, w_ref[...], 0.0).astype(o_ref.dtype)


def _pack_wt(wt, segments, tr=512):
    n, k = wt.shape
    starts, src_off, valid = [], [], []
    pos = 0
    for lo, hi in segments:
        pos = -(-pos // W_ALIGN) * W_ALIGN
        starts.append(pos)
        while len(src_off) < pos // tr:
            src_off.append(0)
            valid.append(0)
        for r in range(lo, hi, tr):
            src_off.append(min(r, n - tr))
            valid.append(min(tr, hi - r))
            assert r <= n - tr or hi - r == tr
        pos += -(-(hi - lo) // tr) * tr
    total = -(-pos // W_ALIGN) * W_ALIGN
    while len(src_off) < total // tr:
        src_off.append(0)
        valid.append(0)
    grid_spec = pltpu.PrefetchScalarGridSpec(
        num_scalar_prefetch=2,
        grid=(total // tr,),
        in_specs=[pl.BlockSpec((pl.Element(tr), pl.Element(k)), lambda t, v, off: (off[t] * 8, 0))],
        out_specs=pl.BlockSpec((tr, k), lambda t, v, off: (t, 0)),
    )
    packed = pl.pallas_call(
        _pack_wt_kernel,
        out_shape=jax.ShapeDtypeStruct((total, k), BF16),
        grid_spec=grid_spec,
        compiler_params=_cparams(("parallel",)),
        name="pack_wt",
    )(jnp.asarray(valid, jnp.int32), jnp.asarray(src_off, jnp.int32) // 8, wt)
    return packed, starts


_CONV_PAD = 8


def _conv7_kernel(ctx_ref, lat_ref, w_ref, b_ref, o_ref, pad_ref, *, l_ctx, l_lat):
    p = _CONV_PAD
    zeros = jnp.zeros((p, LANE), F32)
    off_ctx = p
    off_lat = 2 * p + l_ctx
    pad_ref[0:p, :] = zeros
    pad_ref[off_ctx + l_ctx:off_lat, :] = zeros
    pad_ref[off_lat + l_lat:off_lat + l_lat + p, :] = zeros
    pad_ref[off_ctx:off_ctx + l_ctx, :] = ctx_ref[0]
    pad_ref[off_lat:off_lat + l_lat, :] = lat_ref[0]
    reach = SSM_CONV // 2
    bias = b_ref[...]

    def chunk(pad_base, out_base):
        acc = jnp.broadcast_to(bias, (CHUNK, LANE))
        for k in range(SSM_CONV):
            tap = pad_ref[pl.ds(pad_base - reach + k, CHUNK), :]
            acc = acc + tap * w_ref[k:k + 1, :]
        o_ref[0, 0, pl.ds(out_base, CHUNK), :] = _silu(acc)

    def ctx_body(j, c):
        base = pl.multiple_of(j * CHUNK, CHUNK)
        chunk(off_ctx + base, base)
        return c

    def lat_body(j, c):
        base = pl.multiple_of(j * CHUNK, CHUNK)
        chunk(off_lat + base, l_ctx + base)
        return c

    lax.fori_loop(0, l_ctx // CHUNK, ctx_body, 0)
    lax.fori_loop(0, l_lat // CHUNK, lat_body, 0, unroll=2)


def _conv7(ctx_raw, lat_raw, w, b):
    bsz, l_ctx, c = ctx_raw.shape
    l_lat = lat_raw.shape[1]
    ltot = l_ctx + l_lat
    nct = c // LANE
    return pl.pallas_call(
        functools.partial(_conv7_kernel, l_ctx=l_ctx, l_lat=l_lat),
        out_shape=jax.ShapeDtypeStruct((bsz, nct, ltot, LANE), F32),
        grid=(bsz, nct),
        in_specs=[pl.BlockSpec((1, l_ctx, LANE), lambda bi, ci: (bi, 0, ci)),
                  pl.BlockSpec((1, l_lat, LANE), lambda bi, ci: (bi, 0, ci)),
                  pl.BlockSpec((SSM_CONV, LANE), lambda bi, ci: (0, ci)),
                  pl.BlockSpec((1, LANE), lambda bi, ci: (0, ci))],
        out_specs=pl.BlockSpec((1, 1, ltot, LANE), lambda bi, ci: (bi, ci, 0, 0)),
        scratch_shapes=[pltpu.VMEM((ltot + 3 * _CONV_PAD, LANE), F32)],
        compiler_params=_cparams(("parallel", "parallel")),
        name="conv7",
    )(ctx_raw, lat_raw, w, b.reshape(1, c))


def _ssd_kernel(xbc_ref, dtc_ref, dtl_ref, par_ref, dexp_ref, ex_ref, *rest, reverse, n_ctx, fuse_norm):
    if fuse_norm:
        yo_ref, sz_ref, nw_ref, o_ref, st_ref, cumt_ref, y_ref = rest
    else:
        y_ref, st_ref, cumt_ref = rest
    i = pl.program_id(1)

    @pl.when(i == 0)
    def _():
        st_ref[...] = jnp.zeros_like(st_ref)

    dt_raw = jnp.where(i < n_ctx, dtc_ref[0], dtl_ref[0])
    bias = par_ref[0:1, :]
    a = -jnp.exp(par_ref[1:2, :])
    dt = jax.nn.softplus(dt_raw + bias)
    cum = dt * a
    row = lax.broadcasted_iota(jnp.int32, (CHUNK, LANE), 0)
    k = 1
    while k < CHUNK:
        if reverse:
            cum = cum + jnp.where(row < CHUNK - k, pltpu.roll(cum, CHUNK - k, 0), 0.0)
        else:
            cum = cum + jnp.where(row >= k, pltpu.roll(cum, k, 0), 0.0)
        k *= 2
    last = 0 if reverse else CHUNK - 1
    cum = cum * LOG2E
    cumt_ref[...] = cum.T
    li = lax.broadcasted_iota(jnp.int32, (CHUNK, CHUNK), 0)
    si = lax.broadcasted_iota(jnp.int32, (CHUNK, CHUNK), 1)
    causal = (li <= si) if reverse else (li >= si)
    lo = lax.broadcasted_iota(jnp.int32, (CHUNK, LANE), 1) < HEAD_DIM
    heads_per_group = N_HEADS // N_GROUPS
    pairs = heads_per_group // 2
    x_tiles = N_HEADS // 2

    def group(g, carry):
        shift = (LANE - heads_per_group * g) & (LANE - 1)
        cum_g = pltpu.roll(cum, shift, 1)
        dt_g = pltpu.roll(dt, shift, 1)
        cum_t = cumt_ref[pl.ds(pl.multiple_of(heads_per_group * g, heads_per_group), heads_per_group), :]
        bb = xbc_ref[0, x_tiles + g].astype(BF16)
        cb = xbc_ref[0, x_tiles + N_GROUPS + g].astype(BF16)
        scores = lax.dot_general(cb, bb, (((1,), (1,)), ((), ())), preferred_element_type=F32)
        h_t = st_ref[g]
        y_off = jnp.dot(cb, h_t.astype(BF16), preferred_element_type=F32)
        d_hi = dt_g.astype(BF16)
        r_hi = dt_g - d_hi.astype(F32)
        d_mid = r_hi.astype(BF16)
        d_lo = (r_hi - d_mid.astype(F32)).astype(BF16)
        dt_x = (jnp.dot(jnp.concatenate([d_hi, d_mid], axis=1), ex_ref[...], preferred_element_type=F32)
                + jnp.dot(d_lo, ex_ref[0:LANE, :], preferred_element_type=F32))
        xw_parts, dec_parts = [], []
        for p in range(pairs):
            j0, j1 = 2 * p, 2 * p + 1
            x2 = xbc_ref[0, pairs * g + p]
            c0 = cum_g[:, j0:j0 + 1]
            c1 = cum_g[:, j1:j1 + 1]
            l0 = jnp.exp2(jnp.where(causal, c0 - cum_t[j0:j0 + 1, :], -jnp.inf))
            l1 = jnp.exp2(jnp.where(causal, c1 - cum_t[j1:j1 + 1, :], -jnp.inf))
            m0 = (scores * l0).astype(BF16)
            m1 = (scores * l1).astype(BF16)
            dt2 = dt_x[:, p * LANE:(p + 1) * LANE]
            c2 = jnp.where(lo, c0, c1)
            xdt = x2 * dt2
            xdt_b = xdt.astype(BF16)
            zero = jnp.zeros_like(xdt_b)
            y_diag = (jnp.dot(m0, jnp.where(lo, xdt_b, zero), preferred_element_type=F32)
                      + jnp.dot(m1, jnp.where(lo, zero, xdt_b), preferred_element_type=F32))
            e2 = jnp.exp2(c2)
            y = y_diag + y_off[:, p * LANE:(p + 1) * LANE] * e2
            y_ref[0, pairs * g + p] = y + dexp_ref[pairs * g + p] * x2
            to_end = jnp.exp2(c2[last:last + 1, :] - c2)
            xw_parts.append((xdt * to_end).astype(BF16))
            dec_parts.append(e2[last:last + 1, :])
        xw = jnp.concatenate(xw_parts, axis=1)
        dec = jnp.concatenate(dec_parts, axis=1)
        upd = lax.dot_general(bb, xw, (((0,), (0,)), ((), ())), preferred_element_type=F32)
        st_ref[g] = h_t * dec + upd
        return carry

    lax.fori_loop(0, N_GROUPS, group, 0, unroll=2)

    if fuse_norm:
        sq = jnp.zeros((CHUNK, LANE), F32)
        for j in range(x_tiles):
            gj = (y_ref[0, j] + yo_ref[0, j]) * sz_ref[:, j * LANE:(j + 1) * LANE]
            y_ref[0, j] = gj
            sq = sq + gj * gj
        r = lax.rsqrt(jnp.sum(sq, axis=-1, keepdims=True) / (x_tiles * LANE) + EPS)
        for j in range(x_tiles):
            sl = slice(j * LANE, (j + 1) * LANE)
            o_ref[:, sl] = (y_ref[0, j] * r * nw_ref[:, sl]).astype(o_ref.dtype)


def _ssd(xbc_act, dt_ctx, dt_lat, par, dexp, reverse, norm_with=None):
    bsz, ntile, ltot, _ = xbc_act.shape
    l_ctx = dt_ctx.shape[1]
    l_lat = dt_lat.shape[1]
    n_ctx = l_ctx // CHUNK
    n_lat = l_lat // CHUNK
    steps = n_ctx + n_lat
    x_tiles = N_HEADS // 2
    gw = (N_HEADS // N_GROUPS) * HEAD_DIM
    e1 = (jnp.arange(gw)[None, :] // HEAD_DIM == jnp.arange(LANE)[:, None]).astype(BF16)
    expand = jnp.concatenate([e1, e1], axis=0)

    if reverse:
        def cat_chunk(i):
            return jnp.where(i < n_ctx, n_ctx - 1 - i, n_ctx + steps - 1 - i)

        def ctx_chunk(i):
            return jnp.maximum(n_ctx - 1 - i, 0)

        def lat_chunk(i):
            return jnp.minimum(steps - 1 - i, n_lat - 1)
    else:
        def cat_chunk(i):
            return i

        def ctx_chunk(i):
            return jnp.minimum(i, n_ctx - 1)

        def lat_chunk(i):
            return jnp.maximum(i - n_ctx, 0)

    y_spec = pl.BlockSpec((1, x_tiles, CHUNK, LANE), lambda b, i: (b, 0, lat_chunk(i), 0))
    in_specs = [pl.BlockSpec((1, ntile, CHUNK, LANE), lambda b, i: (b, 0, cat_chunk(i), 0)),
                pl.BlockSpec((1, CHUNK, LANE), lambda b, i: (b, ctx_chunk(i), 0)),
                pl.BlockSpec((1, CHUNK, LANE), lambda b, i: (b, lat_chunk(i), 0)),
                pl.BlockSpec((8, LANE), lambda b, i: (0, 0)),
                pl.BlockSpec((x_tiles, 1, LANE), lambda b, i: (0, 0, 0)),
                pl.BlockSpec((2 * LANE, gw), lambda b, i: (0, 0))]
    args = [xbc_act, dt_ctx, dt_lat, par, dexp, expand]
    scratch = [pltpu.VMEM((N_GROUPS, D_STATE, gw), F32), pltpu.VMEM((LANE, CHUNK), F32)]
    if norm_with is None:
        out_shape = jax.ShapeDtypeStruct((bsz, x_tiles, l_lat, LANE), F32)
        out_spec = y_spec
    else:
        y_other, silu_z, norm_w = norm_with
        dn = x_tiles * LANE
        row_spec = pl.BlockSpec((CHUNK, dn), lambda b, i: (b * n_lat + lat_chunk(i), 0))
        in_specs += [y_spec, row_spec, pl.BlockSpec((1, dn), lambda b, i: (0, 0))]
        args += [y_other, silu_z, norm_w.reshape(1, dn)]
        out_shape = jax.ShapeDtypeStruct((bsz * l_lat, dn), BF16)
        out_spec = row_spec
        scratch.append(pltpu.VMEM((1, x_tiles, CHUNK, LANE), F32))
    return pl.pallas_call(
        functools.partial(_ssd_kernel, reverse=reverse, n_ctx=n_ctx, fuse_norm=norm_with is not None),
        out_shape=out_shape,
        grid=(bsz, steps),
        in_specs=in_specs,
        out_specs=out_spec,
        scratch_shapes=scratch,
        compiler_params=_cparams(("parallel", "arbitrary")),
        name="ssd_bwd" if reverse else "ssd_fwd",
    )(*args)


def _conv31_kernel(u_ref, w_ref, b_ref, o_ref, pad_ref, *, seq):
    halo = (CF_KERNEL // 2) * GRID_W
    zeros = jnp.zeros((halo, LANE), F32)
    pad_ref[0:halo, :] = zeros
    pad_ref[halo + seq:halo + seq + halo, :] = zeros
    pad_ref[halo:halo + seq, :] = u_ref[0]
    bias = b_ref[...]

    def body(j, c):
        base = pl.multiple_of(j * CHUNK, CHUNK)
        acc = jnp.broadcast_to(bias, (CHUNK, LANE))
        for k in range(CF_KERNEL):
            tap = pad_ref[pl.ds(pl.multiple_of(base + k * GRID_W, GRID_W), CHUNK), :]
            acc = acc + tap * w_ref[k:k + 1, :]
        o_ref[0, pl.ds(base, CHUNK), :] = acc
        return c

    lax.fori_loop(0, seq // CHUNK, body, 0, unroll=2)


def _conv31(u3, w, b):
    bsz, s, c = u3.shape
    halo = (CF_KERNEL // 2) * GRID_W
    return pl.pallas_call(
        functools.partial(_conv31_kernel, seq=s),
        out_shape=jax.ShapeDtypeStruct((bsz, s, c), F32),
        grid=(bsz, c // LANE),
        in_specs=[pl.BlockSpec((1, s, LANE), lambda bi, ci: (bi, 0, ci)),
                  pl.BlockSpec((CF_KERNEL, LANE), lambda bi, ci: (0, ci)),
                  pl.BlockSpec((1, LANE), lambda bi, ci: (0, ci))],
        out_specs=pl.BlockSpec((1, s, LANE), lambda bi, ci: (bi, 0, ci)),
        scratch_shapes=[pltpu.VMEM((s + 2 * halo, LANE), F32)],
        compiler_params=_cparams(("parallel", "parallel")),
        name="conv31",
    )(u3, w, b.reshape(1, c))


def _route_kernel(x_ref, w_ref, sh_ref, sc_ref, rw_ref, rb_ref, h_ref, eid_ref, ew_ref):
    xf = x_ref[...]
    ms = jnp.mean(xf * xf, axis=-1, keepdims=True)
    h = xf * lax.rsqrt(ms + EPS) * w_ref[...]
    h = h * (1.0 + sc_ref[0]) + sh_ref[0]
    tm = xf.shape[0]
    nt = xf.shape[1] // LANE
    pitch = _pitch(nt)
    for j in range(nt):
        h_ref[pl.ds(j, tm, stride=pitch), :] = h[:, j * LANE:(j + 1) * LANE]
    for j in range(nt, pitch):
        h_ref[pl.ds(j, tm, stride=pitch), :] = jnp.zeros((tm, LANE), F32)
    logits = jnp.dot(h.astype(BF16), rw_ref[...], preferred_element_type=F32) + rb_ref[...]
    lane = lax.broadcasted_iota(jnp.int32, (tm, LANE), 1)
    lane_f = lane.astype(F32)
    ninf = -jnp.inf
    gl = jnp.where(lane < MOE_GROUPS, logits, ninf)
    gmax = jnp.max(gl, axis=-1, keepdims=True)
    gidx = jnp.min(jnp.where(gl == gmax, lane_f, float(LANE)), axis=-1, keepdims=True)
    gsum = jnp.sum(jnp.exp(gl - gmax), axis=-1, keepdims=True)
    g_p = 1.0 / gsum
    first = float(MOE_GROUPS) + gidx * float(EXPERTS_PER_GROUP)
    in_group = (lane_f >= first) & (lane_f < first + float(EXPERTS_PER_GROUP))
    el = jnp.where(in_group, logits, ninf)
    m1 = jnp.max(el, axis=-1, keepdims=True)
    i1 = jnp.min(jnp.where(el == m1, lane_f, float(LANE)), axis=-1, keepdims=True)
    el2 = jnp.where(lane_f == i1, ninf, el)
    m2 = jnp.max(el2, axis=-1, keepdims=True)
    i2 = jnp.min(jnp.where(el2 == m2, lane_f, float(LANE)), axis=-1, keepdims=True)
    e21 = jnp.exp(m2 - m1)
    den = 1.0 + e21
    w1 = (1.0 / den) * g_p
    w2 = (e21 / den) * g_p
    e1 = (i1 - float(MOE_GROUPS)).astype(jnp.int32)
    e2 = (i2 - float(MOE_GROUPS)).astype(jnp.int32)
    eid_ref[...] = jnp.where(lane == 0, e1, jnp.where(lane == 1, e2, 0))
    ew_ref[...] = jnp.where(lane == 0, w1, jnp.where(lane == 1, w2, 0.0))


def _route(x2, w, mod3, shift_chunk, scale_chunk, rows_per_batch, rw, rb, tm=256):
    m, d = x2.shape
    pitch = _pitch(d // LANE)
    tiles_per_batch = rows_per_batch // tm
    return pl.pallas_call(
        _route_kernel,
        out_shape=(jax.ShapeDtypeStruct((m * pitch, LANE), F32),
                   jax.ShapeDtypeStruct((m, LANE), jnp.int32),
                   jax.ShapeDtypeStruct((m, LANE), F32)),
        grid=(m // tm,),
        in_specs=[pl.BlockSpec((tm, d), lambda i: (i, 0)),
                  pl.BlockSpec((1, d), lambda i: (0, 0)),
                  pl.BlockSpec((1, 1, d), lambda i: (i // tiles_per_batch, 0, shift_chunk)),
                  pl.BlockSpec((1, 1, d), lambda i: (i // tiles_per_batch, 0, scale_chunk)),
                  pl.BlockSpec((d, LANE), lambda i: (0, 0)),
                  pl.BlockSpec((1, LANE), lambda i: (0, 0))],
        out_specs=(pl.BlockSpec((tm * pitch, LANE), lambda i: (i, 0)),
                   pl.BlockSpec((tm, LANE), lambda i: (i, 0)),
                   pl.BlockSpec((tm, LANE), lambda i: (i, 0))),
        compiler_params=_cparams(("parallel",)),
        name="route",
    )(x2, w.reshape(1, d), mod3, mod3, rw, rb)


_DMA_UNROLL = 8


def _rows_to_matrix(ref, tm, nt):
    return jnp.concatenate([ref[pl.ds(j, tm, stride=_pitch(nt)), :] for j in range(nt)], axis=1)


def _bulk_wait(src, dst, sem, total_rows):
    pltpu.make_async_copy(src.at[pl.ds(0, total_rows), :], dst.at[pl.ds(0, total_rows), :], sem).wait()


def _for_rows(n, body):
    groups = lax.shift_right_logical(n, _DMA_UNROLL.bit_length() - 1)

    def group(g, c):
        for u in range(_DMA_UNROLL):
            body(g * _DMA_UNROLL + u)
        return c

    def tail(r, c):
        body(r)
        return c

    lax.fori_loop(0, groups, group, 0)
    lax.fori_loop(groups * _DMA_UNROLL, n, tail, 0)


def _stream_expert_weights(b, be_ref, eord_ref, enext_ref, w_hbms, w_bufs, w_caches, wsem):
    prev = jnp.maximum(b - 1, 0)

    def copies(e, slot):
        return [pltpu.make_async_copy(w.at[e], buf.at[slot], wsem.at[slot]) for w, buf in zip(w_hbms, w_bufs)]

    @pl.when(b == 0)
    def _():
        for cp in copies(be_ref[0], 0):
            cp.start()

    @pl.when((b == 0) | (be_ref[b] != be_ref[prev]))
    def _():
        for s in range(2):
            @pl.when((eord_ref[b] & 1) == s)
            def _(s=s):
                for cp in copies(be_ref[b], s):
                    cp.wait()

                @pl.when(enext_ref[b] >= 0)
                def _():
                    for cp in copies(enext_ref[b], 1 - s):
                        cp.start()

                for buf, cache in zip(w_bufs, w_caches):
                    cache[...] = buf[s].astype(BF16)


def _expert_up_kernel(be_ref, nused_ref, eord_ref, enext_ref, rowc_ref, rown_ref, h_hbm, wg_hbm, wu_hbm,
                      o_ref, xs0_ref, xs1_ref, wgs_ref, wus_ref, wgb_ref, wub_ref, sem, wsem, *, fchunk, nt):
    b = pl.program_id(0)
    n_used = nused_ref[0]
    dff = wgb_ref.shape[1]
    slots = (xs0_ref, xs1_ref)
    pitch = _pitch(nt)

    def gather_row(row_ref, r, slot):
        return pltpu.make_async_copy(h_hbm.at[pl.ds(row_ref[0, 0, r], nt), :],
                                     slots[slot].at[pl.ds(r * pitch, nt), :], sem.at[slot])

    @pl.when(b == 0)
    def _():
        _for_rows(MOE_BLOCK, lambda r: gather_row(rowc_ref, r, 0).start())

    for slot in range(2):
        @pl.when((b <= n_used) & (lax.rem(b, 2) == slot))
        def _(slot=slot):
            _bulk_wait(h_hbm, slots[slot], sem.at[slot], MOE_BLOCK * nt)

    @pl.when(b < n_used)
    def _():
        _stream_expert_weights(b, be_ref, eord_ref, enext_ref, (wg_hbm, wu_hbm), (wgs_ref, wus_ref),
                               (wgb_ref, wub_ref), wsem)

        for slot in range(2):
            @pl.when(lax.rem(b, 2) == slot)
            def _(slot=slot):
                xb = _rows_to_matrix(slots[slot], MOE_BLOCK, nt).astype(BF16)
                nf = dff // fchunk
                per = MOE_BLOCK // nf
                for f in range(nf):
                    for r in range(f * per, (f + 1) * per):
                        gather_row(rown_ref, r, 1 - slot).start()
                    sl = slice(f * fchunk, (f + 1) * fchunk)
                    gate = jnp.dot(xb, wgb_ref[:, sl], preferred_element_type=F32)
                    up = jnp.dot(xb, wub_ref[:, sl], preferred_element_type=F32)
                    o_ref[:, sl] = (_silu(gate) * up).astype(o_ref.dtype)

    @pl.when(b >= n_used)
    def _():
        o_ref[...] = jnp.zeros_like(o_ref)


def _expert_up(h2t, src_rows, w_gate, w_up, tables, fchunk=256):
    n_blocks = src_rows.shape[0]
    _, d, dff = w_gate.shape
    nt = d // LANE
    slot_rows = MOE_BLOCK * _pitch(nt)
    grid_spec = pltpu.PrefetchScalarGridSpec(
        num_scalar_prefetch=len(tables),
        grid=(n_blocks,),
        in_specs=[pl.BlockSpec((1, 1, MOE_BLOCK), lambda b, *_: (b, 0, 0), memory_space=pltpu.SMEM),
                  pl.BlockSpec((1, 1, MOE_BLOCK), lambda b, *_: (jnp.minimum(b + 1, n_blocks - 1), 0, 0),
                               memory_space=pltpu.SMEM),
                  pl.BlockSpec(memory_space=pl.ANY),
                  pl.BlockSpec(memory_space=pl.ANY),
                  pl.BlockSpec(memory_space=pl.ANY)],
        out_specs=pl.BlockSpec((MOE_BLOCK, dff), lambda b, *_: (b, 0)),
        scratch_shapes=[pltpu.VMEM((slot_rows, LANE), F32),
                        pltpu.VMEM((slot_rows, LANE), F32),
                        pltpu.VMEM((2, d, dff), F32),
                        pltpu.VMEM((2, d, dff), F32),
                        pltpu.VMEM((d, dff), BF16),
                        pltpu.VMEM((d, dff), BF16),
                        pltpu.SemaphoreType.DMA((2,)),
                        pltpu.SemaphoreType.DMA((2,))],
    )
    return pl.pallas_call(
        functools.partial(_expert_up_kernel, fchunk=fchunk, nt=nt),
        out_shape=jax.ShapeDtypeStruct((n_blocks * MOE_BLOCK, dff), BF16),
        grid_spec=grid_spec,
        compiler_params=_cparams(("arbitrary",)),
        name="expert_up",
    )(*tables, src_rows, src_rows, h2t, w_gate, w_up)


def _expert_down_kernel(be_ref, nused_ref, eord_ref, enext_ref, dst_ref, h_ref, wd_hbm, y_hbm,
                        ys0_ref, ys1_ref, wds_ref, wdb_ref, sem, wsem, *, nchunk, nt):
    b = pl.program_id(0)
    n_used = nused_ref[0]
    d = wdb_ref.shape[1]
    slots = (ys0_ref, ys1_ref)
    pitch = _pitch(nt)

    def scatter_row(r, slot):
        return pltpu.make_async_copy(slots[slot].at[pl.ds(r * pitch, pitch), :],
                                     y_hbm.at[pl.ds(dst_ref[0, 0, r], pitch), :], sem.at[slot])

    @pl.when(b == 0)
    def _():
        ys0_ref[...] = jnp.zeros_like(ys0_ref)
        ys1_ref[...] = jnp.zeros_like(ys1_ref)

    for slot in range(2):
        @pl.when((b >= 1) & (b <= n_used) & (lax.rem(b, 2) == slot))
        def _(slot=slot):
            _bulk_wait(slots[slot], y_hbm, sem.at[slot], MOE_BLOCK * pitch)

    @pl.when(b < n_used)
    def _():
        _stream_expert_weights(b, be_ref, eord_ref, enext_ref, (wd_hbm,), (wds_ref,), (wdb_ref,), wsem)
        hb = h_ref[...]
        for slot in range(2):
            @pl.when(lax.rem(b, 2) == slot)
            def _(slot=slot):
                nc = d // nchunk
                per = MOE_BLOCK // nc
                for c in range(nc):
                    for r in range(c * per, (c + 1) * per):
                        scatter_row(r, 1 - slot).start()
                    out = jnp.dot(hb, wdb_ref[:, c * nchunk:(c + 1) * nchunk], preferred_element_type=F32)
                    for j in range(nchunk // LANE):
                        slots[slot][pl.ds(c * (nchunk // LANE) + j, MOE_BLOCK, stride=pitch), :] = (
                            out[:, j * LANE:(j + 1) * LANE])

    for slot in range(2):
        @pl.when((b == n_used) & (lax.rem(b, 2) == slot))
        def _(slot=slot):
            _for_rows(MOE_BLOCK, lambda r: scatter_row(r, 1 - slot).start())
            _bulk_wait(slots[1 - slot], y_hbm, sem.at[1 - slot], MOE_BLOCK * pitch)


def _expert_down(hid, dst_rows, w_down, tables, y_slots, nchunk=512):
    n_rows, dff = hid.shape
    n_blocks = n_rows // MOE_BLOCK
    d = w_down.shape[2]
    nt = d // LANE
    pitch = _pitch(nt)
    grid_spec = pltpu.PrefetchScalarGridSpec(
        num_scalar_prefetch=len(tables),
        grid=(n_blocks,),
        in_specs=[pl.BlockSpec((1, 1, MOE_BLOCK), lambda b, *_: (b, 0, 0), memory_space=pltpu.SMEM),
                  pl.BlockSpec((MOE_BLOCK, dff), lambda b, be, n, *_: (jnp.minimum(b, n[0] - 1), 0)),
                  pl.BlockSpec(memory_space=pl.ANY)],
        out_specs=pl.BlockSpec(memory_space=pl.ANY),
        scratch_shapes=[pltpu.VMEM((MOE_BLOCK * pitch, LANE), F32),
                        pltpu.VMEM((MOE_BLOCK * pitch, LANE), F32),
                        pltpu.VMEM((2, dff, d), F32),
                        pltpu.VMEM((dff, d), BF16),
                        pltpu.SemaphoreType.DMA((2,)),
                        pltpu.SemaphoreType.DMA((2,))],
    )
    return pl.pallas_call(
        functools.partial(_expert_down_kernel, nchunk=nchunk, nt=nt),
        out_shape=jax.ShapeDtypeStruct((y_slots * pitch, LANE), F32),
        grid_spec=grid_spec,
        compiler_params=_cparams(("arbitrary",)),
        name="expert_down",
    )(*tables, dst_rows, hid, w_down)


def _combine_kernel(y0_ref, y1_ref, ew_ref, x_ref, g_ref, w_ref, o_ref):
    tm, d = x_ref.shape
    nt = d // LANE
    ew = ew_ref[...]
    moe = (_rows_to_matrix(y0_ref, tm, nt) * ew[:, 0:1]
           + _rows_to_matrix(y1_ref, tm, nt) * ew[:, 1:2])
    xo = x_ref[...] + g_ref[0] * moe
    ms = jnp.mean(xo * xo, axis=-1, keepdims=True)
    o_ref[...] = xo * lax.rsqrt(ms + EPS) * w_ref[...]


def _combine(y, ew, x2, mod3, gate_chunk, rows_per_batch, final_w, tm=256):
    m, d = x2.shape
    pitch = _pitch(d // LANE)
    tiles = m // tm
    tiles_per_batch = rows_per_batch // tm
    return pl.pallas_call(
        _combine_kernel,
        out_shape=jax.ShapeDtypeStruct((m, d), F32),
        grid=(tiles,),
        in_specs=[pl.BlockSpec((tm * pitch, LANE), lambda i: (i, 0)),
                  pl.BlockSpec((tm * pitch, LANE), lambda i: (tiles + i, 0)),
                  pl.BlockSpec((tm, LANE), lambda i: (i, 0)),
                  pl.BlockSpec((tm, d), lambda i: (i, 0)),
                  pl.BlockSpec((1, 1, d), lambda i: (i // tiles_per_batch, 0, gate_chunk)),
                  pl.BlockSpec((1, d), lambda i: (0, 0))],
        out_specs=pl.BlockSpec((tm, d), lambda i: (i, 0)),
        compiler_params=_cparams(("parallel",)),
        name="moe_combine",
    )(y, y, ew, x2, mod3, final_w.reshape(1, d))


def _dispatch_tables(eid, n_tok, pitch):
    top_k = eid.shape[1]
    n_assign = n_tok * top_k
    expert = eid.reshape(-1)
    key = jnp.sort(expert * n_assign + jnp.arange(n_assign, dtype=jnp.int32))
    sorted_assign = key % n_assign
    bounds = jnp.arange(N_EXPERTS + 1, dtype=jnp.int32) * n_assign
    start = jnp.searchsorted(key, bounds, side="left").astype(jnp.int32)
    counts = start[1:] - start[:-1]
    nblk = (counts + MOE_BLOCK - 1) // MOE_BLOCK
    blk_end = jnp.cumsum(nblk)
    blk_start = blk_end - nblk
    steps = -(-n_assign // MOE_BLOCK) + N_EXPERTS + 1
    bidx = jnp.arange(steps, dtype=jnp.int32)
    lane = jnp.arange(MOE_BLOCK, dtype=jnp.int32)[None, :]
    block_expert = jnp.minimum(jnp.searchsorted(blk_end, bidx, side="right"), N_EXPERTS - 1).astype(jnp.int32)
    in_expert = (bidx - blk_start[block_expert]) * MOE_BLOCK
    n_valid = jnp.clip(counts[block_expert] - in_expert, 0, MOE_BLOCK)
    src = start[block_expert][:, None] + in_expert[:, None] + lane
    valid = lane < n_valid[:, None]
    assign = sorted_assign[jnp.clip(src, 0, n_assign - 1)]
    tok = assign // top_k
    src_rows = jnp.where(valid, tok, 0) * pitch
    dst_slot = jnp.where(valid, (assign % top_k) * n_tok + tok, top_k * n_tok + lane)
    dst_rows = jnp.concatenate([top_k * n_tok + lane, dst_slot[:-1]], axis=0) * pitch
    n_used = blk_end[-1].astype(jnp.int32)
    first = jnp.concatenate([jnp.ones((1,), jnp.int32),
                             (block_expert[1:] != block_expert[:-1]).astype(jnp.int32)])
    expert_ordinal = (jnp.cumsum(first) - 1).astype(jnp.int32)
    next_blk = blk_end[block_expert]
    next_expert = jnp.where(next_blk < n_used, block_expert[jnp.minimum(next_blk, steps - 1)], -1).astype(jnp.int32)
    tables = (block_expert, n_used.reshape(1), expert_ordinal, next_expert)
    return (src_rows.astype(jnp.int32).reshape(steps, 1, MOE_BLOCK),
            dst_rows.astype(jnp.int32).reshape(steps, 1, MOE_BLOCK), tables)


def kernel(x, c, ctx, c_ctx, ada_w, ada_b, norm1_w, w_in, ssm_conv_w, ssm_conv_b, dt_bias, a_log, d_skip, ssm_norm_w, ssm_out_w, cf_dw_w, cf_dw_b, cf_ln_w, cf_ln_b, cf_out_w, cf_out_b, w_o, norm2_w, router_group_w, router_group_b, router_expert_w, router_expert_b, expert_w_gate, expert_w_up, expert_w_down, final_norm_w):
    bsz, seq, d = x.shape
    l_ctx = ctx.shape[1]
    n_tok = bsz * seq
    d_inner = ssm_norm_w.shape[1]
    gn = N_GROUPS * D_STATE
    xbc_dim = d_inner + 2 * gn
    off_dt = xbc_dim
    off_z = off_dt + N_HEADS
    off_glu = off_z + d_inner
    off_gate = off_glu + 2 * d

    ctx_row = bsz
    crows = jnp.zeros((8, d), F32).at[:bsz].set(c).at[ctx_row].set(c_ctx)
    mod = _ada(crows, ada_w[0], ada_b[0])
    mod3 = mod.reshape(8, 1, 6 * d)
    lat_rows = jnp.arange(bsz, dtype=jnp.int32)
    ctx_rows = jnp.full((bsz,), ctx_row, jnp.int32)

    h_lat = _normmod(x, norm1_w[0], mod3, lat_rows, 0, 1, BF16).reshape(n_tok, d)
    h_ctx = _normmod(ctx, norm1_w[0], mod3, ctx_rows, 0, 1, BF16).reshape(bsz * l_ctx, d)

    wt, (r_xbc, r_dt, r_z, r_glu, r_gate) = _pack_wt(
        jnp.transpose(w_in[0]),
        [(0, xbc_dim), (off_dt, off_z), (off_z, off_glu), (off_glu, off_gate), (off_gate, off_gate + 2 * d)])

    xbc_lat = _mm(h_lat, wt, tn=2048, name="in_xbc", rows=(r_xbc, xbc_dim)).reshape(bsz, seq, xbc_dim)
    xbc_ctx = _mm(h_ctx, wt, tm=512, name="in_xbc_ctx", rows=(r_xbc, xbc_dim)).reshape(bsz, l_ctx, xbc_dim)
    dt_lat = _mm(h_lat, wt, name="in_dt", rows=(r_dt, LANE)).reshape(bsz, seq, LANE)
    dt_ctx = _mm(h_ctx, wt, tm=512, name="in_dt_ctx", rows=(r_dt, LANE)).reshape(bsz, l_ctx, LANE)
    sz = _mm(h_lat, wt, act="silu", tn=2048, name="in_z", rows=(r_z, d_inner))
    u = _mm_glu(h_lat, wt, r_glu, d, tn=1024)

    xbc_act = _conv7(xbc_ctx, xbc_lat, ssm_conv_w[0], ssm_conv_b[0])

    def ssd_params(k):
        par = jnp.zeros((8, LANE), F32).at[0, :N_HEADS].set(dt_bias[0, k]).at[1, :N_HEADS].set(a_log[0, k])
        return par, jnp.repeat(d_skip[0, k], HEAD_DIM).reshape(N_HEADS // 2, 1, LANE)

    y_bwd = _ssd(xbc_act, dt_ctx, dt_lat, *ssd_params(1), reverse=True)
    gnorm = _ssd(xbc_act, dt_ctx, dt_lat, *ssd_params(0), reverse=False,
                 norm_with=(y_bwd, sz, ssm_norm_w[0]))
    y_ssd = _mm(gnorm, ssm_out_w[0].astype(BF16), tn=512, name="ssm_out")

    cv = _conv31(u.reshape(bsz, seq, d), cf_dw_w[0], cf_dw_b[0]).reshape(n_tok, d)
    merged = _mm_merge(cv, cf_ln_w[0], cf_ln_b[0], cf_out_w[0].astype(BF16), cf_out_b[0],
                       h_lat, wt, r_gate, y_ssd)
    x1 = _mm_resid(merged, w_o[0].astype(BF16), x.reshape(n_tok, d), mod3, 2, seq, tm=2048)

    n_r = MOE_GROUPS + N_EXPERTS
    rw = jnp.pad(jnp.concatenate([router_group_w[0], router_expert_w[0]], axis=1),
                 ((0, 0), (0, LANE - n_r))).astype(BF16)
    rb = jnp.pad(jnp.concatenate([router_group_b[0], router_expert_b[0]]), (0, LANE - n_r)).reshape(1, LANE)
    h2t, eid, ew = _route(x1, norm2_w[0], mod3, 3, 4, seq, rw, rb)

    src_rows, dst_rows, tables = _dispatch_tables(eid[:, :2], n_tok, _pitch(d // LANE))
    hid = _expert_up(h2t, src_rows, expert_w_gate[0], expert_w_up[0], tables)
    y = _expert_down(hid, dst_rows, expert_w_down[0], tables, 2 * n_tok + MOE_BLOCK)
    out = _combine(y, ew, x1, mod3, 5, seq, final_norm_w)
    return out.reshape(bsz, seq, d)
```

```python
import functools

import jax
import jax.numpy as jnp
from jax import lax
from jax.experimental import pallas as pl
from jax.experimental.pallas import tpu as pltpu

F32 = jnp.float32
BF16 = jnp.bfloat16

EPS = 1e-6
GRID_W = 64
HEAD_DIM = 64
N_HEADS = 64
N_GROUPS = 8
D_STATE = 128
CHUNK = 128
SSM_CONV = 7
CF_KERNEL = 31
MOE_GROUPS = 8
EXPERTS_PER_GROUP = 8
N_EXPERTS = 64
MOE_BLOCK = 256
LANE = 128
LOG2E = 1.4426950408889634
VMEM_LIMIT = 56 * 1024 * 1024


def _cparams(sem):
    return pltpu.CompilerParams(dimension_semantics=sem, vmem_limit_bytes=VMEM_LIMIT)


def _silu(v):
    return v * jax.nn.sigmoid(v)


def _pitch(nt):
    return nt + 1


def _ada_kernel(c_ref, w_ref, b_ref, o_ref):
    s = _silu(c_ref[...])
    o_ref[...] = jnp.dot(s.astype(BF16), w_ref[...].astype(BF16),
                         preferred_element_type=F32) + b_ref[...]


def _ada(crows, ada_w, ada_b, tn=1024):
    r, d = crows.shape
    n = ada_w.shape[1]
    return pl.pallas_call(
        _ada_kernel,
        out_shape=jax.ShapeDtypeStruct((r, n), F32),
        grid=(n // tn,),
        in_specs=[pl.BlockSpec((r, d), lambda j: (0, 0)),
                  pl.BlockSpec((d, tn), lambda j: (0, j)),
                  pl.BlockSpec((1, tn), lambda j: (0, j))],
        out_specs=pl.BlockSpec((r, tn), lambda j: (0, j)),
        compiler_params=_cparams(("parallel",)),
        name="ada",
    )(crows, ada_w, ada_b.reshape(1, n))


def _normmod_kernel(rows_ref, x_ref, w_ref, sh_ref, sc_ref, o_ref):
    del rows_ref
    xf = x_ref[0]
    ms = jnp.mean(xf * xf, axis=-1, keepdims=True)
    y = xf * lax.rsqrt(ms + EPS) * w_ref[...]
    o_ref[0] = (y * (1.0 + sc_ref[0]) + sh_ref[0]).astype(o_ref.dtype)


def _normmod(x3, w, mod3, rows, shift_chunk, scale_chunk, out_dtype, tm=256):
    bx, l, d = x3.shape
    grid_spec = pltpu.PrefetchScalarGridSpec(
        num_scalar_prefetch=1,
        grid=(bx, l // tm),
        in_specs=[pl.BlockSpec((1, tm, d), lambda b, i, r: (b, i, 0)),
                  pl.BlockSpec((1, d), lambda b, i, r: (0, 0)),
                  pl.BlockSpec((1, 1, d), lambda b, i, r: (r[b], 0, shift_chunk)),
                  pl.BlockSpec((1, 1, d), lambda b, i, r: (r[b], 0, scale_chunk))],
        out_specs=pl.BlockSpec((1, tm, d), lambda b, i, r: (b, i, 0)),
    )
    return pl.pallas_call(
        _normmod_kernel,
        out_shape=jax.ShapeDtypeStruct((bx, l, d), out_dtype),
        grid_spec=grid_spec,
        compiler_params=_cparams(("parallel", "parallel")),
        name="normmod",
    )(rows, x3, w.reshape(1, d), mod3, mod3)


def _dot_nt(a, wt):
    return lax.dot_general(a, wt, (((1,), (1,)), ((), ())), preferred_element_type=F32)


def _mm_kernel(a_ref, w_ref, *rest, act, has_bias, w_rows):
    o_ref = rest[-1]
    a = a_ref[...]
    acc = _dot_nt(a, w_ref[...]) if w_rows else jnp.dot(a, w_ref[...], preferred_element_type=F32)
    if has_bias:
        acc = acc + rest[0][...]
    if act == "silu":
        acc = _silu(acc)
    elif act == "sigmoid":
        acc = jax.nn.sigmoid(acc)
    o_ref[...] = acc.astype(o_ref.dtype)


def _mm(a, w, bias=None, act=None, out_dtype=F32, tm=1024, tn=1024, name="mm", rows=None):
    m, k = a.shape
    start, n = (0, w.shape[1]) if rows is None else rows
    tm, tn = min(tm, m), min(tn, n)
    j0 = start // tn
    w_spec = (pl.BlockSpec((k, tn), lambda i, j: (0, j)) if rows is None
              else pl.BlockSpec((tn, k), lambda i, j: (j0 + j, 0)))
    in_specs = [pl.BlockSpec((tm, k), lambda i, j: (i, 0)), w_spec]
    args = [a, w]
    if bias is not None:
        in_specs.append(pl.BlockSpec((1, tn), lambda i, j: (0, j)))
        args.append(bias.reshape(1, n))
    return pl.pallas_call(
        functools.partial(_mm_kernel, act=act, has_bias=bias is not None, w_rows=rows is not None),
        out_shape=jax.ShapeDtypeStruct((m, n), out_dtype),
        grid=(m // tm, n // tn),
        in_specs=in_specs,
        out_specs=pl.BlockSpec((tm, tn), lambda i, j: (i, j)),
        compiler_params=_cparams(("parallel", "parallel")),
        name=name,
    )(*args)


def _mm_glu_kernel(a_ref, wa_ref, wb_ref, o_ref):
    a = a_ref[...]
    va = _dot_nt(a, wa_ref[...])
    vb = _dot_nt(a, wb_ref[...])
    o_ref[...] = va * jax.nn.sigmoid(vb)


def _mm_glu(a, wt, start, n, tm=1024, tn=512):
    m, k = a.shape
    tm = min(tm, m)
    ja, jb = start // tn, (start + n) // tn
    return pl.pallas_call(
        _mm_glu_kernel,
        out_shape=jax.ShapeDtypeStruct((m, n), F32),
        grid=(m // tm, n // tn),
        in_specs=[pl.BlockSpec((tm, k), lambda i, j: (i, 0)),
                  pl.BlockSpec((tn, k), lambda i, j: (ja + j, 0)),
                  pl.BlockSpec((tn, k), lambda i, j: (jb + j, 0))],
        out_specs=pl.BlockSpec((tm, tn), lambda i, j: (i, j)),
        compiler_params=_cparams(("parallel", "parallel")),
        name="mm_glu",
    )(a, wt, wt)


def _mm_merge_kernel(cv_ref, lw_ref, lb_ref, w_ref, b_ref, h_ref, wga_ref, wgb_ref, ys_ref, o_ref, u_ref):
    @pl.when(pl.program_id(1) == 0)
    def _():
        xf = cv_ref[...]
        mu = jnp.mean(xf, axis=-1, keepdims=True)
        xc = xf - mu
        var = jnp.mean(xc * xc, axis=-1, keepdims=True)
        y = xc * lax.rsqrt(var + EPS) * lw_ref[...] + lb_ref[...]
        u_ref[...] = _silu(y).astype(u_ref.dtype)

    ycf = jnp.dot(u_ref[...], w_ref[...], preferred_element_type=F32) + b_ref[...]
    h = h_ref[...]
    gate_a = jax.nn.sigmoid(_dot_nt(h, wga_ref[...]))
    gate_b = jax.nn.sigmoid(_dot_nt(h, wgb_ref[...]))
    o_ref[...] = (gate_a * ys_ref[...] + gate_b * ycf).astype(o_ref.dtype)


def _mm_merge(cv, ln_w, ln_b, w, bias, h, wt, gate_start, y_ssd, tm=512, tn=512):
    m, k = cv.shape
    n = w.shape[1]
    tm = min(tm, m)
    ja, jb = gate_start // tn, (gate_start + n) // tn
    return pl.pallas_call(
        _mm_merge_kernel,
        out_shape=jax.ShapeDtypeStruct((m, n), BF16),
        grid=(m // tm, n // tn),
        in_specs=[pl.BlockSpec((tm, k), lambda i, j: (i, 0)),
                  pl.BlockSpec((1, k), lambda i, j: (0, 0)),
                  pl.BlockSpec((1, k), lambda i, j: (0, 0)),
                  pl.BlockSpec((k, tn), lambda i, j: (0, j)),
                  pl.BlockSpec((1, tn), lambda i, j: (0, j)),
                  pl.BlockSpec((tm, k), lambda i, j: (i, 0)),
                  pl.BlockSpec((tn, k), lambda i, j: (ja + j, 0)),
                  pl.BlockSpec((tn, k), lambda i, j: (jb + j, 0)),
                  pl.BlockSpec((tm, tn), lambda i, j: (i, j))],
        out_specs=pl.BlockSpec((tm, tn), lambda i, j: (i, j)),
        scratch_shapes=[pltpu.VMEM((tm, k), BF16)],
        compiler_params=_cparams(("parallel", "arbitrary")),
        name="mm_merge",
    )(cv, ln_w.reshape(1, k), ln_b.reshape(1, k), w, bias.reshape(1, n), h, wt, wt, y_ssd)


def _mm_resid_kernel(a_ref, w_ref, x_ref, g_ref, o_ref):
    out = jnp.dot(a_ref[...], w_ref[...], preferred_element_type=F32)
    o_ref[...] = x_ref[...] + g_ref[0] * out


def _mm_resid(a, w, x2, mod3, gate_chunk, rows_per_batch, tm=1024, tn=512):
    m, k = a.shape
    n = w.shape[1]
    tm = min(tm, rows_per_batch)
    nj = n // tn
    tiles_per_batch = rows_per_batch // tm
    return pl.pallas_call(
        _mm_resid_kernel,
        out_shape=jax.ShapeDtypeStruct((m, n), F32),
        grid=(m // tm, nj),
        in_specs=[pl.BlockSpec((tm, k), lambda i, j: (i, 0)),
                  pl.BlockSpec((k, tn), lambda i, j: (0, j)),
                  pl.BlockSpec((tm, tn), lambda i, j: (i, j)),
                  pl.BlockSpec((1, 1, tn),
                               lambda i, j: (i // tiles_per_batch, 0, gate_chunk * nj + j))],
        out_specs=pl.BlockSpec((tm, tn), lambda i, j: (i, j)),
        compiler_params=_cparams(("parallel", "parallel")),
        name="mm_resid",
    )(a, w, x2, mod3)


W_ALIGN = 2048


def _pack_wt_kernel(valid_ref, off_ref, w_ref, o_ref):
    del off_ref
    nrow = valid_ref[pl.program_id(0)]
    row = lax.broadcasted_iota(jnp.int32, o_ref.shape, 0)
    o_ref[...] = jnp.where(row < nrow, w_ref[...], 0.0).astype(o_ref.dtype)


def _pack_wt(wt, segments, tr=512):
    n, k = wt.shape
    starts, src_off, valid = [], [], []
    pos = 0
    for lo, hi in segments:
        pos = -(-pos // W_ALIGN) * W_ALIGN
        starts.append(pos)
        while len(src_off) < pos // tr:
            src_off.append(0)
            valid.append(0)
        for r in range(lo, hi, tr):
            src_off.append(min(r, n - tr))
            valid.append(min(tr, hi - r))
            assert r <= n - tr or hi - r == tr
        pos += -(-(hi - lo) // tr) * tr
    total = -(-pos // W_ALIGN) * W_ALIGN
    while len(src_off) < total // tr:
        src_off.append(0)
        valid.append(0)
    grid_spec = pltpu.PrefetchScalarGridSpec(
        num_scalar_prefetch=2,
        grid=(total // tr,),
        in_specs=[pl.BlockSpec((pl.Element(tr), pl.Element(k)), lambda t, v, off: (off[t] * 8, 0))],
        out_specs=pl.BlockSpec((tr, k), lambda t, v, off: (t, 0)),
    )
    packed = pl.pallas_call(
        _pack_wt_kernel,
        out_shape=jax.ShapeDtypeStruct((total, k), BF16),
        grid_spec=grid_spec,
        compiler_params=_cparams(("parallel",)),
        name="pack_wt",
    )(jnp.asarray(valid, jnp.int32), jnp.asarray(src_off, jnp.int32) // 8, wt)
    return packed, starts


_CONV_PAD = 8


def _conv7_kernel(ctx_ref, lat_ref, w_ref, b_ref, o_ref, pad_ref, *, l_ctx, l_lat):
    p = _CONV_PAD
    zeros = jnp.zeros((p, LANE), F32)
    off_ctx = p
    off_lat = 2 * p + l_ctx
    pad_ref[0:p, :] = zeros
    pad_ref[off_ctx + l_ctx:off_lat, :] = zeros
    pad_ref[off_lat + l_lat:off_lat + l_lat + p, :] = zeros
    pad_ref[off_ctx:off_ctx + l_ctx, :] = ctx_ref[0]
    pad_ref[off_lat:off_lat + l_lat, :] = lat_ref[0]
    reach = SSM_CONV // 2
    bias = b_ref[...]

    def chunk(pad_base, out_base):
        acc = jnp.broadcast_to(bias, (CHUNK, LANE))
        for k in range(SSM_CONV):
            tap = pad_ref[pl.ds(pad_base - reach + k, CHUNK), :]
            acc = acc + tap * w_ref[k:k + 1, :]
        o_ref[0, 0, pl.ds(out_base, CHUNK), :] = _silu(acc)

    def ctx_body(j, c):
        base = pl.multiple_of(j * CHUNK, CHUNK)
        chunk(off_ctx + base, base)
        return c

    def lat_body(j, c):
        base = pl.multiple_of(j * CHUNK, CHUNK)
        chunk(off_lat + base, l_ctx + base)
        return c

    lax.fori_loop(0, l_ctx // CHUNK, ctx_body, 0)
    lax.fori_loop(0, l_lat // CHUNK, lat_body, 0, unroll=2)


def _conv7(ctx_raw, lat_raw, w, b):
    bsz, l_ctx, c = ctx_raw.shape
    l_lat = lat_raw.shape[1]
    ltot = l_ctx + l_lat
    nct = c // LANE
    return pl.pallas_call(
        functools.partial(_conv7_kernel, l_ctx=l_ctx, l_lat=l_lat),
        out_shape=jax.ShapeDtypeStruct((bsz, nct, ltot, LANE), F32),
        grid=(bsz, nct),
        in_specs=[pl.BlockSpec((1, l_ctx, LANE), lambda bi, ci: (bi, 0, ci)),
                  pl.BlockSpec((1, l_lat, LANE), lambda bi, ci: (bi, 0, ci)),
                  pl.BlockSpec((SSM_CONV, LANE), lambda bi, ci: (0, ci)),
                  pl.BlockSpec((1, LANE), lambda bi, ci: (0, ci))],
        out_specs=pl.BlockSpec((1, 1, ltot, LANE), lambda bi, ci: (bi, ci, 0, 0)),
        scratch_shapes=[pltpu.VMEM((ltot + 3 * _CONV_PAD, LANE), F32)],
        compiler_params=_cparams(("parallel", "parallel")),
        name="conv7",
    )(ctx_raw, lat_raw, w, b.reshape(1, c))


def _ssd_kernel(xbc_ref, dtc_ref, dtl_ref, par_ref, dexp_ref, ex_ref, *rest, reverse, n_ctx, fuse_norm):
    if fuse_norm:
        yo_ref, sz_ref, nw_ref, o_ref, st_ref, cumt_ref, y_ref = rest
    else:
        y_ref, st_ref, cumt_ref = rest
    i = pl.program_id(1)

    @pl.when(i == 0)
    def _():
        st_ref[...] = jnp.zeros_like(st_ref)

    dt_raw = jnp.where(i < n_ctx, dtc_ref[0], dtl_ref[0])
    bias = par_ref[0:1, :]
    a = -jnp.exp(par_ref[1:2, :])
    dt = jax.nn.softplus(dt_raw + bias)
    cum = dt * a
    row = lax.broadcasted_iota(jnp.int32, (CHUNK, LANE), 0)
    k = 1
    while k < CHUNK:
        if reverse:
            cum = cum + jnp.where(row < CHUNK - k, pltpu.roll(cum, CHUNK - k, 0), 0.0)
        else:
            cum = cum + jnp.where(row >= k, pltpu.roll(cum, k, 0), 0.0)
        k *= 2
    last = 0 if reverse else CHUNK - 1
    cum = cum * LOG2E
    cumt_ref[...] = cum.T
    li = lax.broadcasted_iota(jnp.int32, (CHUNK, CHUNK), 0)
    si = lax.broadcasted_iota(jnp.int32, (CHUNK, CHUNK), 1)
    causal = (li <= si) if reverse else (li >= si)
    lo = lax.broadcasted_iota(jnp.int32, (CHUNK, LANE), 1) < HEAD_DIM
    heads_per_group = N_HEADS // N_GROUPS
    pairs = heads_per_group // 2
    x_tiles = N_HEADS // 2

    def group(g, carry):
        shift = (LANE - heads_per_group * g) & (LANE - 1)
        cum_g = pltpu.roll(cum, shift, 1)
        dt_g = pltpu.roll(dt, shift, 1)
        cum_t = cumt_ref[pl.ds(pl.multiple_of(heads_per_group * g, heads_per_group), heads_per_group), :]
        bb = xbc_ref[0, x_tiles + g].astype(BF16)
        cb = xbc_ref[0, x_tiles + N_GROUPS + g].astype(BF16)
        scores = lax.dot_general(cb, bb, (((1,), (1,)), ((), ())), preferred_element_type=F32)
        h_t = st_ref[g]
        y_off = jnp.dot(cb, h_t.astype(BF16), preferred_element_type=F32)
        d_hi = dt_g.astype(BF16)
        r_hi = dt_g - d_hi.astype(F32)
        d_mid = r_hi.astype(BF16)
        d_lo = (r_hi - d_mid.astype(F32)).astype(BF16)
        dt_x = (jnp.dot(jnp.concatenate([d_hi, d_mid], axis=1), ex_ref[...], preferred_element_type=F32)
                + jnp.dot(d_lo, ex_ref[0:LANE, :], preferred_element_type=F32))
        xw_parts, dec_parts = [], []
        for p in range(pairs):
            j0, j1 = 2 * p, 2 * p + 1
            x2 = xbc_ref[0, pairs * g + p]
            c0 = cum_g[:, j0:j0 + 1]
            c1 = cum_g[:, j1:j1 + 1]
            l0 = jnp.exp2(jnp.where(causal, c0 - cum_t[j0:j0 + 1, :], -jnp.inf))
            l1 = jnp.exp2(jnp.where(causal, c1 - cum_t[j1:j1 + 1, :], -jnp.inf))
            m0 = (scores * l0).astype(BF16)
            m1 = (scores * l1).astype(BF16)
            dt2 = dt_x[:, p * LANE:(p + 1) * LANE]
            c2 = jnp.where(lo, c0, c1)
            xdt = x2 * dt2
            xdt_b = xdt.astype(BF16)
            zero = jnp.zeros_like(xdt_b)
            y_diag = (jnp.dot(m0, jnp.where(lo, xdt_b, zero), preferred_element_type=F32)
                      + jnp.dot(m1, jnp.where(lo, zero, xdt_b), preferred_element_type=F32))
            e2 = jnp.exp2(c2)
            y = y_diag + y_off[:, p * LANE:(p + 1) * LANE] * e2
            y_ref[0, pairs * g + p] = y + dexp_ref[pairs * g + p] * x2
            to_end = jnp.exp2(c2[last:last + 1, :] - c2)
            xw_parts.append((xdt * to_end).astype(BF16))
            dec_parts.append(e2[last:last + 1, :])
        xw = jnp.concatenate(xw_parts, axis=1)
        dec = jnp.concatenate(dec_parts, axis=1)
        upd = lax.dot_general(bb, xw, (((0,), (0,)), ((), ())), preferred_element_type=F32)
        st_ref[g] = h_t * dec + upd
        return carry

    lax.fori_loop(0, N_GROUPS, group, 0, unroll=2)

    if fuse_norm:
        sq = jnp.zeros((CHUNK, LANE), F32)
        for j in range(x_tiles):
            gj = (y_ref[0, j] + yo_ref[0, j]) * sz_ref[:, j * LANE:(j + 1) * LANE]
            y_ref[0, j] = gj
            sq = sq + gj * gj
        r = lax.rsqrt(jnp.sum(sq, axis=-1, keepdims=True) / (x_tiles * LANE) + EPS)
        for j in range(x_tiles):
            sl = slice(j * LANE, (j + 1) * LANE)
            o_ref[:, sl] = (y_ref[0, j] * r * nw_ref[:, sl]).astype(o_ref.dtype)


def _ssd(xbc_act, dt_ctx, dt_lat, par, dexp, reverse, norm_with=None):
    bsz, ntile, ltot, _ = xbc_act.shape
    l_ctx = dt_ctx.shape[1]
    l_lat = dt_lat.shape[1]
    n_ctx = l_ctx // CHUNK
    n_lat = l_lat // CHUNK
    steps = n_ctx + n_lat
    x_tiles = N_HEADS // 2
    gw = (N_HEADS // N_GROUPS) * HEAD_DIM
    e1 = (jnp.arange(gw)[None, :] // HEAD_DIM == jnp.arange(LANE)[:, None]).astype(BF16)
    expand = jnp.concatenate([e1, e1], axis=0)

    if reverse:
        def cat_chunk(i):
            return jnp.where(i < n_ctx, n_ctx - 1 - i, n_ctx + steps - 1 - i)

        def ctx_chunk(i):
            return jnp.maximum(n_ctx - 1 - i, 0)

        def lat_chunk(i):
            return jnp.minimum(steps - 1 - i, n_lat - 1)
    else:
        def cat_chunk(i):
            return i

        def ctx_chunk(i):
            return jnp.minimum(i, n_ctx - 1)

        def lat_chunk(i):
            return jnp.maximum(i - n_ctx, 0)

    y_spec = pl.BlockSpec((1, x_tiles, CHUNK, LANE), lambda b, i: (b, 0, lat_chunk(i), 0))
    in_specs = [pl.BlockSpec((1, ntile, CHUNK, LANE), lambda b, i: (b, 0, cat_chunk(i), 0)),
                pl.BlockSpec((1, CHUNK, LANE), lambda b, i: (b, ctx_chunk(i), 0)),
                pl.BlockSpec((1, CHUNK, LANE), lambda b, i: (b, lat_chunk(i), 0)),
                pl.BlockSpec((8, LANE), lambda b, i: (0, 0)),
                pl.BlockSpec((x_tiles, 1, LANE), lambda b, i: (0, 0, 0)),
                pl.BlockSpec((2 * LANE, gw), lambda b, i: (0, 0))]
    args = [xbc_act, dt_ctx, dt_lat, par, dexp, expand]
    scratch = [pltpu.VMEM((N_GROUPS, D_STATE, gw), F32), pltpu.VMEM((LANE, CHUNK), F32)]
    if norm_with is None:
        out_shape = jax.ShapeDtypeStruct((bsz, x_tiles, l_lat, LANE), F32)
        out_spec = y_spec
    else:
        y_other, silu_z, norm_w = norm_with
        dn = x_tiles * LANE
        row_spec = pl.BlockSpec((CHUNK, dn), lambda b, i: (b * n_lat + lat_chunk(i), 0))
        in_specs += [y_spec, row_spec, pl.BlockSpec((1, dn), lambda b, i: (0, 0))]
        args += [y_other, silu_z, norm_w.reshape(1, dn)]
        out_shape = jax.ShapeDtypeStruct((bsz * l_lat, dn), BF16)
        out_spec = row_spec
        scratch.append(pltpu.VMEM((1, x_tiles, CHUNK, LANE), F32))
    return pl.pallas_call(
        functools.partial(_ssd_kernel, reverse=reverse, n_ctx=n_ctx, fuse_norm=norm_with is not None),
        out_shape=out_shape,
        grid=(bsz, steps),
        in_specs=in_specs,
        out_specs=out_spec,
        scratch_shapes=scratch,
        compiler_params=_cparams(("parallel", "arbitrary")),
        name="ssd_bwd" if reverse else "ssd_fwd",
    )(*args)


def _conv31_kernel(u_ref, w_ref, b_ref, o_ref, pad_ref, *, seq):
    halo = (CF_KERNEL // 2) * GRID_W
    zeros = jnp.zeros((halo, LANE), F32)
    pad_ref[0:halo, :] = zeros
    pad_ref[halo + seq:halo + seq + halo, :] = zeros
    pad_ref[halo:halo + seq, :] = u_ref[0]
    bias = b_ref[...]

    def body(j, c):
        base = pl.multiple_of(j * CHUNK, CHUNK)
        acc = jnp.broadcast_to(bias, (CHUNK, LANE))
        for k in range(CF_KERNEL):
            tap = pad_ref[pl.ds(pl.multiple_of(base + k * GRID_W, GRID_W), CHUNK), :]
            acc = acc + tap * w_ref[k:k + 1, :]
        o_ref[0, pl.ds(base, CHUNK), :] = acc
        return c

    lax.fori_loop(0, seq // CHUNK, body, 0, unroll=2)


def _conv31(u3, w, b):
    bsz, s, c = u3.shape
    halo = (CF_KERNEL // 2) * GRID_W
    return pl.pallas_call(
        functools.partial(_conv31_kernel, seq=s),
        out_shape=jax.ShapeDtypeStruct((bsz, s, c), F32),
        grid=(bsz, c // LANE),
        in_specs=[pl.BlockSpec((1, s, LANE), lambda bi, ci: (bi, 0, ci)),
                  pl.BlockSpec((CF_KERNEL, LANE), lambda bi, ci: (0, ci)),
                  pl.BlockSpec((1, LANE), lambda bi, ci: (0, ci))],
        out_specs=pl.BlockSpec((1, s, LANE), lambda bi, ci: (bi, 0, ci)),
        scratch_shapes=[pltpu.VMEM((s + 2 * halo, LANE), F32)],
        compiler_params=_cparams(("parallel", "parallel")),
        name="conv31",
    )(u3, w, b.reshape(1, c))


def _route_kernel(x_ref, w_ref, sh_ref, sc_ref, rw_ref, rb_ref, h_ref, eid_ref, ew_ref):
    xf = x_ref[...]
    ms = jnp.mean(xf * xf, axis=-1, keepdims=True)
    h = xf * lax.rsqrt(ms + EPS) * w_ref[...]
    h = h * (1.0 + sc_ref[0]) + sh_ref[0]
    tm = xf.shape[0]
    nt = xf.shape[1] // LANE
    pitch = _pitch(nt)
    for j in range(nt):
        h_ref[pl.ds(j, tm, stride=pitch), :] = h[:, j * LANE:(j + 1) * LANE]
    for j in range(nt, pitch):
        h_ref[pl.ds(j, tm, stride=pitch), :] = jnp.zeros((tm, LANE), F32)
    logits = jnp.dot(h.astype(BF16), rw_ref[...], preferred_element_type=F32) + rb_ref[...]
    lane = lax.broadcasted_iota(jnp.int32, (tm, LANE), 1)
    lane_f = lane.astype(F32)
    ninf = -jnp.inf
    gl = jnp.where(lane < MOE_GROUPS, logits, ninf)
    gmax = jnp.max(gl, axis=-1, keepdims=True)
    gidx = jnp.min(jnp.where(gl == gmax, lane_f, float(LANE)), axis=-1, keepdims=True)
    gsum = jnp.sum(jnp.exp(gl - gmax), axis=-1, keepdims=True)
    g_p = 1.0 / gsum
    first = float(MOE_GROUPS) + gidx * float(EXPERTS_PER_GROUP)
    in_group = (lane_f >= first) & (lane_f < first + float(EXPERTS_PER_GROUP))
    el = jnp.where(in_group, logits, ninf)
    m1 = jnp.max(el, axis=-1, keepdims=True)
    i1 = jnp.min(jnp.where(el == m1, lane_f, float(LANE)), axis=-1, keepdims=True)
    el2 = jnp.where(lane_f == i1, ninf, el)
    m2 = jnp.max(el2, axis=-1, keepdims=True)
    i2 = jnp.min(jnp.where(el2 == m2, lane_f, float(LANE)), axis=-1, keepdims=True)
    e21 = jnp.exp(m2 - m1)
    den = 1.0 + e21
    w1 = (1.0 / den) * g_p
    w2 = (e21 / den) * g_p
    e1 = (i1 - float(MOE_GROUPS)).astype(jnp.int32)
    e2 = (i2 - float(MOE_GROUPS)).astype(jnp.int32)
    eid_ref[...] = jnp.where(lane == 0, e1, jnp.where(lane == 1, e2, 0))
    ew_ref[...] = jnp.where(lane == 0, w1, jnp.where(lane == 1, w2, 0.0))


def _route(x2, w, mod3, shift_chunk, scale_chunk, rows_per_batch, rw, rb, tm=256):
    m, d = x2.shape
    pitch = _pitch(d // LANE)
    tiles_per_batch = rows_per_batch // tm
    return pl.pallas_call(
        _route_kernel,
        out_shape=(jax.ShapeDtypeStruct((m * pitch, LANE), F32),
                   jax.ShapeDtypeStruct((m, LANE), jnp.int32),
                   jax.ShapeDtypeStruct((m, LANE), F32)),
        grid=(m // tm,),
        in_specs=[pl.BlockSpec((tm, d), lambda i: (i, 0)),
                  pl.BlockSpec((1, d), lambda i: (0, 0)),
                  pl.BlockSpec((1, 1, d), lambda i: (i // tiles_per_batch, 0, shift_chunk)),
                  pl.BlockSpec((1, 1, d), lambda i: (i // tiles_per_batch, 0, scale_chunk)),
                  pl.BlockSpec((d, LANE), lambda i: (0, 0)),
                  pl.BlockSpec((1, LANE), lambda i: (0, 0))],
        out_specs=(pl.BlockSpec((tm * pitch, LANE), lambda i: (i, 0)),
                   pl.BlockSpec((tm, LANE), lambda i: (i, 0)),
                   pl.BlockSpec((tm, LANE), lambda i: (i, 0))),
        compiler_params=_cparams(("parallel",)),
        name="route",
    )(x2, w.reshape(1, d), mod3, mod3, rw, rb)


_DMA_UNROLL = 8


def _rows_to_matrix(ref, tm, nt):
    return jnp.concatenate([ref[pl.ds(j, tm, stride=_pitch(nt)), :] for j in range(nt)], axis=1)


def _bulk_wait(src, dst, sem, total_rows):
    pltpu.make_async_copy(src.at[pl.ds(0, total_rows), :], dst.at[pl.ds(0, total_rows), :], sem).wait()


def _for_rows(n, body):
    groups = lax.shift_right_logical(n, _DMA_UNROLL.bit_length() - 1)

    def group(g, c):
        for u in range(_DMA_UNROLL):
            body(g * _DMA_UNROLL + u)
        return c

    def tail(r, c):
        body(r)
        return c

    lax.fori_loop(0, groups, group, 0)
    lax.fori_loop(groups * _DMA_UNROLL, n, tail, 0)


def _stream_expert_weights(b, be_ref, eord_ref, enext_ref, w_hbms, w_bufs, w_caches, wsem):
    prev = jnp.maximum(b - 1, 0)

    def copies(e, slot):
        return [pltpu.make_async_copy(w.at[e], buf.at[slot], wsem.at[slot]) for w, buf in zip(w_hbms, w_bufs)]

    @pl.when(b == 0)
    def _():
        for cp in copies(be_ref[0], 0):
            cp.start(priority=1)

    @pl.when((b == 0) | (be_ref[b] != be_ref[prev]))
    def _():
        for s in range(2):
            @pl.when((eord_ref[b] & 1) == s)
            def _(s=s):
                for cp in copies(be_ref[b], s):
                    cp.wait()

                @pl.when(enext_ref[b] >= 0)
                def _():
                    for cp in copies(enext_ref[b], 1 - s):
                        cp.start(priority=1)

                for buf, cache in zip(w_bufs, w_caches):
                    cache[...] = buf[s].astype(BF16)


def _expert_up_kernel(be_ref, nused_ref, eord_ref, enext_ref, rowc_ref, rown_ref, h_hbm, wg_hbm, wu_hbm,
                      o_ref, xs0_ref, xs1_ref, wgs_ref, wus_ref, wgb_ref, wub_ref, sem, wsem, *, fchunk, nt):
    b = pl.program_id(0)
    n_used = nused_ref[0]
    dff = wgb_ref.shape[1]
    slots = (xs0_ref, xs1_ref)
    pitch = _pitch(nt)

    def gather_row(row_ref, r, slot):
        return pltpu.make_async_copy(h_hbm.at[pl.ds(row_ref[0, 0, r], nt), :],
                                     slots[slot].at[pl.ds(r * pitch, nt), :], sem.at[slot])

    @pl.when(b == 0)
    def _():
        _for_rows(MOE_BLOCK, lambda r: gather_row(rowc_ref, r, 0).start())

    for slot in range(2):
        @pl.when((b <= n_used) & (lax.rem(b, 2) == slot))
        def _(slot=slot):
            _bulk_wait(h_hbm, slots[slot], sem.at[slot], MOE_BLOCK * nt)

    @pl.when(b < n_used)
    def _():
        _stream_expert_weights(b, be_ref, eord_ref, enext_ref, (wg_hbm, wu_hbm), (wgs_ref, wus_ref),
                               (wgb_ref, wub_ref), wsem)

        for slot in range(2):
            @pl.when(lax.rem(b, 2) == slot)
            def _(slot=slot):
                xb = _rows_to_matrix(slots[slot], MOE_BLOCK, nt).astype(BF16)
                nf = dff // fchunk
                per = MOE_BLOCK // (2 * nf)

                def request(part):
                    for r in range(part * per, (part + 1) * per):
                        gather_row(rown_ref, r, 1 - slot).start()

                for f in range(nf):
                    sl = slice(f * fchunk, (f + 1) * fchunk)
                    request(2 * f)
                    gate = jnp.dot(xb, wgb_ref[:, sl], preferred_element_type=F32)
                    request(2 * f + 1)
                    up = jnp.dot(xb, wub_ref[:, sl], preferred_element_type=F32)
                    o_ref[:, sl] = (_silu(gate) * up).astype(o_ref.dtype)

    @pl.when(b >= n_used)
    def _():
        o_ref[...] = jnp.zeros_like(o_ref)


def _expert_up(h2t, src_rows, w_gate, w_up, tables, fchunk=256):
    n_blocks = src_rows.shape[0]
    _, d, dff = w_gate.shape
    nt = d // LANE
    slot_rows = MOE_BLOCK * _pitch(nt)
    grid_spec = pltpu.PrefetchScalarGridSpec(
        num_scalar_prefetch=len(tables),
        grid=(n_blocks,),
        in_specs=[pl.BlockSpec((1, 1, MOE_BLOCK), lambda b, *_: (b, 0, 0), memory_space=pltpu.SMEM),
                  pl.BlockSpec((1, 1, MOE_BLOCK), lambda b, *_: (jnp.minimum(b + 1, n_blocks - 1), 0, 0),
                               memory_space=pltpu.SMEM),
                  pl.BlockSpec(memory_space=pl.ANY),
                  pl.BlockSpec(memory_space=pl.ANY),
                  pl.BlockSpec(memory_space=pl.ANY)],
        out_specs=pl.BlockSpec((MOE_BLOCK, dff), lambda b, *_: (b, 0)),
        scratch_shapes=[pltpu.VMEM((slot_rows, LANE), F32),
                        pltpu.VMEM((slot_rows, LANE), F32),
                        pltpu.VMEM((2, d, dff), F32),
                        pltpu.VMEM((2, d, dff), F32),
                        pltpu.VMEM((d, dff), BF16),
                        pltpu.VMEM((d, dff), BF16),
                        pltpu.SemaphoreType.DMA((2,)),
                        pltpu.SemaphoreType.DMA((2,))],
    )
    return pl.pallas_call(
        functools.partial(_expert_up_kernel, fchunk=fchunk, nt=nt),
        out_shape=jax.ShapeDtypeStruct((n_blocks * MOE_BLOCK, dff), BF16),
        grid_spec=grid_spec,
        compiler_params=_cparams(("arbitrary",)),
        name="expert_up",
    )(*tables, src_rows, src_rows, h2t, w_gate, w_up)


def _expert_down_kernel(be_ref, nused_ref, eord_ref, enext_ref, dst_ref, h_ref, wd_hbm, y_hbm,
                        ys0_ref, ys1_ref, wds_ref, wdb_ref, sem, wsem, *, nchunk, nt):
    b = pl.program_id(0)
    n_used = nused_ref[0]
    d = wdb_ref.shape[1]
    slots = (ys0_ref, ys1_ref)
    pitch = _pitch(nt)

    def scatter_row(r, slot):
        return pltpu.make_async_copy(slots[slot].at[pl.ds(r * pitch, pitch), :],
                                     y_hbm.at[pl.ds(dst_ref[0, 0, r], pitch), :], sem.at[slot])

    @pl.when(b == 0)
    def _():
        ys0_ref[...] = jnp.zeros_like(ys0_ref)
        ys1_ref[...] = jnp.zeros_like(ys1_ref)

    for slot in range(2):
        @pl.when((b >= 1) & (b <= n_used) & (lax.rem(b, 2) == slot))
        def _(slot=slot):
            _bulk_wait(slots[slot], y_hbm, sem.at[slot], MOE_BLOCK * pitch)

    @pl.when(b < n_used)
    def _():
        _stream_expert_weights(b, be_ref, eord_ref, enext_ref, (wd_hbm,), (wds_ref,), (wdb_ref,), wsem)
        hb = h_ref[...]
        for slot in range(2):
            @pl.when(lax.rem(b, 2) == slot)
            def _(slot=slot):
                nc = d // nchunk
                per = MOE_BLOCK // nc
                for c in range(nc):
                    for r in range(c * per, (c + 1) * per):
                        scatter_row(r, 1 - slot).start()
                    out = jnp.dot(hb, wdb_ref[:, c * nchunk:(c + 1) * nchunk], preferred_element_type=F32)
                    for j in range(nchunk // LANE):
                        slots[slot][pl.ds(c * (nchunk // LANE) + j, MOE_BLOCK, stride=pitch), :] = (
                            out[:, j * LANE:(j + 1) * LANE])

    for slot in range(2):
        @pl.when((b == n_used) & (lax.rem(b, 2) == slot))
        def _(slot=slot):
            _for_rows(MOE_BLOCK, lambda r: scatter_row(r, 1 - slot).start())
            _bulk_wait(slots[1 - slot], y_hbm, sem.at[1 - slot], MOE_BLOCK * pitch)


def _expert_down(hid, dst_rows, w_down, tables, y_slots, nchunk=256):
    n_rows, dff = hid.shape
    n_blocks = n_rows // MOE_BLOCK
    d = w_down.shape[2]
    nt = d // LANE
    pitch = _pitch(nt)
    grid_spec = pltpu.PrefetchScalarGridSpec(
        num_scalar_prefetch=len(tables),
        grid=(n_blocks,),
        in_specs=[pl.BlockSpec((1, 1, MOE_BLOCK), lambda b, *_: (b, 0, 0), memory_space=pltpu.SMEM),
                  pl.BlockSpec((MOE_BLOCK, dff), lambda b, be, n, *_: (jnp.minimum(b, n[0] - 1), 0)),
                  pl.BlockSpec(memory_space=pl.ANY)],
        out_specs=pl.BlockSpec(memory_space=pl.ANY),
        scratch_shapes=[pltpu.VMEM((MOE_BLOCK * pitch, LANE), F32),
                        pltpu.VMEM((MOE_BLOCK * pitch, LANE), F32),
                        pltpu.VMEM((2, dff, d), F32),
                        pltpu.VMEM((dff, d), BF16),
                        pltpu.SemaphoreType.DMA((2,)),
                        pltpu.SemaphoreType.DMA((2,))],
    )
    return pl.pallas_call(
        functools.partial(_expert_down_kernel, nchunk=nchunk, nt=nt),
        out_shape=jax.ShapeDtypeStruct((y_slots * pitch, LANE), F32),
        grid_spec=grid_spec,
        compiler_params=_cparams(("arbitrary",)),
        name="expert_down",
    )(*tables, dst_rows, hid, w_down)


def _combine_kernel(y0_ref, y1_ref, ew_ref, x_ref, g_ref, w_ref, o_ref):
    tm, d = x_ref.shape
    nt = d // LANE
    ew = ew_ref[...]
    moe = (_rows_to_matrix(y0_ref, tm, nt) * ew[:, 0:1]
           + _rows_to_matrix(y1_ref, tm, nt) * ew[:, 1:2])
    xo = x_ref[...] + g_ref[0] * moe
    ms = jnp.mean(xo * xo, axis=-1, keepdims=True)
    o_ref[...] = xo * lax.rsqrt(ms + EPS) * w_ref[...]


def _combine(y, ew, x2, mod3, gate_chunk, rows_per_batch, final_w, tm=256):
    m, d = x2.shape
    pitch = _pitch(d // LANE)
    tiles = m // tm
    tiles_per_batch = rows_per_batch // tm
    return pl.pallas_call(
        _combine_kernel,
        out_shape=jax.ShapeDtypeStruct((m, d), F32),
        grid=(tiles,),
        in_specs=[pl.BlockSpec((tm * pitch, LANE), lambda i: (i, 0)),
                  pl.BlockSpec((tm * pitch, LANE), lambda i: (tiles + i, 0)),
                  pl.BlockSpec((tm, LANE), lambda i: (i, 0)),
                  pl.BlockSpec((tm, d), lambda i: (i, 0)),
                  pl.BlockSpec((1, 1, d), lambda i: (i // tiles_per_batch, 0, gate_chunk)),
                  pl.BlockSpec((1, d), lambda i: (0, 0))],
        out_specs=pl.BlockSpec((tm, d), lambda i: (i, 0)),
        compiler_params=_cparams(("parallel",)),
        name="moe_combine",
    )(y, y, ew, x2, mod3, final_w.reshape(1, d))


def _dispatch_tables(eid, n_tok, pitch):
    top_k = eid.shape[1]
    n_assign = n_tok * top_k
    expert = eid.reshape(-1)
    key = jnp.sort(expert * n_assign + jnp.arange(n_assign, dtype=jnp.int32))
    sorted_assign = key % n_assign
    bounds = jnp.arange(N_EXPERTS + 1, dtype=jnp.int32) * n_assign
    start = jnp.searchsorted(key, bounds, side="left").astype(jnp.int32)
    counts = start[1:] - start[:-1]
    nblk = (counts + MOE_BLOCK - 1) // MOE_BLOCK
    blk_end = jnp.cumsum(nblk)
    blk_start = blk_end - nblk
    steps = -(-n_assign // MOE_BLOCK) + N_EXPERTS + 1
    bidx = jnp.arange(steps, dtype=jnp.int32)
    lane = jnp.arange(MOE_BLOCK, dtype=jnp.int32)[None, :]
    block_expert = jnp.minimum(jnp.searchsorted(blk_end, bidx, side="right"), N_EXPERTS - 1).astype(jnp.int32)
    in_expert = (bidx - blk_start[block_expert]) * MOE_BLOCK
    n_valid = jnp.clip(counts[block_expert] - in_expert, 0, MOE_BLOCK)
    src = start[block_expert][:, None] + in_expert[:, None] + lane
    valid = lane < n_valid[:, None]
    assign = sorted_assign[jnp.clip(src, 0, n_assign - 1)]
    tok = assign // top_k
    src_rows = jnp.where(valid, tok, (bidx[:, None] * MOE_BLOCK + lane) % n_tok) * pitch
    dst_slot = jnp.where(valid, (assign % top_k) * n_tok + tok, top_k * n_tok + lane)
    dst_rows = jnp.concatenate([top_k * n_tok + lane, dst_slot[:-1]], axis=0) * pitch
    n_used = blk_end[-1].astype(jnp.int32)
    first = jnp.concatenate([jnp.ones((1,), jnp.int32),
                             (block_expert[1:] != block_expert[:-1]).astype(jnp.int32)])
    expert_ordinal = (jnp.cumsum(first) - 1).astype(jnp.int32)
    next_blk = blk_end[block_expert]
    next_expert = jnp.where(next_blk < n_used, block_expert[jnp.minimum(next_blk, steps - 1)], -1).astype(jnp.int32)
    tables = (block_expert, n_used.reshape(1), expert_ordinal, next_expert)
    return (src_rows.astype(jnp.int32).reshape(steps, 1, MOE_BLOCK),
            dst_rows.astype(jnp.int32).reshape(steps, 1, MOE_BLOCK), tables)


def kernel(x, c, ctx, c_ctx, ada_w, ada_b, norm1_w, w_in, ssm_conv_w, ssm_conv_b, dt_bias, a_log, d_skip, ssm_norm_w, ssm_out_w, cf_dw_w, cf_dw_b, cf_ln_w, cf_ln_b, cf_out_w, cf_out_b, w_o, norm2_w, router_group_w, router_group_b, router_expert_w, router_expert_b, expert_w_gate, expert_w_up, expert_w_down, final_norm_w):
    bsz, seq, d = x.shape
    l_ctx = ctx.shape[1]
    n_tok = bsz * seq
    d_inner = ssm_norm_w.shape[1]
    gn = N_GROUPS * D_STATE
    xbc_dim = d_inner + 2 * gn
    off_dt = xbc_dim
    off_z = off_dt + N_HEADS
    off_glu = off_z + d_inner
    off_gate = off_glu + 2 * d

    ctx_row = bsz
    crows = jnp.zeros((8, d), F32).at[:bsz].set(c).at[ctx_row].set(c_ctx)
    mod = _ada(crows, ada_w[0], ada_b[0])
    mod3 = mod.reshape(8, 1, 6 * d)
    lat_rows = jnp.arange(bsz, dtype=jnp.int32)
    ctx_rows = jnp.full((bsz,), ctx_row, jnp.int32)

    h_lat = _normmod(x, norm1_w[0], mod3, lat_rows, 0, 1, BF16).reshape(n_tok, d)
    h_ctx = _normmod(ctx, norm1_w[0], mod3, ctx_rows, 0, 1, BF16).reshape(bsz * l_ctx, d)

    wt, (r_xbc, r_dt, r_z, r_glu, r_gate) = _pack_wt(
        jnp.transpose(w_in[0]),
        [(0, xbc_dim), (off_dt, off_z), (off_z, off_glu), (off_glu, off_gate), (off_gate, off_gate + 2 * d)])

    xbc_lat = _mm(h_lat, wt, tn=2048, name="in_xbc", rows=(r_xbc, xbc_dim)).reshape(bsz, seq, xbc_dim)
    xbc_ctx = _mm(h_ctx, wt, tm=512, name="in_xbc_ctx", rows=(r_xbc, xbc_dim)).reshape(bsz, l_ctx, xbc_dim)
    dt_lat = _mm(h_lat, wt, name="in_dt", rows=(r_dt, LANE)).reshape(bsz, seq, LANE)
    dt_ctx = _mm(h_ctx, wt, tm=512, name="in_dt_ctx", rows=(r_dt, LANE)).reshape(bsz, l_ctx, LANE)
    sz = _mm(h_lat, wt, act="silu", tn=2048, name="in_z", rows=(r_z, d_inner))
    u = _mm_glu(h_lat, wt, r_glu, d, tn=1024)

    xbc_act = _conv7(xbc_ctx, xbc_lat, ssm_conv_w[0], ssm_conv_b[0])

    def ssd_params(k):
        par = jnp.zeros((8, LANE), F32).at[0, :N_HEADS].set(dt_bias[0, k]).at[1, :N_HEADS].set(a_log[0, k])
        return par, jnp.repeat(d_skip[0, k], HEAD_DIM).reshape(N_HEADS // 2, 1, LANE)

    y_bwd = _ssd(xbc_act, dt_ctx, dt_lat, *ssd_params(1), reverse=True)
    gnorm = _ssd(xbc_act, dt_ctx, dt_lat, *ssd_params(0), reverse=False,
                 norm_with=(y_bwd, sz, ssm_norm_w[0]))
    y_ssd = _mm(gnorm, ssm_out_w[0].astype(BF16), tn=512, name="ssm_out")

    cv = _conv31(u.reshape(bsz, seq, d), cf_dw_w[0], cf_dw_b[0]).reshape(n_tok, d)
    merged = _mm_merge(cv, cf_ln_w[0], cf_ln_b[0], cf_out_w[0].astype(BF16), cf_out_b[0],
                       h_lat, wt, r_gate, y_ssd)
    x1 = _mm_resid(merged, w_o[0].astype(BF16), x.reshape(n_tok, d), mod3, 2, seq, tm=2048)

    n_r = MOE_GROUPS + N_EXPERTS
    rw = jnp.pad(jnp.concatenate([router_group_w[0], router_expert_w[0]], axis=1),
                 ((0, 0), (0, LANE - n_r))).astype(BF16)
    rb = jnp.pad(jnp.concatenate([router_group_b[0], router_expert_b[0]]), (0, LANE - n_r)).reshape(1, LANE)
    h2t, eid, ew = _route(x1, norm2_w[0], mod3, 3, 4, seq, rw, rb)

    src_rows, dst_rows, tables = _dispatch_tables(eid[:, :2], n_tok, _pitch(d // LANE))
    hid = _expert_up(h2t, src_rows, expert_w_gate[0], expert_w_up[0], tables)
    y = _expert_down(hid, dst_rows, expert_w_down[0], tables, 2 * n_tok + MOE_BLOCK)
    out = _combine(y, ew, x1, mod3, 5, seq, final_norm_w)
    return out.reshape(bsz, seq, d)
```

```python
import functools

import jax
import jax.numpy as jnp
from jax import lax
from jax.experimental import pallas as pl
from jax.experimental.pallas import tpu as pltpu

F32 = jnp.float32
BF16 = jnp.bfloat16

EPS = 1e-6
GRID_W = 64
HEAD_DIM = 64
N_HEADS = 64
N_GROUPS = 8
D_STATE = 128
CHUNK = 128
SSM_CONV = 7
CF_KERNEL = 31
MOE_GROUPS = 8
EXPERTS_PER_GROUP = 8
N_EXPERTS = 64
MOE_BLOCK = 256
LANE = 128
LOG2E = 1.4426950408889634
VMEM_LIMIT = 56 * 1024 * 1024


def _cparams(sem):
    return pltpu.CompilerParams(dimension_semantics=sem, vmem_limit_bytes=VMEM_LIMIT)


def _silu(v):
    return v * jax.nn.sigmoid(v)


def _pitch(nt):
    return nt + 1


def _ada_kernel(c_ref, w_ref, b_ref, o_ref):
    s = _silu(c_ref[...])
    o_ref[...] = jnp.dot(s.astype(BF16), w_ref[...].astype(BF16),
                         preferred_element_type=F32) + b_ref[...]


def _ada(crows, ada_w, ada_b, tn=1024):
    r, d = crows.shape
    n = ada_w.shape[1]
    return pl.pallas_call(
        _ada_kernel,
        out_shape=jax.ShapeDtypeStruct((r, n), F32),
        grid=(n // tn,),
        in_specs=[pl.BlockSpec((r, d), lambda j: (0, 0)),
                  pl.BlockSpec((d, tn), lambda j: (0, j)),
                  pl.BlockSpec((1, tn), lambda j: (0, j))],
        out_specs=pl.BlockSpec((r, tn), lambda j: (0, j)),
        compiler_params=_cparams(("parallel",)),
        name="ada",
    )(crows, ada_w, ada_b.reshape(1, n))


def _normmod_kernel(rows_ref, x_ref, w_ref, sh_ref, sc_ref, o_ref):
    del rows_ref
    xf = x_ref[0]
    ms = jnp.mean(xf * xf, axis=-1, keepdims=True)
    y = xf * lax.rsqrt(ms + EPS) * w_ref[...]
    o_ref[0] = (y * (1.0 + sc_ref[0]) + sh_ref[0]).astype(o_ref.dtype)


def _normmod(x3, w, mod3, rows, shift_chunk, scale_chunk, out_dtype, tm=256):
    bx, l, d = x3.shape
    grid_spec = pltpu.PrefetchScalarGridSpec(
        num_scalar_prefetch=1,
        grid=(bx, l // tm),
        in_specs=[pl.BlockSpec((1, tm, d), lambda b, i, r: (b, i, 0)),
                  pl.BlockSpec((1, d), lambda b, i, r: (0, 0)),
                  pl.BlockSpec((1, 1, d), lambda b, i, r: (r[b], 0, shift_chunk)),
                  pl.BlockSpec((1, 1, d), lambda b, i, r: (r[b], 0, scale_chunk))],
        out_specs=pl.BlockSpec((1, tm, d), lambda b, i, r: (b, i, 0)),
    )
    return pl.pallas_call(
        _normmod_kernel,
        out_shape=jax.ShapeDtypeStruct((bx, l, d), out_dtype),
        grid_spec=grid_spec,
        compiler_params=_cparams(("parallel", "parallel")),
        name="normmod",
    )(rows, x3, w.reshape(1, d), mod3, mod3)


def _dot_nt(a, wt):
    return lax.dot_general(a, wt, (((1,), (1,)), ((), ())), preferred_element_type=F32)


def _mm_kernel(a_ref, w_ref, *rest, act, has_bias, w_rows):
    o_ref = rest[-1]
    a = a_ref[...]
    acc = _dot_nt(a, w_ref[...]) if w_rows else jnp.dot(a, w_ref[...], preferred_element_type=F32)
    if has_bias:
        acc = acc + rest[0][...]
    if act == "silu":
        acc = _silu(acc)
    elif act == "sigmoid":
        acc = jax.nn.sigmoid(acc)
    o_ref[...] = acc.astype(o_ref.dtype)


def _mm(a, w, bias=None, act=None, out_dtype=F32, tm=1024, tn=1024, name="mm", rows=None):
    m, k = a.shape
    start, n = (0, w.shape[1]) if rows is None else rows
    tm, tn = min(tm, m), min(tn, n)
    j0 = start // tn
    w_spec = (pl.BlockSpec((k, tn), lambda i, j: (0, j)) if rows is None
              else pl.BlockSpec((tn, k), lambda i, j: (j0 + j, 0)))
    in_specs = [pl.BlockSpec((tm, k), lambda i, j: (i, 0)), w_spec]
    args = [a, w]
    if bias is not None:
        in_specs.append(pl.BlockSpec((1, tn), lambda i, j: (0, j)))
        args.append(bias.reshape(1, n))
    return pl.pallas_call(
        functools.partial(_mm_kernel, act=act, has_bias=bias is not None, w_rows=rows is not None),
        out_shape=jax.ShapeDtypeStruct((m, n), out_dtype),
        grid=(m // tm, n // tn),
        in_specs=in_specs,
        out_specs=pl.BlockSpec((tm, tn), lambda i, j: (i, j)),
        compiler_params=_cparams(("parallel", "parallel")),
        name=name,
    )(*args)


def _mm_glu_kernel(a_ref, wa_ref, wb_ref, o_ref):
    a = a_ref[...]
    va = _dot_nt(a, wa_ref[...])
    vb = _dot_nt(a, wb_ref[...])
    o_ref[...] = va * jax.nn.sigmoid(vb)


def _mm_glu(a, wt, start, n, tm=1024, tn=512):
    m, k = a.shape
    tm = min(tm, m)
    ja, jb = start // tn, (start + n) // tn
    return pl.pallas_call(
        _mm_glu_kernel,
        out_shape=jax.ShapeDtypeStruct((m, n), F32),
        grid=(m // tm, n // tn),
        in_specs=[pl.BlockSpec((tm, k), lambda i, j: (i, 0)),
                  pl.BlockSpec((tn, k), lambda i, j: (ja + j, 0)),
                  pl.BlockSpec((tn, k), lambda i, j: (jb + j, 0))],
        out_specs=pl.BlockSpec((tm, tn), lambda i, j: (i, j)),
        compiler_params=_cparams(("parallel", "parallel")),
        name="mm_glu",
    )(a, wt, wt)


def _mm_merge_kernel(cv_ref, lw_ref, lb_ref, w_ref, b_ref, h_ref, wga_ref, wgb_ref, ys_ref, o_ref, u_ref):
    @pl.when(pl.program_id(1) == 0)
    def _():
        xf = cv_ref[...]
        mu = jnp.mean(xf, axis=-1, keepdims=True)
        xc = xf - mu
        var = jnp.mean(xc * xc, axis=-1, keepdims=True)
        y = xc * lax.rsqrt(var + EPS) * lw_ref[...] + lb_ref[...]
        u_ref[...] = _silu(y).astype(u_ref.dtype)

    ycf = jnp.dot(u_ref[...], w_ref[...], preferred_element_type=F32) + b_ref[...]
    h = h_ref[...]
    gate_a = jax.nn.sigmoid(_dot_nt(h, wga_ref[...]))
    gate_b = jax.nn.sigmoid(_dot_nt(h, wgb_ref[...]))
    o_ref[...] = (gate_a * ys_ref[...] + gate_b * ycf).astype(o_ref.dtype)


def _mm_merge(cv, ln_w, ln_b, w, bias, h, wt, gate_start, y_ssd, tm=512, tn=512):
    m, k = cv.shape
    n = w.shape[1]
    tm = min(tm, m)
    ja, jb = gate_start // tn, (gate_start + n) // tn
    return pl.pallas_call(
        _mm_merge_kernel,
        out_shape=jax.ShapeDtypeStruct((m, n), BF16),
        grid=(m // tm, n // tn),
        in_specs=[pl.BlockSpec((tm, k), lambda i, j: (i, 0)),
                  pl.BlockSpec((1, k), lambda i, j: (0, 0)),
                  pl.BlockSpec((1, k), lambda i, j: (0, 0)),
                  pl.BlockSpec((k, tn), lambda i, j: (0, j)),
                  pl.BlockSpec((1, tn), lambda i, j: (0, j)),
                  pl.BlockSpec((tm, k), lambda i, j: (i, 0)),
                  pl.BlockSpec((tn, k), lambda i, j: (ja + j, 0)),
                  pl.BlockSpec((tn, k), lambda i, j: (jb + j, 0)),
                  pl.BlockSpec((tm, tn), lambda i, j: (i, j))],
        out_specs=pl.BlockSpec((tm, tn), lambda i, j: (i, j)),
        scratch_shapes=[pltpu.VMEM((tm, k), BF16)],
        compiler_params=_cparams(("parallel", "arbitrary")),
        name="mm_merge",
    )(cv, ln_w.reshape(1, k), ln_b.reshape(1, k), w, bias.reshape(1, n), h, wt, wt, y_ssd)


def _mm_resid_kernel(a_ref, w_ref, x_ref, g_ref, o_ref):
    out = jnp.dot(a_ref[...], w_ref[...], preferred_element_type=F32)
    o_ref[...] = x_ref[...] + g_ref[0] * out


def _mm_resid(a, w, x2, mod3, gate_chunk, rows_per_batch, tm=1024, tn=512):
    m, k = a.shape
    n = w.shape[1]
    tm = min(tm, rows_per_batch)
    nj = n // tn
    tiles_per_batch = rows_per_batch // tm
    return pl.pallas_call(
        _mm_resid_kernel,
        out_shape=jax.ShapeDtypeStruct((m, n), F32),
        grid=(m // tm, nj),
        in_specs=[pl.BlockSpec((tm, k), lambda i, j: (i, 0)),
                  pl.BlockSpec((k, tn), lambda i, j: (0, j)),
                  pl.BlockSpec((tm, tn), lambda i, j: (i, j)),
                  pl.BlockSpec((1, 1, tn),
                               lambda i, j: (i // tiles_per_batch, 0, gate_chunk * nj + j))],
        out_specs=pl.BlockSpec((tm, tn), lambda i, j: (i, j)),
        compiler_params=_cparams(("parallel", "parallel")),
        name="mm_resid",
    )(a, w, x2, mod3)


W_ALIGN = 2048


def _pack_wt_kernel(valid_ref, off_ref, w_ref, o_ref):
    del off_ref
    nrow = valid_ref[pl.program_id(0)]
    row = lax.broadcasted_iota(jnp.int32, o_ref.shape, 0)
    o_ref[...] = jnp.where(row < nrow, w_ref[...], 0.0).astype(o_ref.dtype)


def _pack_wt(wt, segments, tr=512):
    n, k = wt.shape
    starts, src_off, valid = [], [], []
    pos = 0
    for lo, hi in segments:
        pos = -(-pos // W_ALIGN) * W_ALIGN
        starts.append(pos)
        while len(src_off) < pos // tr:
            src_off.append(0)
            valid.append(0)
        for r in range(lo, hi, tr):
            src_off.append(min(r, n - tr))
            valid.append(min(tr, hi - r))
            assert r <= n - tr or hi - r == tr
        pos += -(-(hi - lo) // tr) * tr
    total = -(-pos // W_ALIGN) * W_ALIGN
    while len(src_off) < total // tr:
        src_off.append(0)
        valid.append(0)
    grid_spec = pltpu.PrefetchScalarGridSpec(
        num_scalar_prefetch=2,
        grid=(total // tr,),
        in_specs=[pl.BlockSpec((pl.Element(tr), pl.Element(k)), lambda t, v, off: (off[t] * 8, 0))],
        out_specs=pl.BlockSpec((tr, k), lambda t, v, off: (t, 0)),
    )
    packed = pl.pallas_call(
        _pack_wt_kernel,
        out_shape=jax.ShapeDtypeStruct((total, k), BF16),
        grid_spec=grid_spec,
        compiler_params=_cparams(("parallel",)),
        name="pack_wt",
    )(jnp.asarray(valid, jnp.int32), jnp.asarray(src_off, jnp.int32) // 8, wt)
    return packed, starts


_CONV_PAD = 8


def _conv7_kernel(ctx_ref, lat_ref, w_ref, b_ref, o_ref, pad_ref, *, l_ctx, l_lat):
    p = _CONV_PAD
    zeros = jnp.zeros((p, LANE), F32)
    off_ctx = p
    off_lat = 2 * p + l_ctx
    pad_ref[0:p, :] = zeros
    pad_ref[off_ctx + l_ctx:off_lat, :] = zeros
    pad_ref[off_lat + l_lat:off_lat + l_lat + p, :] = zeros
    pad_ref[off_ctx:off_ctx + l_ctx, :] = ctx_ref[0]
    pad_ref[off_lat:off_lat + l_lat, :] = lat_ref[0]
    reach = SSM_CONV // 2
    bias = b_ref[...]

    def chunk(pad_base, out_base):
        acc = jnp.broadcast_to(bias, (CHUNK, LANE))
        for k in range(SSM_CONV):
            tap = pad_ref[pl.ds(pad_base - reach + k, CHUNK), :]
            acc = acc + tap * w_ref[k:k + 1, :]
        o_ref[0, 0, pl.ds(out_base, CHUNK), :] = _silu(acc)

    def ctx_body(j, c):
        base = pl.multiple_of(j * CHUNK, CHUNK)
        chunk(off_ctx + base, base)
        return c

    def lat_body(j, c):
        base = pl.multiple_of(j * CHUNK, CHUNK)
        chunk(off_lat + base, l_ctx + base)
        return c

    lax.fori_loop(0, l_ctx // CHUNK, ctx_body, 0)
    lax.fori_loop(0, l_lat // CHUNK, lat_body, 0, unroll=2)


def _conv7(ctx_raw, lat_raw, w, b):
    bsz, l_ctx, c = ctx_raw.shape
    l_lat = lat_raw.shape[1]
    ltot = l_ctx + l_lat
    nct = c // LANE
    return pl.pallas_call(
        functools.partial(_conv7_kernel, l_ctx=l_ctx, l_lat=l_lat),
        out_shape=jax.ShapeDtypeStruct((bsz, nct, ltot, LANE), F32),
        grid=(bsz, nct),
        in_specs=[pl.BlockSpec((1, l_ctx, LANE), lambda bi, ci: (bi, 0, ci)),
                  pl.BlockSpec((1, l_lat, LANE), lambda bi, ci: (bi, 0, ci)),
                  pl.BlockSpec((SSM_CONV, LANE), lambda bi, ci: (0, ci)),
                  pl.BlockSpec((1, LANE), lambda bi, ci: (0, ci))],
        out_specs=pl.BlockSpec((1, 1, ltot, LANE), lambda bi, ci: (bi, ci, 0, 0)),
        scratch_shapes=[pltpu.VMEM((ltot + 3 * _CONV_PAD, LANE), F32)],
        compiler_params=_cparams(("parallel", "parallel")),
        name="conv7",
    )(ctx_raw, lat_raw, w, b.reshape(1, c))


def _ssd_kernel(xbc_ref, dtc_ref, dtl_ref, par_ref, dexp_ref, ex_ref, *rest, reverse, n_ctx, fuse_norm):
    if fuse_norm:
        yo_ref, sz_ref, nw_ref, o_ref, st_ref, cumt_ref, y_ref = rest
    else:
        y_ref, st_ref, cumt_ref = rest
    i = pl.program_id(1)

    @pl.when(i == 0)
    def _():
        st_ref[...] = jnp.zeros_like(st_ref)

    dt_raw = jnp.where(i < n_ctx, dtc_ref[0], dtl_ref[0])
    bias = par_ref[0:1, :]
    a = -jnp.exp(par_ref[1:2, :])
    dt = jax.nn.softplus(dt_raw + bias)
    cum = dt * a
    row = lax.broadcasted_iota(jnp.int32, (CHUNK, LANE), 0)
    k = 1
    while k < CHUNK:
        if reverse:
            cum = cum + jnp.where(row < CHUNK - k, pltpu.roll(cum, CHUNK - k, 0), 0.0)
        else:
            cum = cum + jnp.where(row >= k, pltpu.roll(cum, k, 0), 0.0)
        k *= 2
    last = 0 if reverse else CHUNK - 1
    cum = cum * LOG2E
    cumt_ref[...] = cum.T
    li = lax.broadcasted_iota(jnp.int32, (CHUNK, CHUNK), 0)
    si = lax.broadcasted_iota(jnp.int32, (CHUNK, CHUNK), 1)
    causal = (li <= si) if reverse else (li >= si)
    lo = lax.broadcasted_iota(jnp.int32, (CHUNK, LANE), 1) < HEAD_DIM
    heads_per_group = N_HEADS // N_GROUPS
    pairs = heads_per_group // 2
    x_tiles = N_HEADS // 2

    def group(g, carry):
        shift = (LANE - heads_per_group * g) & (LANE - 1)
        cum_g = pltpu.roll(cum, shift, 1)
        dt_g = pltpu.roll(dt, shift, 1)
        cum_t = cumt_ref[pl.ds(pl.multiple_of(heads_per_group * g, heads_per_group), heads_per_group), :]
        bb = xbc_ref[0, x_tiles + g].astype(BF16)
        cb = xbc_ref[0, x_tiles + N_GROUPS + g].astype(BF16)
        scores = lax.dot_general(cb, bb, (((1,), (1,)), ((), ())), preferred_element_type=F32)
        h_t = st_ref[g]
        y_off = jnp.dot(cb, h_t.astype(BF16), preferred_element_type=F32)
        d_hi = dt_g.astype(BF16)
        r_hi = dt_g - d_hi.astype(F32)
        d_mid = r_hi.astype(BF16)
        d_lo = (r_hi - d_mid.astype(F32)).astype(BF16)
        dt_x = (jnp.dot(jnp.concatenate([d_hi, d_mid], axis=1), ex_ref[...], preferred_element_type=F32)
                + jnp.dot(d_lo, ex_ref[0:LANE, :], preferred_element_type=F32))
        xw_parts, dec_parts = [], []
        for p in range(pairs):
            j0, j1 = 2 * p, 2 * p + 1
            x2 = xbc_ref[0, pairs * g + p]
            c0 = cum_g[:, j0:j0 + 1]
            c1 = cum_g[:, j1:j1 + 1]
            l0 = jnp.exp2(jnp.where(causal, c0 - cum_t[j0:j0 + 1, :], -jnp.inf))
            l1 = jnp.exp2(jnp.where(causal, c1 - cum_t[j1:j1 + 1, :], -jnp.inf))
            m0 = (scores * l0).astype(BF16)
            m1 = (scores * l1).astype(BF16)
            dt2 = dt_x[:, p * LANE:(p + 1) * LANE]
            c2 = jnp.where(lo, c0, c1)
            xdt = x2 * dt2
            xdt_b = xdt.astype(BF16)
            zero = jnp.zeros_like(xdt_b)
            y_diag = jnp.dot(jnp.concatenate([m0, m1], axis=1),
                             jnp.concatenate([jnp.where(lo, xdt_b, zero), jnp.where(lo, zero, xdt_b)], axis=0),
                             preferred_element_type=F32)
            e2 = jnp.exp2(c2)
            y = y_diag + y_off[:, p * LANE:(p + 1) * LANE] * e2
            y_ref[0, pairs * g + p] = y + dexp_ref[pairs * g + p] * x2
            to_end = jnp.exp2(c2[last:last + 1, :] - c2)
            xw_parts.append((xdt * to_end).astype(BF16))
            dec_parts.append(e2[last:last + 1, :])
        xw = jnp.concatenate(xw_parts, axis=1)
        dec = jnp.concatenate(dec_parts, axis=1)
        upd = lax.dot_general(bb, xw, (((0,), (0,)), ((), ())), preferred_element_type=F32)
        st_ref[g] = h_t * dec + upd
        return carry

    lax.fori_loop(0, N_GROUPS, group, 0, unroll=2)

    if fuse_norm:
        sq = jnp.zeros((CHUNK, LANE), F32)
        for j in range(x_tiles):
            gj = (y_ref[0, j] + yo_ref[0, j]) * sz_ref[:, j * LANE:(j + 1) * LANE]
            y_ref[0, j] = gj
            sq = sq + gj * gj
        r = lax.rsqrt(jnp.sum(sq, axis=-1, keepdims=True) / (x_tiles * LANE) + EPS)
        for j in range(x_tiles):
            sl = slice(j * LANE, (j + 1) * LANE)
            o_ref[:, sl] = (y_ref[0, j] * r * nw_ref[:, sl]).astype(o_ref.dtype)


def _ssd(xbc_act, dt_ctx, dt_lat, par, dexp, reverse, norm_with=None):
    bsz, ntile, ltot, _ = xbc_act.shape
    l_ctx = dt_ctx.shape[1]
    l_lat = dt_lat.shape[1]
    n_ctx = l_ctx // CHUNK
    n_lat = l_lat // CHUNK
    steps = n_ctx + n_lat
    x_tiles = N_HEADS // 2
    gw = (N_HEADS // N_GROUPS) * HEAD_DIM
    e1 = (jnp.arange(gw)[None, :] // HEAD_DIM == jnp.arange(LANE)[:, None]).astype(BF16)
    expand = jnp.concatenate([e1, e1], axis=0)

    if reverse:
        def cat_chunk(i):
            return jnp.where(i < n_ctx, n_ctx - 1 - i, n_ctx + steps - 1 - i)

        def ctx_chunk(i):
            return jnp.maximum(n_ctx - 1 - i, 0)

        def lat_chunk(i):
            return jnp.minimum(steps - 1 - i, n_lat - 1)
    else:
        def cat_chunk(i):
            return i

        def ctx_chunk(i):
            return jnp.minimum(i, n_ctx - 1)

        def lat_chunk(i):
            return jnp.maximum(i - n_ctx, 0)

    y_spec = pl.BlockSpec((1, x_tiles, CHUNK, LANE), lambda b, i: (b, 0, lat_chunk(i), 0))
    in_specs = [pl.BlockSpec((1, ntile, CHUNK, LANE), lambda b, i: (b, 0, cat_chunk(i), 0)),
                pl.BlockSpec((1, CHUNK, LANE), lambda b, i: (b, ctx_chunk(i), 0)),
                pl.BlockSpec((1, CHUNK, LANE), lambda b, i: (b, lat_chunk(i), 0)),
                pl.BlockSpec((8, LANE), lambda b, i: (0, 0)),
                pl.BlockSpec((x_tiles, 1, LANE), lambda b, i: (0, 0, 0)),
                pl.BlockSpec((2 * LANE, gw), lambda b, i: (0, 0))]
    args = [xbc_act, dt_ctx, dt_lat, par, dexp, expand]
    scratch = [pltpu.VMEM((N_GROUPS, D_STATE, gw), F32), pltpu.VMEM((LANE, CHUNK), F32)]
    if norm_with is None:
        out_shape = jax.ShapeDtypeStruct((bsz, x_tiles, l_lat, LANE), F32)
        out_spec = y_spec
    else:
        y_other, silu_z, norm_w = norm_with
        dn = x_tiles * LANE
        row_spec = pl.BlockSpec((CHUNK, dn), lambda b, i: (b * n_lat + lat_chunk(i), 0))
        in_specs += [y_spec, row_spec, pl.BlockSpec((1, dn), lambda b, i: (0, 0))]
        args += [y_other, silu_z, norm_w.reshape(1, dn)]
        out_shape = jax.ShapeDtypeStruct((bsz * l_lat, dn), BF16)
        out_spec = row_spec
        scratch.append(pltpu.VMEM((1, x_tiles, CHUNK, LANE), F32))
    return pl.pallas_call(
        functools.partial(_ssd_kernel, reverse=reverse, n_ctx=n_ctx, fuse_norm=norm_with is not None),
        out_shape=out_shape,
        grid=(bsz, steps),
        in_specs=in_specs,
        out_specs=out_spec,
        scratch_shapes=scratch,
        compiler_params=_cparams(("parallel", "arbitrary")),
        name="ssd_bwd" if reverse else "ssd_fwd",
    )(*args)


def _conv31_kernel(u_ref, w_ref, b_ref, o_ref, pad_ref, *, seq):
    halo = (CF_KERNEL // 2) * GRID_W
    zeros = jnp.zeros((halo, LANE), F32)
    pad_ref[0:halo, :] = zeros
    pad_ref[halo + seq:halo + seq + halo, :] = zeros
    pad_ref[halo:halo + seq, :] = u_ref[0]
    bias = b_ref[...]

    def body(j, c):
        base = pl.multiple_of(j * CHUNK, CHUNK)
        acc = jnp.broadcast_to(bias, (CHUNK, LANE))
        for k in range(CF_KERNEL):
            tap = pad_ref[pl.ds(pl.multiple_of(base + k * GRID_W, GRID_W), CHUNK), :]
            acc = acc + tap * w_ref[k:k + 1, :]
        o_ref[0, pl.ds(base, CHUNK), :] = acc
        return c

    lax.fori_loop(0, seq // CHUNK, body, 0, unroll=2)


def _conv31(u3, w, b):
    bsz, s, c = u3.shape
    halo = (CF_KERNEL // 2) * GRID_W
    return pl.pallas_call(
        functools.partial(_conv31_kernel, seq=s),
        out_shape=jax.ShapeDtypeStruct((bsz, s, c), F32),
        grid=(bsz, c // LANE),
        in_specs=[pl.BlockSpec((1, s, LANE), lambda bi, ci: (bi, 0, ci)),
                  pl.BlockSpec((CF_KERNEL, LANE), lambda bi, ci: (0, ci)),
                  pl.BlockSpec((1, LANE), lambda bi, ci: (0, ci))],
        out_specs=pl.BlockSpec((1, s, LANE), lambda bi, ci: (bi, 0, ci)),
        scratch_shapes=[pltpu.VMEM((s + 2 * halo, LANE), F32)],
        compiler_params=_cparams(("parallel", "parallel")),
        name="conv31",
    )(u3, w, b.reshape(1, c))


def _route_kernel(x_ref, w_ref, sh_ref, sc_ref, rw_ref, rb_ref, h_ref, eid_ref, ew_ref):
    xf = x_ref[...]
    ms = jnp.mean(xf * xf, axis=-1, keepdims=True)
    h = xf * lax.rsqrt(ms + EPS) * w_ref[...]
    h = h * (1.0 + sc_ref[0]) + sh_ref[0]
    tm = xf.shape[0]
    nt = xf.shape[1] // LANE
    pitch = _pitch(nt)
    for j in range(nt):
        h_ref[pl.ds(j, tm, stride=pitch), :] = h[:, j * LANE:(j + 1) * LANE]
    for j in range(nt, pitch):
        h_ref[pl.ds(j, tm, stride=pitch), :] = jnp.zeros((tm, LANE), F32)
    logits = jnp.dot(h.astype(BF16), rw_ref[...], preferred_element_type=F32) + rb_ref[...]
    lane = lax.broadcasted_iota(jnp.int32, (tm, LANE), 1)
    lane_f = lane.astype(F32)
    ninf = -jnp.inf
    gl = jnp.where(lane < MOE_GROUPS, logits, ninf)
    gmax = jnp.max(gl, axis=-1, keepdims=True)
    gidx = jnp.min(jnp.where(gl == gmax, lane_f, float(LANE)), axis=-1, keepdims=True)
    gsum = jnp.sum(jnp.exp(gl - gmax), axis=-1, keepdims=True)
    g_p = 1.0 / gsum
    first = float(MOE_GROUPS) + gidx * float(EXPERTS_PER_GROUP)
    in_group = (lane_f >= first) & (lane_f < first + float(EXPERTS_PER_GROUP))
    el = jnp.where(in_group, logits, ninf)
    m1 = jnp.max(el, axis=-1, keepdims=True)
    i1 = jnp.min(jnp.where(el == m1, lane_f, float(LANE)), axis=-1, keepdims=True)
    el2 = jnp.where(lane_f == i1, ninf, el)
    m2 = jnp.max(el2, axis=-1, keepdims=True)
    i2 = jnp.min(jnp.where(el2 == m2, lane_f, float(LANE)), axis=-1, keepdims=True)
    e21 = jnp.exp(m2 - m1)
    den = 1.0 + e21
    w1 = (1.0 / den) * g_p
    w2 = (e21 / den) * g_p
    e1 = (i1 - float(MOE_GROUPS)).astype(jnp.int32)
    e2 = (i2 - float(MOE_GROUPS)).astype(jnp.int32)
    eid_ref[...] = jnp.where(lane == 0, e1, jnp.where(lane == 1, e2, 0))
    ew_ref[...] = jnp.where(lane == 0, w1, jnp.where(lane == 1, w2, 0.0))


def _route(x2, w, mod3, shift_chunk, scale_chunk, rows_per_batch, rw, rb, tm=256):
    m, d = x2.shape
    pitch = _pitch(d // LANE)
    tiles_per_batch = rows_per_batch // tm
    return pl.pallas_call(
        _route_kernel,
        out_shape=(jax.ShapeDtypeStruct((m * pitch, LANE), F32),
                   jax.ShapeDtypeStruct((m, LANE), jnp.int32),
                   jax.ShapeDtypeStruct((m, LANE), F32)),
        grid=(m // tm,),
        in_specs=[pl.BlockSpec((tm, d), lambda i: (i, 0)),
                  pl.BlockSpec((1, d), lambda i: (0, 0)),
                  pl.BlockSpec((1, 1, d), lambda i: (i // tiles_per_batch, 0, shift_chunk)),
                  pl.BlockSpec((1, 1, d), lambda i: (i // tiles_per_batch, 0, scale_chunk)),
                  pl.BlockSpec((d, LANE), lambda i: (0, 0)),
                  pl.BlockSpec((1, LANE), lambda i: (0, 0))],
        out_specs=(pl.BlockSpec((tm * pitch, LANE), lambda i: (i, 0)),
                   pl.BlockSpec((tm, LANE), lambda i: (i, 0)),
                   pl.BlockSpec((tm, LANE), lambda i: (i, 0))),
        compiler_params=_cparams(("parallel",)),
        name="route",
    )(x2, w.reshape(1, d), mod3, mod3, rw, rb)


_DMA_UNROLL = 8


def _rows_to_matrix(ref, tm, nt):
    return jnp.concatenate([ref[pl.ds(j, tm, stride=_pitch(nt)), :] for j in range(nt)], axis=1)


def _bulk_wait(src, dst, sem, total_rows):
    pltpu.make_async_copy(src.at[pl.ds(0, total_rows), :], dst.at[pl.ds(0, total_rows), :], sem).wait()


def _for_rows(n, body):
    groups = lax.shift_right_logical(n, _DMA_UNROLL.bit_length() - 1)

    def group(g, c):
        for u in range(_DMA_UNROLL):
            body(g * _DMA_UNROLL + u)
        return c

    def tail(r, c):
        body(r)
        return c

    lax.fori_loop(0, groups, group, 0)
    lax.fori_loop(groups * _DMA_UNROLL, n, tail, 0)


def _stream_expert_weights(b, be_ref, eord_ref, enext_ref, w_hbms, w_bufs, w_caches, wsem, both_queues):
    prev = jnp.maximum(b - 1, 0)

    def copies(e, slot):
        out = []
        for w, buf in zip(w_hbms, w_bufs):
            if both_queues:
                half = w.shape[1] // 2
                out.append((pltpu.make_async_copy(w.at[e, 0:half], buf.at[slot, 0:half], wsem.at[slot]), 1))
                out.append((pltpu.make_async_copy(w.at[e, half:], buf.at[slot, half:], wsem.at[slot]), 0))
            else:
                out.append((pltpu.make_async_copy(w.at[e], buf.at[slot], wsem.at[slot]), 1))
        return out

    @pl.when(b == 0)
    def _():
        for cp, prio in copies(be_ref[0], 0):
            cp.start(priority=prio)

    @pl.when((b == 0) | (be_ref[b] != be_ref[prev]))
    def _():
        for s in range(2):
            @pl.when((eord_ref[b] & 1) == s)
            def _(s=s):
                for cp, _ in copies(be_ref[b], s):
                    cp.wait()

                @pl.when(enext_ref[b] >= 0)
                def _():
                    for cp, prio in copies(enext_ref[b], 1 - s):
                        cp.start(priority=prio)

                for buf, cache in zip(w_bufs, w_caches):
                    cache[...] = buf[s].astype(BF16)


def _expert_up_kernel(be_ref, nused_ref, eord_ref, enext_ref, rowc_ref, rown_ref, h_hbm, wg_hbm, wu_hbm,
                      o_ref, xs0_ref, xs1_ref, wgs_ref, wus_ref, wgb_ref, wub_ref, sem, wsem, *, fchunk, nt):
    b = pl.program_id(0)
    n_used = nused_ref[0]
    dff = wgb_ref.shape[1]
    slots = (xs0_ref, xs1_ref)
    pitch = _pitch(nt)

    def gather_row(row_ref, r, slot):
        return pltpu.make_async_copy(h_hbm.at[pl.ds(row_ref[0, 0, r], nt), :],
                                     slots[slot].at[pl.ds(r * pitch, nt), :], sem.at[slot])

    @pl.when(b == 0)
    def _():
        _for_rows(MOE_BLOCK, lambda r: gather_row(rowc_ref, r, 0).start())

    for slot in range(2):
        @pl.when((b <= n_used) & (lax.rem(b, 2) == slot))
        def _(slot=slot):
            _bulk_wait(h_hbm, slots[slot], sem.at[slot], MOE_BLOCK * nt)

    @pl.when(b < n_used)
    def _():
        _stream_expert_weights(b, be_ref, eord_ref, enext_ref, (wg_hbm, wu_hbm), (wgs_ref, wus_ref),
                               (wgb_ref, wub_ref), wsem, both_queues=False)

        for slot in range(2):
            @pl.when(lax.rem(b, 2) == slot)
            def _(slot=slot):
                xb = _rows_to_matrix(slots[slot], MOE_BLOCK, nt).astype(BF16)
                nf = dff // fchunk
                per = MOE_BLOCK // (2 * nf)

                def request(part):
                    for r in range(part * per, (part + 1) * per):
                        gather_row(rown_ref, r, 1 - slot).start()

                for f in range(nf):
                    sl = slice(f * fchunk, (f + 1) * fchunk)
                    request(2 * f)
                    gate = jnp.dot(xb, wgb_ref[:, sl], preferred_element_type=F32)
                    request(2 * f + 1)
                    up = jnp.dot(xb, wub_ref[:, sl], preferred_element_type=F32)
                    o_ref[:, sl] = (_silu(gate) * up).astype(o_ref.dtype)

    @pl.when(b >= n_used)
    def _():
        o_ref[...] = jnp.zeros_like(o_ref)


def _expert_up(h2t, src_rows, w_gate, w_up, tables, fchunk=256):
    n_blocks = src_rows.shape[0]
    _, d, dff = w_gate.shape
    nt = d // LANE
    slot_rows = MOE_BLOCK * _pitch(nt)
    grid_spec = pltpu.PrefetchScalarGridSpec(
        num_scalar_prefetch=len(tables),
        grid=(n_blocks,),
        in_specs=[pl.BlockSpec((1, 1, MOE_BLOCK), lambda b, *_: (b, 0, 0), memory_space=pltpu.SMEM),
                  pl.BlockSpec((1, 1, MOE_BLOCK), lambda b, *_: (jnp.minimum(b + 1, n_blocks - 1), 0, 0),
                               memory_space=pltpu.SMEM),
                  pl.BlockSpec(memory_space=pl.ANY),
                  pl.BlockSpec(memory_space=pl.ANY),
                  pl.BlockSpec(memory_space=pl.ANY)],
        out_specs=pl.BlockSpec((MOE_BLOCK, dff), lambda b, *_: (b, 0)),
        scratch_shapes=[pltpu.VMEM((slot_rows, LANE), F32),
                        pltpu.VMEM((slot_rows, LANE), F32),
                        pltpu.VMEM((2, d, dff), F32),
                        pltpu.VMEM((2, d, dff), F32),
                        pltpu.VMEM((d, dff), BF16),
                        pltpu.VMEM((d, dff), BF16),
                        pltpu.SemaphoreType.DMA((2,)),
                        pltpu.SemaphoreType.DMA((2,))],
    )
    return pl.pallas_call(
        functools.partial(_expert_up_kernel, fchunk=fchunk, nt=nt),
        out_shape=jax.ShapeDtypeStruct((n_blocks * MOE_BLOCK, dff), BF16),
        grid_spec=grid_spec,
        compiler_params=_cparams(("arbitrary",)),
        name="expert_up",
    )(*tables, src_rows, src_rows, h2t, w_gate, w_up)


def _expert_down_kernel(be_ref, nused_ref, eord_ref, enext_ref, dst_ref, h_ref, wd_hbm, y_hbm,
                        ys0_ref, ys1_ref, wds_ref, wdb_ref, sem, wsem, *, nchunk, nt):
    b = pl.program_id(0)
    n_used = nused_ref[0]
    d = wdb_ref.shape[1]
    slots = (ys0_ref, ys1_ref)
    pitch = _pitch(nt)

    def scatter_row(r, slot):
        return pltpu.make_async_copy(slots[slot].at[pl.ds(r * pitch, pitch), :],
                                     y_hbm.at[pl.ds(dst_ref[0, 0, r], pitch), :], sem.at[slot])

    @pl.when(b == 0)
    def _():
        ys0_ref[...] = jnp.zeros_like(ys0_ref)
        ys1_ref[...] = jnp.zeros_like(ys1_ref)

    for slot in range(2):
        @pl.when((b >= 1) & (b <= n_used) & (lax.rem(b, 2) == slot))
        def _(slot=slot):
            _bulk_wait(slots[slot], y_hbm, sem.at[slot], MOE_BLOCK * pitch)

    @pl.when(b < n_used)
    def _():
        _stream_expert_weights(b, be_ref, eord_ref, enext_ref, (wd_hbm,), (wds_ref,), (wdb_ref,), wsem,
                               both_queues=True)
        hb = h_ref[...]
        for slot in range(2):
            @pl.when(lax.rem(b, 2) == slot)
            def _(slot=slot):
                nc = d // nchunk
                per = MOE_BLOCK // nc
                for c in range(nc):
                    for r in range(c * per, (c + 1) * per):
                        scatter_row(r, 1 - slot).start()
                    out = jnp.dot(hb, wdb_ref[:, c * nchunk:(c + 1) * nchunk], preferred_element_type=F32)
                    for j in range(nchunk // LANE):
                        slots[slot][pl.ds(c * (nchunk // LANE) + j, MOE_BLOCK, stride=pitch), :] = (
                            out[:, j * LANE:(j + 1) * LANE])

    for slot in range(2):
        @pl.when((b == n_used) & (lax.rem(b, 2) == slot))
        def _(slot=slot):
            _for_rows(MOE_BLOCK, lambda r: scatter_row(r, 1 - slot).start())
            _bulk_wait(slots[1 - slot], y_hbm, sem.at[1 - slot], MOE_BLOCK * pitch)


def _expert_down(hid, dst_rows, w_down, tables, y_slots, nchunk=256):
    n_rows, dff = hid.shape
    n_blocks = n_rows // MOE_BLOCK
    d = w_down.shape[2]
    nt = d // LANE
    pitch = _pitch(nt)
    grid_spec = pltpu.PrefetchScalarGridSpec(
        num_scalar_prefetch=len(tables),
        grid=(n_blocks,),
        in_specs=[pl.BlockSpec((1, 1, MOE_BLOCK), lambda b, *_: (b, 0, 0), memory_space=pltpu.SMEM),
                  pl.BlockSpec((MOE_BLOCK, dff), lambda b, be, n, *_: (jnp.minimum(b, n[0] - 1), 0)),
                  pl.BlockSpec(memory_space=pl.ANY)],
        out_specs=pl.BlockSpec(memory_space=pl.ANY),
        scratch_shapes=[pltpu.VMEM((MOE_BLOCK * pitch, LANE), F32),
                        pltpu.VMEM((MOE_BLOCK * pitch, LANE), F32),
                        pltpu.VMEM((2, dff, d), F32),
                        pltpu.VMEM((dff, d), BF16),
                        pltpu.SemaphoreType.DMA((2,)),
                        pltpu.SemaphoreType.DMA((2,))],
    )
    return pl.pallas_call(
        functools.partial(_expert_down_kernel, nchunk=nchunk, nt=nt),
        out_shape=jax.ShapeDtypeStruct((y_slots * pitch, LANE), F32),
        grid_spec=grid_spec,
        compiler_params=_cparams(("arbitrary",)),
        name="expert_down",
    )(*tables, dst_rows, hid, w_down)


def _combine_kernel(y0_ref, y1_ref, ew_ref, x_ref, g_ref, w_ref, o_ref):
    tm, d = x_ref.shape
    nt = d // LANE
    ew = ew_ref[...]
    moe = (_rows_to_matrix(y0_ref, tm, nt) * ew[:, 0:1]
           + _rows_to_matrix(y1_ref, tm, nt) * ew[:, 1:2])
    xo = x_ref[...] + g_ref[0] * moe
    ms = jnp.mean(xo * xo, axis=-1, keepdims=True)
    o_ref[...] = xo * lax.rsqrt(ms + EPS) * w_ref[...]


def _combine(y, ew, x2, mod3, gate_chunk, rows_per_batch, final_w, tm=256):
    m, d = x2.shape
    pitch = _pitch(d // LANE)
    tiles = m // tm
    tiles_per_batch = rows_per_batch // tm
    return pl.pallas_call(
        _combine_kernel,
        out_shape=jax.ShapeDtypeStruct((m, d), F32),
        grid=(tiles,),
        in_specs=[pl.BlockSpec((tm * pitch, LANE), lambda i: (i, 0)),
                  pl.BlockSpec((tm * pitch, LANE), lambda i: (tiles + i, 0)),
                  pl.BlockSpec((tm, LANE), lambda i: (i, 0)),
                  pl.BlockSpec((tm, d), lambda i: (i, 0)),
                  pl.BlockSpec((1, 1, d), lambda i: (i // tiles_per_batch, 0, gate_chunk)),
                  pl.BlockSpec((1, d), lambda i: (0, 0))],
        out_specs=pl.BlockSpec((tm, d), lambda i: (i, 0)),
        compiler_params=_cparams(("parallel",)),
        name="moe_combine",
    )(y, y, ew, x2, mod3, final_w.reshape(1, d))


def _dispatch_tables(eid, n_tok, pitch):
    top_k = eid.shape[1]
    n_assign = n_tok * top_k
    expert = eid.reshape(-1)
    key = jnp.sort(expert * n_assign + jnp.arange(n_assign, dtype=jnp.int32))
    sorted_assign = key % n_assign
    bounds = jnp.arange(N_EXPERTS + 1, dtype=jnp.int32) * n_assign
    start = jnp.searchsorted(key, bounds, side="left").astype(jnp.int32)
    counts = start[1:] - start[:-1]
    nblk = (counts + MOE_BLOCK - 1) // MOE_BLOCK
    blk_end = jnp.cumsum(nblk)
    blk_start = blk_end - nblk
    steps = -(-n_assign // MOE_BLOCK) + N_EXPERTS + 1
    bidx = jnp.arange(steps, dtype=jnp.int32)
    lane = jnp.arange(MOE_BLOCK, dtype=jnp.int32)[None, :]
    block_expert = jnp.minimum(jnp.searchsorted(blk_end, bidx, side="right"), N_EXPERTS - 1).astype(jnp.int32)
    in_expert = (bidx - blk_start[block_expert]) * MOE_BLOCK
    n_valid = jnp.clip(counts[block_expert] - in_expert, 0, MOE_BLOCK)
    src = start[block_expert][:, None] + in_expert[:, None] + lane
    valid = lane < n_valid[:, None]
    assign = sorted_assign[jnp.clip(src, 0, n_assign - 1)]
    tok = assign // top_k
    src_rows = jnp.where(valid, tok, (bidx[:, None] * MOE_BLOCK + lane) % n_tok) * pitch
    dst_slot = jnp.where(valid, (assign % top_k) * n_tok + tok, top_k * n_tok + lane)
    dst_rows = jnp.concatenate([top_k * n_tok + lane, dst_slot[:-1]], axis=0) * pitch
    n_used = blk_end[-1].astype(jnp.int32)
    first = jnp.concatenate([jnp.ones((1,), jnp.int32),
                             (block_expert[1:] != block_expert[:-1]).astype(jnp.int32)])
    expert_ordinal = (jnp.cumsum(first) - 1).astype(jnp.int32)
    next_blk = blk_end[block_expert]
    next_expert = jnp.where(next_blk < n_used, block_expert[jnp.minimum(next_blk, steps - 1)], -1).astype(jnp.int32)
    tables = (block_expert, n_used.reshape(1), expert_ordinal, next_expert)
    return (src_rows.astype(jnp.int32).reshape(steps, 1, MOE_BLOCK),
            dst_rows.astype(jnp.int32).reshape(steps, 1, MOE_BLOCK), tables)


def kernel(x, c, ctx, c_ctx, ada_w, ada_b, norm1_w, w_in, ssm_conv_w, ssm_conv_b, dt_bias, a_log, d_skip, ssm_norm_w, ssm_out_w, cf_dw_w, cf_dw_b, cf_ln_w, cf_ln_b, cf_out_w, cf_out_b, w_o, norm2_w, router_group_w, router_group_b, router_expert_w, router_expert_b, expert_w_gate, expert_w_up, expert_w_down, final_norm_w):
    bsz, seq, d = x.shape
    l_ctx = ctx.shape[1]
    n_tok = bsz * seq
    d_inner = ssm_norm_w.shape[1]
    gn = N_GROUPS * D_STATE
    xbc_dim = d_inner + 2 * gn
    off_dt = xbc_dim
    off_z = off_dt + N_HEADS
    off_glu = off_z + d_inner
    off_gate = off_glu + 2 * d

    ctx_row = bsz
    crows = jnp.zeros((8, d), F32).at[:bsz].set(c).at[ctx_row].set(c_ctx)
    mod = _ada(crows, ada_w[0], ada_b[0])
    mod3 = mod.reshape(8, 1, 6 * d)
    lat_rows = jnp.arange(bsz, dtype=jnp.int32)
    ctx_rows = jnp.full((bsz,), ctx_row, jnp.int32)

    h_lat = _normmod(x, norm1_w[0], mod3, lat_rows, 0, 1, BF16).reshape(n_tok, d)
    h_ctx = _normmod(ctx, norm1_w[0], mod3, ctx_rows, 0, 1, BF16).reshape(bsz * l_ctx, d)

    wt, (r_xbc, r_dt, r_z, r_glu, r_gate) = _pack_wt(
        jnp.transpose(w_in[0]),
        [(0, xbc_dim), (off_dt, off_z), (off_z, off_glu), (off_glu, off_gate), (off_gate, off_gate + 2 * d)])

    xbc_lat = _mm(h_lat, wt, tn=2048, name="in_xbc", rows=(r_xbc, xbc_dim)).reshape(bsz, seq, xbc_dim)
    xbc_ctx = _mm(h_ctx, wt, tm=512, name="in_xbc_ctx", rows=(r_xbc, xbc_dim)).reshape(bsz, l_ctx, xbc_dim)
    dt_lat = _mm(h_lat, wt, name="in_dt", rows=(r_dt, LANE)).reshape(bsz, seq, LANE)
    dt_ctx = _mm(h_ctx, wt, tm=512, name="in_dt_ctx", rows=(r_dt, LANE)).reshape(bsz, l_ctx, LANE)
    sz = _mm(h_lat, wt, act="silu", tn=2048, name="in_z", rows=(r_z, d_inner))
    u = _mm_glu(h_lat, wt, r_glu, d, tn=1024)

    xbc_act = _conv7(xbc_ctx, xbc_lat, ssm_conv_w[0], ssm_conv_b[0])

    def ssd_params(k):
        par = jnp.zeros((8, LANE), F32).at[0, :N_HEADS].set(dt_bias[0, k]).at[1, :N_HEADS].set(a_log[0, k])
        return par, jnp.repeat(d_skip[0, k], HEAD_DIM).reshape(N_HEADS // 2, 1, LANE)

    y_bwd = _ssd(xbc_act, dt_ctx, dt_lat, *ssd_params(1), reverse=True)
    gnorm = _ssd(xbc_act, dt_ctx, dt_lat, *ssd_params(0), reverse=False,
                 norm_with=(y_bwd, sz, ssm_norm_w[0]))
    y_ssd = _mm(gnorm, ssm_out_w[0].astype(BF16), tn=512, name="ssm_out")

    cv = _conv31(u.reshape(bsz, seq, d), cf_dw_w[0], cf_dw_b[0]).reshape(n_tok, d)
    merged = _mm_merge(cv, cf_ln_w[0], cf_ln_b[0], cf_out_w[0].astype(BF16), cf_out_b[0],
                       h_lat, wt, r_gate, y_ssd)
    x1 = _mm_resid(merged, w_o[0].astype(BF16), x.reshape(n_tok, d), mod3, 2, seq, tm=2048)

    n_r = MOE_GROUPS + N_EXPERTS
    rw = jnp.pad(jnp.concatenate([router_group_w[0], router_expert_w[0]], axis=1),
                 ((0, 0), (0, LANE - n_r))).astype(BF16)
    rb = jnp.pad(jnp.concatenate([router_group_b[0], router_expert_b[0]]), (0, LANE - n_r)).reshape(1, LANE)
    h2t, eid, ew = _route(x1, norm2_w[0], mod3, 3, 4, seq, rw, rb)

    src_rows, dst_rows, tables = _dispatch_tables(eid[:, :2], n_tok, _pitch(d // LANE))
    hid = _expert_up(h2t, src_rows, expert_w_gate[0], expert_w_up[0], tables)
    y = _expert_down(hid, dst_rows, expert_w_down[0], tables, 2 * n_tok + MOE_BLOCK)
    out = _combine(y, ew, x1, mod3, 5, seq, final_norm_w)
    return out.reshape(bsz, seq, d)
```

```python
import functools

import jax
import jax.numpy as jnp
from jax import lax
from jax.experimental import pallas as pl
from jax.experimental.pallas import tpu as pltpu

F32 = jnp.float32
BF16 = jnp.bfloat16

EPS = 1e-6
GRID_W = 64
HEAD_DIM = 64
N_HEADS = 64
N_GROUPS = 8
D_STATE = 128
CHUNK = 128
SSM_CONV = 7
CF_KERNEL = 31
MOE_GROUPS = 8
EXPERTS_PER_GROUP = 8
N_EXPERTS = 64
MOE_BLOCK = 256
LANE = 128
LOG2E = 1.4426950408889634
VMEM_LIMIT = 56 * 1024 * 1024


def _cparams(sem):
    return pltpu.CompilerParams(dimension_semantics=sem, vmem_limit_bytes=VMEM_LIMIT)


def _silu(v):
    return v * jax.nn.sigmoid(v)


def _pitch(nt):
    return nt + 1


def _ada_kernel(c_ref, w_ref, b_ref, o_ref):
    s = _silu(c_ref[...])
    o_ref[...] = jnp.dot(s.astype(BF16), w_ref[...].astype(BF16),
                         preferred_element_type=F32) + b_ref[...]


def _ada(crows, ada_w, ada_b, tn=1024):
    r, d = crows.shape
    n = ada_w.shape[1]
    return pl.pallas_call(
        _ada_kernel,
        out_shape=jax.ShapeDtypeStruct((r, n), F32),
        grid=(n // tn,),
        in_specs=[pl.BlockSpec((r, d), lambda j: (0, 0)),
                  pl.BlockSpec((d, tn), lambda j: (0, j)),
                  pl.BlockSpec((1, tn), lambda j: (0, j))],
        out_specs=pl.BlockSpec((r, tn), lambda j: (0, j)),
        compiler_params=_cparams(("parallel",)),
        name="ada",
    )(crows, ada_w, ada_b.reshape(1, n))


def _normmod_kernel(rows_ref, x_ref, w_ref, sh_ref, sc_ref, o_ref):
    del rows_ref
    xf = x_ref[0]
    ms = jnp.mean(xf * xf, axis=-1, keepdims=True)
    y = xf * lax.rsqrt(ms + EPS) * w_ref[...]
    o_ref[0] = (y * (1.0 + sc_ref[0]) + sh_ref[0]).astype(o_ref.dtype)


def _normmod(x3, w, mod3, rows, shift_chunk, scale_chunk, out_dtype, tm=256):
    bx, l, d = x3.shape
    grid_spec = pltpu.PrefetchScalarGridSpec(
        num_scalar_prefetch=1,
        grid=(bx, l // tm),
        in_specs=[pl.BlockSpec((1, tm, d), lambda b, i, r: (b, i, 0)),
                  pl.BlockSpec((1, d), lambda b, i, r: (0, 0)),
                  pl.BlockSpec((1, 1, d), lambda b, i, r: (r[b], 0, shift_chunk)),
                  pl.BlockSpec((1, 1, d), lambda b, i, r: (r[b], 0, scale_chunk))],
        out_specs=pl.BlockSpec((1, tm, d), lambda b, i, r: (b, i, 0)),
    )
    return pl.pallas_call(
        _normmod_kernel,
        out_shape=jax.ShapeDtypeStruct((bx, l, d), out_dtype),
        grid_spec=grid_spec,
        compiler_params=_cparams(("parallel", "parallel")),
        name="normmod",
    )(rows, x3, w.reshape(1, d), mod3, mod3)


def _dot_nt(a, wt):
    return lax.dot_general(a, wt, (((1,), (1,)), ((), ())), preferred_element_type=F32)


def _mm_kernel(a_ref, w_ref, *rest, act, has_bias, w_rows):
    o_ref = rest[-1]
    a = a_ref[...]
    acc = _dot_nt(a, w_ref[...]) if w_rows else jnp.dot(a, w_ref[...], preferred_element_type=F32)
    if has_bias:
        acc = acc + rest[0][...]
    if act == "silu":
        acc = _silu(acc)
    elif act == "sigmoid":
        acc = jax.nn.sigmoid(acc)
    o_ref[...] = acc.astype(o_ref.dtype)


def _mm(a, w, bias=None, act=None, out_dtype=F32, tm=1024, tn=1024, name="mm", rows=None):
    m, k = a.shape
    start, n = (0, w.shape[1]) if rows is None else rows
    tm, tn = min(tm, m), min(tn, n)
    j0 = start // tn
    w_spec = (pl.BlockSpec((k, tn), lambda i, j: (0, j)) if rows is None
              else pl.BlockSpec((tn, k), lambda i, j: (j0 + j, 0)))
    in_specs = [pl.BlockSpec((tm, k), lambda i, j: (i, 0)), w_spec]
    args = [a, w]
    if bias is not None:
        in_specs.append(pl.BlockSpec((1, tn), lambda i, j: (0, j)))
        args.append(bias.reshape(1, n))
    return pl.pallas_call(
        functools.partial(_mm_kernel, act=act, has_bias=bias is not None, w_rows=rows is not None),
        out_shape=jax.ShapeDtypeStruct((m, n), out_dtype),
        grid=(m // tm, n // tn),
        in_specs=in_specs,
        out_specs=pl.BlockSpec((tm, tn), lambda i, j: (i, j)),
        compiler_params=_cparams(("parallel", "parallel")),
        name=name,
    )(*args)


def _mm_glu_kernel(a_ref, wa_ref, wb_ref, o_ref):
    a = a_ref[...]
    va = _dot_nt(a, wa_ref[...])
    vb = _dot_nt(a, wb_ref[...])
    o_ref[...] = va * jax.nn.sigmoid(vb)


def _mm_glu(a, wt, start, n, tm=1024, tn=512):
    m, k = a.shape
    tm = min(tm, m)
    ja, jb = start // tn, (start + n) // tn
    return pl.pallas_call(
        _mm_glu_kernel,
        out_shape=jax.ShapeDtypeStruct((m, n), F32),
        grid=(m // tm, n // tn),
        in_specs=[pl.BlockSpec((tm, k), lambda i, j: (i, 0)),
                  pl.BlockSpec((tn, k), lambda i, j: (ja + j, 0)),
                  pl.BlockSpec((tn, k), lambda i, j: (jb + j, 0))],
        out_specs=pl.BlockSpec((tm, tn), lambda i, j: (i, j)),
        compiler_params=_cparams(("parallel", "parallel")),
        name="mm_glu",
    )(a, wt, wt)


def _mm_merge_kernel(cv_ref, lw_ref, lb_ref, w_ref, b_ref, h_ref, wga_ref, wgb_ref, ys_ref, o_ref, u_ref):
    @pl.when(pl.program_id(1) == 0)
    def _():
        xf = cv_ref[...]
        mu = jnp.mean(xf, axis=-1, keepdims=True)
        xc = xf - mu
        var = jnp.mean(xc * xc, axis=-1, keepdims=True)
        y = xc * lax.rsqrt(var + EPS) * lw_ref[...] + lb_ref[...]
        u_ref[...] = _silu(y).astype(u_ref.dtype)

    ycf = jnp.dot(u_ref[...], w_ref[...], preferred_element_type=F32) + b_ref[...]
    h = h_ref[...]
    gate_a = jax.nn.sigmoid(_dot_nt(h, wga_ref[...]))
    gate_b = jax.nn.sigmoid(_dot_nt(h, wgb_ref[...]))
    o_ref[...] = (gate_a * ys_ref[...] + gate_b * ycf).astype(o_ref.dtype)


def _mm_merge(cv, ln_w, ln_b, w, bias, h, wt, gate_start, y_ssd, tm=512, tn=512):
    m, k = cv.shape
    n = w.shape[1]
    tm = min(tm, m)
    ja, jb = gate_start // tn, (gate_start + n) // tn
    return pl.pallas_call(
        _mm_merge_kernel,
        out_shape=jax.ShapeDtypeStruct((m, n), BF16),
        grid=(m // tm, n // tn),
        in_specs=[pl.BlockSpec((tm, k), lambda i, j: (i, 0)),
                  pl.BlockSpec((1, k), lambda i, j: (0, 0)),
                  pl.BlockSpec((1, k), lambda i, j: (0, 0)),
                  pl.BlockSpec((k, tn), lambda i, j: (0, j)),
                  pl.BlockSpec((1, tn), lambda i, j: (0, j)),
                  pl.BlockSpec((tm, k), lambda i, j: (i, 0)),
                  pl.BlockSpec((tn, k), lambda i, j: (ja + j, 0)),
                  pl.BlockSpec((tn, k), lambda i, j: (jb + j, 0)),
                  pl.BlockSpec((tm, tn), lambda i, j: (i, j))],
        out_specs=pl.BlockSpec((tm, tn), lambda i, j: (i, j)),
        scratch_shapes=[pltpu.VMEM((tm, k), BF16)],
        compiler_params=_cparams(("parallel", "arbitrary")),
        name="mm_merge",
    )(cv, ln_w.reshape(1, k), ln_b.reshape(1, k), w, bias.reshape(1, n), h, wt, wt, y_ssd)


def _mm_resid_kernel(a_ref, w_ref, x_ref, g_ref, o_ref):
    out = jnp.dot(a_ref[...], w_ref[...], preferred_element_type=F32)
    o_ref[...] = x_ref[...] + g_ref[0] * out


def _mm_resid(a, w, x2, mod3, gate_chunk, rows_per_batch, tm=1024, tn=512):
    m, k = a.shape
    n = w.shape[1]
    tm = min(tm, rows_per_batch)
    nj = n // tn
    tiles_per_batch = rows_per_batch // tm
    return pl.pallas_call(
        _mm_resid_kernel,
        out_shape=jax.ShapeDtypeStruct((m, n), F32),
        grid=(m // tm, nj),
        in_specs=[pl.BlockSpec((tm, k), lambda i, j: (i, 0)),
                  pl.BlockSpec((k, tn), lambda i, j: (0, j)),
                  pl.BlockSpec((tm, tn), lambda i, j: (i, j)),
                  pl.BlockSpec((1, 1, tn),
                               lambda i, j: (i // tiles_per_batch, 0, gate_chunk * nj + j))],
        out_specs=pl.BlockSpec((tm, tn), lambda i, j: (i, j)),
        compiler_params=_cparams(("parallel", "parallel")),
        name="mm_resid",
    )(a, w, x2, mod3)


W_ALIGN = 2048


def _pack_wt_kernel(valid_ref, off_ref, w_ref, o_ref):
    del off_ref
    nrow = valid_ref[pl.program_id(0)]
    row = lax.broadcasted_iota(jnp.int32, o_ref.shape, 0)
    o_ref[...] = jnp.where(row < nrow, w_ref[...], 0.0).astype(o_ref.dtype)


def _pack_wt(wt, segments, tr=512):
    n, k = wt.shape
    starts, src_off, valid = [], [], []
    pos = 0
    for lo, hi in segments:
        pos = -(-pos // W_ALIGN) * W_ALIGN
        starts.append(pos)
        while len(src_off) < pos // tr:
            src_off.append(0)
            valid.append(0)
        for r in range(lo, hi, tr):
            src_off.append(min(r, n - tr))
            valid.append(min(tr, hi - r))
            assert r <= n - tr or hi - r == tr
        pos += -(-(hi - lo) // tr) * tr
    total = -(-pos // W_ALIGN) * W_ALIGN
    while len(src_off) < total // tr:
        src_off.append(0)
        valid.append(0)
    grid_spec = pltpu.PrefetchScalarGridSpec(
        num_scalar_prefetch=2,
        grid=(total // tr,),
        in_specs=[pl.BlockSpec((pl.Element(tr), pl.Element(k)), lambda t, v, off: (off[t] * 8, 0))],
        out_specs=pl.BlockSpec((tr, k), lambda t, v, off: (t, 0)),
    )
    packed = pl.pallas_call(
        _pack_wt_kernel,
        out_shape=jax.ShapeDtypeStruct((total, k), BF16),
        grid_spec=grid_spec,
        compiler_params=_cparams(("parallel",)),
        name="pack_wt",
    )(jnp.asarray(valid, jnp.int32), jnp.asarray(src_off, jnp.int32) // 8, wt)
    return packed, starts


_CONV_PAD = 8


def _conv7_kernel(ctx_ref, lat_ref, w_ref, b_ref, o_ref, pad_ref, *, l_ctx, l_lat):
    p = _CONV_PAD
    zeros = jnp.zeros((p, LANE), F32)
    off_ctx = p
    off_lat = 2 * p + l_ctx
    pad_ref[0:p, :] = zeros
    pad_ref[off_ctx + l_ctx:off_lat, :] = zeros
    pad_ref[off_lat + l_lat:off_lat + l_lat + p, :] = zeros
    pad_ref[off_ctx:off_ctx + l_ctx, :] = ctx_ref[0]
    pad_ref[off_lat:off_lat + l_lat, :] = lat_ref[0]
    reach = SSM_CONV // 2
    bias = b_ref[...]

    def chunk(pad_base, out_base):
        acc = jnp.broadcast_to(bias, (CHUNK, LANE))
        for k in range(SSM_CONV):
            tap = pad_ref[pl.ds(pad_base - reach + k, CHUNK), :]
            acc = acc + tap * w_ref[k:k + 1, :]
        o_ref[0, 0, pl.ds(out_base, CHUNK), :] = _silu(acc)

    def ctx_body(j, c):
        base = pl.multiple_of(j * CHUNK, CHUNK)
        chunk(off_ctx + base, base)
        return c

    def lat_body(j, c):
        base = pl.multiple_of(j * CHUNK, CHUNK)
        chunk(off_lat + base, l_ctx + base)
        return c

    lax.fori_loop(0, l_ctx // CHUNK, ctx_body, 0)
    lax.fori_loop(0, l_lat // CHUNK, lat_body, 0, unroll=2)


def _conv7(ctx_raw, lat_raw, w, b):
    bsz, l_ctx, c = ctx_raw.shape
    l_lat = lat_raw.shape[1]
    ltot = l_ctx + l_lat
    nct = c // LANE
    return pl.pallas_call(
        functools.partial(_conv7_kernel, l_ctx=l_ctx, l_lat=l_lat),
        out_shape=jax.ShapeDtypeStruct((bsz, nct, ltot, LANE), F32),
        grid=(bsz, nct),
        in_specs=[pl.BlockSpec((1, l_ctx, LANE), lambda bi, ci: (bi, 0, ci)),
                  pl.BlockSpec((1, l_lat, LANE), lambda bi, ci: (bi, 0, ci)),
                  pl.BlockSpec((SSM_CONV, LANE), lambda bi, ci: (0, ci)),
                  pl.BlockSpec((1, LANE), lambda bi, ci: (0, ci))],
        out_specs=pl.BlockSpec((1, 1, ltot, LANE), lambda bi, ci: (bi, ci, 0, 0)),
        scratch_shapes=[pltpu.VMEM((ltot + 3 * _CONV_PAD, LANE), F32)],
        compiler_params=_cparams(("parallel", "parallel")),
        name="conv7",
    )(ctx_raw, lat_raw, w, b.reshape(1, c))


def _ssd_kernel(xbc_ref, dtc_ref, dtl_ref, par_ref, dexp_ref, ex_ref, *rest, reverse, n_ctx, fuse_norm):
    if fuse_norm:
        yo_ref, sz_ref, nw_ref, o_ref, st_ref, cumt_ref, y_ref = rest
    else:
        y_ref, st_ref, cumt_ref = rest
    i = pl.program_id(1)

    @pl.when(i == 0)
    def _():
        st_ref[...] = jnp.zeros_like(st_ref)

    dt_raw = jnp.where(i < n_ctx, dtc_ref[0], dtl_ref[0])
    bias = par_ref[0:1, :]
    a = -jnp.exp(par_ref[1:2, :])
    dt = jax.nn.softplus(dt_raw + bias)
    cum = dt * a
    row = lax.broadcasted_iota(jnp.int32, (CHUNK, LANE), 0)
    k = 1
    while k < CHUNK:
        if reverse:
            cum = cum + jnp.where(row < CHUNK - k, pltpu.roll(cum, CHUNK - k, 0), 0.0)
        else:
            cum = cum + jnp.where(row >= k, pltpu.roll(cum, k, 0), 0.0)
        k *= 2
    last = 0 if reverse else CHUNK - 1
    cum = cum * LOG2E
    cumt_ref[...] = cum.T
    li = lax.broadcasted_iota(jnp.int32, (CHUNK, CHUNK), 0)
    si = lax.broadcasted_iota(jnp.int32, (CHUNK, CHUNK), 1)
    causal = (li <= si) if reverse else (li >= si)
    lo = lax.broadcasted_iota(jnp.int32, (CHUNK, LANE), 1) < HEAD_DIM
    heads_per_group = N_HEADS // N_GROUPS
    pairs = heads_per_group // 2
    x_tiles = N_HEADS // 2

    def group(g, carry):
        shift = (LANE - heads_per_group * g) & (LANE - 1)
        cum_g = pltpu.roll(cum, shift, 1)
        dt_g = pltpu.roll(dt, shift, 1)
        cum_t = cumt_ref[pl.ds(pl.multiple_of(heads_per_group * g, heads_per_group), heads_per_group), :]
        bb = xbc_ref[0, x_tiles + g].astype(BF16)
        cb = xbc_ref[0, x_tiles + N_GROUPS + g].astype(BF16)
        scores = lax.dot_general(cb, bb, (((1,), (1,)), ((), ())), preferred_element_type=F32)
        h_t = st_ref[g]
        y_off = jnp.dot(cb, h_t.astype(BF16), preferred_element_type=F32)
        d_hi = dt_g.astype(BF16)
        r_hi = dt_g - d_hi.astype(F32)
        d_mid = r_hi.astype(BF16)
        d_lo = (r_hi - d_mid.astype(F32)).astype(BF16)
        dt_x = (jnp.dot(jnp.concatenate([d_hi, d_mid], axis=1), ex_ref[...], preferred_element_type=F32)
                + jnp.dot(d_lo, ex_ref[0:LANE, :], preferred_element_type=F32))
        xw_parts, dec_parts = [], []
        for p in range(pairs):
            j0, j1 = 2 * p, 2 * p + 1
            x2 = xbc_ref[0, pairs * g + p]
            c0 = cum_g[:, j0:j0 + 1]
            c1 = cum_g[:, j1:j1 + 1]
            l0 = jnp.exp2(jnp.where(causal, c0 - cum_t[j0:j0 + 1, :], -jnp.inf))
            l1 = jnp.exp2(jnp.where(causal, c1 - cum_t[j1:j1 + 1, :], -jnp.inf))
            m0 = (scores * l0).astype(BF16)
            m1 = (scores * l1).astype(BF16)
            dt2 = dt_x[:, p * LANE:(p + 1) * LANE]
            c2 = jnp.where(lo, c0, c1)
            xdt = x2 * dt2
            xdt_b = xdt.astype(BF16)
            zero = jnp.zeros_like(xdt_b)
            y_diag = jnp.dot(jnp.concatenate([m0, m1], axis=1),
                             jnp.concatenate([jnp.where(lo, xdt_b, zero), jnp.where(lo, zero, xdt_b)], axis=0),
                             preferred_element_type=F32)
            e2 = jnp.exp2(c2)
            y = y_diag + y_off[:, p * LANE:(p + 1) * LANE] * e2
            y_ref[0, pairs * g + p] = y + dexp_ref[pairs * g + p] * x2
            to_end = jnp.exp2(c2[last:last + 1, :] - c2)
            xw_parts.append((xdt * to_end).astype(BF16))
            dec_parts.append(e2[last:last + 1, :])
        xw = jnp.concatenate(xw_parts, axis=1)
        dec = jnp.concatenate(dec_parts, axis=1)
        upd = lax.dot_general(bb, xw, (((0,), (0,)), ((), ())), preferred_element_type=F32)
        st_ref[g] = h_t * dec + upd
        return carry

    lax.fori_loop(0, N_GROUPS, group, 0, unroll=2)

    if fuse_norm:
        sq = jnp.zeros((CHUNK, LANE), F32)
        for j in range(x_tiles):
            gj = (y_ref[0, j] + yo_ref[0, j]) * sz_ref[:, j * LANE:(j + 1) * LANE]
            y_ref[0, j] = gj
            sq = sq + gj * gj
        r = lax.rsqrt(jnp.sum(sq, axis=-1, keepdims=True) / (x_tiles * LANE) + EPS)
        for j in range(x_tiles):
            sl = slice(j * LANE, (j + 1) * LANE)
            o_ref[:, sl] = (y_ref[0, j] * r * nw_ref[:, sl]).astype(o_ref.dtype)


def _ssd(xbc_act, dt_ctx, dt_lat, par, dexp, reverse, norm_with=None):
    bsz, ntile, ltot, _ = xbc_act.shape
    l_ctx = dt_ctx.shape[1]
    l_lat = dt_lat.shape[1]
    n_ctx = l_ctx // CHUNK
    n_lat = l_lat // CHUNK
    steps = n_ctx + n_lat
    x_tiles = N_HEADS // 2
    gw = (N_HEADS // N_GROUPS) * HEAD_DIM
    e1 = (jnp.arange(gw)[None, :] // HEAD_DIM == jnp.arange(LANE)[:, None]).astype(BF16)
    expand = jnp.concatenate([e1, e1], axis=0)

    if reverse:
        def cat_chunk(i):
            return jnp.where(i < n_ctx, n_ctx - 1 - i, n_ctx + steps - 1 - i)

        def ctx_chunk(i):
            return jnp.maximum(n_ctx - 1 - i, 0)

        def lat_chunk(i):
            return jnp.minimum(steps - 1 - i, n_lat - 1)
    else:
        def cat_chunk(i):
            return i

        def ctx_chunk(i):
            return jnp.minimum(i, n_ctx - 1)

        def lat_chunk(i):
            return jnp.maximum(i - n_ctx, 0)

    y_spec = pl.BlockSpec((1, x_tiles, CHUNK, LANE), lambda b, i: (b, 0, lat_chunk(i), 0))
    in_specs = [pl.BlockSpec((1, ntile, CHUNK, LANE), lambda b, i: (b, 0, cat_chunk(i), 0)),
                pl.BlockSpec((1, CHUNK, LANE), lambda b, i: (b, ctx_chunk(i), 0)),
                pl.BlockSpec((1, CHUNK, LANE), lambda b, i: (b, lat_chunk(i), 0)),
                pl.BlockSpec((8, LANE), lambda b, i: (0, 0)),
                pl.BlockSpec((x_tiles, 1, LANE), lambda b, i: (0, 0, 0)),
                pl.BlockSpec((2 * LANE, gw), lambda b, i: (0, 0))]
    args = [xbc_act, dt_ctx, dt_lat, par, dexp, expand]
    scratch = [pltpu.VMEM((N_GROUPS, D_STATE, gw), F32), pltpu.VMEM((LANE, CHUNK), F32)]
    if norm_with is None:
        out_shape = jax.ShapeDtypeStruct((bsz, x_tiles, l_lat, LANE), F32)
        out_spec = y_spec
    else:
        y_other, silu_z, norm_w = norm_with
        dn = x_tiles * LANE
        row_spec = pl.BlockSpec((CHUNK, dn), lambda b, i: (b * n_lat + lat_chunk(i), 0))
        in_specs += [y_spec, row_spec, pl.BlockSpec((1, dn), lambda b, i: (0, 0))]
        args += [y_other, silu_z, norm_w.reshape(1, dn)]
        out_shape = jax.ShapeDtypeStruct((bsz * l_lat, dn), BF16)
        out_spec = row_spec
        scratch.append(pltpu.VMEM((1, x_tiles, CHUNK, LANE), F32))
    return pl.pallas_call(
        functools.partial(_ssd_kernel, reverse=reverse, n_ctx=n_ctx, fuse_norm=norm_with is not None),
        out_shape=out_shape,
        grid=(bsz, steps),
        in_specs=in_specs,
        out_specs=out_spec,
        scratch_shapes=scratch,
        compiler_params=_cparams(("parallel", "arbitrary")),
        name="ssd_bwd" if reverse else "ssd_fwd",
    )(*args)


def _conv31_kernel(u_ref, w_ref, b_ref, o_ref, pad_ref, *, seq):
    halo = (CF_KERNEL // 2) * GRID_W
    zeros = jnp.zeros((halo, LANE), F32)
    pad_ref[0:halo, :] = zeros
    pad_ref[halo + seq:halo + seq + halo, :] = zeros
    pad_ref[halo:halo + seq, :] = u_ref[0]
    bias = b_ref[...]

    def body(j, c):
        base = pl.multiple_of(j * CHUNK, CHUNK)
        acc = jnp.broadcast_to(bias, (CHUNK, LANE))
        for k in range(CF_KERNEL):
            tap = pad_ref[pl.ds(pl.multiple_of(base + k * GRID_W, GRID_W), CHUNK), :]
            acc = acc + tap * w_ref[k:k + 1, :]
        o_ref[0, pl.ds(base, CHUNK), :] = acc
        return c

    lax.fori_loop(0, seq // CHUNK, body, 0, unroll=2)


def _conv31(u3, w, b):
    bsz, s, c = u3.shape
    halo = (CF_KERNEL // 2) * GRID_W
    return pl.pallas_call(
        functools.partial(_conv31_kernel, seq=s),
        out_shape=jax.ShapeDtypeStruct((bsz, s, c), F32),
        grid=(bsz, c // LANE),
        in_specs=[pl.BlockSpec((1, s, LANE), lambda bi, ci: (bi, 0, ci)),
                  pl.BlockSpec((CF_KERNEL, LANE), lambda bi, ci: (0, ci)),
                  pl.BlockSpec((1, LANE), lambda bi, ci: (0, ci))],
        out_specs=pl.BlockSpec((1, s, LANE), lambda bi, ci: (bi, 0, ci)),
        scratch_shapes=[pltpu.VMEM((s + 2 * halo, LANE), F32)],
        compiler_params=_cparams(("parallel", "parallel")),
        name="conv31",
    )(u3, w, b.reshape(1, c))


def _route_kernel(x_ref, w_ref, sh_ref, sc_ref, rw_ref, rb_ref, h_ref, eid_ref, ew_ref):
    xf = x_ref[...]
    ms = jnp.mean(xf * xf, axis=-1, keepdims=True)
    h = xf * lax.rsqrt(ms + EPS) * w_ref[...]
    h = h * (1.0 + sc_ref[0]) + sh_ref[0]
    tm = xf.shape[0]
    nt = xf.shape[1] // LANE
    pitch = _pitch(nt)
    for j in range(nt):
        h_ref[pl.ds(j, tm, stride=pitch), :] = h[:, j * LANE:(j + 1) * LANE]
    for j in range(nt, pitch):
        h_ref[pl.ds(j, tm, stride=pitch), :] = jnp.zeros((tm, LANE), F32)
    logits = jnp.dot(h.astype(BF16), rw_ref[...], preferred_element_type=F32) + rb_ref[...]
    lane = lax.broadcasted_iota(jnp.int32, (tm, LANE), 1)
    lane_f = lane.astype(F32)
    ninf = -jnp.inf
    gl = jnp.where(lane < MOE_GROUPS, logits, ninf)
    gmax = jnp.max(gl, axis=-1, keepdims=True)
    gidx = jnp.min(jnp.where(gl == gmax, lane_f, float(LANE)), axis=-1, keepdims=True)
    gsum = jnp.sum(jnp.exp(gl - gmax), axis=-1, keepdims=True)
    g_p = 1.0 / gsum
    first = float(MOE_GROUPS) + gidx * float(EXPERTS_PER_GROUP)
    in_group = (lane_f >= first) & (lane_f < first + float(EXPERTS_PER_GROUP))
    el = jnp.where(in_group, logits, ninf)
    m1 = jnp.max(el, axis=-1, keepdims=True)
    i1 = jnp.min(jnp.where(el == m1, lane_f, float(LANE)), axis=-1, keepdims=True)
    el2 = jnp.where(lane_f == i1, ninf, el)
    m2 = jnp.max(el2, axis=-1, keepdims=True)
    i2 = jnp.min(jnp.where(el2 == m2, lane_f, float(LANE)), axis=-1, keepdims=True)
    e21 = jnp.exp(m2 - m1)
    den = 1.0 + e21
    w1 = (1.0 / den) * g_p
    w2 = (e21 / den) * g_p
    e1 = (i1 - float(MOE_GROUPS)).astype(jnp.int32)
    e2 = (i2 - float(MOE_GROUPS)).astype(jnp.int32)
    eid_ref[...] = jnp.where(lane == 0, e1, jnp.where(lane == 1, e2, 0))
    ew_ref[...] = jnp.where(lane == 0, w1, jnp.where(lane == 1, w2, 0.0))


def _route(x2, w, mod3, shift_chunk, scale_chunk, rows_per_batch, rw, rb, tm=256):
    m, d = x2.shape
    pitch = _pitch(d // LANE)
    tiles_per_batch = rows_per_batch // tm
    return pl.pallas_call(
        _route_kernel,
        out_shape=(jax.ShapeDtypeStruct((m * pitch, LANE), F32),
                   jax.ShapeDtypeStruct((m, LANE), jnp.int32),
                   jax.ShapeDtypeStruct((m, LANE), F32)),
        grid=(m // tm,),
        in_specs=[pl.BlockSpec((tm, d), lambda i: (i, 0)),
                  pl.BlockSpec((1, d), lambda i: (0, 0)),
                  pl.BlockSpec((1, 1, d), lambda i: (i // tiles_per_batch, 0, shift_chunk)),
                  pl.BlockSpec((1, 1, d), lambda i: (i // tiles_per_batch, 0, scale_chunk)),
                  pl.BlockSpec((d, LANE), lambda i: (0, 0)),
                  pl.BlockSpec((1, LANE), lambda i: (0, 0))],
        out_specs=(pl.BlockSpec((tm * pitch, LANE), lambda i: (i, 0)),
                   pl.BlockSpec((tm, LANE), lambda i: (i, 0)),
                   pl.BlockSpec((tm, LANE), lambda i: (i, 0))),
        compiler_params=_cparams(("parallel",)),
        name="route",
    )(x2, w.reshape(1, d), mod3, mod3, rw, rb)


_DMA_UNROLL = 8


def _rows_to_matrix(ref, tm, nt):
    return jnp.concatenate([ref[pl.ds(j, tm, stride=_pitch(nt)), :] for j in range(nt)], axis=1)


def _bulk_wait(src, dst, sem, total_rows):
    pltpu.make_async_copy(src.at[pl.ds(0, total_rows), :], dst.at[pl.ds(0, total_rows), :], sem).wait()


def _for_rows(n, body):
    groups = lax.shift_right_logical(n, _DMA_UNROLL.bit_length() - 1)

    def group(g, c):
        for u in range(_DMA_UNROLL):
            body(g * _DMA_UNROLL + u)
        return c

    def tail(r, c):
        body(r)
        return c

    lax.fori_loop(0, groups, group, 0)
    lax.fori_loop(groups * _DMA_UNROLL, n, tail, 0)


def _stream_expert_weights(b, be_ref, eord_ref, enext_ref, w_hbms, w_bufs, w_caches, wsem, both_queues,
                           start_next=True):
    prev = jnp.maximum(b - 1, 0)

    def copies(e, slot):
        out = []
        for w, buf in zip(w_hbms, w_bufs):
            if both_queues:
                half = w.shape[1] // 2
                out.append((pltpu.make_async_copy(w.at[e, 0:half], buf.at[slot, 0:half], wsem.at[slot]), 1))
                out.append((pltpu.make_async_copy(w.at[e, half:], buf.at[slot, half:], wsem.at[slot]), 0))
            else:
                out.append((pltpu.make_async_copy(w.at[e], buf.at[slot], wsem.at[slot]), 1))
        return out

    @pl.when(b == 0)
    def _():
        for cp, prio in copies(be_ref[0], 0):
            cp.start(priority=prio)

    @pl.when((b == 0) | (be_ref[b] != be_ref[prev]))
    def _():
        for s in range(2):
            @pl.when((eord_ref[b] & 1) == s)
            def _(s=s):
                for cp, _ in copies(be_ref[b], s):
                    cp.wait()

                if start_next:
                    @pl.when(enext_ref[b] >= 0)
                    def _():
                        for cp, prio in copies(enext_ref[b], 1 - s):
                            cp.start(priority=prio)

                for buf, cache in zip(w_bufs, w_caches):
                    cache[...] = buf[s].astype(BF16)


def _expert_up_kernel(be_ref, nused_ref, eord_ref, enext_ref, rowc_ref, rown_ref, h_hbm, wg_hbm, wu_hbm,
                      o_ref, xs0_ref, xs1_ref, wgs_ref, wus_ref, wgb_ref, wub_ref, sem, wsem, *, fchunk, nt):
    b = pl.program_id(0)
    n_used = nused_ref[0]
    dff = wgb_ref.shape[1]
    slots = (xs0_ref, xs1_ref)
    pitch = _pitch(nt)

    def gather_row(row_ref, r, slot):
        return pltpu.make_async_copy(h_hbm.at[pl.ds(row_ref[0, 0, r], nt), :],
                                     slots[slot].at[pl.ds(r * pitch, nt), :], sem.at[slot])

    @pl.when(b == 0)
    def _():
        _for_rows(MOE_BLOCK, lambda r: gather_row(rowc_ref, r, 0).start())

    for slot in range(2):
        @pl.when((b <= n_used) & (lax.rem(b, 2) == slot))
        def _(slot=slot):
            _bulk_wait(h_hbm, slots[slot], sem.at[slot], MOE_BLOCK * nt)

    @pl.when(b < n_used)
    def _():
        _stream_expert_weights(b, be_ref, eord_ref, enext_ref, (wg_hbm, wu_hbm), (wgs_ref, wus_ref),
                               (wgb_ref, wub_ref), wsem, both_queues=False, start_next=False)
        prev = jnp.maximum(b - 1, 0)
        fetch_next = ((b == 0) | (be_ref[b] != be_ref[prev])) & (enext_ref[b] >= 0)
        nf = dff // fchunk
        parts = 2 * nf
        per = MOE_BLOCK // parts
        wrows = wus_ref.shape[1] // parts

        def compute(slot, fetch):
            xb = _rows_to_matrix(slots[slot], MOE_BLOCK, nt).astype(BF16)
            e_next = enext_ref[b]
            w_slot = 1 - (eord_ref[b] & 1)

            def request(part):
                for r in range(part * per, (part + 1) * per):
                    gather_row(rown_ref, r, 1 - slot).start()
                if fetch:
                    if part == 0:
                        pltpu.make_async_copy(wg_hbm.at[e_next], wgs_ref.at[w_slot],
                                              wsem.at[w_slot]).start(priority=1)
                    rows = pl.ds(part * wrows, wrows)
                    pltpu.make_async_copy(wu_hbm.at[e_next, rows], wus_ref.at[w_slot, rows],
                                          wsem.at[w_slot]).start()

            for f in range(nf):
                sl = slice(f * fchunk, (f + 1) * fchunk)
                request(2 * f)
                gate = jnp.dot(xb, wgb_ref[:, sl], preferred_element_type=F32)
                request(2 * f + 1)
                up = jnp.dot(xb, wub_ref[:, sl], preferred_element_type=F32)
                o_ref[:, sl] = (_silu(gate) * up).astype(o_ref.dtype)

        for slot in range(2):
            for fetch in (False, True):
                @pl.when((lax.rem(b, 2) == slot) & (fetch_next == fetch))
                def _(slot=slot, fetch=fetch):
                    compute(slot, fetch)

    @pl.when(b >= n_used)
    def _():
        o_ref[...] = jnp.zeros_like(o_ref)


def _expert_up(h2t, src_rows, w_gate, w_up, tables, fchunk=256):
    n_blocks = src_rows.shape[0]
    _, d, dff = w_gate.shape
    nt = d // LANE
    slot_rows = MOE_BLOCK * _pitch(nt)
    grid_spec = pltpu.PrefetchScalarGridSpec(
        num_scalar_prefetch=len(tables),
        grid=(n_blocks,),
        in_specs=[pl.BlockSpec((1, 1, MOE_BLOCK), lambda b, *_: (b, 0, 0), memory_space=pltpu.SMEM),
                  pl.BlockSpec((1, 1, MOE_BLOCK), lambda b, *_: (jnp.minimum(b + 1, n_blocks - 1), 0, 0),
                               memory_space=pltpu.SMEM),
                  pl.BlockSpec(memory_space=pl.ANY),
                  pl.BlockSpec(memory_space=pl.ANY),
                  pl.BlockSpec(memory_space=pl.ANY)],
        out_specs=pl.BlockSpec((MOE_BLOCK, dff), lambda b, *_: (b, 0)),
        scratch_shapes=[pltpu.VMEM((slot_rows, LANE), F32),
                        pltpu.VMEM((slot_rows, LANE), F32),
                        pltpu.VMEM((2, d, dff), F32),
                        pltpu.VMEM((2, d, dff), F32),
                        pltpu.VMEM((d, dff), BF16),
                        pltpu.VMEM((d, dff), BF16),
                        pltpu.SemaphoreType.DMA((2,)),
                        pltpu.SemaphoreType.DMA((2,))],
    )
    return pl.pallas_call(
        functools.partial(_expert_up_kernel, fchunk=fchunk, nt=nt),
        out_shape=jax.ShapeDtypeStruct((n_blocks * MOE_BLOCK, dff), BF16),
        grid_spec=grid_spec,
        compiler_params=_cparams(("arbitrary",)),
        name="expert_up",
    )(*tables, src_rows, src_rows, h2t, w_gate, w_up)


def _expert_down_kernel(be_ref, nused_ref, eord_ref, enext_ref, dst_ref, h_ref, wd_hbm, y_hbm,
                        ys0_ref, ys1_ref, wds_ref, wdb_ref, sem, wsem, *, nchunk, nt):
    b = pl.program_id(0)
    n_used = nused_ref[0]
    d = wdb_ref.shape[1]
    slots = (ys0_ref, ys1_ref)
    pitch = _pitch(nt)

    def scatter_row(r, slot):
        return pltpu.make_async_copy(slots[slot].at[pl.ds(r * pitch, pitch), :],
                                     y_hbm.at[pl.ds(dst_ref[0, 0, r], pitch), :], sem.at[slot])

    @pl.when(b == 0)
    def _():
        ys0_ref[...] = jnp.zeros_like(ys0_ref)
        ys1_ref[...] = jnp.zeros_like(ys1_ref)

    for slot in range(2):
        @pl.when((b >= 1) & (b <= n_used) & (lax.rem(b, 2) == slot))
        def _(slot=slot):
            _bulk_wait(slots[slot], y_hbm, sem.at[slot], MOE_BLOCK * pitch)

    @pl.when(b < n_used)
    def _():
        _stream_expert_weights(b, be_ref, eord_ref, enext_ref, (wd_hbm,), (wds_ref,), (wdb_ref,), wsem,
                               both_queues=True)
        hb = h_ref[...]
        for slot in range(2):
            @pl.when(lax.rem(b, 2) == slot)
            def _(slot=slot):
                nc = d // nchunk
                per = MOE_BLOCK // nc
                for c in range(nc):
                    for r in range(c * per, (c + 1) * per):
                        scatter_row(r, 1 - slot).start(priority=r % 2)
                    out = jnp.dot(hb, wdb_ref[:, c * nchunk:(c + 1) * nchunk], preferred_element_type=F32)
                    for j in range(nchunk // LANE):
                        slots[slot][pl.ds(c * (nchunk // LANE) + j, MOE_BLOCK, stride=pitch), :] = (
                            out[:, j * LANE:(j + 1) * LANE])

    for slot in range(2):
        @pl.when((b == n_used) & (lax.rem(b, 2) == slot))
        def _(slot=slot):
            _for_rows(MOE_BLOCK, lambda r: scatter_row(r, 1 - slot).start())
            _bulk_wait(slots[1 - slot], y_hbm, sem.at[1 - slot], MOE_BLOCK * pitch)


def _expert_down(hid, dst_rows, w_down, tables, y_slots, nchunk=256):
    n_rows, dff = hid.shape
    n_blocks = n_rows // MOE_BLOCK
    d = w_down.shape[2]
    nt = d // LANE
    pitch = _pitch(nt)
    grid_spec = pltpu.PrefetchScalarGridSpec(
        num_scalar_prefetch=len(tables),
        grid=(n_blocks,),
        in_specs=[pl.BlockSpec((1, 1, MOE_BLOCK), lambda b, *_: (b, 0, 0), memory_space=pltpu.SMEM),
                  pl.BlockSpec((MOE_BLOCK, dff), lambda b, be, n, *_: (jnp.minimum(b, n[0] - 1), 0)),
                  pl.BlockSpec(memory_space=pl.ANY)],
        out_specs=pl.BlockSpec(memory_space=pl.ANY),
        scratch_shapes=[pltpu.VMEM((MOE_BLOCK * pitch, LANE), F32),
                        pltpu.VMEM((MOE_BLOCK * pitch, LANE), F32),
                        pltpu.VMEM((2, dff, d), F32),
                        pltpu.VMEM((dff, d), BF16),
                        pltpu.SemaphoreType.DMA((2,)),
                        pltpu.SemaphoreType.DMA((2,))],
    )
    return pl.pallas_call(
        functools.partial(_expert_down_kernel, nchunk=nchunk, nt=nt),
        out_shape=jax.ShapeDtypeStruct((y_slots * pitch, LANE), F32),
        grid_spec=grid_spec,
        compiler_params=_cparams(("arbitrary",)),
        name="expert_down",
    )(*tables, dst_rows, hid, w_down)


def _combine_kernel(y0_ref, y1_ref, ew_ref, x_ref, g_ref, w_ref, o_ref):
    tm, d = x_ref.shape
    nt = d // LANE
    ew = ew_ref[...]
    moe = (_rows_to_matrix(y0_ref, tm, nt) * ew[:, 0:1]
           + _rows_to_matrix(y1_ref, tm, nt) * ew[:, 1:2])
    xo = x_ref[...] + g_ref[0] * moe
    ms = jnp.mean(xo * xo, axis=-1, keepdims=True)
    o_ref[...] = xo * lax.rsqrt(ms + EPS) * w_ref[...]


def _combine(y, ew, x2, mod3, gate_chunk, rows_per_batch, final_w, tm=256):
    m, d = x2.shape
    pitch = _pitch(d // LANE)
    tiles = m // tm
    tiles_per_batch = rows_per_batch // tm
    return pl.pallas_call(
        _combine_kernel,
        out_shape=jax.ShapeDtypeStruct((m, d), F32),
        grid=(tiles,),
        in_specs=[pl.BlockSpec((tm * pitch, LANE), lambda i: (i, 0)),
                  pl.BlockSpec((tm * pitch, LANE), lambda i: (tiles + i, 0)),
                  pl.BlockSpec((tm, LANE), lambda i: (i, 0)),
                  pl.BlockSpec((tm, d), lambda i: (i, 0)),
                  pl.BlockSpec((1, 1, d), lambda i: (i // tiles_per_batch, 0, gate_chunk)),
                  pl.BlockSpec((1, d), lambda i: (0, 0))],
        out_specs=pl.BlockSpec((tm, d), lambda i: (i, 0)),
        compiler_params=_cparams(("parallel",)),
        name="moe_combine",
    )(y, y, ew, x2, mod3, final_w.reshape(1, d))


def _dispatch_tables(eid, n_tok, pitch):
    top_k = eid.shape[1]
    n_assign = n_tok * top_k
    expert = eid.reshape(-1)
    key = jnp.sort(expert * n_assign + jnp.arange(n_assign, dtype=jnp.int32))
    sorted_assign = key % n_assign
    bounds = jnp.arange(N_EXPERTS + 1, dtype=jnp.int32) * n_assign
    start = jnp.searchsorted(key, bounds, side="left").astype(jnp.int32)
    counts = start[1:] - start[:-1]
    nblk = (counts + MOE_BLOCK - 1) // MOE_BLOCK
    blk_end = jnp.cumsum(nblk)
    blk_start = blk_end - nblk
    steps = -(-n_assign // MOE_BLOCK) + N_EXPERTS + 1
    bidx = jnp.arange(steps, dtype=jnp.int32)
    lane = jnp.arange(MOE_BLOCK, dtype=jnp.int32)[None, :]
    block_expert = jnp.minimum(jnp.searchsorted(blk_end, bidx, side="right"), N_EXPERTS - 1).astype(jnp.int32)
    in_expert = (bidx - blk_start[block_expert]) * MOE_BLOCK
    n_valid = jnp.clip(counts[block_expert] - in_expert, 0, MOE_BLOCK)
    src = start[block_expert][:, None] + in_expert[:, None] + lane
    valid = lane < n_valid[:, None]
    assign = sorted_assign[jnp.clip(src, 0, n_assign - 1)]
    tok = assign // top_k
    src_rows = jnp.where(valid, tok, (bidx[:, None] * MOE_BLOCK + lane) % n_tok) * pitch
    dst_slot = jnp.where(valid, (assign % top_k) * n_tok + tok, top_k * n_tok + lane)
    dst_rows = jnp.concatenate([top_k * n_tok + lane, dst_slot[:-1]], axis=0) * pitch
    n_used = blk_end[-1].astype(jnp.int32)
    first = jnp.concatenate([jnp.ones((1,), jnp.int32),
                             (block_expert[1:] != block_expert[:-1]).astype(jnp.int32)])
    expert_ordinal = (jnp.cumsum(first) - 1).astype(jnp.int32)
    next_blk = blk_end[block_expert]
    next_expert = jnp.where(next_blk < n_used, block_expert[jnp.minimum(next_blk, steps - 1)], -1).astype(jnp.int32)
    tables = (block_expert, n_used.reshape(1), expert_ordinal, next_expert)
    return (src_rows.astype(jnp.int32).reshape(steps, 1, MOE_BLOCK),
            dst_rows.astype(jnp.int32).reshape(steps, 1, MOE_BLOCK), tables)


def kernel(x, c, ctx, c_ctx, ada_w, ada_b, norm1_w, w_in, ssm_conv_w, ssm_conv_b, dt_bias, a_log, d_skip, ssm_norm_w, ssm_out_w, cf_dw_w, cf_dw_b, cf_ln_w, cf_ln_b, cf_out_w, cf_out_b, w_o, norm2_w, router_group_w, router_group_b, router_expert_w, router_expert_b, expert_w_gate, expert_w_up, expert_w_down, final_norm_w):
    bsz, seq, d = x.shape
    l_ctx = ctx.shape[1]
    n_tok = bsz * seq
    d_inner = ssm_norm_w.shape[1]
    gn = N_GROUPS * D_STATE
    xbc_dim = d_inner + 2 * gn
    off_dt = xbc_dim
    off_z = off_dt + N_HEADS
    off_glu = off_z + d_inner
    off_gate = off_glu + 2 * d

    ctx_row = bsz
    crows = jnp.zeros((8, d), F32).at[:bsz].set(c).at[ctx_row].set(c_ctx)
    mod = _ada(crows, ada_w[0], ada_b[0])
    mod3 = mod.reshape(8, 1, 6 * d)
    lat_rows = jnp.arange(bsz, dtype=jnp.int32)
    ctx_rows = jnp.full((bsz,), ctx_row, jnp.int32)

    h_lat = _normmod(x, norm1_w[0], mod3, lat_rows, 0, 1, BF16).reshape(n_tok, d)
    h_ctx = _normmod(ctx, norm1_w[0], mod3, ctx_rows, 0, 1, BF16).reshape(bsz * l_ctx, d)

    wt, (r_xbc, r_dt, r_z, r_glu, r_gate) = _pack_wt(
        jnp.transpose(w_in[0]),
        [(0, xbc_dim), (off_dt, off_z), (off_z, off_glu), (off_glu, off_gate), (off_gate, off_gate + 2 * d)])

    xbc_lat = _mm(h_lat, wt, tn=2048, name="in_xbc", rows=(r_xbc, xbc_dim)).reshape(bsz, seq, xbc_dim)
    xbc_ctx = _mm(h_ctx, wt, tm=512, name="in_xbc_ctx", rows=(r_xbc, xbc_dim)).reshape(bsz, l_ctx, xbc_dim)
    dt_lat = _mm(h_lat, wt, name="in_dt", rows=(r_dt, LANE)).reshape(bsz, seq, LANE)
    dt_ctx = _mm(h_ctx, wt, tm=512, name="in_dt_ctx", rows=(r_dt, LANE)).reshape(bsz, l_ctx, LANE)
    sz = _mm(h_lat, wt, act="silu", tn=2048, name="in_z", rows=(r_z, d_inner))
    u = _mm_glu(h_lat, wt, r_glu, d, tn=1024)

    xbc_act = _conv7(xbc_ctx, xbc_lat, ssm_conv_w[0], ssm_conv_b[0])

    def ssd_params(k):
        par = jnp.zeros((8, LANE), F32).at[0, :N_HEADS].set(dt_bias[0, k]).at[1, :N_HEADS].set(a_log[0, k])
        return par, jnp.repeat(d_skip[0, k], HEAD_DIM).reshape(N_HEADS // 2, 1, LANE)

    y_bwd = _ssd(xbc_act, dt_ctx, dt_lat, *ssd_params(1), reverse=True)
    gnorm = _ssd(xbc_act, dt_ctx, dt_lat, *ssd_params(0), reverse=False,
                 norm_with=(y_bwd, sz, ssm_norm_w[0]))
    y_ssd = _mm(gnorm, ssm_out_w[0].astype(BF16), tn=512, name="ssm_out")

    cv = _conv31(u.reshape(bsz, seq, d), cf_dw_w[0], cf_dw_b[0]).reshape(n_tok, d)
    merged = _mm_merge(cv, cf_ln_w[0], cf_ln_b[0], cf_out_w[0].astype(BF16), cf_out_b[0],
                       h_lat, wt, r_gate, y_ssd)
    x1 = _mm_resid(merged, w_o[0].astype(BF16), x.reshape(n_tok, d), mod3, 2, seq, tm=2048)

    n_r = MOE_GROUPS + N_EXPERTS
    rw = jnp.pad(jnp.concatenate([router_group_w[0], router_expert_w[0]], axis=1),
                 ((0, 0), (0, LANE - n_r))).astype(BF16)
    rb = jnp.pad(jnp.concatenate([router_group_b[0], router_expert_b[0]]), (0, LANE - n_r)).reshape(1, LANE)
    h2t, eid, ew = _route(x1, norm2_w[0], mod3, 3, 4, seq, rw, rb)

    src_rows, dst_rows, tables = _dispatch_tables(eid[:, :2], n_tok, _pitch(d // LANE))
    hid = _expert_up(h2t, src_rows, expert_w_gate[0], expert_w_up[0], tables)
    y = _expert_down(hid, dst_rows, expert_w_down[0], tables, 2 * n_tok + MOE_BLOCK)
    out = _combine(y, ew, x1, mod3, 5, seq, final_norm_w)
    return out.reshape(bsz, seq, d)
```

```python
import functools

import jax
import jax.numpy as jnp
from jax import lax
from jax.experimental import pallas as pl
from jax.experimental.pallas import tpu as pltpu

F32 = jnp.float32
BF16 = jnp.bfloat16

EPS = 1e-6
GRID_W = 64
HEAD_DIM = 64
N_HEADS = 64
N_GROUPS = 8
D_STATE = 128
CHUNK = 128
SSM_CONV = 7
CF_KERNEL = 31
MOE_GROUPS = 8
EXPERTS_PER_GROUP = 8
N_EXPERTS = 64
MOE_BLOCK = 256
LANE = 128
LOG2E = 1.4426950408889634
VMEM_LIMIT = 56 * 1024 * 1024


def _cparams(sem):
    return pltpu.CompilerParams(dimension_semantics=sem, vmem_limit_bytes=VMEM_LIMIT)


def _silu(v):
    return v * jax.nn.sigmoid(v)


def _pitch(nt):
    return nt + 1


def _ada_kernel(c_ref, w_ref, b_ref, o_ref):
    s = _silu(c_ref[...])
    o_ref[...] = jnp.dot(s.astype(BF16), w_ref[...].astype(BF16),
                         preferred_element_type=F32) + b_ref[...]


def _ada(crows, ada_w, ada_b, tn=1024):
    r, d = crows.shape
    n = ada_w.shape[1]
    return pl.pallas_call(
        _ada_kernel,
        out_shape=jax.ShapeDtypeStruct((r, n), F32),
        grid=(n // tn,),
        in_specs=[pl.BlockSpec((r, d), lambda j: (0, 0)),
                  pl.BlockSpec((d, tn), lambda j: (0, j)),
                  pl.BlockSpec((1, tn), lambda j: (0, j))],
        out_specs=pl.BlockSpec((r, tn), lambda j: (0, j)),
        compiler_params=_cparams(("parallel",)),
        name="ada",
    )(crows, ada_w, ada_b.reshape(1, n))


def _normmod_kernel(rows_ref, x_ref, w_ref, sh_ref, sc_ref, o_ref):
    del rows_ref
    xf = x_ref[0]
    ms = jnp.mean(xf * xf, axis=-1, keepdims=True)
    y = xf * lax.rsqrt(ms + EPS) * w_ref[...]
    o_ref[0] = (y * (1.0 + sc_ref[0]) + sh_ref[0]).astype(o_ref.dtype)


def _normmod(x3, w, mod3, rows, shift_chunk, scale_chunk, out_dtype, tm=256):
    bx, l, d = x3.shape
    grid_spec = pltpu.PrefetchScalarGridSpec(
        num_scalar_prefetch=1,
        grid=(bx, l // tm),
        in_specs=[pl.BlockSpec((1, tm, d), lambda b, i, r: (b, i, 0)),
                  pl.BlockSpec((1, d), lambda b, i, r: (0, 0)),
                  pl.BlockSpec((1, 1, d), lambda b, i, r: (r[b], 0, shift_chunk)),
                  pl.BlockSpec((1, 1, d), lambda b, i, r: (r[b], 0, scale_chunk))],
        out_specs=pl.BlockSpec((1, tm, d), lambda b, i, r: (b, i, 0)),
    )
    return pl.pallas_call(
        _normmod_kernel,
        out_shape=jax.ShapeDtypeStruct((bx, l, d), out_dtype),
        grid_spec=grid_spec,
        compiler_params=_cparams(("parallel", "parallel")),
        name="normmod",
    )(rows, x3, w.reshape(1, d), mod3, mod3)


def _dot_nt(a, wt):
    return lax.dot_general(a, wt, (((1,), (1,)), ((), ())), preferred_element_type=F32)


def _mm_kernel(a_ref, w_ref, *rest, act, has_bias, w_rows):
    o_ref = rest[-1]
    a = a_ref[...]
    acc = _dot_nt(a, w_ref[...]) if w_rows else jnp.dot(a, w_ref[...], preferred_element_type=F32)
    if has_bias:
        acc = acc + rest[0][...]
    if act == "silu":
        acc = _silu(acc)
    elif act == "sigmoid":
        acc = jax.nn.sigmoid(acc)
    o_ref[...] = acc.astype(o_ref.dtype)


def _mm(a, w, bias=None, act=None, out_dtype=F32, tm=1024, tn=1024, name="mm", rows=None):
    m, k = a.shape
    start, n = (0, w.shape[1]) if rows is None else rows
    tm, tn = min(tm, m), min(tn, n)
    j0 = start // tn
    w_spec = (pl.BlockSpec((k, tn), lambda i, j: (0, j)) if rows is None
              else pl.BlockSpec((tn, k), lambda i, j: (j0 + j, 0)))
    in_specs = [pl.BlockSpec((tm, k), lambda i, j: (i, 0)), w_spec]
    args = [a, w]
    if bias is not None:
        in_specs.append(pl.BlockSpec((1, tn), lambda i, j: (0, j)))
        args.append(bias.reshape(1, n))
    return pl.pallas_call(
        functools.partial(_mm_kernel, act=act, has_bias=bias is not None, w_rows=rows is not None),
        out_shape=jax.ShapeDtypeStruct((m, n), out_dtype),
        grid=(m // tm, n // tn),
        in_specs=in_specs,
        out_specs=pl.BlockSpec((tm, tn), lambda i, j: (i, j)),
        compiler_params=_cparams(("parallel", "parallel")),
        name=name,
    )(*args)


def _mm_glu_kernel(a_ref, wa_ref, wb_ref, o_ref):
    a = a_ref[...]
    va = _dot_nt(a, wa_ref[...])
    vb = _dot_nt(a, wb_ref[...])
    o_ref[...] = va * jax.nn.sigmoid(vb)


def _mm_glu(a, wt, start, n, tm=1024, tn=512):
    m, k = a.shape
    tm = min(tm, m)
    ja, jb = start // tn, (start + n) // tn
    return pl.pallas_call(
        _mm_glu_kernel,
        out_shape=jax.ShapeDtypeStruct((m, n), F32),
        grid=(m // tm, n // tn),
        in_specs=[pl.BlockSpec((tm, k), lambda i, j: (i, 0)),
                  pl.BlockSpec((tn, k), lambda i, j: (ja + j, 0)),
                  pl.BlockSpec((tn, k), lambda i, j: (jb + j, 0))],
        out_specs=pl.BlockSpec((tm, tn), lambda i, j: (i, j)),
        compiler_params=_cparams(("parallel", "parallel")),
        name="mm_glu",
    )(a, wt, wt)


def _mm_merge_kernel(cv_ref, lw_ref, lb_ref, w_ref, b_ref, h_ref, wga_ref, wgb_ref, ys_ref, o_ref, u_ref):
    @pl.when(pl.program_id(1) == 0)
    def _():
        xf = cv_ref[...]
        mu = jnp.mean(xf, axis=-1, keepdims=True)
        xc = xf - mu
        var = jnp.mean(xc * xc, axis=-1, keepdims=True)
        y = xc * lax.rsqrt(var + EPS) * lw_ref[...] + lb_ref[...]
        u_ref[...] = _silu(y).astype(u_ref.dtype)

    ycf = jnp.dot(u_ref[...], w_ref[...], preferred_element_type=F32) + b_ref[...]
    h = h_ref[...]
    gate_a = jax.nn.sigmoid(_dot_nt(h, wga_ref[...]))
    gate_b = jax.nn.sigmoid(_dot_nt(h, wgb_ref[...]))
    o_ref[...] = (gate_a * ys_ref[...] + gate_b * ycf).astype(o_ref.dtype)


def _mm_merge(cv, ln_w, ln_b, w, bias, h, wt, gate_start, y_ssd, tm=512, tn=512):
    m, k = cv.shape
    n = w.shape[1]
    tm = min(tm, m)
    ja, jb = gate_start // tn, (gate_start + n) // tn
    return pl.pallas_call(
        _mm_merge_kernel,
        out_shape=jax.ShapeDtypeStruct((m, n), BF16),
        grid=(m // tm, n // tn),
        in_specs=[pl.BlockSpec((tm, k), lambda i, j: (i, 0)),
                  pl.BlockSpec((1, k), lambda i, j: (0, 0)),
                  pl.BlockSpec((1, k), lambda i, j: (0, 0)),
                  pl.BlockSpec((k, tn), lambda i, j: (0, j)),
                  pl.BlockSpec((1, tn), lambda i, j: (0, j)),
                  pl.BlockSpec((tm, k), lambda i, j: (i, 0)),
                  pl.BlockSpec((tn, k), lambda i, j: (ja + j, 0)),
                  pl.BlockSpec((tn, k), lambda i, j: (jb + j, 0)),
                  pl.BlockSpec((tm, tn), lambda i, j: (i, j))],
        out_specs=pl.BlockSpec((tm, tn), lambda i, j: (i, j)),
        scratch_shapes=[pltpu.VMEM((tm, k), BF16)],
        compiler_params=_cparams(("parallel", "arbitrary")),
        name="mm_merge",
    )(cv, ln_w.reshape(1, k), ln_b.reshape(1, k), w, bias.reshape(1, n), h, wt, wt, y_ssd)


def _mm_resid_kernel(a_ref, w_ref, x_ref, g_ref, o_ref):
    out = jnp.dot(a_ref[...], w_ref[...], preferred_element_type=F32)
    o_ref[...] = x_ref[...] + g_ref[0] * out


def _mm_resid(a, w, x2, mod3, gate_chunk, rows_per_batch, tm=1024, tn=512):
    m, k = a.shape
    n = w.shape[1]
    tm = min(tm, rows_per_batch)
    nj = n // tn
    tiles_per_batch = rows_per_batch // tm
    return pl.pallas_call(
        _mm_resid_kernel,
        out_shape=jax.ShapeDtypeStruct((m, n), F32),
        grid=(m // tm, nj),
        in_specs=[pl.BlockSpec((tm, k), lambda i, j: (i, 0)),
                  pl.BlockSpec((k, tn), lambda i, j: (0, j)),
                  pl.BlockSpec((tm, tn), lambda i, j: (i, j)),
                  pl.BlockSpec((1, 1, tn),
                               lambda i, j: (i // tiles_per_batch, 0, gate_chunk * nj + j))],
        out_specs=pl.BlockSpec((tm, tn), lambda i, j: (i, j)),
        compiler_params=_cparams(("parallel", "parallel")),
        name="mm_resid",
    )(a, w, x2, mod3)


W_ALIGN = 2048


def _pack_wt_kernel(valid_ref, off_ref, w_ref, o_ref):
    del off_ref
    nrow = valid_ref[pl.program_id(0)]
    row = lax.broadcasted_iota(jnp.int32, o_ref.shape, 0)
    o_ref[...] = jnp.where(row < nrow, w_ref[...], 0.0).astype(o_ref.dtype)


def _pack_wt(wt, segments, tr=512):
    n, k = wt.shape
    starts, src_off, valid = [], [], []
    pos = 0
    for lo, hi in segments:
        pos = -(-pos // W_ALIGN) * W_ALIGN
        starts.append(pos)
        while len(src_off) < pos // tr:
            src_off.append(0)
            valid.append(0)
        for r in range(lo, hi, tr):
            src_off.append(min(r, n - tr))
            valid.append(min(tr, hi - r))
            assert r <= n - tr or hi - r == tr
        pos += -(-(hi - lo) // tr) * tr
    total = -(-pos // W_ALIGN) * W_ALIGN
    while len(src_off) < total // tr:
        src_off.append(0)
        valid.append(0)
    grid_spec = pltpu.PrefetchScalarGridSpec(
        num_scalar_prefetch=2,
        grid=(total // tr,),
        in_specs=[pl.BlockSpec((pl.Element(tr), pl.Element(k)), lambda t, v, off: (off[t] * 8, 0))],
        out_specs=pl.BlockSpec((tr, k), lambda t, v, off: (t, 0)),
    )
    packed = pl.pallas_call(
        _pack_wt_kernel,
        out_shape=jax.ShapeDtypeStruct((total, k), BF16),
        grid_spec=grid_spec,
        compiler_params=_cparams(("parallel",)),
        name="pack_wt",
    )(jnp.asarray(valid, jnp.int32), jnp.asarray(src_off, jnp.int32) // 8, wt)
    return packed, starts


_CONV_PAD = 8


def _conv7_kernel(ctx_ref, lat_ref, w_ref, b_ref, o_ref, pad_ref, *, l_ctx, l_lat):
    p = _CONV_PAD
    zeros = jnp.zeros((p, LANE), F32)
    off_ctx = p
    off_lat = 2 * p + l_ctx
    pad_ref[0:p, :] = zeros
    pad_ref[off_ctx + l_ctx:off_lat, :] = zeros
    pad_ref[off_lat + l_lat:off_lat + l_lat + p, :] = zeros
    pad_ref[off_ctx:off_ctx + l_ctx, :] = ctx_ref[0]
    pad_ref[off_lat:off_lat + l_lat, :] = lat_ref[0]
    reach = SSM_CONV // 2
    bias = b_ref[...]

    def chunk(pad_base, out_base):
        acc = jnp.broadcast_to(bias, (CHUNK, LANE))
        for k in range(SSM_CONV):
            tap = pad_ref[pl.ds(pad_base - reach + k, CHUNK), :]
            acc = acc + tap * w_ref[k:k + 1, :]
        o_ref[0, 0, pl.ds(out_base, CHUNK), :] = _silu(acc)

    def ctx_body(j, c):
        base = pl.multiple_of(j * CHUNK, CHUNK)
        chunk(off_ctx + base, base)
        return c

    def lat_body(j, c):
        base = pl.multiple_of(j * CHUNK, CHUNK)
        chunk(off_lat + base, l_ctx + base)
        return c

    lax.fori_loop(0, l_ctx // CHUNK, ctx_body, 0)
    lax.fori_loop(0, l_lat // CHUNK, lat_body, 0, unroll=2)


def _conv7(ctx_raw, lat_raw, w, b):
    bsz, l_ctx, c = ctx_raw.shape
    l_lat = lat_raw.shape[1]
    ltot = l_ctx + l_lat
    nct = c // LANE
    return pl.pallas_call(
        functools.partial(_conv7_kernel, l_ctx=l_ctx, l_lat=l_lat),
        out_shape=jax.ShapeDtypeStruct((bsz, nct, ltot, LANE), F32),
        grid=(bsz, nct),
        in_specs=[pl.BlockSpec((1, l_ctx, LANE), lambda bi, ci: (bi, 0, ci)),
                  pl.BlockSpec((1, l_lat, LANE), lambda bi, ci: (bi, 0, ci)),
                  pl.BlockSpec((SSM_CONV, LANE), lambda bi, ci: (0, ci)),
                  pl.BlockSpec((1, LANE), lambda bi, ci: (0, ci))],
        out_specs=pl.BlockSpec((1, 1, ltot, LANE), lambda bi, ci: (bi, ci, 0, 0)),
        scratch_shapes=[pltpu.VMEM((ltot + 3 * _CONV_PAD, LANE), F32)],
        compiler_params=_cparams(("parallel", "parallel")),
        name="conv7",
    )(ctx_raw, lat_raw, w, b.reshape(1, c))


def _ssd_kernel(xbc_ref, dtc_ref, dtl_ref, par_ref, dexp_ref, ex_ref, *rest, reverse, n_ctx, fuse_norm):
    if fuse_norm:
        yo_ref, sz_ref, nw_ref, o_ref, st_ref, cumt_ref, y_ref = rest
    else:
        y_ref, st_ref, cumt_ref = rest
    i = pl.program_id(1)

    @pl.when(i == 0)
    def _():
        st_ref[...] = jnp.zeros_like(st_ref)

    dt_raw = jnp.where(i < n_ctx, dtc_ref[0], dtl_ref[0])
    bias = par_ref[0:1, :]
    a = -jnp.exp(par_ref[1:2, :])
    dt = jax.nn.softplus(dt_raw + bias)
    cum = dt * a
    row = lax.broadcasted_iota(jnp.int32, (CHUNK, LANE), 0)
    k = 1
    while k < CHUNK:
        if reverse:
            cum = cum + jnp.where(row < CHUNK - k, pltpu.roll(cum, CHUNK - k, 0), 0.0)
        else:
            cum = cum + jnp.where(row >= k, pltpu.roll(cum, k, 0), 0.0)
        k *= 2
    last = 0 if reverse else CHUNK - 1
    cum = cum * LOG2E
    cumt_ref[...] = cum.T
    li = lax.broadcasted_iota(jnp.int32, (CHUNK, CHUNK), 0)
    si = lax.broadcasted_iota(jnp.int32, (CHUNK, CHUNK), 1)
    causal = (li <= si) if reverse else (li >= si)
    lo = lax.broadcasted_iota(jnp.int32, (CHUNK, LANE), 1) < HEAD_DIM
    heads_per_group = N_HEADS // N_GROUPS
    pairs = heads_per_group // 2
    x_tiles = N_HEADS // 2

    def group(g, carry):
        shift = (LANE - heads_per_group * g) & (LANE - 1)
        cum_g = pltpu.roll(cum, shift, 1)
        dt_g = pltpu.roll(dt, shift, 1)
        cum_t = cumt_ref[pl.ds(pl.multiple_of(heads_per_group * g, heads_per_group), heads_per_group), :]
        bb = xbc_ref[0, x_tiles + g].astype(BF16)
        cb = xbc_ref[0, x_tiles + N_GROUPS + g].astype(BF16)
        scores = lax.dot_general(cb, bb, (((1,), (1,)), ((), ())), preferred_element_type=F32)
        h_t = st_ref[g]
        y_off = jnp.dot(cb, h_t.astype(BF16), preferred_element_type=F32)
        d_hi = dt_g.astype(BF16)
        r_hi = dt_g - d_hi.astype(F32)
        d_mid = r_hi.astype(BF16)
        d_lo = (r_hi - d_mid.astype(F32)).astype(BF16)
        dt_x = (jnp.dot(jnp.concatenate([d_hi, d_mid], axis=1), ex_ref[...], preferred_element_type=F32)
                + jnp.dot(d_lo, ex_ref[0:LANE, :], preferred_element_type=F32))
        xw_parts, dec_parts = [], []
        for p in range(pairs):
            j0, j1 = 2 * p, 2 * p + 1
            x2 = xbc_ref[0, pairs * g + p]
            c0 = cum_g[:, j0:j0 + 1]
            c1 = cum_g[:, j1:j1 + 1]
            l0 = jnp.exp2(jnp.where(causal, c0 - cum_t[j0:j0 + 1, :], -jnp.inf))
            l1 = jnp.exp2(jnp.where(causal, c1 - cum_t[j1:j1 + 1, :], -jnp.inf))
            m0 = (scores * l0).astype(BF16)
            m1 = (scores * l1).astype(BF16)
            dt2 = dt_x[:, p * LANE:(p + 1) * LANE]
            c2 = jnp.where(lo, c0, c1)
            xdt = x2 * dt2
            xdt_b = xdt.astype(BF16)
            zero = jnp.zeros_like(xdt_b)
            y_diag = jnp.dot(jnp.concatenate([m0, m1], axis=1),
                             jnp.concatenate([jnp.where(lo, xdt_b, zero), jnp.where(lo, zero, xdt_b)], axis=0),
                             preferred_element_type=F32)
            e2 = jnp.exp2(c2)
            y = y_diag + y_off[:, p * LANE:(p + 1) * LANE] * e2
            y_ref[0, pairs * g + p] = y + dexp_ref[pairs * g + p] * x2
            to_end = jnp.exp2(c2[last:last + 1, :] - c2)
            xw_parts.append((xdt * to_end).astype(BF16))
            dec_parts.append(e2[last:last + 1, :])
        xw = jnp.concatenate(xw_parts, axis=1)
        dec = jnp.concatenate(dec_parts, axis=1)
        upd = lax.dot_general(bb, xw, (((0,), (0,)), ((), ())), preferred_element_type=F32)
        st_ref[g] = h_t * dec + upd
        return carry

    lax.fori_loop(0, N_GROUPS, group, 0, unroll=2)

    if fuse_norm:
        sq = jnp.zeros((CHUNK, LANE), F32)
        for j in range(x_tiles):
            gj = (y_ref[0, j] + yo_ref[0, j]) * sz_ref[:, j * LANE:(j + 1) * LANE]
            y_ref[0, j] = gj
            sq = sq + gj * gj
        r = lax.rsqrt(jnp.sum(sq, axis=-1, keepdims=True) / (x_tiles * LANE) + EPS)
        for j in range(x_tiles):
            sl = slice(j * LANE, (j + 1) * LANE)
            o_ref[:, sl] = (y_ref[0, j] * r * nw_ref[:, sl]).astype(o_ref.dtype)


def _ssd(xbc_act, dt_ctx, dt_lat, par, dexp, reverse, norm_with=None):
    bsz, ntile, ltot, _ = xbc_act.shape
    l_ctx = dt_ctx.shape[1]
    l_lat = dt_lat.shape[1]
    n_ctx = l_ctx // CHUNK
    n_lat = l_lat // CHUNK
    steps = n_ctx + n_lat
    x_tiles = N_HEADS // 2
    gw = (N_HEADS // N_GROUPS) * HEAD_DIM
    e1 = (jnp.arange(gw)[None, :] // HEAD_DIM == jnp.arange(LANE)[:, None]).astype(BF16)
    expand = jnp.concatenate([e1, e1], axis=0)

    if reverse:
        def cat_chunk(i):
            return jnp.where(i < n_ctx, n_ctx - 1 - i, n_ctx + steps - 1 - i)

        def ctx_chunk(i):
            return jnp.maximum(n_ctx - 1 - i, 0)

        def lat_chunk(i):
            return jnp.minimum(steps - 1 - i, n_lat - 1)
    else:
        def cat_chunk(i):
            return i

        def ctx_chunk(i):
            return jnp.minimum(i, n_ctx - 1)

        def lat_chunk(i):
            return jnp.maximum(i - n_ctx, 0)

    y_spec = pl.BlockSpec((1, x_tiles, CHUNK, LANE), lambda b, i: (b, 0, lat_chunk(i), 0))
    in_specs = [pl.BlockSpec((1, ntile, CHUNK, LANE), lambda b, i: (b, 0, cat_chunk(i), 0)),
                pl.BlockSpec((1, CHUNK, LANE), lambda b, i: (b, ctx_chunk(i), 0)),
                pl.BlockSpec((1, CHUNK, LANE), lambda b, i: (b, lat_chunk(i), 0)),
                pl.BlockSpec((8, LANE), lambda b, i: (0, 0)),
                pl.BlockSpec((x_tiles, 1, LANE), lambda b, i: (0, 0, 0)),
                pl.BlockSpec((2 * LANE, gw), lambda b, i: (0, 0))]
    args = [xbc_act, dt_ctx, dt_lat, par, dexp, expand]
    scratch = [pltpu.VMEM((N_GROUPS, D_STATE, gw), F32), pltpu.VMEM((LANE, CHUNK), F32)]
    if norm_with is None:
        out_shape = jax.ShapeDtypeStruct((bsz, x_tiles, l_lat, LANE), F32)
        out_spec = y_spec
    else:
        y_other, silu_z, norm_w = norm_with
        dn = x_tiles * LANE
        row_spec = pl.BlockSpec((CHUNK, dn), lambda b, i: (b * n_lat + lat_chunk(i), 0))
        in_specs += [y_spec, row_spec, pl.BlockSpec((1, dn), lambda b, i: (0, 0))]
        args += [y_other, silu_z, norm_w.reshape(1, dn)]
        out_shape = jax.ShapeDtypeStruct((bsz * l_lat, dn), BF16)
        out_spec = row_spec
        scratch.append(pltpu.VMEM((1, x_tiles, CHUNK, LANE), F32))
    return pl.pallas_call(
        functools.partial(_ssd_kernel, reverse=reverse, n_ctx=n_ctx, fuse_norm=norm_with is not None),
        out_shape=out_shape,
        grid=(bsz, steps),
        in_specs=in_specs,
        out_specs=out_spec,
        scratch_shapes=scratch,
        compiler_params=_cparams(("parallel", "arbitrary")),
        name="ssd_bwd" if reverse else "ssd_fwd",
    )(*args)


def _conv31_kernel(u_ref, w_ref, b_ref, o_ref, pad_ref, *, seq):
    halo = (CF_KERNEL // 2) * GRID_W
    zeros = jnp.zeros((halo, LANE), F32)
    pad_ref[0:halo, :] = zeros
    pad_ref[halo + seq:halo + seq + halo, :] = zeros
    pad_ref[halo:halo + seq, :] = u_ref[0]
    bias = b_ref[...]

    def body(j, c):
        base = pl.multiple_of(j * CHUNK, CHUNK)
        acc = jnp.broadcast_to(bias, (CHUNK, LANE))
        for k in range(CF_KERNEL):
            tap = pad_ref[pl.ds(pl.multiple_of(base + k * GRID_W, GRID_W), CHUNK), :]
            acc = acc + tap * w_ref[k:k + 1, :]
        o_ref[0, pl.ds(base, CHUNK), :] = acc
        return c

    lax.fori_loop(0, seq // CHUNK, body, 0, unroll=2)


def _conv31(u3, w, b):
    bsz, s, c = u3.shape
    halo = (CF_KERNEL // 2) * GRID_W
    return pl.pallas_call(
        functools.partial(_conv31_kernel, seq=s),
        out_shape=jax.ShapeDtypeStruct((bsz, s, c), F32),
        grid=(bsz, c // LANE),
        in_specs=[pl.BlockSpec((1, s, LANE), lambda bi, ci: (bi, 0, ci)),
                  pl.BlockSpec((CF_KERNEL, LANE), lambda bi, ci: (0, ci)),
                  pl.BlockSpec((1, LANE), lambda bi, ci: (0, ci))],
        out_specs=pl.BlockSpec((1, s, LANE), lambda bi, ci: (bi, 0, ci)),
        scratch_shapes=[pltpu.VMEM((s + 2 * halo, LANE), F32)],
        compiler_params=_cparams(("parallel", "parallel")),
        name="conv31",
    )(u3, w, b.reshape(1, c))


def _route_kernel(x_ref, w_ref, sh_ref, sc_ref, rw_ref, rb_ref, h_ref, eid_ref, ew_ref):
    xf = x_ref[...]
    ms = jnp.mean(xf * xf, axis=-1, keepdims=True)
    h = xf * lax.rsqrt(ms + EPS) * w_ref[...]
    h = h * (1.0 + sc_ref[0]) + sh_ref[0]
    tm = xf.shape[0]
    nt = xf.shape[1] // LANE
    pitch = _pitch(nt)
    for j in range(nt):
        h_ref[pl.ds(j, tm, stride=pitch), :] = h[:, j * LANE:(j + 1) * LANE]
    for j in range(nt, pitch):
        h_ref[pl.ds(j, tm, stride=pitch), :] = jnp.zeros((tm, LANE), F32)
    logits = jnp.dot(h.astype(BF16), rw_ref[...], preferred_element_type=F32) + rb_ref[...]
    lane = lax.broadcasted_iota(jnp.int32, (tm, LANE), 1)
    lane_f = lane.astype(F32)
    ninf = -jnp.inf
    gl = jnp.where(lane < MOE_GROUPS, logits, ninf)
    gmax = jnp.max(gl, axis=-1, keepdims=True)
    gidx = jnp.min(jnp.where(gl == gmax, lane_f, float(LANE)), axis=-1, keepdims=True)
    gsum = jnp.sum(jnp.exp(gl - gmax), axis=-1, keepdims=True)
    g_p = 1.0 / gsum
    first = float(MOE_GROUPS) + gidx * float(EXPERTS_PER_GROUP)
    in_group = (lane_f >= first) & (lane_f < first + float(EXPERTS_PER_GROUP))
    el = jnp.where(in_group, logits, ninf)
    m1 = jnp.max(el, axis=-1, keepdims=True)
    i1 = jnp.min(jnp.where(el == m1, lane_f, float(LANE)), axis=-1, keepdims=True)
    el2 = jnp.where(lane_f == i1, ninf, el)
    m2 = jnp.max(el2, axis=-1, keepdims=True)
    i2 = jnp.min(jnp.where(el2 == m2, lane_f, float(LANE)), axis=-1, keepdims=True)
    e21 = jnp.exp(m2 - m1)
    den = 1.0 + e21
    w1 = (1.0 / den) * g_p
    w2 = (e21 / den) * g_p
    e1 = (i1 - float(MOE_GROUPS)).astype(jnp.int32)
    e2 = (i2 - float(MOE_GROUPS)).astype(jnp.int32)
    eid_ref[...] = jnp.where(lane == 0, e1, jnp.where(lane == 1, e2, 0))
    ew_ref[...] = jnp.where(lane == 0, w1, jnp.where(lane == 1, w2, 0.0))


def _route(x2, w, mod3, shift_chunk, scale_chunk, rows_per_batch, rw, rb, tm=256):
    m, d = x2.shape
    pitch = _pitch(d // LANE)
    tiles_per_batch = rows_per_batch // tm
    return pl.pallas_call(
        _route_kernel,
        out_shape=(jax.ShapeDtypeStruct((m * pitch, LANE), F32),
                   jax.ShapeDtypeStruct((m, LANE), jnp.int32),
                   jax.ShapeDtypeStruct((m, LANE), F32)),
        grid=(m // tm,),
        in_specs=[pl.BlockSpec((tm, d), lambda i: (i, 0)),
                  pl.BlockSpec((1, d), lambda i: (0, 0)),
                  pl.BlockSpec((1, 1, d), lambda i: (i // tiles_per_batch, 0, shift_chunk)),
                  pl.BlockSpec((1, 1, d), lambda i: (i // tiles_per_batch, 0, scale_chunk)),
                  pl.BlockSpec((d, LANE), lambda i: (0, 0)),
                  pl.BlockSpec((1, LANE), lambda i: (0, 0))],
        out_specs=(pl.BlockSpec((tm * pitch, LANE), lambda i: (i, 0)),
                   pl.BlockSpec((tm, LANE), lambda i: (i, 0)),
                   pl.BlockSpec((tm, LANE), lambda i: (i, 0))),
        compiler_params=_cparams(("parallel",)),
        name="route",
    )(x2, w.reshape(1, d), mod3, mod3, rw, rb)


_DMA_UNROLL = 8


def _rows_to_matrix(ref, tm, nt):
    return jnp.concatenate([ref[pl.ds(j, tm, stride=_pitch(nt)), :] for j in range(nt)], axis=1)


def _bulk_wait(src, dst, sem, total_rows):
    pltpu.make_async_copy(src.at[pl.ds(0, total_rows), :], dst.at[pl.ds(0, total_rows), :], sem).wait()


def _for_rows(n, body):
    groups = lax.shift_right_logical(n, _DMA_UNROLL.bit_length() - 1)

    def group(g, c):
        for u in range(_DMA_UNROLL):
            body(g * _DMA_UNROLL + u)
        return c

    def tail(r, c):
        body(r)
        return c

    lax.fori_loop(0, groups, group, 0)
    lax.fori_loop(groups * _DMA_UNROLL, n, tail, 0)


def _stream_expert_weights(b, be_ref, eord_ref, enext_ref, w_hbms, w_bufs, w_caches, wsem, both_queues):
    prev = jnp.maximum(b - 1, 0)

    def copies(e, slot):
        out = []
        for w, buf in zip(w_hbms, w_bufs):
            if both_queues:
                half = w.shape[1] // 2
                out.append((pltpu.make_async_copy(w.at[e, 0:half], buf.at[slot, 0:half], wsem.at[slot]), 1))
                out.append((pltpu.make_async_copy(w.at[e, half:], buf.at[slot, half:], wsem.at[slot]), 0))
            else:
                out.append((pltpu.make_async_copy(w.at[e], buf.at[slot], wsem.at[slot]), 1))
        return out

    @pl.when(b == 0)
    def _():
        for cp, prio in copies(be_ref[0], 0):
            cp.start(priority=prio)

    @pl.when((b == 0) | (be_ref[b] != be_ref[prev]))
    def _():
        for s in range(2):
            @pl.when((eord_ref[b] & 1) == s)
            def _(s=s):
                for cp, _ in copies(be_ref[b], s):
                    cp.wait()

                @pl.when(enext_ref[b] >= 0)
                def _():
                    for cp, prio in copies(enext_ref[b], 1 - s):
                        cp.start(priority=prio)

                for buf, cache in zip(w_bufs, w_caches):
                    cache[...] = buf[s].astype(BF16)


def _expert_up_kernel(be_ref, nused_ref, eord_ref, enext_ref, rowc_ref, rown_ref, h_hbm, wg_hbm, wu_hbm,
                      o_ref, xs0_ref, xs1_ref, wgs_ref, wus_ref, wgb_ref, wub_ref, sem, wsem, *, fchunk, nt):
    b = pl.program_id(0)
    n_used = nused_ref[0]
    dff = wgb_ref.shape[1]
    slots = (xs0_ref, xs1_ref)
    pitch = _pitch(nt)

    def gather_row(row_ref, r, slot):
        return pltpu.make_async_copy(h_hbm.at[pl.ds(row_ref[0, 0, r], nt), :],
                                     slots[slot].at[pl.ds(r * pitch, nt), :], sem.at[slot])

    @pl.when(b == 0)
    def _():
        _for_rows(MOE_BLOCK, lambda r: gather_row(rowc_ref, r, 0).start())

    for slot in range(2):
        @pl.when((b <= n_used) & (lax.rem(b, 2) == slot))
        def _(slot=slot):
            _bulk_wait(h_hbm, slots[slot], sem.at[slot], MOE_BLOCK * nt)

    @pl.when(b < n_used)
    def _():
        _stream_expert_weights(b, be_ref, eord_ref, enext_ref, (wg_hbm, wu_hbm), (wgs_ref, wus_ref),
                               (wgb_ref, wub_ref), wsem, both_queues=False)

        for slot in range(2):
            @pl.when(lax.rem(b, 2) == slot)
            def _(slot=slot):
                xb = _rows_to_matrix(slots[slot], MOE_BLOCK, nt).astype(BF16)
                nf = dff // fchunk
                per = MOE_BLOCK // (2 * nf)

                def request(part):
                    for r in range(part * per, (part + 1) * per):
                        gather_row(rown_ref, r, 1 - slot).start()

                for f in range(nf):
                    sl = slice(f * fchunk, (f + 1) * fchunk)
                    request(2 * f)
                    gate = jnp.dot(xb, wgb_ref[:, sl], preferred_element_type=F32)
                    request(2 * f + 1)
                    up = jnp.dot(xb, wub_ref[:, sl], preferred_element_type=F32)
                    o_ref[:, sl] = (_silu(gate) * up).astype(o_ref.dtype)

    @pl.when(b >= n_used)
    def _():
        o_ref[...] = jnp.zeros_like(o_ref)


def _expert_up(h2t, src_rows, w_gate, w_up, tables, fchunk=256):
    n_blocks = src_rows.shape[0]
    _, d, dff = w_gate.shape
    nt = d // LANE
    slot_rows = MOE_BLOCK * _pitch(nt)
    grid_spec = pltpu.PrefetchScalarGridSpec(
        num_scalar_prefetch=len(tables),
        grid=(n_blocks,),
        in_specs=[pl.BlockSpec((1, 1, MOE_BLOCK), lambda b, *_: (b, 0, 0), memory_space=pltpu.SMEM),
                  pl.BlockSpec((1, 1, MOE_BLOCK), lambda b, *_: (jnp.minimum(b + 1, n_blocks - 1), 0, 0),
                               memory_space=pltpu.SMEM),
                  pl.BlockSpec(memory_space=pl.ANY),
                  pl.BlockSpec(memory_space=pl.ANY),
                  pl.BlockSpec(memory_space=pl.ANY)],
        out_specs=pl.BlockSpec((MOE_BLOCK, dff), lambda b, *_: (b, 0)),
        scratch_shapes=[pltpu.VMEM((slot_rows, LANE), F32),
                        pltpu.VMEM((slot_rows, LANE), F32),
                        pltpu.VMEM((2, d, dff), F32),
                        pltpu.VMEM((2, d, dff), F32),
                        pltpu.VMEM((d, dff), BF16),
                        pltpu.VMEM((d, dff), BF16),
                        pltpu.SemaphoreType.DMA((2,)),
                        pltpu.SemaphoreType.DMA((2,))],
    )
    return pl.pallas_call(
        functools.partial(_expert_up_kernel, fchunk=fchunk, nt=nt),
        out_shape=jax.ShapeDtypeStruct((n_blocks * MOE_BLOCK, dff), BF16),
        grid_spec=grid_spec,
        compiler_params=_cparams(("arbitrary",)),
        name="expert_up",
    )(*tables, src_rows, src_rows, h2t, w_gate, w_up)


def _expert_down_kernel(be_ref, nused_ref, eord_ref, enext_ref, dst_ref, h_ref, wd_hbm, y_hbm,
                        ys0_ref, ys1_ref, wds_ref, wdb_ref, sem, wsem, *, nchunk, nt):
    b = pl.program_id(0)
    n_used = nused_ref[0]
    d = wdb_ref.shape[1]
    slots = (ys0_ref, ys1_ref)
    pitch = _pitch(nt)

    def scatter_row(r, slot):
        return pltpu.make_async_copy(slots[slot].at[pl.ds(r * pitch, pitch), :],
                                     y_hbm.at[pl.ds(dst_ref[0, 0, r], pitch), :], sem.at[slot])

    @pl.when(b == 0)
    def _():
        ys0_ref[...] = jnp.zeros_like(ys0_ref)
        ys1_ref[...] = jnp.zeros_like(ys1_ref)

    for slot in range(2):
        @pl.when((b >= 1) & (b <= n_used) & (lax.rem(b, 2) == slot))
        def _(slot=slot):
            _bulk_wait(slots[slot], y_hbm, sem.at[slot], MOE_BLOCK * pitch)

    @pl.when(b < n_used)
    def _():
        _stream_expert_weights(b, be_ref, eord_ref, enext_ref, (wd_hbm,), (wds_ref,), (wdb_ref,), wsem,
                               both_queues=True)
        hb = h_ref[...]
        for slot in range(2):
            @pl.when(lax.rem(b, 2) == slot)
            def _(slot=slot):
                nc = d // nchunk
                per = MOE_BLOCK // nc
                for c in range(nc):
                    for r in range(c * per, (c + 1) * per):
                        scatter_row(r, 1 - slot).start()
                    out = jnp.dot(hb, wdb_ref[:, c * nchunk:(c + 1) * nchunk], preferred_element_type=F32)
                    for j in range(nchunk // LANE):
                        slots[slot][pl.ds(c * (nchunk // LANE) + j, MOE_BLOCK, stride=pitch), :] = (
                            out[:, j * LANE:(j + 1) * LANE])

    for slot in range(2):
        @pl.when((b == n_used) & (lax.rem(b, 2) == slot))
        def _(slot=slot):
            _for_rows(MOE_BLOCK, lambda r: scatter_row(r, 1 - slot).start())
            _bulk_wait(slots[1 - slot], y_hbm, sem.at[1 - slot], MOE_BLOCK * pitch)


def _expert_down(hid, dst_rows, w_down, tables, y_slots, nchunk=256):
    n_rows, dff = hid.shape
    n_blocks = n_rows // MOE_BLOCK
    d = w_down.shape[2]
    nt = d // LANE
    pitch = _pitch(nt)
    grid_spec = pltpu.PrefetchScalarGridSpec(
        num_scalar_prefetch=len(tables),
        grid=(n_blocks,),
        in_specs=[pl.BlockSpec((1, 1, MOE_BLOCK), lambda b, *_: (b, 0, 0), memory_space=pltpu.SMEM),
                  pl.BlockSpec((MOE_BLOCK, dff), lambda b, be, n, *_: (jnp.minimum(b, n[0] - 1), 0)),
                  pl.BlockSpec(memory_space=pl.ANY)],
        out_specs=pl.BlockSpec(memory_space=pl.ANY),
        scratch_shapes=[pltpu.VMEM((MOE_BLOCK * pitch, LANE), F32),
                        pltpu.VMEM((MOE_BLOCK * pitch, LANE), F32),
                        pltpu.VMEM((2, dff, d), F32),
                        pltpu.VMEM((dff, d), BF16),
                        pltpu.SemaphoreType.DMA((2,)),
                        pltpu.SemaphoreType.DMA((2,))],
    )
    return pl.pallas_call(
        functools.partial(_expert_down_kernel, nchunk=nchunk, nt=nt),
        out_shape=jax.ShapeDtypeStruct((y_slots * pitch, LANE), F32),
        grid_spec=grid_spec,
        compiler_params=_cparams(("arbitrary",)),
        name="expert_down",
    )(*tables, dst_rows, hid, w_down)


def _combine_kernel(y0_ref, y1_ref, ew_ref, x_ref, g_ref, w_ref, o_ref):
    tm, d = x_ref.shape
    nt = d // LANE
    ew = ew_ref[...]
    moe = (_rows_to_matrix(y0_ref, tm, nt) * ew[:, 0:1]
           + _rows_to_matrix(y1_ref, tm, nt) * ew[:, 1:2])
    xo = x_ref[...] + g_ref[0] * moe
    ms = jnp.mean(xo * xo, axis=-1, keepdims=True)
    o_ref[...] = xo * lax.rsqrt(ms + EPS) * w_ref[...]


def _combine(y, ew, x2, mod3, gate_chunk, rows_per_batch, final_w, tm=256):
    m, d = x2.shape
    pitch = _pitch(d // LANE)
    tiles = m // tm
    tiles_per_batch = rows_per_batch // tm
    return pl.pallas_call(
        _combine_kernel,
        out_shape=jax.ShapeDtypeStruct((m, d), F32),
        grid=(tiles,),
        in_specs=[pl.BlockSpec((tm * pitch, LANE), lambda i: (i, 0)),
                  pl.BlockSpec((tm * pitch, LANE), lambda i: (tiles + i, 0)),
                  pl.BlockSpec((tm, LANE), lambda i: (i, 0)),
                  pl.BlockSpec((tm, d), lambda i: (i, 0)),
                  pl.BlockSpec((1, 1, d), lambda i: (i // tiles_per_batch, 0, gate_chunk)),
                  pl.BlockSpec((1, d), lambda i: (0, 0))],
        out_specs=pl.BlockSpec((tm, d), lambda i: (i, 0)),
        compiler_params=_cparams(("parallel",)),
        name="moe_combine",
    )(y, y, ew, x2, mod3, final_w.reshape(1, d))


def _dispatch_tables(eid, n_tok, pitch):
    top_k = eid.shape[1]
    n_assign = n_tok * top_k
    expert = eid.reshape(-1)
    key = jnp.sort(expert * n_assign + jnp.arange(n_assign, dtype=jnp.int32))
    sorted_assign = key % n_assign
    bounds = jnp.arange(N_EXPERTS + 1, dtype=jnp.int32) * n_assign
    start = jnp.sum((key[None, :] < bounds[:, None]).astype(jnp.int32), axis=1)
    counts = start[1:] - start[:-1]
    nblk = (counts + MOE_BLOCK - 1) // MOE_BLOCK
    blk_end = jnp.cumsum(nblk)
    blk_start = blk_end - nblk
    steps = -(-n_assign // MOE_BLOCK) + N_EXPERTS + 1
    bidx = jnp.arange(steps, dtype=jnp.int32)
    lane = jnp.arange(MOE_BLOCK, dtype=jnp.int32)[None, :]
    block_expert = jnp.minimum(jnp.sum((blk_end[None, :] <= bidx[:, None]).astype(jnp.int32), axis=1),
                               N_EXPERTS - 1)
    onehot = (block_expert[:, None] == jnp.arange(N_EXPERTS, dtype=jnp.int32)[None, :]).astype(jnp.int32)

    def lookup(table):
        return jnp.sum(onehot * table[None, :], axis=1)

    in_expert = (bidx - lookup(blk_start)) * MOE_BLOCK
    n_valid = jnp.clip(lookup(counts) - in_expert, 0, MOE_BLOCK)
    first_src = jnp.clip(lookup(start[:-1]) + in_expert, 0, n_assign)
    valid = lane < n_valid[:, None]
    padded = jnp.concatenate([sorted_assign, jnp.zeros((MOE_BLOCK,), jnp.int32)])
    assign = jax.vmap(lambda o: lax.dynamic_slice(padded, (o,), (MOE_BLOCK,)))(first_src)
    tok = assign // top_k
    src_rows = jnp.where(valid, tok, (bidx[:, None] * MOE_BLOCK + lane) % n_tok) * pitch
    dst_slot = jnp.where(valid, (assign % top_k) * n_tok + tok, top_k * n_tok + lane)
    dst_rows = jnp.concatenate([top_k * n_tok + lane, dst_slot[:-1]], axis=0) * pitch
    n_used = blk_end[-1].astype(jnp.int32)
    first = jnp.concatenate([jnp.ones((1,), jnp.int32),
                             (block_expert[1:] != block_expert[:-1]).astype(jnp.int32)])
    expert_ordinal = (jnp.cumsum(first) - 1).astype(jnp.int32)
    later = (jnp.arange(N_EXPERTS)[None, :] > block_expert[:, None]) & (counts[None, :] > 0)
    next_expert = jnp.min(jnp.where(later, jnp.arange(N_EXPERTS, dtype=jnp.int32)[None, :], N_EXPERTS), axis=1)
    next_expert = jnp.where(next_expert < N_EXPERTS, next_expert, -1).astype(jnp.int32)
    tables = (block_expert, n_used.reshape(1), expert_ordinal, next_expert)
    return (src_rows.astype(jnp.int32).reshape(steps, 1, MOE_BLOCK),
            dst_rows.astype(jnp.int32).reshape(steps, 1, MOE_BLOCK), tables)


def kernel(x, c, ctx, c_ctx, ada_w, ada_b, norm1_w, w_in, ssm_conv_w, ssm_conv_b, dt_bias, a_log, d_skip, ssm_norm_w, ssm_out_w, cf_dw_w, cf_dw_b, cf_ln_w, cf_ln_b, cf_out_w, cf_out_b, w_o, norm2_w, router_group_w, router_group_b, router_expert_w, router_expert_b, expert_w_gate, expert_w_up, expert_w_down, final_norm_w):
    bsz, seq, d = x.shape
    l_ctx = ctx.shape[1]
    n_tok = bsz * seq
    d_inner = ssm_norm_w.shape[1]
    gn = N_GROUPS * D_STATE
    xbc_dim = d_inner + 2 * gn
    off_dt = xbc_dim
    off_z = off_dt + N_HEADS
    off_glu = off_z + d_inner
    off_gate = off_glu + 2 * d

    ctx_row = bsz
    crows = jnp.zeros((8, d), F32).at[:bsz].set(c).at[ctx_row].set(c_ctx)
    mod = _ada(crows, ada_w[0], ada_b[0])
    mod3 = mod.reshape(8, 1, 6 * d)
    lat_rows = jnp.arange(bsz, dtype=jnp.int32)
    ctx_rows = jnp.full((bsz,), ctx_row, jnp.int32)

    h_lat = _normmod(x, norm1_w[0], mod3, lat_rows, 0, 1, BF16).reshape(n_tok, d)
    h_ctx = _normmod(ctx, norm1_w[0], mod3, ctx_rows, 0, 1, BF16).reshape(bsz * l_ctx, d)

    wt, (r_xbc, r_dt, r_z, r_glu, r_gate) = _pack_wt(
        jnp.transpose(w_in[0]),
        [(0, xbc_dim), (off_dt, off_z), (off_z, off_glu), (off_glu, off_gate), (off_gate, off_gate + 2 * d)])

    xbc_lat = _mm(h_lat, wt, tn=2048, name="in_xbc", rows=(r_xbc, xbc_dim)).reshape(bsz, seq, xbc_dim)
    xbc_ctx = _mm(h_ctx, wt, tm=512, name="in_xbc_ctx", rows=(r_xbc, xbc_dim)).reshape(bsz, l_ctx, xbc_dim)
    dt_lat = _mm(h_lat, wt, name="in_dt", rows=(r_dt, LANE)).reshape(bsz, seq, LANE)
    dt_ctx = _mm(h_ctx, wt, tm=512, name="in_dt_ctx", rows=(r_dt, LANE)).reshape(bsz, l_ctx, LANE)
    sz = _mm(h_lat, wt, act="silu", tn=2048, name="in_z", rows=(r_z, d_inner))
    u = _mm_glu(h_lat, wt, r_glu, d, tn=1024)

    xbc_act = _conv7(xbc_ctx, xbc_lat, ssm_conv_w[0], ssm_conv_b[0])

    def ssd_params(k):
        par = jnp.zeros((8, LANE), F32).at[0, :N_HEADS].set(dt_bias[0, k]).at[1, :N_HEADS].set(a_log[0, k])
        return par, jnp.repeat(d_skip[0, k], HEAD_DIM).reshape(N_HEADS // 2, 1, LANE)

    y_bwd = _ssd(xbc_act, dt_ctx, dt_lat, *ssd_params(1), reverse=True)
    gnorm = _ssd(xbc_act, dt_ctx, dt_lat, *ssd_params(0), reverse=False,
                 norm_with=(y_bwd, sz, ssm_norm_w[0]))
    y_ssd = _mm(gnorm, ssm_out_w[0].astype(BF16), tn=512, name="ssm_out")

    cv = _conv31(u.reshape(bsz, seq, d), cf_dw_w[0], cf_dw_b[0]).reshape(n_tok, d)
    merged = _mm_merge(cv, cf_ln_w[0], cf_ln_b[0], cf_out_w[0].astype(BF16), cf_out_b[0],
                       h_lat, wt, r_gate, y_ssd)
    x1 = _mm_resid(merged, w_o[0].astype(BF16), x.reshape(n_tok, d), mod3, 2, seq, tm=2048)

    n_r = MOE_GROUPS + N_EXPERTS
    rw = jnp.pad(jnp.concatenate([router_group_w[0], router_expert_w[0]], axis=1),
                 ((0, 0), (0, LANE - n_r))).astype(BF16)
    rb = jnp.pad(jnp.concatenate([router_group_b[0], router_expert_b[0]]), (0, LANE - n_r)).reshape(1, LANE)
    h2t, eid, ew = _route(x1, norm2_w[0], mod3, 3, 4, seq, rw, rb)

    src_rows, dst_rows, tables = _dispatch_tables(eid[:, :2], n_tok, _pitch(d // LANE))
    hid = _expert_up(h2t, src_rows, expert_w_gate[0], expert_w_up[0], tables)
    y = _expert_down(hid, dst_rows, expert_w_down[0], tables, 2 * n_tok + MOE_BLOCK)
    out = _combine(y, ew, x1, mod3, 5, seq, final_norm_w)
    return out.reshape(bsz, seq, d)
```

```python
import functools

import jax
import jax.numpy as jnp
from jax import lax
from jax.experimental import pallas as pl
from jax.experimental.pallas import tpu as pltpu

F32 = jnp.float32
BF16 = jnp.bfloat16

EPS = 1e-6
GRID_W = 64
HEAD_DIM = 64
N_HEADS = 64
N_GROUPS = 8
D_STATE = 128
CHUNK = 128
SSM_CONV = 7
CF_KERNEL = 31
MOE_GROUPS = 8
EXPERTS_PER_GROUP = 8
N_EXPERTS = 64
MOE_BLOCK = 256
LANE = 128
LOG2E = 1.4426950408889634
VMEM_LIMIT = 56 * 1024 * 1024


def _cparams(sem):
    return pltpu.CompilerParams(dimension_semantics=sem, vmem_limit_bytes=VMEM_LIMIT)


def _silu(v):
    return v * jax.nn.sigmoid(v)


def _pitch(nt):
    return nt + 1


def _ada_kernel(c_ref, w_ref, b_ref, o_ref):
    s = _silu(c_ref[...])
    o_ref[...] = jnp.dot(s.astype(BF16), w_ref[...].astype(BF16),
                         preferred_element_type=F32) + b_ref[...]


def _ada(crows, ada_w, ada_b, tn=1024):
    r, d = crows.shape
    n = ada_w.shape[1]
    return pl.pallas_call(
        _ada_kernel,
        out_shape=jax.ShapeDtypeStruct((r, n), F32),
        grid=(n // tn,),
        in_specs=[pl.BlockSpec((r, d), lambda j: (0, 0)),
                  pl.BlockSpec((d, tn), lambda j: (0, j)),
                  pl.BlockSpec((1, tn), lambda j: (0, j))],
        out_specs=pl.BlockSpec((r, tn), lambda j: (0, j)),
        compiler_params=_cparams(("parallel",)),
        name="ada",
    )(crows, ada_w, ada_b.reshape(1, n))


def _normmod_kernel(rows_ref, x_ref, w_ref, sh_ref, sc_ref, o_ref):
    del rows_ref
    xf = x_ref[0]
    ms = jnp.mean(xf * xf, axis=-1, keepdims=True)
    y = xf * lax.rsqrt(ms + EPS) * w_ref[...]
    o_ref[0] = (y * (1.0 + sc_ref[0]) + sh_ref[0]).astype(o_ref.dtype)


def _normmod(x3, w, mod3, rows, shift_chunk, scale_chunk, out_dtype, tm=256):
    bx, l, d = x3.shape
    grid_spec = pltpu.PrefetchScalarGridSpec(
        num_scalar_prefetch=1,
        grid=(bx, l // tm),
        in_specs=[pl.BlockSpec((1, tm, d), lambda b, i, r: (b, i, 0)),
                  pl.BlockSpec((1, d), lambda b, i, r: (0, 0)),
                  pl.BlockSpec((1, 1, d), lambda b, i, r: (r[b], 0, shift_chunk)),
                  pl.BlockSpec((1, 1, d), lambda b, i, r: (r[b], 0, scale_chunk))],
        out_specs=pl.BlockSpec((1, tm, d), lambda b, i, r: (b, i, 0)),
    )
    return pl.pallas_call(
        _normmod_kernel,
        out_shape=jax.ShapeDtypeStruct((bx, l, d), out_dtype),
        grid_spec=grid_spec,
        compiler_params=_cparams(("parallel", "parallel")),
        name="normmod",
    )(rows, x3, w.reshape(1, d), mod3, mod3)


def _dot_nt(a, wt):
    return lax.dot_general(a, wt, (((1,), (1,)), ((), ())), preferred_element_type=F32)


def _mm_kernel(a_ref, w_ref, *rest, act, has_bias, w_rows):
    o_ref = rest[-1]
    a = a_ref[...]
    acc = _dot_nt(a, w_ref[...]) if w_rows else jnp.dot(a, w_ref[...], preferred_element_type=F32)
    if has_bias:
        acc = acc + rest[0][...]
    if act == "silu":
        acc = _silu(acc)
    elif act == "sigmoid":
        acc = jax.nn.sigmoid(acc)
    o_ref[...] = acc.astype(o_ref.dtype)


def _mm(a, w, bias=None, act=None, out_dtype=F32, tm=1024, tn=1024, name="mm", rows=None):
    m, k = a.shape
    start, n = (0, w.shape[1]) if rows is None else rows
    tm, tn = min(tm, m), min(tn, n)
    j0 = start // tn
    w_spec = (pl.BlockSpec((k, tn), lambda i, j: (0, j)) if rows is None
              else pl.BlockSpec((tn, k), lambda i, j: (j0 + j, 0)))
    in_specs = [pl.BlockSpec((tm, k), lambda i, j: (i, 0)), w_spec]
    args = [a, w]
    if bias is not None:
        in_specs.append(pl.BlockSpec((1, tn), lambda i, j: (0, j)))
        args.append(bias.reshape(1, n))
    return pl.pallas_call(
        functools.partial(_mm_kernel, act=act, has_bias=bias is not None, w_rows=rows is not None),
        out_shape=jax.ShapeDtypeStruct((m, n), out_dtype),
        grid=(m // tm, n // tn),
        in_specs=in_specs,
        out_specs=pl.BlockSpec((tm, tn), lambda i, j: (i, j)),
        compiler_params=_cparams(("parallel", "parallel")),
        name=name,
    )(*args)


def _mm_glu_kernel(a_ref, wa_ref, wb_ref, o_ref):
    a = a_ref[...]
    va = _dot_nt(a, wa_ref[...])
    vb = _dot_nt(a, wb_ref[...])
    o_ref[...] = va * jax.nn.sigmoid(vb)


def _mm_glu(a, wt, start, n, tm=1024, tn=512):
    m, k = a.shape
    tm = min(tm, m)
    ja, jb = start // tn, (start + n) // tn
    return pl.pallas_call(
        _mm_glu_kernel,
        out_shape=jax.ShapeDtypeStruct((m, n), F32),
        grid=(m // tm, n // tn),
        in_specs=[pl.BlockSpec((tm, k), lambda i, j: (i, 0)),
                  pl.BlockSpec((tn, k), lambda i, j: (ja + j, 0)),
                  pl.BlockSpec((tn, k), lambda i, j: (jb + j, 0))],
        out_specs=pl.BlockSpec((tm, tn), lambda i, j: (i, j)),
        compiler_params=_cparams(("parallel", "parallel")),
        name="mm_glu",
    )(a, wt, wt)


def _mm_merge_kernel(cv_ref, lw_ref, lb_ref, w_ref, b_ref, h_ref, wga_ref, wgb_ref, ys_ref, o_ref, u_ref):
    @pl.when(pl.program_id(1) == 0)
    def _():
        xf = cv_ref[...]
        mu = jnp.mean(xf, axis=-1, keepdims=True)
        xc = xf - mu
        var = jnp.mean(xc * xc, axis=-1, keepdims=True)
        y = xc * lax.rsqrt(var + EPS) * lw_ref[...] + lb_ref[...]
        u_ref[...] = _silu(y).astype(u_ref.dtype)

    ycf = jnp.dot(u_ref[...], w_ref[...], preferred_element_type=F32) + b_ref[...]
    h = h_ref[...]
    gate_a = jax.nn.sigmoid(_dot_nt(h, wga_ref[...]))
    gate_b = jax.nn.sigmoid(_dot_nt(h, wgb_ref[...]))
    o_ref[...] = (gate_a * ys_ref[...] + gate_b * ycf).astype(o_ref.dtype)


def _mm_merge(cv, ln_w, ln_b, w, bias, h, wt, gate_start, y_ssd, tm=512, tn=512):
    m, k = cv.shape
    n = w.shape[1]
    tm = min(tm, m)
    ja, jb = gate_start // tn, (gate_start + n) // tn
    return pl.pallas_call(
        _mm_merge_kernel,
        out_shape=jax.ShapeDtypeStruct((m, n), BF16),
        grid=(m // tm, n // tn),
        in_specs=[pl.BlockSpec((tm, k), lambda i, j: (i, 0)),
                  pl.BlockSpec((1, k), lambda i, j: (0, 0)),
                  pl.BlockSpec((1, k), lambda i, j: (0, 0)),
                  pl.BlockSpec((k, tn), lambda i, j: (0, j)),
                  pl.BlockSpec((1, tn), lambda i, j: (0, j)),
                  pl.BlockSpec((tm, k), lambda i, j: (i, 0)),
                  pl.BlockSpec((tn, k), lambda i, j: (ja + j, 0)),
                  pl.BlockSpec((tn, k), lambda i, j: (jb + j, 0)),
                  pl.BlockSpec((tm, tn), lambda i, j: (i, j))],
        out_specs=pl.BlockSpec((tm, tn), lambda i, j: (i, j)),
        scratch_shapes=[pltpu.VMEM((tm, k), BF16)],
        compiler_params=_cparams(("parallel", "arbitrary")),
        name="mm_merge",
    )(cv, ln_w.reshape(1, k), ln_b.reshape(1, k), w, bias.reshape(1, n), h, wt, wt, y_ssd)


def _mm_resid_kernel(a_ref, w_ref, x_ref, g_ref, o_ref):
    out = jnp.dot(a_ref[...], w_ref[...], preferred_element_type=F32)
    o_ref[...] = x_ref[...] + g_ref[0] * out


def _mm_resid(a, w, x2, mod3, gate_chunk, rows_per_batch, tm=1024, tn=512):
    m, k = a.shape
    n = w.shape[1]
    tm = min(tm, rows_per_batch)
    nj = n // tn
    tiles_per_batch = rows_per_batch // tm
    return pl.pallas_call(
        _mm_resid_kernel,
        out_shape=jax.ShapeDtypeStruct((m, n), F32),
        grid=(m // tm, nj),
        in_specs=[pl.BlockSpec((tm, k), lambda i, j: (i, 0)),
                  pl.BlockSpec((k, tn), lambda i, j: (0, j)),
                  pl.BlockSpec((tm, tn), lambda i, j: (i, j)),
                  pl.BlockSpec((1, 1, tn),
                               lambda i, j: (i // tiles_per_batch, 0, gate_chunk * nj + j))],
        out_specs=pl.BlockSpec((tm, tn), lambda i, j: (i, j)),
        compiler_params=_cparams(("parallel", "parallel")),
        name="mm_resid",
    )(a, w, x2, mod3)


W_ALIGN = 2048


def _pack_wt_kernel(valid_ref, off_ref, w_ref, o_ref):
    del off_ref
    nrow = valid_ref[pl.program_id(0)]
    row = lax.broadcasted_iota(jnp.int32, o_ref.shape, 0)
    o_ref[...] = jnp.where(row < nrow, w_ref[...], 0.0).astype(o_ref.dtype)


def _pack_wt(wt, segments, tr=512):
    n, k = wt.shape
    starts, src_off, valid = [], [], []
    pos = 0
    for lo, hi in segments:
        pos = -(-pos // W_ALIGN) * W_ALIGN
        starts.append(pos)
        while len(src_off) < pos // tr:
            src_off.append(0)
            valid.append(0)
        for r in range(lo, hi, tr):
            src_off.append(min(r, n - tr))
            valid.append(min(tr, hi - r))
            assert r <= n - tr or hi - r == tr
        pos += -(-(hi - lo) // tr) * tr
    total = -(-pos // W_ALIGN) * W_ALIGN
    while len(src_off) < total // tr:
        src_off.append(0)
        valid.append(0)
    grid_spec = pltpu.PrefetchScalarGridSpec(
        num_scalar_prefetch=2,
        grid=(total // tr,),
        in_specs=[pl.BlockSpec((pl.Element(tr), pl.Element(k)), lambda t, v, off: (off[t] * 8, 0))],
        out_specs=pl.BlockSpec((tr, k), lambda t, v, off: (t, 0)),
    )
    packed = pl.pallas_call(
        _pack_wt_kernel,
        out_shape=jax.ShapeDtypeStruct((total, k), BF16),
        grid_spec=grid_spec,
        compiler_params=_cparams(("parallel",)),
        name="pack_wt",
    )(jnp.asarray(valid, jnp.int32), jnp.asarray(src_off, jnp.int32) // 8, wt)
    return packed, starts


_CONV_PAD = 8


def _conv7_kernel(ctx_ref, lat_ref, w_ref, b_ref, o_ref, pad_ref, *, l_ctx, l_lat):
    p = _CONV_PAD
    zeros = jnp.zeros((p, LANE), F32)
    off_ctx = p
    off_lat = 2 * p + l_ctx
    pad_ref[0:p, :] = zeros
    pad_ref[off_ctx + l_ctx:off_lat, :] = zeros
    pad_ref[off_lat + l_lat:off_lat + l_lat + p, :] = zeros
    pad_ref[off_ctx:off_ctx + l_ctx, :] = ctx_ref[0]
    pad_ref[off_lat:off_lat + l_lat, :] = lat_ref[0]
    reach = SSM_CONV // 2
    bias = b_ref[...]

    def chunk(pad_base, out_base):
        acc = jnp.broadcast_to(bias, (CHUNK, LANE))
        for k in range(SSM_CONV):
            tap = pad_ref[pl.ds(pad_base - reach + k, CHUNK), :]
            acc = acc + tap * w_ref[k:k + 1, :]
        o_ref[0, 0, pl.ds(out_base, CHUNK), :] = _silu(acc)

    def ctx_body(j, c):
        base = pl.multiple_of(j * CHUNK, CHUNK)
        chunk(off_ctx + base, base)
        return c

    def lat_body(j, c):
        base = pl.multiple_of(j * CHUNK, CHUNK)
        chunk(off_lat + base, l_ctx + base)
        return c

    lax.fori_loop(0, l_ctx // CHUNK, ctx_body, 0)
    lax.fori_loop(0, l_lat // CHUNK, lat_body, 0, unroll=2)


def _conv7(ctx_raw, lat_raw, w, b):
    bsz, l_ctx, c = ctx_raw.shape
    l_lat = lat_raw.shape[1]
    ltot = l_ctx + l_lat
    nct = c // LANE
    return pl.pallas_call(
        functools.partial(_conv7_kernel, l_ctx=l_ctx, l_lat=l_lat),
        out_shape=jax.ShapeDtypeStruct((bsz, nct, ltot, LANE), F32),
        grid=(bsz, nct),
        in_specs=[pl.BlockSpec((1, l_ctx, LANE), lambda bi, ci: (bi, 0, ci)),
                  pl.BlockSpec((1, l_lat, LANE), lambda bi, ci: (bi, 0, ci)),
                  pl.BlockSpec((SSM_CONV, LANE), lambda bi, ci: (0, ci)),
                  pl.BlockSpec((1, LANE), lambda bi, ci: (0, ci))],
        out_specs=pl.BlockSpec((1, 1, ltot, LANE), lambda bi, ci: (bi, ci, 0, 0)),
        scratch_shapes=[pltpu.VMEM((ltot + 3 * _CONV_PAD, LANE), F32)],
        compiler_params=_cparams(("parallel", "parallel")),
        name="conv7",
    )(ctx_raw, lat_raw, w, b.reshape(1, c))


def _ssd_kernel(xbc_ref, dtc_ref, dtl_ref, par_ref, dexp_ref, ex_ref, *rest, reverse, n_ctx, fuse_norm):
    if fuse_norm:
        yo_ref, sz_ref, nw_ref, o_ref, st_ref, cumt_ref, y_ref = rest
    else:
        y_ref, st_ref, cumt_ref = rest
    i = pl.program_id(1)

    @pl.when(i == 0)
    def _():
        st_ref[...] = jnp.zeros_like(st_ref)

    dt_raw = jnp.where(i < n_ctx, dtc_ref[0], dtl_ref[0])
    bias = par_ref[0:1, :]
    a = -jnp.exp(par_ref[1:2, :])
    dt = jax.nn.softplus(dt_raw + bias)
    cum = dt * a
    row = lax.broadcasted_iota(jnp.int32, (CHUNK, LANE), 0)
    k = 1
    while k < CHUNK:
        if reverse:
            cum = cum + jnp.where(row < CHUNK - k, pltpu.roll(cum, CHUNK - k, 0), 0.0)
        else:
            cum = cum + jnp.where(row >= k, pltpu.roll(cum, k, 0), 0.0)
        k *= 2
    last = 0 if reverse else CHUNK - 1
    cum = cum * LOG2E
    cumt_ref[...] = cum.T
    li = lax.broadcasted_iota(jnp.int32, (CHUNK, CHUNK), 0)
    si = lax.broadcasted_iota(jnp.int32, (CHUNK, CHUNK), 1)
    causal = (li <= si) if reverse else (li >= si)
    lo = lax.broadcasted_iota(jnp.int32, (CHUNK, LANE), 1) < HEAD_DIM
    heads_per_group = N_HEADS // N_GROUPS
    pairs = heads_per_group // 2
    x_tiles = N_HEADS // 2

    def group(g, carry):
        shift = (LANE - heads_per_group * g) & (LANE - 1)
        cum_g = pltpu.roll(cum, shift, 1)
        dt_g = pltpu.roll(dt, shift, 1)
        cum_t = cumt_ref[pl.ds(pl.multiple_of(heads_per_group * g, heads_per_group), heads_per_group), :]
        bb = xbc_ref[0, x_tiles + g].astype(BF16)
        cb = xbc_ref[0, x_tiles + N_GROUPS + g].astype(BF16)
        scores = lax.dot_general(cb, bb, (((1,), (1,)), ((), ())), preferred_element_type=F32)
        h_t = st_ref[g]
        y_off = jnp.dot(cb, h_t.astype(BF16), preferred_element_type=F32)
        d_hi = dt_g.astype(BF16)
        r_hi = dt_g - d_hi.astype(F32)
        d_mid = r_hi.astype(BF16)
        d_lo = (r_hi - d_mid.astype(F32)).astype(BF16)
        dt_x = (jnp.dot(jnp.concatenate([d_hi, d_mid], axis=1), ex_ref[...], preferred_element_type=F32)
                + jnp.dot(d_lo, ex_ref[0:LANE, :], preferred_element_type=F32))
        xw_parts, dec_parts = [], []
        for p in range(pairs):
            j0, j1 = 2 * p, 2 * p + 1
            x2 = xbc_ref[0, pairs * g + p]
            c0 = cum_g[:, j0:j0 + 1]
            c1 = cum_g[:, j1:j1 + 1]
            l0 = jnp.exp2(jnp.where(causal, c0 - cum_t[j0:j0 + 1, :], -jnp.inf))
            l1 = jnp.exp2(jnp.where(causal, c1 - cum_t[j1:j1 + 1, :], -jnp.inf))
            m0 = (scores * l0).astype(BF16)
            m1 = (scores * l1).astype(BF16)
            dt2 = dt_x[:, p * LANE:(p + 1) * LANE]
            c2 = jnp.where(lo, c0, c1)
            xdt = x2 * dt2
            xdt_b = xdt.astype(BF16)
            zero = jnp.zeros_like(xdt_b)
            y_diag = jnp.dot(jnp.concatenate([m0, m1], axis=1),
                             jnp.concatenate([jnp.where(lo, xdt_b, zero), jnp.where(lo, zero, xdt_b)], axis=0),
                             preferred_element_type=F32)
            e2 = jnp.exp2(c2)
            y = y_diag + y_off[:, p * LANE:(p + 1) * LANE] * e2
            y_ref[0, pairs * g + p] = y + dexp_ref[pairs * g + p] * x2
            to_end = jnp.exp2(c2[last:last + 1, :] - c2)
            xw_parts.append((xdt * to_end).astype(BF16))
            dec_parts.append(e2[last:last + 1, :])
        xw = jnp.concatenate(xw_parts, axis=1)
        dec = jnp.concatenate(dec_parts, axis=1)
        upd = lax.dot_general(bb, xw, (((0,), (0,)), ((), ())), preferred_element_type=F32)
        st_ref[g] = h_t * dec + upd
        return carry

    lax.fori_loop(0, N_GROUPS, group, 0, unroll=2)

    if fuse_norm:
        sq = jnp.zeros((CHUNK, LANE), F32)
        for j in range(x_tiles):
            gj = (y_ref[0, j] + yo_ref[0, j]) * sz_ref[:, j * LANE:(j + 1) * LANE]
            y_ref[0, j] = gj
            sq = sq + gj * gj
        r = lax.rsqrt(jnp.sum(sq, axis=-1, keepdims=True) / (x_tiles * LANE) + EPS)
        for j in range(x_tiles):
            sl = slice(j * LANE, (j + 1) * LANE)
            o_ref[:, sl] = (y_ref[0, j] * r * nw_ref[:, sl]).astype(o_ref.dtype)


def _ssd(xbc_act, dt_ctx, dt_lat, par, dexp, reverse, norm_with=None):
    bsz, ntile, ltot, _ = xbc_act.shape
    l_ctx = dt_ctx.shape[1]
    l_lat = dt_lat.shape[1]
    n_ctx = l_ctx // CHUNK
    n_lat = l_lat // CHUNK
    steps = n_ctx + n_lat
    x_tiles = N_HEADS // 2
    gw = (N_HEADS // N_GROUPS) * HEAD_DIM
    e1 = (jnp.arange(gw)[None, :] // HEAD_DIM == jnp.arange(LANE)[:, None]).astype(BF16)
    expand = jnp.concatenate([e1, e1], axis=0)

    if reverse:
        def cat_chunk(i):
            return jnp.where(i < n_ctx, n_ctx - 1 - i, n_ctx + steps - 1 - i)

        def ctx_chunk(i):
            return jnp.maximum(n_ctx - 1 - i, 0)

        def lat_chunk(i):
            return jnp.minimum(steps - 1 - i, n_lat - 1)
    else:
        def cat_chunk(i):
            return i

        def ctx_chunk(i):
            return jnp.minimum(i, n_ctx - 1)

        def lat_chunk(i):
            return jnp.maximum(i - n_ctx, 0)

    y_spec = pl.BlockSpec((1, x_tiles, CHUNK, LANE), lambda b, i: (b, 0, lat_chunk(i), 0))
    in_specs = [pl.BlockSpec((1, ntile, CHUNK, LANE), lambda b, i: (b, 0, cat_chunk(i), 0)),
                pl.BlockSpec((1, CHUNK, LANE), lambda b, i: (b, ctx_chunk(i), 0)),
                pl.BlockSpec((1, CHUNK, LANE), lambda b, i: (b, lat_chunk(i), 0)),
                pl.BlockSpec((8, LANE), lambda b, i: (0, 0)),
                pl.BlockSpec((x_tiles, 1, LANE), lambda b, i: (0, 0, 0)),
                pl.BlockSpec((2 * LANE, gw), lambda b, i: (0, 0))]
    args = [xbc_act, dt_ctx, dt_lat, par, dexp, expand]
    scratch = [pltpu.VMEM((N_GROUPS, D_STATE, gw), F32), pltpu.VMEM((LANE, CHUNK), F32)]
    if norm_with is None:
        out_shape = jax.ShapeDtypeStruct((bsz, x_tiles, l_lat, LANE), F32)
        out_spec = y_spec
    else:
        y_other, silu_z, norm_w = norm_with
        dn = x_tiles * LANE
        row_spec = pl.BlockSpec((CHUNK, dn), lambda b, i: (b * n_lat + lat_chunk(i), 0))
        in_specs += [y_spec, row_spec, pl.BlockSpec((1, dn), lambda b, i: (0, 0))]
        args += [y_other, silu_z, norm_w.reshape(1, dn)]
        out_shape = jax.ShapeDtypeStruct((bsz * l_lat, dn), BF16)
        out_spec = row_spec
        scratch.append(pltpu.VMEM((1, x_tiles, CHUNK, LANE), F32))
    return pl.pallas_call(
        functools.partial(_ssd_kernel, reverse=reverse, n_ctx=n_ctx, fuse_norm=norm_with is not None),
        out_shape=out_shape,
        grid=(bsz, steps),
        in_specs=in_specs,
        out_specs=out_spec,
        scratch_shapes=scratch,
        compiler_params=_cparams(("parallel", "arbitrary")),
        name="ssd_bwd" if reverse else "ssd_fwd",
    )(*args)


def _conv31_kernel(u_ref, w_ref, b_ref, o_ref, pad_ref, *, seq):
    halo = (CF_KERNEL // 2) * GRID_W
    zeros = jnp.zeros((halo, LANE), F32)
    pad_ref[0:halo, :] = zeros
    pad_ref[halo + seq:halo + seq + halo, :] = zeros
    pad_ref[halo:halo + seq, :] = u_ref[0]
    bias = b_ref[...]

    def body(j, c):
        base = pl.multiple_of(j * CHUNK, CHUNK)
        acc = jnp.broadcast_to(bias, (CHUNK, LANE))
        for k in range(CF_KERNEL):
            tap = pad_ref[pl.ds(pl.multiple_of(base + k * GRID_W, GRID_W), CHUNK), :]
            acc = acc + tap * w_ref[k:k + 1, :]
        o_ref[0, pl.ds(base, CHUNK), :] = acc
        return c

    lax.fori_loop(0, seq // CHUNK, body, 0, unroll=2)


def _conv31(u3, w, b):
    bsz, s, c = u3.shape
    halo = (CF_KERNEL // 2) * GRID_W
    return pl.pallas_call(
        functools.partial(_conv31_kernel, seq=s),
        out_shape=jax.ShapeDtypeStruct((bsz, s, c), F32),
        grid=(bsz, c // LANE),
        in_specs=[pl.BlockSpec((1, s, LANE), lambda bi, ci: (bi, 0, ci)),
                  pl.BlockSpec((CF_KERNEL, LANE), lambda bi, ci: (0, ci)),
                  pl.BlockSpec((1, LANE), lambda bi, ci: (0, ci))],
        out_specs=pl.BlockSpec((1, s, LANE), lambda bi, ci: (bi, 0, ci)),
        scratch_shapes=[pltpu.VMEM((s + 2 * halo, LANE), F32)],
        compiler_params=_cparams(("parallel", "parallel")),
        name="conv31",
    )(u3, w, b.reshape(1, c))


def _route_kernel(x_ref, w_ref, sh_ref, sc_ref, rw_ref, rb_ref, h_ref, eid_ref, ew_ref):
    xf = x_ref[...]
    ms = jnp.mean(xf * xf, axis=-1, keepdims=True)
    h = xf * lax.rsqrt(ms + EPS) * w_ref[...]
    h = h * (1.0 + sc_ref[0]) + sh_ref[0]
    tm = xf.shape[0]
    nt = xf.shape[1] // LANE
    pitch = _pitch(nt)
    for j in range(nt):
        h_ref[pl.ds(j, tm, stride=pitch), :] = h[:, j * LANE:(j + 1) * LANE]
    for j in range(nt, pitch):
        h_ref[pl.ds(j, tm, stride=pitch), :] = jnp.zeros((tm, LANE), F32)
    logits = jnp.dot(h.astype(BF16), rw_ref[...], preferred_element_type=F32) + rb_ref[...]
    lane = lax.broadcasted_iota(jnp.int32, (tm, LANE), 1)
    lane_f = lane.astype(F32)
    ninf = -jnp.inf
    gl = jnp.where(lane < MOE_GROUPS, logits, ninf)
    gmax = jnp.max(gl, axis=-1, keepdims=True)
    gidx = jnp.min(jnp.where(gl == gmax, lane_f, float(LANE)), axis=-1, keepdims=True)
    gsum = jnp.sum(jnp.exp(gl - gmax), axis=-1, keepdims=True)
    g_p = 1.0 / gsum
    first = float(MOE_GROUPS) + gidx * float(EXPERTS_PER_GROUP)
    in_group = (lane_f >= first) & (lane_f < first + float(EXPERTS_PER_GROUP))
    el = jnp.where(in_group, logits, ninf)
    m1 = jnp.max(el, axis=-1, keepdims=True)
    i1 = jnp.min(jnp.where(el == m1, lane_f, float(LANE)), axis=-1, keepdims=True)
    el2 = jnp.where(lane_f == i1, ninf, el)
    m2 = jnp.max(el2, axis=-1, keepdims=True)
    i2 = jnp.min(jnp.where(el2 == m2, lane_f, float(LANE)), axis=-1, keepdims=True)
    e21 = jnp.exp(m2 - m1)
    den = 1.0 + e21
    w1 = (1.0 / den) * g_p
    w2 = (e21 / den) * g_p
    e1 = (i1 - float(MOE_GROUPS)).astype(jnp.int32)
    e2 = (i2 - float(MOE_GROUPS)).astype(jnp.int32)
    eid_ref[...] = jnp.where(lane == 0, e1, jnp.where(lane == 1, e2, 0))
    ew_ref[...] = jnp.where(lane == 0, w1, jnp.where(lane == 1, w2, 0.0))


def _route(x2, w, mod3, shift_chunk, scale_chunk, rows_per_batch, rw, rb, tm=256):
    m, d = x2.shape
    pitch = _pitch(d // LANE)
    tiles_per_batch = rows_per_batch // tm
    return pl.pallas_call(
        _route_kernel,
        out_shape=(jax.ShapeDtypeStruct((m * pitch, LANE), F32),
                   jax.ShapeDtypeStruct((m, LANE), jnp.int32),
                   jax.ShapeDtypeStruct((m, LANE), F32)),
        grid=(m // tm,),
        in_specs=[pl.BlockSpec((tm, d), lambda i: (i, 0)),
                  pl.BlockSpec((1, d), lambda i: (0, 0)),
                  pl.BlockSpec((1, 1, d), lambda i: (i // tiles_per_batch, 0, shift_chunk)),
                  pl.BlockSpec((1, 1, d), lambda i: (i // tiles_per_batch, 0, scale_chunk)),
                  pl.BlockSpec((d, LANE), lambda i: (0, 0)),
                  pl.BlockSpec((1, LANE), lambda i: (0, 0))],
        out_specs=(pl.BlockSpec((tm * pitch, LANE), lambda i: (i, 0)),
                   pl.BlockSpec((tm, LANE), lambda i: (i, 0)),
                   pl.BlockSpec((tm, LANE), lambda i: (i, 0))),
        compiler_params=_cparams(("parallel",)),
        name="route",
    )(x2, w.reshape(1, d), mod3, mod3, rw, rb)


_DMA_UNROLL = 8


def _rows_to_matrix(ref, tm, nt):
    return jnp.concatenate([ref[pl.ds(j, tm, stride=_pitch(nt)), :] for j in range(nt)], axis=1)


def _bulk_wait(src, dst, sem, total_rows):
    pltpu.make_async_copy(src.at[pl.ds(0, total_rows), :], dst.at[pl.ds(0, total_rows), :], sem).wait()


def _for_rows(n, body):
    groups = lax.shift_right_logical(n, _DMA_UNROLL.bit_length() - 1)

    def group(g, c):
        for u in range(_DMA_UNROLL):
            body(g * _DMA_UNROLL + u)
        return c

    def tail(r, c):
        body(r)
        return c

    lax.fori_loop(0, groups, group, 0)
    lax.fori_loop(groups * _DMA_UNROLL, n, tail, 0)


def _stream_expert_weights(b, be_ref, eord_ref, enext_ref, w_hbms, w_bufs, w_caches, wsem, both_queues):
    prev = jnp.maximum(b - 1, 0)

    def copies(e, slot):
        out = []
        for w, buf in zip(w_hbms, w_bufs):
            if both_queues:
                half = w.shape[1] // 2
                out.append((pltpu.make_async_copy(w.at[e, 0:half], buf.at[slot, 0:half], wsem.at[slot]), 1))
                out.append((pltpu.make_async_copy(w.at[e, half:], buf.at[slot, half:], wsem.at[slot]), 0))
            else:
                out.append((pltpu.make_async_copy(w.at[e], buf.at[slot], wsem.at[slot]), 1))
        return out

    @pl.when(b == 0)
    def _():
        for cp, prio in copies(be_ref[0], 0):
            cp.start(priority=prio)

    @pl.when((b == 0) | (be_ref[b] != be_ref[prev]))
    def _():
        for s in range(2):
            @pl.when((eord_ref[b] & 1) == s)
            def _(s=s):
                for cp, _ in copies(be_ref[b], s):
                    cp.wait()

                @pl.when(enext_ref[b] >= 0)
                def _():
                    for cp, prio in copies(enext_ref[b], 1 - s):
                        cp.start(priority=prio)

                for buf, cache in zip(w_bufs, w_caches):
                    cache[...] = buf[s].astype(BF16)


def _expert_up_kernel(be_ref, nused_ref, eord_ref, enext_ref, rowc_ref, rown_ref, h_hbm, wg_hbm, wu_hbm,
                      o_ref, xs0_ref, xs1_ref, wgs_ref, wus_ref, wgb_ref, wub_ref, sem, wsem, *, fchunk, nt):
    b = pl.program_id(0)
    n_used = nused_ref[0]
    dff = wgb_ref.shape[1]
    slots = (xs0_ref, xs1_ref)
    pitch = _pitch(nt)

    def gather_row(row_ref, r, slot):
        return pltpu.make_async_copy(h_hbm.at[pl.ds(row_ref[0, 0, r], nt), :],
                                     slots[slot].at[pl.ds(r * pitch, nt), :], sem.at[slot])

    @pl.when(b == 0)
    def _():
        _for_rows(MOE_BLOCK, lambda r: gather_row(rowc_ref, r, 0).start())

    for slot in range(2):
        @pl.when((b <= n_used) & (lax.rem(b, 2) == slot))
        def _(slot=slot):
            _bulk_wait(h_hbm, slots[slot], sem.at[slot], MOE_BLOCK * nt)

    @pl.when(b < n_used)
    def _():
        _stream_expert_weights(b, be_ref, eord_ref, enext_ref, (wg_hbm, wu_hbm), (wgs_ref, wus_ref),
                               (wgb_ref, wub_ref), wsem, both_queues=False)

        for slot in range(2):
            @pl.when(lax.rem(b, 2) == slot)
            def _(slot=slot):
                xb = _rows_to_matrix(slots[slot], MOE_BLOCK, nt).astype(BF16)
                nf = dff // fchunk
                per = MOE_BLOCK // (2 * nf)

                def request(part):
                    for r in range(part * per, (part + 1) * per):
                        gather_row(rown_ref, r, 1 - slot).start()

                for f in range(nf):
                    sl = slice(f * fchunk, (f + 1) * fchunk)
                    request(2 * f)
                    gate = jnp.dot(xb, wgb_ref[:, sl], preferred_element_type=F32)
                    request(2 * f + 1)
                    up = jnp.dot(xb, wub_ref[:, sl], preferred_element_type=F32)
                    o_ref[:, sl] = (_silu(gate) * up).astype(o_ref.dtype)

    @pl.when(b >= n_used)
    def _():
        o_ref[...] = jnp.zeros_like(o_ref)


def _expert_up(h2t, src_rows, w_gate, w_up, tables, fchunk=256):
    n_blocks = src_rows.shape[0]
    _, d, dff = w_gate.shape
    nt = d // LANE
    slot_rows = MOE_BLOCK * _pitch(nt)
    grid_spec = pltpu.PrefetchScalarGridSpec(
        num_scalar_prefetch=len(tables),
        grid=(n_blocks,),
        in_specs=[pl.BlockSpec((1, 1, MOE_BLOCK), lambda b, *_: (b, 0, 0), memory_space=pltpu.SMEM),
                  pl.BlockSpec((1, 1, MOE_BLOCK), lambda b, *_: (jnp.minimum(b + 1, n_blocks - 1), 0, 0),
                               memory_space=pltpu.SMEM),
                  pl.BlockSpec(memory_space=pl.ANY),
                  pl.BlockSpec(memory_space=pl.ANY),
                  pl.BlockSpec(memory_space=pl.ANY)],
        out_specs=pl.BlockSpec((MOE_BLOCK, dff), lambda b, *_: (b, 0)),
        scratch_shapes=[pltpu.VMEM((slot_rows, LANE), F32),
                        pltpu.VMEM((slot_rows, LANE), F32),
                        pltpu.VMEM((2, d, dff), F32),
                        pltpu.VMEM((2, d, dff), F32),
                        pltpu.VMEM((d, dff), BF16),
                        pltpu.VMEM((d, dff), BF16),
                        pltpu.SemaphoreType.DMA((2,)),
                        pltpu.SemaphoreType.DMA((2,))],
    )
    return pl.pallas_call(
        functools.partial(_expert_up_kernel, fchunk=fchunk, nt=nt),
        out_shape=jax.ShapeDtypeStruct((n_blocks * MOE_BLOCK, dff), BF16),
        grid_spec=grid_spec,
        compiler_params=_cparams(("arbitrary",)),
        name="expert_up",
    )(*tables, src_rows, src_rows, h2t, w_gate, w_up)


def _expert_down_kernel(be_ref, nused_ref, eord_ref, enext_ref, dst_ref, h_ref, wd_hbm, y_hbm,
                        ys0_ref, ys1_ref, wds_ref, wdb_ref, sem, wsem, *, nchunk, nt):
    b = pl.program_id(0)
    n_used = nused_ref[0]
    d = wdb_ref.shape[1]
    slots = (ys0_ref, ys1_ref)
    pitch = _pitch(nt)

    def scatter_row(r, slot):
        return pltpu.make_async_copy(slots[slot].at[pl.ds(r * pitch, pitch), :],
                                     y_hbm.at[pl.ds(dst_ref[0, 0, r], pitch), :], sem.at[slot])

    @pl.when(b == 0)
    def _():
        ys0_ref[...] = jnp.zeros_like(ys0_ref)
        ys1_ref[...] = jnp.zeros_like(ys1_ref)

    for slot in range(2):
        @pl.when((b >= 1) & (b <= n_used) & (lax.rem(b, 2) == slot))
        def _(slot=slot):
            _bulk_wait(slots[slot], y_hbm, sem.at[slot], MOE_BLOCK * pitch)

    @pl.when(b < n_used)
    def _():
        _stream_expert_weights(b, be_ref, eord_ref, enext_ref, (wd_hbm,), (wds_ref,), (wdb_ref,), wsem,
                               both_queues=True)
        hb = h_ref[...]
        for slot in range(2):
            @pl.when(lax.rem(b, 2) == slot)
            def _(slot=slot):
                nc = d // nchunk
                per = MOE_BLOCK // nc
                for c in range(nc):
                    for r in range(c * per, (c + 1) * per):
                        scatter_row(r, 1 - slot).start()
                    out = jnp.dot(hb, wdb_ref[:, c * nchunk:(c + 1) * nchunk], preferred_element_type=F32)
                    for j in range(nchunk // LANE):
                        slots[slot][pl.ds(c * (nchunk // LANE) + j, MOE_BLOCK, stride=pitch), :] = (
                            out[:, j * LANE:(j + 1) * LANE])

    for slot in range(2):
        @pl.when((b == n_used) & (lax.rem(b, 2) == slot))
        def _(slot=slot):
            _for_rows(MOE_BLOCK, lambda r: scatter_row(r, 1 - slot).start())
            _bulk_wait(slots[1 - slot], y_hbm, sem.at[1 - slot], MOE_BLOCK * pitch)


def _expert_down(hid, dst_rows, w_down, tables, y_slots, nchunk=256):
    n_rows, dff = hid.shape
    n_blocks = n_rows // MOE_BLOCK
    d = w_down.shape[2]
    nt = d // LANE
    pitch = _pitch(nt)
    grid_spec = pltpu.PrefetchScalarGridSpec(
        num_scalar_prefetch=len(tables),
        grid=(n_blocks,),
        in_specs=[pl.BlockSpec((1, 1, MOE_BLOCK), lambda b, *_: (b, 0, 0), memory_space=pltpu.SMEM),
                  pl.BlockSpec((MOE_BLOCK, dff), lambda b, be, n, *_: (jnp.minimum(b, n[0] - 1), 0)),
                  pl.BlockSpec(memory_space=pl.ANY)],
        out_specs=pl.BlockSpec(memory_space=pl.ANY),
        scratch_shapes=[pltpu.VMEM((MOE_BLOCK * pitch, LANE), F32),
                        pltpu.VMEM((MOE_BLOCK * pitch, LANE), F32),
                        pltpu.VMEM((2, dff, d), F32),
                        pltpu.VMEM((dff, d), BF16),
                        pltpu.SemaphoreType.DMA((2,)),
                        pltpu.SemaphoreType.DMA((2,))],
    )
    return pl.pallas_call(
        functools.partial(_expert_down_kernel, nchunk=nchunk, nt=nt),
        out_shape=jax.ShapeDtypeStruct((y_slots * pitch, LANE), F32),
        grid_spec=grid_spec,
        compiler_params=_cparams(("arbitrary",)),
        name="expert_down",
    )(*tables, dst_rows, hid, w_down)


def _combine_kernel(y0_ref, y1_ref, ew_ref, x_ref, g_ref, w_ref, o_ref):
    tm, d = x_ref.shape
    nt = d // LANE
    ew = ew_ref[...]
    moe = (_rows_to_matrix(y0_ref, tm, nt) * ew[:, 0:1]
           + _rows_to_matrix(y1_ref, tm, nt) * ew[:, 1:2])
    xo = x_ref[...] + g_ref[0] * moe
    ms = jnp.mean(xo * xo, axis=-1, keepdims=True)
    o_ref[...] = xo * lax.rsqrt(ms + EPS) * w_ref[...]


def _combine(y, ew, x2, mod3, gate_chunk, rows_per_batch, final_w, tm=256):
    m, d = x2.shape
    pitch = _pitch(d // LANE)
    tiles = m // tm
    tiles_per_batch = rows_per_batch // tm
    return pl.pallas_call(
        _combine_kernel,
        out_shape=jax.ShapeDtypeStruct((m, d), F32),
        grid=(tiles,),
        in_specs=[pl.BlockSpec((tm * pitch, LANE), lambda i: (i, 0)),
                  pl.BlockSpec((tm * pitch, LANE), lambda i: (tiles + i, 0)),
                  pl.BlockSpec((tm, LANE), lambda i: (i, 0)),
                  pl.BlockSpec((tm, d), lambda i: (i, 0)),
                  pl.BlockSpec((1, 1, d), lambda i: (i // tiles_per_batch, 0, gate_chunk)),
                  pl.BlockSpec((1, d), lambda i: (0, 0))],
        out_specs=pl.BlockSpec((tm, d), lambda i: (i, 0)),
        compiler_params=_cparams(("parallel",)),
        name="moe_combine",
    )(y, y, ew, x2, mod3, final_w.reshape(1, d))


def _dispatch_tables(eid, n_tok, pitch):
    top_k = eid.shape[1]
    n_assign = n_tok * top_k
    expert = eid.reshape(-1)
    key = jnp.sort(expert * n_assign + jnp.arange(n_assign, dtype=jnp.int32))
    sorted_assign = key % n_assign
    bounds = jnp.arange(N_EXPERTS + 1, dtype=jnp.int32) * n_assign
    start = jnp.sum((key[None, :] < bounds[:, None]).astype(jnp.int32), axis=1)
    counts = start[1:] - start[:-1]
    nblk = (counts + MOE_BLOCK - 1) // MOE_BLOCK
    blk_end = jnp.cumsum(nblk)
    blk_start = blk_end - nblk
    steps = -(-n_assign // MOE_BLOCK) + N_EXPERTS + 1
    bidx = jnp.arange(steps, dtype=jnp.int32)
    lane = jnp.arange(MOE_BLOCK, dtype=jnp.int32)[None, :]
    block_expert = jnp.minimum(jnp.sum((blk_end[None, :] <= bidx[:, None]).astype(jnp.int32), axis=1),
                               N_EXPERTS - 1)
    onehot = (block_expert[:, None] == jnp.arange(N_EXPERTS, dtype=jnp.int32)[None, :]).astype(jnp.int32)

    def lookup(table):
        return jnp.sum(onehot * table[None, :], axis=1)

    in_expert = (bidx - lookup(blk_start)) * MOE_BLOCK
    n_valid = jnp.clip(lookup(counts) - in_expert, 0, MOE_BLOCK)
    first_src = jnp.clip(lookup(start[:-1]) + in_expert, 0, n_assign)
    valid = lane < n_valid[:, None]
    assign = sorted_assign[jnp.minimum(first_src[:, None] + lane, n_assign - 1)]
    tok = assign // top_k
    src_rows = jnp.where(valid, tok, (bidx[:, None] * MOE_BLOCK + lane) % n_tok) * pitch
    dst_slot = jnp.where(valid, (assign % top_k) * n_tok + tok, top_k * n_tok + lane)
    dst_rows = jnp.concatenate([top_k * n_tok + lane, dst_slot[:-1]], axis=0) * pitch
    n_used = blk_end[-1].astype(jnp.int32)
    first = jnp.concatenate([jnp.ones((1,), jnp.int32),
                             (block_expert[1:] != block_expert[:-1]).astype(jnp.int32)])
    expert_ordinal = (jnp.cumsum(first) - 1).astype(jnp.int32)
    later = (jnp.arange(N_EXPERTS)[None, :] > block_expert[:, None]) & (counts[None, :] > 0)
    next_expert = jnp.min(jnp.where(later, jnp.arange(N_EXPERTS, dtype=jnp.int32)[None, :], N_EXPERTS), axis=1)
    next_expert = jnp.where(next_expert < N_EXPERTS, next_expert, -1).astype(jnp.int32)
    tables = (block_expert, n_used.reshape(1), expert_ordinal, next_expert)
    return (src_rows.astype(jnp.int32).reshape(steps, 1, MOE_BLOCK),
            dst_rows.astype(jnp.int32).reshape(steps, 1, MOE_BLOCK), tables)


def kernel(x, c, ctx, c_ctx, ada_w, ada_b, norm1_w, w_in, ssm_conv_w, ssm_conv_b, dt_bias, a_log, d_skip, ssm_norm_w, ssm_out_w, cf_dw_w, cf_dw_b, cf_ln_w, cf_ln_b, cf_out_w, cf_out_b, w_o, norm2_w, router_group_w, router_group_b, router_expert_w, router_expert_b, expert_w_gate, expert_w_up, expert_w_down, final_norm_w):
    bsz, seq, d = x.shape
    l_ctx = ctx.shape[1]
    n_tok = bsz * seq
    d_inner = ssm_norm_w.shape[1]
    gn = N_GROUPS * D_STATE
    xbc_dim = d_inner + 2 * gn
    off_dt = xbc_dim
    off_z = off_dt + N_HEADS
    off_glu = off_z + d_inner
    off_gate = off_glu + 2 * d

    ctx_row = bsz
    crows = jnp.zeros((8, d), F32).at[:bsz].set(c).at[ctx_row].set(c_ctx)
    mod = _ada(crows, ada_w[0], ada_b[0])
    mod3 = mod.reshape(8, 1, 6 * d)
    lat_rows = jnp.arange(bsz, dtype=jnp.int32)
    ctx_rows = jnp.full((bsz,), ctx_row, jnp.int32)

    h_lat = _normmod(x, norm1_w[0], mod3, lat_rows, 0, 1, BF16).reshape(n_tok, d)
    h_ctx = _normmod(ctx, norm1_w[0], mod3, ctx_rows, 0, 1, BF16).reshape(bsz * l_ctx, d)

    wt, (r_xbc, r_dt, r_z, r_glu, r_gate) = _pack_wt(
        jnp.transpose(w_in[0]),
        [(0, xbc_dim), (off_dt, off_z), (off_z, off_glu), (off_glu, off_gate), (off_gate, off_gate + 2 * d)])

    xbc_lat = _mm(h_lat, wt, tn=2048, name="in_xbc", rows=(r_xbc, xbc_dim)).reshape(bsz, seq, xbc_dim)
    xbc_ctx = _mm(h_ctx, wt, tm=512, name="in_xbc_ctx", rows=(r_xbc, xbc_dim)).reshape(bsz, l_ctx, xbc_dim)
    dt_lat = _mm(h_lat, wt, name="in_dt", rows=(r_dt, LANE)).reshape(bsz, seq, LANE)
    dt_ctx = _mm(h_ctx, wt, tm=512, name="in_dt_ctx", rows=(r_dt, LANE)).reshape(bsz, l_ctx, LANE)
    sz = _mm(h_lat, wt, act="silu", tn=2048, name="in_z", rows=(r_z, d_inner))
    u = _mm_glu(h_lat, wt, r_glu, d, tn=1024)

    xbc_act = _conv7(xbc_ctx, xbc_lat, ssm_conv_w[0], ssm_conv_b[0])

    def ssd_params(k):
        par = jnp.zeros((8, LANE), F32).at[0, :N_HEADS].set(dt_bias[0, k]).at[1, :N_HEADS].set(a_log[0, k])
        return par, jnp.repeat(d_skip[0, k], HEAD_DIM).reshape(N_HEADS // 2, 1, LANE)

    y_bwd = _ssd(xbc_act, dt_ctx, dt_lat, *ssd_params(1), reverse=True)
    gnorm = _ssd(xbc_act, dt_ctx, dt_lat, *ssd_params(0), reverse=False,
                 norm_with=(y_bwd, sz, ssm_norm_w[0]))
    y_ssd = _mm(gnorm, ssm_out_w[0].astype(BF16), tn=512, name="ssm_out")

    cv = _conv31(u.reshape(bsz, seq, d), cf_dw_w[0], cf_dw_b[0]).reshape(n_tok, d)
    merged = _mm_merge(cv, cf_ln_w[0], cf_ln_b[0], cf_out_w[0].astype(BF16), cf_out_b[0],
                       h_lat, wt, r_gate, y_ssd)
    x1 = _mm_resid(merged, w_o[0].astype(BF16), x.reshape(n_tok, d), mod3, 2, seq, tm=2048)

    n_r = MOE_GROUPS + N_EXPERTS
    rw = jnp.pad(jnp.concatenate([router_group_w[0], router_expert_w[0]], axis=1),
                 ((0, 0), (0, LANE - n_r))).astype(BF16)
    rb = jnp.pad(jnp.concatenate([router_group_b[0], router_expert_b[0]]), (0, LANE - n_r)).reshape(1, LANE)
    h2t, eid, ew = _route(x1, norm2_w[0], mod3, 3, 4, seq, rw, rb)

    src_rows, dst_rows, tables = _dispatch_tables(eid[:, :2], n_tok, _pitch(d // LANE))
    hid = _expert_up(h2t, src_rows, expert_w_gate[0], expert_w_up[0], tables)
    y = _expert_down(hid, dst_rows, expert_w_down[0], tables, 2 * n_tok + MOE_BLOCK)
    out = _combine(y, ew, x1, mod3, 5, seq, final_norm_w)
    return out.reshape(bsz, seq, d)
```

```python
import functools

import jax
import jax.numpy as jnp
from jax import lax
from jax.experimental import pallas as pl
from jax.experimental.pallas import tpu as pltpu

F32 = jnp.float32
BF16 = jnp.bfloat16

EPS = 1e-6
GRID_W = 64
HEAD_DIM = 64
N_HEADS = 64
N_GROUPS = 8
D_STATE = 128
CHUNK = 128
SSM_CONV = 7
CF_KERNEL = 31
MOE_GROUPS = 8
EXPERTS_PER_GROUP = 8
N_EXPERTS = 64
MOE_BLOCK = 256
LANE = 128
LOG2E = 1.4426950408889634
VMEM_LIMIT = 56 * 1024 * 1024


def _cparams(sem):
    return pltpu.CompilerParams(dimension_semantics=sem, vmem_limit_bytes=VMEM_LIMIT)


def _silu(v):
    return v * jax.nn.sigmoid(v)


def _pitch(nt):
    return nt + 1


def _ada_kernel(c_ref, w_ref, b_ref, o_ref):
    s = _silu(c_ref[...])
    o_ref[...] = jnp.dot(s.astype(BF16), w_ref[...].astype(BF16),
                         preferred_element_type=F32) + b_ref[...]


def _ada(crows, ada_w, ada_b, tn=1024):
    r, d = crows.shape
    n = ada_w.shape[1]
    return pl.pallas_call(
        _ada_kernel,
        out_shape=jax.ShapeDtypeStruct((r, n), F32),
        grid=(n // tn,),
        in_specs=[pl.BlockSpec((r, d), lambda j: (0, 0)),
                  pl.BlockSpec((d, tn), lambda j: (0, j)),
                  pl.BlockSpec((1, tn), lambda j: (0, j))],
        out_specs=pl.BlockSpec((r, tn), lambda j: (0, j)),
        compiler_params=_cparams(("parallel",)),
        name="ada",
    )(crows, ada_w, ada_b.reshape(1, n))


def _normmod_kernel(rows_ref, x_ref, w_ref, sh_ref, sc_ref, o_ref):
    del rows_ref
    xf = x_ref[0]
    ms = jnp.mean(xf * xf, axis=-1, keepdims=True)
    y = xf * lax.rsqrt(ms + EPS) * w_ref[...]
    o_ref[0] = (y * (1.0 + sc_ref[0]) + sh_ref[0]).astype(o_ref.dtype)


def _normmod(x3, w, mod3, rows, shift_chunk, scale_chunk, out_dtype, tm=512):
    bx, l, d = x3.shape
    tm = min(tm, l)
    grid_spec = pltpu.PrefetchScalarGridSpec(
        num_scalar_prefetch=1,
        grid=(bx, l // tm),
        in_specs=[pl.BlockSpec((1, tm, d), lambda b, i, r: (b, i, 0)),
                  pl.BlockSpec((1, d), lambda b, i, r: (0, 0)),
                  pl.BlockSpec((1, 1, d), lambda b, i, r: (r[b], 0, shift_chunk)),
                  pl.BlockSpec((1, 1, d), lambda b, i, r: (r[b], 0, scale_chunk))],
        out_specs=pl.BlockSpec((1, tm, d), lambda b, i, r: (b, i, 0)),
    )
    return pl.pallas_call(
        _normmod_kernel,
        out_shape=jax.ShapeDtypeStruct((bx, l, d), out_dtype),
        grid_spec=grid_spec,
        compiler_params=_cparams(("parallel", "parallel")),
        name="normmod",
    )(rows, x3, w.reshape(1, d), mod3, mod3)


def _dot_nt(a, wt):
    return lax.dot_general(a, wt, (((1,), (1,)), ((), ())), preferred_element_type=F32)


def _mm_kernel(a_ref, w_ref, *rest, act, has_bias, w_rows):
    o_ref = rest[-1]
    a = a_ref[...]
    acc = _dot_nt(a, w_ref[...]) if w_rows else jnp.dot(a, w_ref[...], preferred_element_type=F32)
    if has_bias:
        acc = acc + rest[0][...]
    if act == "silu":
        acc = _silu(acc)
    elif act == "sigmoid":
        acc = jax.nn.sigmoid(acc)
    o_ref[...] = acc.astype(o_ref.dtype)


def _mm(a, w, bias=None, act=None, out_dtype=F32, tm=1024, tn=1024, name="mm", rows=None):
    m, k = a.shape
    start, n = (0, w.shape[1]) if rows is None else rows
    tm, tn = min(tm, m), min(tn, n)
    j0 = start // tn
    w_spec = (pl.BlockSpec((k, tn), lambda i, j: (0, j)) if rows is None
              else pl.BlockSpec((tn, k), lambda i, j: (j0 + j, 0)))
    in_specs = [pl.BlockSpec((tm, k), lambda i, j: (i, 0)), w_spec]
    args = [a, w]
    if bias is not None:
        in_specs.append(pl.BlockSpec((1, tn), lambda i, j: (0, j)))
        args.append(bias.reshape(1, n))
    return pl.pallas_call(
        functools.partial(_mm_kernel, act=act, has_bias=bias is not None, w_rows=rows is not None),
        out_shape=jax.ShapeDtypeStruct((m, n), out_dtype),
        grid=(m // tm, n // tn),
        in_specs=in_specs,
        out_specs=pl.BlockSpec((tm, tn), lambda i, j: (i, j)),
        compiler_params=_cparams(("parallel", "parallel")),
        name=name,
    )(*args)


def _mm_glu_kernel(a_ref, wa_ref, wb_ref, o_ref):
    a = a_ref[...]
    va = _dot_nt(a, wa_ref[...])
    vb = _dot_nt(a, wb_ref[...])
    o_ref[...] = va * jax.nn.sigmoid(vb)


def _mm_glu(a, wt, start, n, tm=1024, tn=512):
    m, k = a.shape
    tm = min(tm, m)
    ja, jb = start // tn, (start + n) // tn
    return pl.pallas_call(
        _mm_glu_kernel,
        out_shape=jax.ShapeDtypeStruct((m, n), F32),
        grid=(m // tm, n // tn),
        in_specs=[pl.BlockSpec((tm, k), lambda i, j: (i, 0)),
                  pl.BlockSpec((tn, k), lambda i, j: (ja + j, 0)),
                  pl.BlockSpec((tn, k), lambda i, j: (jb + j, 0))],
        out_specs=pl.BlockSpec((tm, tn), lambda i, j: (i, j)),
        compiler_params=_cparams(("parallel", "parallel")),
        name="mm_glu",
    )(a, wt, wt)


def _mm_merge_kernel(cv_ref, lw_ref, lb_ref, w_ref, b_ref, h_ref, wga_ref, wgb_ref, ys_ref, o_ref, u_ref):
    @pl.when(pl.program_id(1) == 0)
    def _():
        xf = cv_ref[...]
        mu = jnp.mean(xf, axis=-1, keepdims=True)
        xc = xf - mu
        var = jnp.mean(xc * xc, axis=-1, keepdims=True)
        y = xc * lax.rsqrt(var + EPS) * lw_ref[...] + lb_ref[...]
        u_ref[...] = _silu(y).astype(u_ref.dtype)

    ycf = jnp.dot(u_ref[...], w_ref[...], preferred_element_type=F32) + b_ref[...]
    h = h_ref[...]
    gate_a = jax.nn.sigmoid(_dot_nt(h, wga_ref[...]))
    gate_b = jax.nn.sigmoid(_dot_nt(h, wgb_ref[...]))
    o_ref[...] = (gate_a * ys_ref[...] + gate_b * ycf).astype(o_ref.dtype)


def _mm_merge(cv, ln_w, ln_b, w, bias, h, wt, gate_start, y_ssd, tm=512, tn=512):
    m, k = cv.shape
    n = w.shape[1]
    tm = min(tm, m)
    ja, jb = gate_start // tn, (gate_start + n) // tn
    return pl.pallas_call(
        _mm_merge_kernel,
        out_shape=jax.ShapeDtypeStruct((m, n), BF16),
        grid=(m // tm, n // tn),
        in_specs=[pl.BlockSpec((tm, k), lambda i, j: (i, 0)),
                  pl.BlockSpec((1, k), lambda i, j: (0, 0)),
                  pl.BlockSpec((1, k), lambda i, j: (0, 0)),
                  pl.BlockSpec((k, tn), lambda i, j: (0, j)),
                  pl.BlockSpec((1, tn), lambda i, j: (0, j)),
                  pl.BlockSpec((tm, k), lambda i, j: (i, 0)),
                  pl.BlockSpec((tn, k), lambda i, j: (ja + j, 0)),
                  pl.BlockSpec((tn, k), lambda i, j: (jb + j, 0)),
                  pl.BlockSpec((tm, tn), lambda i, j: (i, j))],
        out_specs=pl.BlockSpec((tm, tn), lambda i, j: (i, j)),
        scratch_shapes=[pltpu.VMEM((tm, k), BF16)],
        compiler_params=_cparams(("parallel", "arbitrary")),
        name="mm_merge",
    )(cv, ln_w.reshape(1, k), ln_b.reshape(1, k), w, bias.reshape(1, n), h, wt, wt, y_ssd)


def _mm_resid_kernel(a_ref, w_ref, x_ref, g_ref, o_ref):
    out = jnp.dot(a_ref[...], w_ref[...], preferred_element_type=F32)
    o_ref[...] = x_ref[...] + g_ref[0] * out


def _mm_resid(a, w, x2, mod3, gate_chunk, rows_per_batch, tm=1024, tn=512):
    m, k = a.shape
    n = w.shape[1]
    tm = min(tm, rows_per_batch)
    nj = n // tn
    tiles_per_batch = rows_per_batch // tm
    return pl.pallas_call(
        _mm_resid_kernel,
        out_shape=jax.ShapeDtypeStruct((m, n), F32),
        grid=(m // tm, nj),
        in_specs=[pl.BlockSpec((tm, k), lambda i, j: (i, 0)),
                  pl.BlockSpec((k, tn), lambda i, j: (0, j)),
                  pl.BlockSpec((tm, tn), lambda i, j: (i, j)),
                  pl.BlockSpec((1, 1, tn),
                               lambda i, j: (i // tiles_per_batch, 0, gate_chunk * nj + j))],
        out_specs=pl.BlockSpec((tm, tn), lambda i, j: (i, j)),
        compiler_params=_cparams(("parallel", "parallel")),
        name="mm_resid",
    )(a, w, x2, mod3)


W_ALIGN = 2048


def _pack_wt_kernel(valid_ref, off_ref, w_ref, o_ref):
    del off_ref
    nrow = valid_ref[pl.program_id(0)]
    row = lax.broadcasted_iota(jnp.int32, o_ref.shape, 0)
    o_ref[...] = jnp.where(row < nrow, w_ref[...], 0.0).astype(o_ref.dtype)


def _pack_wt(wt, segments, tr=1024):
    n, k = wt.shape
    starts, src_off, valid = [], [], []
    pos = 0
    for lo, hi in segments:
        pos = -(-pos // W_ALIGN) * W_ALIGN
        starts.append(pos)
        while len(src_off) < pos // tr:
            src_off.append(0)
            valid.append(0)
        for r in range(lo, hi, tr):
            src_off.append(min(r, n - tr))
            valid.append(min(tr, hi - r))
            assert r <= n - tr or hi - r == tr
        pos += -(-(hi - lo) // tr) * tr
    total = -(-pos // W_ALIGN) * W_ALIGN
    while len(src_off) < total // tr:
        src_off.append(0)
        valid.append(0)
    grid_spec = pltpu.PrefetchScalarGridSpec(
        num_scalar_prefetch=2,
        grid=(total // tr,),
        in_specs=[pl.BlockSpec((pl.Element(tr), pl.Element(k)), lambda t, v, off: (off[t] * 8, 0))],
        out_specs=pl.BlockSpec((tr, k), lambda t, v, off: (t, 0)),
    )
    packed = pl.pallas_call(
        _pack_wt_kernel,
        out_shape=jax.ShapeDtypeStruct((total, k), BF16),
        grid_spec=grid_spec,
        compiler_params=_cparams(("parallel",)),
        name="pack_wt",
    )(jnp.asarray(valid, jnp.int32), jnp.asarray(src_off, jnp.int32) // 8, wt)
    return packed, starts


_CONV_PAD = 8


def _conv7_kernel(ctx_ref, lat_ref, w_ref, b_ref, o_ref, pad_ref, *, l_ctx, l_lat):
    p = _CONV_PAD
    zeros = jnp.zeros((p, LANE), F32)
    off_ctx = p
    off_lat = 2 * p + l_ctx
    pad_ref[0:p, :] = zeros
    pad_ref[off_ctx + l_ctx:off_lat, :] = zeros
    pad_ref[off_lat + l_lat:off_lat + l_lat + p, :] = zeros
    pad_ref[off_ctx:off_ctx + l_ctx, :] = ctx_ref[0]
    pad_ref[off_lat:off_lat + l_lat, :] = lat_ref[0]
    reach = SSM_CONV // 2
    bias = b_ref[...]

    def chunk(pad_base, out_base):
        acc = jnp.broadcast_to(bias, (CHUNK, LANE))
        for k in range(SSM_CONV):
            tap = pad_ref[pl.ds(pad_base - reach + k, CHUNK), :]
            acc = acc + tap * w_ref[k:k + 1, :]
        o_ref[0, 0, pl.ds(out_base, CHUNK), :] = _silu(acc)

    def ctx_body(j, c):
        base = pl.multiple_of(j * CHUNK, CHUNK)
        chunk(off_ctx + base, base)
        return c

    def lat_body(j, c):
        base = pl.multiple_of(j * CHUNK, CHUNK)
        chunk(off_lat + base, l_ctx + base)
        return c

    lax.fori_loop(0, l_ctx // CHUNK, ctx_body, 0)
    lax.fori_loop(0, l_lat // CHUNK, lat_body, 0, unroll=2)


def _conv7(ctx_raw, lat_raw, w, b):
    bsz, l_ctx, c = ctx_raw.shape
    l_lat = lat_raw.shape[1]
    ltot = l_ctx + l_lat
    nct = c // LANE
    return pl.pallas_call(
        functools.partial(_conv7_kernel, l_ctx=l_ctx, l_lat=l_lat),
        out_shape=jax.ShapeDtypeStruct((bsz, nct, ltot, LANE), F32),
        grid=(bsz, nct),
        in_specs=[pl.BlockSpec((1, l_ctx, LANE), lambda bi, ci: (bi, 0, ci)),
                  pl.BlockSpec((1, l_lat, LANE), lambda bi, ci: (bi, 0, ci)),
                  pl.BlockSpec((SSM_CONV, LANE), lambda bi, ci: (0, ci)),
                  pl.BlockSpec((1, LANE), lambda bi, ci: (0, ci))],
        out_specs=pl.BlockSpec((1, 1, ltot, LANE), lambda bi, ci: (bi, ci, 0, 0)),
        scratch_shapes=[pltpu.VMEM((ltot + 3 * _CONV_PAD, LANE), F32)],
        compiler_params=_cparams(("parallel", "parallel")),
        name="conv7",
    )(ctx_raw, lat_raw, w, b.reshape(1, c))


def _ssd_kernel(xbc_ref, dtc_ref, dtl_ref, par_ref, dexp_ref, ex_ref, *rest, reverse, n_ctx, fuse_norm):
    if fuse_norm:
        yo_ref, sz_ref, nw_ref, o_ref, st_ref, cumt_ref, y_ref = rest
    else:
        y_ref, st_ref, cumt_ref = rest
    i = pl.program_id(1)

    @pl.when(i == 0)
    def _():
        st_ref[...] = jnp.zeros_like(st_ref)

    dt_raw = jnp.where(i < n_ctx, dtc_ref[0], dtl_ref[0])
    bias = par_ref[0:1, :]
    a = -jnp.exp(par_ref[1:2, :])
    dt = jax.nn.softplus(dt_raw + bias)
    cum = dt * a
    row = lax.broadcasted_iota(jnp.int32, (CHUNK, LANE), 0)
    k = 1
    while k < CHUNK:
        if reverse:
            cum = cum + jnp.where(row < CHUNK - k, pltpu.roll(cum, CHUNK - k, 0), 0.0)
        else:
            cum = cum + jnp.where(row >= k, pltpu.roll(cum, k, 0), 0.0)
        k *= 2
    last = 0 if reverse else CHUNK - 1
    cum = cum * LOG2E
    cumt_ref[...] = cum.T
    li = lax.broadcasted_iota(jnp.int32, (CHUNK, CHUNK), 0)
    si = lax.broadcasted_iota(jnp.int32, (CHUNK, CHUNK), 1)
    causal = (li <= si) if reverse else (li >= si)
    lo = lax.broadcasted_iota(jnp.int32, (CHUNK, LANE), 1) < HEAD_DIM
    heads_per_group = N_HEADS // N_GROUPS
    pairs = heads_per_group // 2
    x_tiles = N_HEADS // 2

    def group(g, carry):
        shift = (LANE - heads_per_group * g) & (LANE - 1)
        cum_g = pltpu.roll(cum, shift, 1)
        dt_g = pltpu.roll(dt, shift, 1)
        cum_t = cumt_ref[pl.ds(pl.multiple_of(heads_per_group * g, heads_per_group), heads_per_group), :]
        bb = xbc_ref[0, x_tiles + g].astype(BF16)
        cb = xbc_ref[0, x_tiles + N_GROUPS + g].astype(BF16)
        scores = lax.dot_general(cb, bb, (((1,), (1,)), ((), ())), preferred_element_type=F32)
        y_off = jnp.dot(cb, st_ref[g].astype(BF16), preferred_element_type=F32)
        d_hi = dt_g.astype(BF16)
        r_hi = dt_g - d_hi.astype(F32)
        d_mid = r_hi.astype(BF16)
        d_lo = (r_hi - d_mid.astype(F32)).astype(BF16)
        dt_x = (jnp.dot(jnp.concatenate([d_hi, d_mid], axis=1), ex_ref[...], preferred_element_type=F32)
                + jnp.dot(d_lo, ex_ref[0:LANE, :], preferred_element_type=F32))
        xw_parts, dec_parts = [], []
        for p in range(pairs):
            j0, j1 = 2 * p, 2 * p + 1
            x2 = xbc_ref[0, pairs * g + p]
            c0 = cum_g[:, j0:j0 + 1]
            c1 = cum_g[:, j1:j1 + 1]
            l0 = jnp.exp2(jnp.where(causal, c0 - cum_t[j0:j0 + 1, :], -jnp.inf))
            l1 = jnp.exp2(jnp.where(causal, c1 - cum_t[j1:j1 + 1, :], -jnp.inf))
            m0 = (scores * l0).astype(BF16)
            m1 = (scores * l1).astype(BF16)
            dt2 = dt_x[:, p * LANE:(p + 1) * LANE]
            c2 = jnp.where(lo, c0, c1)
            xdt = x2 * dt2
            xdt_b = xdt.astype(BF16)
            zero = jnp.zeros_like(xdt_b)
            y_diag = jnp.dot(jnp.concatenate([m0, m1], axis=1),
                             jnp.concatenate([jnp.where(lo, xdt_b, zero), jnp.where(lo, zero, xdt_b)], axis=0),
                             preferred_element_type=F32)
            e2 = jnp.exp2(c2)
            y = y_diag + y_off[:, p * LANE:(p + 1) * LANE] * e2
            y_ref[0, pairs * g + p] = y + dexp_ref[pairs * g + p] * x2
            to_end = jnp.exp2(c2[last:last + 1, :] - c2)
            xw_parts.append((xdt * to_end).astype(BF16))
            dec_parts.append(e2[last:last + 1, :])
        xw = jnp.concatenate(xw_parts, axis=1)
        dec = jnp.concatenate(dec_parts, axis=1)
        upd = lax.dot_general(bb, xw, (((0,), (0,)), ((), ())), preferred_element_type=F32)
        st_ref[g] = st_ref[g] * dec + upd
        return carry

    lax.fori_loop(0, N_GROUPS, group, 0, unroll=2)

    if fuse_norm:
        sq = jnp.zeros((CHUNK, LANE), F32)
        for j in range(x_tiles):
            gj = (y_ref[0, j] + yo_ref[0, j]) * sz_ref[:, j * LANE:(j + 1) * LANE]
            y_ref[0, j] = gj
            sq = sq + gj * gj
        r = lax.rsqrt(jnp.sum(sq, axis=-1, keepdims=True) / (x_tiles * LANE) + EPS)
        for j in range(x_tiles):
            sl = slice(j * LANE, (j + 1) * LANE)
            o_ref[:, sl] = (y_ref[0, j] * r * nw_ref[:, sl]).astype(o_ref.dtype)


def _ssd(xbc_act, dt_ctx, dt_lat, par, dexp, reverse, norm_with=None):
    bsz, ntile, ltot, _ = xbc_act.shape
    l_ctx = dt_ctx.shape[1]
    l_lat = dt_lat.shape[1]
    n_ctx = l_ctx // CHUNK
    n_lat = l_lat // CHUNK
    steps = n_ctx + n_lat
    x_tiles = N_HEADS // 2
    gw = (N_HEADS // N_GROUPS) * HEAD_DIM
    e1 = (jnp.arange(gw)[None, :] // HEAD_DIM == jnp.arange(LANE)[:, None]).astype(BF16)
    expand = jnp.concatenate([e1, e1], axis=0)

    if reverse:
        def cat_chunk(i):
            return jnp.where(i < n_ctx, n_ctx - 1 - i, n_ctx + steps - 1 - i)

        def ctx_chunk(i):
            return jnp.maximum(n_ctx - 1 - i, 0)

        def lat_chunk(i):
            return jnp.minimum(steps - 1 - i, n_lat - 1)
    else:
        def cat_chunk(i):
            return i

        def ctx_chunk(i):
            return jnp.minimum(i, n_ctx - 1)

        def lat_chunk(i):
            return jnp.maximum(i - n_ctx, 0)

    y_spec = pl.BlockSpec((1, x_tiles, CHUNK, LANE), lambda b, i: (b, 0, lat_chunk(i), 0))
    in_specs = [pl.BlockSpec((1, ntile, CHUNK, LANE), lambda b, i: (b, 0, cat_chunk(i), 0)),
                pl.BlockSpec((1, CHUNK, LANE), lambda b, i: (b, ctx_chunk(i), 0)),
                pl.BlockSpec((1, CHUNK, LANE), lambda b, i: (b, lat_chunk(i), 0)),
                pl.BlockSpec((8, LANE), lambda b, i: (0, 0)),
                pl.BlockSpec((x_tiles, 1, LANE), lambda b, i: (0, 0, 0)),
                pl.BlockSpec((2 * LANE, gw), lambda b, i: (0, 0))]
    args = [xbc_act, dt_ctx, dt_lat, par, dexp, expand]
    scratch = [pltpu.VMEM((N_GROUPS, D_STATE, gw), F32), pltpu.VMEM((LANE, CHUNK), F32)]
    if norm_with is None:
        out_shape = jax.ShapeDtypeStruct((bsz, x_tiles, l_lat, LANE), F32)
        out_spec = y_spec
    else:
        y_other, silu_z, norm_w = norm_with
        dn = x_tiles * LANE
        row_spec = pl.BlockSpec((CHUNK, dn), lambda b, i: (b * n_lat + lat_chunk(i), 0))
        in_specs += [y_spec, row_spec, pl.BlockSpec((1, dn), lambda b, i: (0, 0))]
        args += [y_other, silu_z, norm_w.reshape(1, dn)]
        out_shape = jax.ShapeDtypeStruct((bsz * l_lat, dn), BF16)
        out_spec = row_spec
        scratch.append(pltpu.VMEM((1, x_tiles, CHUNK, LANE), F32))
    return pl.pallas_call(
        functools.partial(_ssd_kernel, reverse=reverse, n_ctx=n_ctx, fuse_norm=norm_with is not None),
        out_shape=out_shape,
        grid=(bsz, steps),
        in_specs=in_specs,
        out_specs=out_spec,
        scratch_shapes=scratch,
        compiler_params=_cparams(("parallel", "arbitrary")),
        name="ssd_bwd" if reverse else "ssd_fwd",
    )(*args)


def _conv31_kernel(u_ref, w_ref, b_ref, o_ref, pad_ref, *, seq):
    halo = (CF_KERNEL // 2) * GRID_W
    zeros = jnp.zeros((halo, LANE), F32)
    pad_ref[0:halo, :] = zeros
    pad_ref[halo + seq:halo + seq + halo, :] = zeros
    pad_ref[halo:halo + seq, :] = u_ref[0]
    bias = b_ref[...]

    def body(j, c):
        base = pl.multiple_of(j * CHUNK, CHUNK)
        acc = jnp.broadcast_to(bias, (CHUNK, LANE))
        for k in range(CF_KERNEL):
            tap = pad_ref[pl.ds(pl.multiple_of(base + k * GRID_W, GRID_W), CHUNK), :]
            acc = acc + tap * w_ref[k:k + 1, :]
        o_ref[0, pl.ds(base, CHUNK), :] = acc
        return c

    lax.fori_loop(0, seq // CHUNK, body, 0, unroll=2)


def _conv31(u3, w, b):
    bsz, s, c = u3.shape
    halo = (CF_KERNEL // 2) * GRID_W
    return pl.pallas_call(
        functools.partial(_conv31_kernel, seq=s),
        out_shape=jax.ShapeDtypeStruct((bsz, s, c), F32),
        grid=(bsz, c // LANE),
        in_specs=[pl.BlockSpec((1, s, LANE), lambda bi, ci: (bi, 0, ci)),
                  pl.BlockSpec((CF_KERNEL, LANE), lambda bi, ci: (0, ci)),
                  pl.BlockSpec((1, LANE), lambda bi, ci: (0, ci))],
        out_specs=pl.BlockSpec((1, s, LANE), lambda bi, ci: (bi, 0, ci)),
        scratch_shapes=[pltpu.VMEM((s + 2 * halo, LANE), F32)],
        compiler_params=_cparams(("parallel", "parallel")),
        name="conv31",
    )(u3, w, b.reshape(1, c))


def _route_kernel(x_ref, w_ref, sh_ref, sc_ref, rw_ref, rb_ref, h_ref, eid_ref, ew_ref):
    xf = x_ref[...]
    ms = jnp.mean(xf * xf, axis=-1, keepdims=True)
    h = xf * lax.rsqrt(ms + EPS) * w_ref[...]
    h = h * (1.0 + sc_ref[0]) + sh_ref[0]
    tm = xf.shape[0]
    nt = xf.shape[1] // LANE
    pitch = _pitch(nt)
    for j in range(nt):
        h_ref[pl.ds(j, tm, stride=pitch), :] = h[:, j * LANE:(j + 1) * LANE]
    for j in range(nt, pitch):
        h_ref[pl.ds(j, tm, stride=pitch), :] = jnp.zeros((tm, LANE), F32)
    logits = jnp.dot(h.astype(BF16), rw_ref[...], preferred_element_type=F32) + rb_ref[...]
    lane = lax.broadcasted_iota(jnp.int32, (tm, LANE), 1)
    lane_f = lane.astype(F32)
    ninf = -jnp.inf
    gl = jnp.where(lane < MOE_GROUPS, logits, ninf)
    gmax = jnp.max(gl, axis=-1, keepdims=True)
    gidx = jnp.min(jnp.where(gl == gmax, lane_f, float(LANE)), axis=-1, keepdims=True)
    gsum = jnp.sum(jnp.exp(gl - gmax), axis=-1, keepdims=True)
    g_p = 1.0 / gsum
    first = float(MOE_GROUPS) + gidx * float(EXPERTS_PER_GROUP)
    in_group = (lane_f >= first) & (lane_f < first + float(EXPERTS_PER_GROUP))
    el = jnp.where(in_group, logits, ninf)
    m1 = jnp.max(el, axis=-1, keepdims=True)
    i1 = jnp.min(jnp.where(el == m1, lane_f, float(LANE)), axis=-1, keepdims=True)
    el2 = jnp.where(lane_f == i1, ninf, el)
    m2 = jnp.max(el2, axis=-1, keepdims=True)
    i2 = jnp.min(jnp.where(el2 == m2, lane_f, float(LANE)), axis=-1, keepdims=True)
    e21 = jnp.exp(m2 - m1)
    den = 1.0 + e21
    w1 = (1.0 / den) * g_p
    w2 = (e21 / den) * g_p
    e1 = (i1 - float(MOE_GROUPS)).astype(jnp.int32)
    e2 = (i2 - float(MOE_GROUPS)).astype(jnp.int32)
    eid_ref[...] = jnp.where(lane == 0, e1, jnp.where(lane == 1, e2, 0))
    ew_ref[...] = jnp.where(lane == 0, w1, jnp.where(lane == 1, w2, 0.0))


def _route(x2, w, mod3, shift_chunk, scale_chunk, rows_per_batch, rw, rb, tm=512):
    m, d = x2.shape
    pitch = _pitch(d // LANE)
    tm = min(tm, rows_per_batch)
    tiles_per_batch = rows_per_batch // tm
    return pl.pallas_call(
        _route_kernel,
        out_shape=(jax.ShapeDtypeStruct((m * pitch, LANE), F32),
                   jax.ShapeDtypeStruct((m, LANE), jnp.int32),
                   jax.ShapeDtypeStruct((m, LANE), F32)),
        grid=(m // tm,),
        in_specs=[pl.BlockSpec((tm, d), lambda i: (i, 0)),
                  pl.BlockSpec((1, d), lambda i: (0, 0)),
                  pl.BlockSpec((1, 1, d), lambda i: (i // tiles_per_batch, 0, shift_chunk)),
                  pl.BlockSpec((1, 1, d), lambda i: (i // tiles_per_batch, 0, scale_chunk)),
                  pl.BlockSpec((d, LANE), lambda i: (0, 0)),
                  pl.BlockSpec((1, LANE), lambda i: (0, 0))],
        out_specs=(pl.BlockSpec((tm * pitch, LANE), lambda i: (i, 0)),
                   pl.BlockSpec((tm, LANE), lambda i: (i, 0)),
                   pl.BlockSpec((tm, LANE), lambda i: (i, 0))),
        compiler_params=_cparams(("parallel",)),
        name="route",
    )(x2, w.reshape(1, d), mod3, mod3, rw, rb)


_DMA_UNROLL = 8


def _rows_to_matrix(ref, tm, nt):
    return jnp.concatenate([ref[pl.ds(j, tm, stride=_pitch(nt)), :] for j in range(nt)], axis=1)


def _bulk_wait(src, dst, sem, total_rows):
    pltpu.make_async_copy(src.at[pl.ds(0, total_rows), :], dst.at[pl.ds(0, total_rows), :], sem).wait()


def _for_rows(n, body):
    groups = lax.shift_right_logical(n, _DMA_UNROLL.bit_length() - 1)

    def group(g, c):
        for u in range(_DMA_UNROLL):
            body(g * _DMA_UNROLL + u)
        return c

    def tail(r, c):
        body(r)
        return c

    lax.fori_loop(0, groups, group, 0)
    lax.fori_loop(groups * _DMA_UNROLL, n, tail, 0)


def _stream_expert_weights(b, be_ref, eord_ref, enext_ref, w_hbms, w_bufs, w_caches, wsem, both_queues):
    prev = jnp.maximum(b - 1, 0)

    def copies(e, slot):
        out = []
        for w, buf in zip(w_hbms, w_bufs):
            if both_queues:
                half = w.shape[1] // 2
                out.append((pltpu.make_async_copy(w.at[e, 0:half], buf.at[slot, 0:half], wsem.at[slot]), 1))
                out.append((pltpu.make_async_copy(w.at[e, half:], buf.at[slot, half:], wsem.at[slot]), 0))
            else:
                out.append((pltpu.make_async_copy(w.at[e], buf.at[slot], wsem.at[slot]), 1))
        return out

    @pl.when(b == 0)
    def _():
        for cp, prio in copies(be_ref[0], 0):
            cp.start(priority=prio)

    @pl.when((b == 0) | (be_ref[b] != be_ref[prev]))
    def _():
        for s in range(2):
            @pl.when((eord_ref[b] & 1) == s)
            def _(s=s):
                for cp, _ in copies(be_ref[b], s):
                    cp.wait()

                @pl.when(enext_ref[b] >= 0)
                def _():
                    for cp, prio in copies(enext_ref[b], 1 - s):
                        cp.start(priority=prio)

                for buf, cache in zip(w_bufs, w_caches):
                    cache[...] = buf[s].astype(BF16)


def _expert_up_kernel(be_ref, nused_ref, eord_ref, enext_ref, rowc_ref, rown_ref, h_hbm, wg_hbm, wu_hbm,
                      o_ref, xs0_ref, xs1_ref, wgs_ref, wus_ref, wgb_ref, wub_ref, sem, wsem, *, fchunk, nt):
    b = pl.program_id(0)
    n_used = nused_ref[0]
    dff = wgb_ref.shape[1]
    slots = (xs0_ref, xs1_ref)
    pitch = _pitch(nt)

    def gather_row(row_ref, r, slot):
        return pltpu.make_async_copy(h_hbm.at[pl.ds(row_ref[0, 0, r], nt), :],
                                     slots[slot].at[pl.ds(r * pitch, nt), :], sem.at[slot])

    @pl.when(b == 0)
    def _():
        _for_rows(MOE_BLOCK, lambda r: gather_row(rowc_ref, r, 0).start())

    for slot in range(2):
        @pl.when((b <= n_used) & (lax.rem(b, 2) == slot))
        def _(slot=slot):
            _bulk_wait(h_hbm, slots[slot], sem.at[slot], MOE_BLOCK * nt)

    @pl.when(b < n_used)
    def _():
        _stream_expert_weights(b, be_ref, eord_ref, enext_ref, (wg_hbm, wu_hbm), (wgs_ref, wus_ref),
                               (wgb_ref, wub_ref), wsem, both_queues=False)

        for slot in range(2):
            @pl.when(lax.rem(b, 2) == slot)
            def _(slot=slot):
                xb = _rows_to_matrix(slots[slot], MOE_BLOCK, nt).astype(BF16)
                nf = dff // fchunk
                per = MOE_BLOCK // (2 * nf)

                def request(part):
                    for r in range(part * per, (part + 1) * per):
                        gather_row(rown_ref, r, 1 - slot).start()

                for f in range(nf):
                    sl = slice(f * fchunk, (f + 1) * fchunk)
                    request(2 * f)
                    gate = jnp.dot(xb, wgb_ref[:, sl], preferred_element_type=F32)
                    request(2 * f + 1)
                    up = jnp.dot(xb, wub_ref[:, sl], preferred_element_type=F32)
                    o_ref[:, sl] = (_silu(gate) * up).astype(o_ref.dtype)

    @pl.when(b >= n_used)
    def _():
        o_ref[...] = jnp.zeros_like(o_ref)


def _expert_up(h2t, src_rows, w_gate, w_up, tables, fchunk=256):
    n_blocks = src_rows.shape[0]
    _, d, dff = w_gate.shape
    nt = d // LANE
    slot_rows = MOE_BLOCK * _pitch(nt)
    grid_spec = pltpu.PrefetchScalarGridSpec(
        num_scalar_prefetch=len(tables),
        grid=(n_blocks,),
        in_specs=[pl.BlockSpec((1, 1, MOE_BLOCK), lambda b, *_: (b, 0, 0), memory_space=pltpu.SMEM),
                  pl.BlockSpec((1, 1, MOE_BLOCK), lambda b, *_: (jnp.minimum(b + 1, n_blocks - 1), 0, 0),
                               memory_space=pltpu.SMEM),
                  pl.BlockSpec(memory_space=pl.ANY),
                  pl.BlockSpec(memory_space=pl.ANY),
                  pl.BlockSpec(memory_space=pl.ANY)],
        out_specs=pl.BlockSpec((MOE_BLOCK, dff), lambda b, *_: (b, 0)),
        scratch_shapes=[pltpu.VMEM((slot_rows, LANE), F32),
                        pltpu.VMEM((slot_rows, LANE), F32),
                        pltpu.VMEM((2, d, dff), F32),
                        pltpu.VMEM((2, d, dff), F32),
                        pltpu.VMEM((d, dff), BF16),
                        pltpu.VMEM((d, dff), BF16),
                        pltpu.SemaphoreType.DMA((2,)),
                        pltpu.SemaphoreType.DMA((2,))],
    )
    return pl.pallas_call(
        functools.partial(_expert_up_kernel, fchunk=fchunk, nt=nt),
        out_shape=jax.ShapeDtypeStruct((n_blocks * MOE_BLOCK, dff), BF16),
        grid_spec=grid_spec,
        compiler_params=_cparams(("arbitrary",)),
        name="expert_up",
    )(*tables, src_rows, src_rows, h2t, w_gate, w_up)


def _expert_down_kernel(be_ref, nused_ref, eord_ref, enext_ref, dst_ref, h_ref, wd_hbm, y_hbm,
                        ys0_ref, ys1_ref, wds_ref, wdb_ref, sem, wsem, *, nchunk, nt):
    b = pl.program_id(0)
    n_used = nused_ref[0]
    d = wdb_ref.shape[1]
    slots = (ys0_ref, ys1_ref)
    pitch = _pitch(nt)

    def scatter_row(r, slot):
        return pltpu.make_async_copy(slots[slot].at[pl.ds(r * pitch, pitch), :],
                                     y_hbm.at[pl.ds(dst_ref[0, 0, r], pitch), :], sem.at[slot])

    @pl.when(b == 0)
    def _():
        ys0_ref[...] = jnp.zeros_like(ys0_ref)
        ys1_ref[...] = jnp.zeros_like(ys1_ref)

    for slot in range(2):
        @pl.when((b >= 1) & (b <= n_used) & (lax.rem(b, 2) == slot))
        def _(slot=slot):
            _bulk_wait(slots[slot], y_hbm, sem.at[slot], MOE_BLOCK * pitch)

    @pl.when(b < n_used)
    def _():
        _stream_expert_weights(b, be_ref, eord_ref, enext_ref, (wd_hbm,), (wds_ref,), (wdb_ref,), wsem,
                               both_queues=True)
        hb = h_ref[...]
        for slot in range(2):
            @pl.when(lax.rem(b, 2) == slot)
            def _(slot=slot):
                nc = d // nchunk
                per = MOE_BLOCK // nc
                for c in range(nc):
                    for r in range(c * per, (c + 1) * per):
                        scatter_row(r, 1 - slot).start()
                    out = jnp.dot(hb, wdb_ref[:, c * nchunk:(c + 1) * nchunk], preferred_element_type=F32)
                    for j in range(nchunk // LANE):
                        slots[slot][pl.ds(c * (nchunk // LANE) + j, MOE_BLOCK, stride=pitch), :] = (
                            out[:, j * LANE:(j + 1) * LANE])

    for slot in range(2):
        @pl.when((b == n_used) & (lax.rem(b, 2) == slot))
        def _(slot=slot):
            _for_rows(MOE_BLOCK, lambda r: scatter_row(r, 1 - slot).start())
            _bulk_wait(slots[1 - slot], y_hbm, sem.at[1 - slot], MOE_BLOCK * pitch)


def _expert_down(hid, dst_rows, w_down, tables, y_slots, nchunk=256):
    n_rows, dff = hid.shape
    n_blocks = n_rows // MOE_BLOCK
    d = w_down.shape[2]
    nt = d // LANE
    pitch = _pitch(nt)
    grid_spec = pltpu.PrefetchScalarGridSpec(
        num_scalar_prefetch=len(tables),
        grid=(n_blocks,),
        in_specs=[pl.BlockSpec((1, 1, MOE_BLOCK), lambda b, *_: (b, 0, 0), memory_space=pltpu.SMEM),
                  pl.BlockSpec((MOE_BLOCK, dff), lambda b, be, n, *_: (jnp.minimum(b, n[0] - 1), 0)),
                  pl.BlockSpec(memory_space=pl.ANY)],
        out_specs=pl.BlockSpec(memory_space=pl.ANY),
        scratch_shapes=[pltpu.VMEM((MOE_BLOCK * pitch, LANE), F32),
                        pltpu.VMEM((MOE_BLOCK * pitch, LANE), F32),
                        pltpu.VMEM((2, dff, d), F32),
                        pltpu.VMEM((dff, d), BF16),
                        pltpu.SemaphoreType.DMA((2,)),
                        pltpu.SemaphoreType.DMA((2,))],
    )
    return pl.pallas_call(
        functools.partial(_expert_down_kernel, nchunk=nchunk, nt=nt),
        out_shape=jax.ShapeDtypeStruct((y_slots * pitch, LANE), F32),
        grid_spec=grid_spec,
        compiler_params=_cparams(("arbitrary",)),
        name="expert_down",
    )(*tables, dst_rows, hid, w_down)


def _combine_kernel(y0_ref, y1_ref, ew_ref, x_ref, g_ref, w_ref, o_ref):
    tm, d = x_ref.shape
    nt = d // LANE
    ew = ew_ref[...]
    moe = (_rows_to_matrix(y0_ref, tm, nt) * ew[:, 0:1]
           + _rows_to_matrix(y1_ref, tm, nt) * ew[:, 1:2])
    xo = x_ref[...] + g_ref[0] * moe
    ms = jnp.mean(xo * xo, axis=-1, keepdims=True)
    o_ref[...] = xo * lax.rsqrt(ms + EPS) * w_ref[...]


def _combine(y, ew, x2, mod3, gate_chunk, rows_per_batch, final_w, tm=512):
    m, d = x2.shape
    pitch = _pitch(d // LANE)
    tm = min(tm, rows_per_batch)
    tiles = m // tm
    tiles_per_batch = rows_per_batch // tm
    return pl.pallas_call(
        _combine_kernel,
        out_shape=jax.ShapeDtypeStruct((m, d), F32),
        grid=(tiles,),
        in_specs=[pl.BlockSpec((tm * pitch, LANE), lambda i: (i, 0)),
                  pl.BlockSpec((tm * pitch, LANE), lambda i: (tiles + i, 0)),
                  pl.BlockSpec((tm, LANE), lambda i: (i, 0)),
                  pl.BlockSpec((tm, d), lambda i: (i, 0)),
                  pl.BlockSpec((1, 1, d), lambda i: (i // tiles_per_batch, 0, gate_chunk)),
                  pl.BlockSpec((1, d), lambda i: (0, 0))],
        out_specs=pl.BlockSpec((tm, d), lambda i: (i, 0)),
        compiler_params=_cparams(("parallel",)),
        name="moe_combine",
    )(y, y, ew, x2, mod3, final_w.reshape(1, d))


def _dispatch_tables(eid, n_tok, pitch):
    top_k = eid.shape[1]
    n_assign = n_tok * top_k
    expert = eid.reshape(-1)
    key = jnp.sort(expert * n_assign + jnp.arange(n_assign, dtype=jnp.int32))
    sorted_assign = key % n_assign
    bounds = jnp.arange(N_EXPERTS + 1, dtype=jnp.int32) * n_assign
    start = jnp.sum((key[None, :] < bounds[:, None]).astype(jnp.int32), axis=1)
    counts = start[1:] - start[:-1]
    nblk = (counts + MOE_BLOCK - 1) // MOE_BLOCK
    blk_end = jnp.cumsum(nblk)
    blk_start = blk_end - nblk
    steps = -(-n_assign // MOE_BLOCK) + N_EXPERTS + 1
    bidx = jnp.arange(steps, dtype=jnp.int32)
    lane = jnp.arange(MOE_BLOCK, dtype=jnp.int32)[None, :]
    block_expert = jnp.minimum(jnp.sum((blk_end[None, :] <= bidx[:, None]).astype(jnp.int32), axis=1),
                               N_EXPERTS - 1)
    onehot = (block_expert[:, None] == jnp.arange(N_EXPERTS, dtype=jnp.int32)[None, :]).astype(jnp.int32)

    def lookup(table):
        return jnp.sum(onehot * table[None, :], axis=1)

    in_expert = (bidx - lookup(blk_start)) * MOE_BLOCK
    n_valid = jnp.clip(lookup(counts) - in_expert, 0, MOE_BLOCK)
    first_src = jnp.clip(lookup(start[:-1]) + in_expert, 0, n_assign)
    valid = lane < n_valid[:, None]
    assign = sorted_assign[jnp.minimum(first_src[:, None] + lane, n_assign - 1)]
    tok = assign // top_k
    src_rows = jnp.where(valid, tok, (bidx[:, None] * MOE_BLOCK + lane) % n_tok) * pitch
    dst_slot = jnp.where(valid, (assign % top_k) * n_tok + tok, top_k * n_tok + lane)
    dst_rows = jnp.concatenate([top_k * n_tok + lane, dst_slot[:-1]], axis=0) * pitch
    n_used = blk_end[-1].astype(jnp.int32)
    first = jnp.concatenate([jnp.ones((1,), jnp.int32),
                             (block_expert[1:] != block_expert[:-1]).astype(jnp.int32)])
    expert_ordinal = (jnp.cumsum(first) - 1).astype(jnp.int32)
    later = (jnp.arange(N_EXPERTS)[None, :] > block_expert[:, None]) & (counts[None, :] > 0)
    next_expert = jnp.min(jnp.where(later, jnp.arange(N_EXPERTS, dtype=jnp.int32)[None, :], N_EXPERTS), axis=1)
    next_expert = jnp.where(next_expert < N_EXPERTS, next_expert, -1).astype(jnp.int32)
    tables = (block_expert, n_used.reshape(1), expert_ordinal, next_expert)
    return (src_rows.astype(jnp.int32).reshape(steps, 1, MOE_BLOCK),
            dst_rows.astype(jnp.int32).reshape(steps, 1, MOE_BLOCK), tables)


def kernel(x, c, ctx, c_ctx, ada_w, ada_b, norm1_w, w_in, ssm_conv_w, ssm_conv_b, dt_bias, a_log, d_skip, ssm_norm_w, ssm_out_w, cf_dw_w, cf_dw_b, cf_ln_w, cf_ln_b, cf_out_w, cf_out_b, w_o, norm2_w, router_group_w, router_group_b, router_expert_w, router_expert_b, expert_w_gate, expert_w_up, expert_w_down, final_norm_w):
    bsz, seq, d = x.shape
    l_ctx = ctx.shape[1]
    n_tok = bsz * seq
    d_inner = ssm_norm_w.shape[1]
    gn = N_GROUPS * D_STATE
    xbc_dim = d_inner + 2 * gn
    off_dt = xbc_dim
    off_z = off_dt + N_HEADS
    off_glu = off_z + d_inner
    off_gate = off_glu + 2 * d

    ctx_row = bsz
    crows = jnp.zeros((8, d), F32).at[:bsz].set(c).at[ctx_row].set(c_ctx)
    mod = _ada(crows, ada_w[0], ada_b[0])
    mod3 = mod.reshape(8, 1, 6 * d)
    lat_rows = jnp.arange(bsz, dtype=jnp.int32)
    ctx_rows = jnp.full((bsz,), ctx_row, jnp.int32)

    h_lat = _normmod(x, norm1_w[0], mod3, lat_rows, 0, 1, BF16).reshape(n_tok, d)
    h_ctx = _normmod(ctx, norm1_w[0], mod3, ctx_rows, 0, 1, BF16).reshape(bsz * l_ctx, d)

    wt, (r_xbc, r_dt, r_z, r_glu, r_gate) = _pack_wt(
        jnp.transpose(w_in[0]),
        [(0, xbc_dim), (off_dt, off_z), (off_z, off_glu), (off_glu, off_gate), (off_gate, off_gate + 2 * d)])

    xbc_lat = _mm(h_lat, wt, tn=2048, name="in_xbc", rows=(r_xbc, xbc_dim)).reshape(bsz, seq, xbc_dim)
    xbc_ctx = _mm(h_ctx, wt, tm=512, name="in_xbc_ctx", rows=(r_xbc, xbc_dim)).reshape(bsz, l_ctx, xbc_dim)
    dt_lat = _mm(h_lat, wt, name="in_dt", rows=(r_dt, LANE)).reshape(bsz, seq, LANE)
    dt_ctx = _mm(h_ctx, wt, tm=512, name="in_dt_ctx", rows=(r_dt, LANE)).reshape(bsz, l_ctx, LANE)
    sz = _mm(h_lat, wt, act="silu", tn=2048, name="in_z", rows=(r_z, d_inner))
    u = _mm_glu(h_lat, wt, r_glu, d, tn=1024)

    xbc_act = _conv7(xbc_ctx, xbc_lat, ssm_conv_w[0], ssm_conv_b[0])

    def ssd_params(k):
        par = jnp.zeros((8, LANE), F32).at[0, :N_HEADS].set(dt_bias[0, k]).at[1, :N_HEADS].set(a_log[0, k])
        return par, jnp.repeat(d_skip[0, k], HEAD_DIM).reshape(N_HEADS // 2, 1, LANE)

    y_bwd = _ssd(xbc_act, dt_ctx, dt_lat, *ssd_params(1), reverse=True)
    gnorm = _ssd(xbc_act, dt_ctx, dt_lat, *ssd_params(0), reverse=False,
                 norm_with=(y_bwd, sz, ssm_norm_w[0]))
    y_ssd = _mm(gnorm, ssm_out_w[0].astype(BF16), tn=512, name="ssm_out")

    cv = _conv31(u.reshape(bsz, seq, d), cf_dw_w[0], cf_dw_b[0]).reshape(n_tok, d)
    merged = _mm_merge(cv, cf_ln_w[0], cf_ln_b[0], cf_out_w[0].astype(BF16), cf_out_b[0],
                       h_lat, wt, r_gate, y_ssd)
    x1 = _mm_resid(merged, w_o[0].astype(BF16), x.reshape(n_tok, d), mod3, 2, seq, tm=2048)

    n_r = MOE_GROUPS + N_EXPERTS
    rw = jnp.pad(jnp.concatenate([router_group_w[0], router_expert_w[0]], axis=1),
                 ((0, 0), (0, LANE - n_r))).astype(BF16)
    rb = jnp.pad(jnp.concatenate([router_group_b[0], router_expert_b[0]]), (0, LANE - n_r)).reshape(1, LANE)
    h2t, eid, ew = _route(x1, norm2_w[0], mod3, 3, 4, seq, rw, rb)

    src_rows, dst_rows, tables = _dispatch_tables(eid[:, :2], n_tok, _pitch(d // LANE))
    hid = _expert_up(h2t, src_rows, expert_w_gate[0], expert_w_up[0], tables)
    y = _expert_down(hid, dst_rows, expert_w_down[0], tables, 2 * n_tok + 2 * MOE_BLOCK)
    out = _combine(y, ew, x1, mod3, 5, seq, final_norm_w)
    return out.reshape(bsz, seq, d)
```

```python
import functools

import jax
import jax.numpy as jnp
from jax import lax
from jax.experimental import pallas as pl
from jax.experimental.pallas import tpu as pltpu

F32 = jnp.float32
BF16 = jnp.bfloat16

EPS = 1e-6
GRID_W = 64
HEAD_DIM = 64
N_HEADS = 64
N_GROUPS = 8
D_STATE = 128
CHUNK = 128
SSM_CONV = 7
CF_KERNEL = 31
MOE_GROUPS = 8
EXPERTS_PER_GROUP = 8
N_EXPERTS = 64
MOE_BLOCK = 256
LANE = 128
LOG2E = 1.4426950408889634
VMEM_LIMIT = 56 * 1024 * 1024


def _cparams(sem):
    return pltpu.CompilerParams(dimension_semantics=sem, vmem_limit_bytes=VMEM_LIMIT)


def _silu(v):
    return v * jax.nn.sigmoid(v)


def _pitch(nt):
    return nt + 1


def _ada_kernel(c_ref, w_ref, b_ref, o_ref):
    s = _silu(c_ref[...])
    o_ref[...] = jnp.dot(s.astype(BF16), w_ref[...].astype(BF16),
                         preferred_element_type=F32) + b_ref[...]


def _ada(crows, ada_w, ada_b, tn=1024):
    r, d = crows.shape
    n = ada_w.shape[1]
    return pl.pallas_call(
        _ada_kernel,
        out_shape=jax.ShapeDtypeStruct((r, n), F32),
        grid=(n // tn,),
        in_specs=[pl.BlockSpec((r, d), lambda j: (0, 0)),
                  pl.BlockSpec((d, tn), lambda j: (0, j)),
                  pl.BlockSpec((1, tn), lambda j: (0, j))],
        out_specs=pl.BlockSpec((r, tn), lambda j: (0, j)),
        compiler_params=_cparams(("parallel",)),
        name="ada",
    )(crows, ada_w, ada_b.reshape(1, n))


def _normmod_kernel(rows_ref, x_ref, w_ref, sh_ref, sc_ref, o_ref):
    del rows_ref
    xf = x_ref[0]
    ms = jnp.mean(xf * xf, axis=-1, keepdims=True)
    y = xf * lax.rsqrt(ms + EPS) * w_ref[...]
    o_ref[0] = (y * (1.0 + sc_ref[0]) + sh_ref[0]).astype(o_ref.dtype)


def _normmod(x3, w, mod3, rows, shift_chunk, scale_chunk, out_dtype, tm=512):
    bx, l, d = x3.shape
    tm = min(tm, l)
    grid_spec = pltpu.PrefetchScalarGridSpec(
        num_scalar_prefetch=1,
        grid=(bx, l // tm),
        in_specs=[pl.BlockSpec((1, tm, d), lambda b, i, r: (b, i, 0)),
                  pl.BlockSpec((1, d), lambda b, i, r: (0, 0)),
                  pl.BlockSpec((1, 1, d), lambda b, i, r: (r[b], 0, shift_chunk)),
                  pl.BlockSpec((1, 1, d), lambda b, i, r: (r[b], 0, scale_chunk))],
        out_specs=pl.BlockSpec((1, tm, d), lambda b, i, r: (b, i, 0)),
    )
    return pl.pallas_call(
        _normmod_kernel,
        out_shape=jax.ShapeDtypeStruct((bx, l, d), out_dtype),
        grid_spec=grid_spec,
        compiler_params=_cparams(("parallel", "parallel")),
        name="normmod",
    )(rows, x3, w.reshape(1, d), mod3, mod3)


def _dot_nt(a, wt):
    return lax.dot_general(a, wt, (((1,), (1,)), ((), ())), preferred_element_type=F32)


def _mm_kernel(a_ref, w_ref, *rest, act, has_bias, w_rows):
    o_ref = rest[-1]
    a = a_ref[...]
    acc = _dot_nt(a, w_ref[...]) if w_rows else jnp.dot(a, w_ref[...], preferred_element_type=F32)
    if has_bias:
        acc = acc + rest[0][...]
    if act == "silu":
        acc = _silu(acc)
    elif act == "sigmoid":
        acc = jax.nn.sigmoid(acc)
    o_ref[...] = acc.astype(o_ref.dtype)


def _mm(a, w, bias=None, act=None, out_dtype=F32, tm=1024, tn=1024, name="mm", rows=None):
    m, k = a.shape
    start, n = (0, w.shape[1]) if rows is None else rows
    tm, tn = min(tm, m), min(tn, n)
    j0 = start // tn
    w_spec = (pl.BlockSpec((k, tn), lambda i, j: (0, j)) if rows is None
              else pl.BlockSpec((tn, k), lambda i, j: (j0 + j, 0)))
    in_specs = [pl.BlockSpec((tm, k), lambda i, j: (i, 0)), w_spec]
    args = [a, w]
    if bias is not None:
        in_specs.append(pl.BlockSpec((1, tn), lambda i, j: (0, j)))
        args.append(bias.reshape(1, n))
    return pl.pallas_call(
        functools.partial(_mm_kernel, act=act, has_bias=bias is not None, w_rows=rows is not None),
        out_shape=jax.ShapeDtypeStruct((m, n), out_dtype),
        grid=(m // tm, n // tn),
        in_specs=in_specs,
        out_specs=pl.BlockSpec((tm, tn), lambda i, j: (i, j)),
        compiler_params=_cparams(("parallel", "parallel")),
        name=name,
    )(*args)


def _mm_glu_kernel(a_ref, wa_ref, wb_ref, o_ref):
    a = a_ref[...]
    va = _dot_nt(a, wa_ref[...])
    vb = _dot_nt(a, wb_ref[...])
    o_ref[...] = va * jax.nn.sigmoid(vb)


def _mm_glu(a, wt, start, n, tm=1024, tn=512):
    m, k = a.shape
    tm = min(tm, m)
    ja, jb = start // tn, (start + n) // tn
    return pl.pallas_call(
        _mm_glu_kernel,
        out_shape=jax.ShapeDtypeStruct((m, n), F32),
        grid=(m // tm, n // tn),
        in_specs=[pl.BlockSpec((tm, k), lambda i, j: (i, 0)),
                  pl.BlockSpec((tn, k), lambda i, j: (ja + j, 0)),
                  pl.BlockSpec((tn, k), lambda i, j: (jb + j, 0))],
        out_specs=pl.BlockSpec((tm, tn), lambda i, j: (i, j)),
        compiler_params=_cparams(("parallel", "parallel")),
        name="mm_glu",
    )(a, wt, wt)


def _mm_merge_kernel(cv_ref, lw_ref, lb_ref, w_ref, b_ref, h_ref, wga_ref, wgb_ref, ys_ref, o_ref, u_ref):
    @pl.when(pl.program_id(1) == 0)
    def _():
        xf = cv_ref[...]
        mu = jnp.mean(xf, axis=-1, keepdims=True)
        xc = xf - mu
        var = jnp.mean(xc * xc, axis=-1, keepdims=True)
        y = xc * lax.rsqrt(var + EPS) * lw_ref[...] + lb_ref[...]
        u_ref[...] = _silu(y).astype(u_ref.dtype)

    ycf = jnp.dot(u_ref[...], w_ref[...], preferred_element_type=F32) + b_ref[...]
    h = h_ref[...]
    gate_a = jax.nn.sigmoid(_dot_nt(h, wga_ref[...]))
    gate_b = jax.nn.sigmoid(_dot_nt(h, wgb_ref[...]))
    o_ref[...] = (gate_a * ys_ref[...] + gate_b * ycf).astype(o_ref.dtype)


def _mm_merge(cv, ln_w, ln_b, w, bias, h, wt, gate_start, y_ssd, tm=512, tn=512):
    m, k = cv.shape
    n = w.shape[1]
    tm = min(tm, m)
    ja, jb = gate_start // tn, (gate_start + n) // tn
    return pl.pallas_call(
        _mm_merge_kernel,
        out_shape=jax.ShapeDtypeStruct((m, n), BF16),
        grid=(m // tm, n // tn),
        in_specs=[pl.BlockSpec((tm, k), lambda i, j: (i, 0)),
                  pl.BlockSpec((1, k), lambda i, j: (0, 0)),
                  pl.BlockSpec((1, k), lambda i, j: (0, 0)),
                  pl.BlockSpec((k, tn), lambda i, j: (0, j)),
                  pl.BlockSpec((1, tn), lambda i, j: (0, j)),
                  pl.BlockSpec((tm, k), lambda i, j: (i, 0)),
                  pl.BlockSpec((tn, k), lambda i, j: (ja + j, 0)),
                  pl.BlockSpec((tn, k), lambda i, j: (jb + j, 0)),
                  pl.BlockSpec((tm, tn), lambda i, j: (i, j))],
        out_specs=pl.BlockSpec((tm, tn), lambda i, j: (i, j)),
        scratch_shapes=[pltpu.VMEM((tm, k), BF16)],
        compiler_params=_cparams(("parallel", "arbitrary")),
        name="mm_merge",
    )(cv, ln_w.reshape(1, k), ln_b.reshape(1, k), w, bias.reshape(1, n), h, wt, wt, y_ssd)


def _mm_resid_kernel(a_ref, w_ref, x_ref, g_ref, o_ref):
    out = jnp.dot(a_ref[...], w_ref[...], preferred_element_type=F32)
    o_ref[...] = x_ref[...] + g_ref[0] * out


def _mm_resid(a, w, x2, mod3, gate_chunk, rows_per_batch, tm=1024, tn=512):
    m, k = a.shape
    n = w.shape[1]
    tm = min(tm, rows_per_batch)
    nj = n // tn
    tiles_per_batch = rows_per_batch // tm
    return pl.pallas_call(
        _mm_resid_kernel,
        out_shape=jax.ShapeDtypeStruct((m, n), F32),
        grid=(m // tm, nj),
        in_specs=[pl.BlockSpec((tm, k), lambda i, j: (i, 0)),
                  pl.BlockSpec((k, tn), lambda i, j: (0, j)),
                  pl.BlockSpec((tm, tn), lambda i, j: (i, j)),
                  pl.BlockSpec((1, 1, tn),
                               lambda i, j: (i // tiles_per_batch, 0, gate_chunk * nj + j))],
        out_specs=pl.BlockSpec((tm, tn), lambda i, j: (i, j)),
        compiler_params=_cparams(("parallel", "parallel")),
        name="mm_resid",
    )(a, w, x2, mod3)


W_ALIGN = 2048


def _pack_wt_kernel(valid_ref, off_ref, w_ref, o_ref):
    del off_ref
    nrow = valid_ref[pl.program_id(0)]
    row = lax.broadcasted_iota(jnp.int32, o_ref.shape, 0)
    o_ref[...] = jnp.where(row < nrow, w_ref[...], 0.0).astype(o_ref.dtype)


def _pack_wt(wt, segments, tr=1024):
    n, k = wt.shape
    starts, src_off, valid = [], [], []
    pos = 0
    for lo, hi in segments:
        pos = -(-pos // W_ALIGN) * W_ALIGN
        starts.append(pos)
        while len(src_off) < pos // tr:
            src_off.append(0)
            valid.append(0)
        for r in range(lo, hi, tr):
            src_off.append(min(r, n - tr))
            valid.append(min(tr, hi - r))
            assert r <= n - tr or hi - r == tr
        pos += -(-(hi - lo) // tr) * tr
    total = -(-pos // W_ALIGN) * W_ALIGN
    while len(src_off) < total // tr:
        src_off.append(0)
        valid.append(0)
    grid_spec = pltpu.PrefetchScalarGridSpec(
        num_scalar_prefetch=2,
        grid=(total // tr,),
        in_specs=[pl.BlockSpec((pl.Element(tr), pl.Element(k)), lambda t, v, off: (off[t] * 8, 0))],
        out_specs=pl.BlockSpec((tr, k), lambda t, v, off: (t, 0)),
    )
    packed = pl.pallas_call(
        _pack_wt_kernel,
        out_shape=jax.ShapeDtypeStruct((total, k), BF16),
        grid_spec=grid_spec,
        compiler_params=_cparams(("parallel",)),
        name="pack_wt",
    )(jnp.asarray(valid, jnp.int32), jnp.asarray(src_off, jnp.int32) // 8, wt)
    return packed, starts


_CONV_PAD = 8


def _conv7_kernel(ctx_ref, lat_ref, w_ref, b_ref, o_ref, pad_ref, *, l_ctx, l_lat):
    p = _CONV_PAD
    zeros = jnp.zeros((p, LANE), F32)
    off_ctx = p
    off_lat = 2 * p + l_ctx
    pad_ref[0:p, :] = zeros
    pad_ref[off_ctx + l_ctx:off_lat, :] = zeros
    pad_ref[off_lat + l_lat:off_lat + l_lat + p, :] = zeros
    pad_ref[off_ctx:off_ctx + l_ctx, :] = ctx_ref[0]
    pad_ref[off_lat:off_lat + l_lat, :] = lat_ref[0]
    reach = SSM_CONV // 2
    bias = b_ref[...]

    def chunk(pad_base, out_base):
        acc = jnp.broadcast_to(bias, (CHUNK, LANE))
        for k in range(SSM_CONV):
            tap = pad_ref[pl.ds(pad_base - reach + k, CHUNK), :]
            acc = acc + tap * w_ref[k:k + 1, :]
        o_ref[0, 0, pl.ds(out_base, CHUNK), :] = _silu(acc)

    def ctx_body(j, c):
        base = pl.multiple_of(j * CHUNK, CHUNK)
        chunk(off_ctx + base, base)
        return c

    def lat_body(j, c):
        base = pl.multiple_of(j * CHUNK, CHUNK)
        chunk(off_lat + base, l_ctx + base)
        return c

    lax.fori_loop(0, l_ctx // CHUNK, ctx_body, 0)
    lax.fori_loop(0, l_lat // CHUNK, lat_body, 0, unroll=2)


def _conv7(ctx_raw, lat_raw, w, b):
    bsz, l_ctx, c = ctx_raw.shape
    l_lat = lat_raw.shape[1]
    ltot = l_ctx + l_lat
    nct = c // LANE
    return pl.pallas_call(
        functools.partial(_conv7_kernel, l_ctx=l_ctx, l_lat=l_lat),
        out_shape=jax.ShapeDtypeStruct((bsz, nct, ltot, LANE), F32),
        grid=(bsz, nct),
        in_specs=[pl.BlockSpec((1, l_ctx, LANE), lambda bi, ci: (bi, 0, ci)),
                  pl.BlockSpec((1, l_lat, LANE), lambda bi, ci: (bi, 0, ci)),
                  pl.BlockSpec((SSM_CONV, LANE), lambda bi, ci: (0, ci)),
                  pl.BlockSpec((1, LANE), lambda bi, ci: (0, ci))],
        out_specs=pl.BlockSpec((1, 1, ltot, LANE), lambda bi, ci: (bi, ci, 0, 0)),
        scratch_shapes=[pltpu.VMEM((ltot + 3 * _CONV_PAD, LANE), F32)],
        compiler_params=_cparams(("parallel", "parallel")),
        name="conv7",
    )(ctx_raw, lat_raw, w, b.reshape(1, c))


def _ssd_kernel(xbc_ref, dtc_ref, dtl_ref, par_ref, dexp_ref, ex_ref, *rest, reverse, n_ctx, fuse_norm):
    if fuse_norm:
        yo_ref, sz_ref, nw_ref, o_ref, st_ref, cumt_ref, y_ref = rest
    else:
        y_ref, st_ref, cumt_ref = rest
    i = pl.program_id(1)

    @pl.when(i == 0)
    def _():
        st_ref[...] = jnp.zeros_like(st_ref)

    dt_raw = jnp.where(i < n_ctx, dtc_ref[0], dtl_ref[0])
    bias = par_ref[0:1, :]
    a = -jnp.exp(par_ref[1:2, :])
    dt = jax.nn.softplus(dt_raw + bias)
    cum = dt * a
    row = lax.broadcasted_iota(jnp.int32, (CHUNK, LANE), 0)
    k = 1
    while k < CHUNK:
        if reverse:
            cum = cum + jnp.where(row < CHUNK - k, pltpu.roll(cum, CHUNK - k, 0), 0.0)
        else:
            cum = cum + jnp.where(row >= k, pltpu.roll(cum, k, 0), 0.0)
        k *= 2
    last = 0 if reverse else CHUNK - 1
    cum = cum * LOG2E
    cumt_ref[...] = cum.T
    li = lax.broadcasted_iota(jnp.int32, (CHUNK, CHUNK), 0)
    si = lax.broadcasted_iota(jnp.int32, (CHUNK, CHUNK), 1)
    causal = (li <= si) if reverse else (li >= si)
    lo = lax.broadcasted_iota(jnp.int32, (CHUNK, LANE), 1) < HEAD_DIM
    heads_per_group = N_HEADS // N_GROUPS
    pairs = heads_per_group // 2
    x_tiles = N_HEADS // 2

    def group(g, carry):
        shift = (LANE - heads_per_group * g) & (LANE - 1)
        cum_g = pltpu.roll(cum, shift, 1)
        dt_g = pltpu.roll(dt, shift, 1)
        cum_t = cumt_ref[pl.ds(pl.multiple_of(heads_per_group * g, heads_per_group), heads_per_group), :]
        bb = xbc_ref[0, x_tiles + g].astype(BF16)
        cb = xbc_ref[0, x_tiles + N_GROUPS + g].astype(BF16)
        scores = lax.dot_general(cb, bb, (((1,), (1,)), ((), ())), preferred_element_type=F32)
        y_off = jnp.dot(cb, st_ref[g].astype(BF16), preferred_element_type=F32)
        d_hi = dt_g.astype(BF16)
        r_hi = dt_g - d_hi.astype(F32)
        d_mid = r_hi.astype(BF16)
        d_lo = (r_hi - d_mid.astype(F32)).astype(BF16)
        dt_x = (jnp.dot(jnp.concatenate([d_hi, d_mid], axis=1), ex_ref[...], preferred_element_type=F32)
                + jnp.dot(d_lo, ex_ref[0:LANE, :], preferred_element_type=F32))
        xw_parts, dec_parts = [], []
        for p in range(pairs):
            j0, j1 = 2 * p, 2 * p + 1
            x2 = xbc_ref[0, pairs * g + p]
            c0 = cum_g[:, j0:j0 + 1]
            c1 = cum_g[:, j1:j1 + 1]
            l0 = jnp.exp2(jnp.where(causal, c0 - cum_t[j0:j0 + 1, :], -jnp.inf))
            l1 = jnp.exp2(jnp.where(causal, c1 - cum_t[j1:j1 + 1, :], -jnp.inf))
            m0 = (scores * l0).astype(BF16)
            m1 = (scores * l1).astype(BF16)
            dt2 = dt_x[:, p * LANE:(p + 1) * LANE]
            c2 = jnp.where(lo, c0, c1)
            xdt = x2 * dt2
            xdt_b = xdt.astype(BF16)
            zero = jnp.zeros_like(xdt_b)
            y_diag = jnp.dot(jnp.concatenate([m0, m1], axis=1),
                             jnp.concatenate([jnp.where(lo, xdt_b, zero), jnp.where(lo, zero, xdt_b)], axis=0),
                             preferred_element_type=F32)
            e2 = jnp.exp2(c2)
            y = y_diag + y_off[:, p * LANE:(p + 1) * LANE] * e2
            y_ref[0, pairs * g + p] = y + dexp_ref[pairs * g + p] * x2
            to_end = jnp.exp2(c2[last:last + 1, :] - c2)
            xw_parts.append((xdt * to_end).astype(BF16))
            dec_parts.append(e2[last:last + 1, :])
        xw = jnp.concatenate(xw_parts, axis=1)
        dec = jnp.concatenate(dec_parts, axis=1)
        upd = lax.dot_general(bb, xw, (((0,), (0,)), ((), ())), preferred_element_type=F32)
        st_ref[g] = st_ref[g] * dec + upd
        return carry

    lax.fori_loop(0, N_GROUPS, group, 0, unroll=2)

    if fuse_norm:
        sq = jnp.zeros((CHUNK, LANE), F32)
        for j in range(x_tiles):
            gj = (y_ref[0, j] + yo_ref[0, j]) * sz_ref[:, j * LANE:(j + 1) * LANE]
            y_ref[0, j] = gj
            sq = sq + gj * gj
        r = lax.rsqrt(jnp.sum(sq, axis=-1, keepdims=True) / (x_tiles * LANE) + EPS)
        for j in range(x_tiles):
            sl = slice(j * LANE, (j + 1) * LANE)
            o_ref[:, sl] = (y_ref[0, j] * r * nw_ref[:, sl]).astype(o_ref.dtype)


def _ssd(xbc_act, dt_ctx, dt_lat, par, dexp, reverse, norm_with=None):
    bsz, ntile, ltot, _ = xbc_act.shape
    l_ctx = dt_ctx.shape[1]
    l_lat = dt_lat.shape[1]
    n_ctx = l_ctx // CHUNK
    n_lat = l_lat // CHUNK
    steps = n_ctx + n_lat
    x_tiles = N_HEADS // 2
    gw = (N_HEADS // N_GROUPS) * HEAD_DIM
    e1 = (jnp.arange(gw)[None, :] // HEAD_DIM == jnp.arange(LANE)[:, None]).astype(BF16)
    expand = jnp.concatenate([e1, e1], axis=0)

    if reverse:
        def cat_chunk(i):
            return jnp.where(i < n_ctx, n_ctx - 1 - i, n_ctx + steps - 1 - i)

        def ctx_chunk(i):
            return jnp.maximum(n_ctx - 1 - i, 0)

        def lat_chunk(i):
            return jnp.minimum(steps - 1 - i, n_lat - 1)
    else:
        def cat_chunk(i):
            return i

        def ctx_chunk(i):
            return jnp.minimum(i, n_ctx - 1)

        def lat_chunk(i):
            return jnp.maximum(i - n_ctx, 0)

    y_spec = pl.BlockSpec((1, x_tiles, CHUNK, LANE), lambda b, i: (b, 0, lat_chunk(i), 0))
    in_specs = [pl.BlockSpec((1, ntile, CHUNK, LANE), lambda b, i: (b, 0, cat_chunk(i), 0)),
                pl.BlockSpec((1, CHUNK, LANE), lambda b, i: (b, ctx_chunk(i), 0)),
                pl.BlockSpec((1, CHUNK, LANE), lambda b, i: (b, lat_chunk(i), 0)),
                pl.BlockSpec((8, LANE), lambda b, i: (0, 0)),
                pl.BlockSpec((x_tiles, 1, LANE), lambda b, i: (0, 0, 0)),
                pl.BlockSpec((2 * LANE, gw), lambda b, i: (0, 0))]
    args = [xbc_act, dt_ctx, dt_lat, par, dexp, expand]
    scratch = [pltpu.VMEM((N_GROUPS, D_STATE, gw), F32), pltpu.VMEM((LANE, CHUNK), F32)]
    if norm_with is None:
        out_shape = jax.ShapeDtypeStruct((bsz, x_tiles, l_lat, LANE), F32)
        out_spec = y_spec
    else:
        y_other, silu_z, norm_w = norm_with
        dn = x_tiles * LANE
        row_spec = pl.BlockSpec((CHUNK, dn), lambda b, i: (b * n_lat + lat_chunk(i), 0))
        in_specs += [y_spec, row_spec, pl.BlockSpec((1, dn), lambda b, i: (0, 0))]
        args += [y_other, silu_z, norm_w.reshape(1, dn)]
        out_shape = jax.ShapeDtypeStruct((bsz * l_lat, dn), BF16)
        out_spec = row_spec
        scratch.append(pltpu.VMEM((1, x_tiles, CHUNK, LANE), F32))
    return pl.pallas_call(
        functools.partial(_ssd_kernel, reverse=reverse, n_ctx=n_ctx, fuse_norm=norm_with is not None),
        out_shape=out_shape,
        grid=(bsz, steps),
        in_specs=in_specs,
        out_specs=out_spec,
        scratch_shapes=scratch,
        compiler_params=_cparams(("parallel", "arbitrary")),
        name="ssd_bwd" if reverse else "ssd_fwd",
    )(*args)


def _conv31_kernel(u_ref, w_ref, b_ref, o_ref, pad_ref, *, seq):
    halo = (CF_KERNEL // 2) * GRID_W
    zeros = jnp.zeros((halo, LANE), F32)
    pad_ref[0:halo, :] = zeros
    pad_ref[halo + seq:halo + seq + halo, :] = zeros
    pad_ref[halo:halo + seq, :] = u_ref[0]
    bias = b_ref[...]

    def body(j, c):
        base = pl.multiple_of(j * CHUNK, CHUNK)
        acc = jnp.broadcast_to(bias, (CHUNK, LANE))
        for k in range(CF_KERNEL):
            tap = pad_ref[pl.ds(pl.multiple_of(base + k * GRID_W, GRID_W), CHUNK), :]
            acc = acc + tap * w_ref[k:k + 1, :]
        o_ref[0, pl.ds(base, CHUNK), :] = acc
        return c

    lax.fori_loop(0, seq // CHUNK, body, 0, unroll=2)


def _conv31(u3, w, b):
    bsz, s, c = u3.shape
    halo = (CF_KERNEL // 2) * GRID_W
    return pl.pallas_call(
        functools.partial(_conv31_kernel, seq=s),
        out_shape=jax.ShapeDtypeStruct((bsz, s, c), F32),
        grid=(bsz, c // LANE),
        in_specs=[pl.BlockSpec((1, s, LANE), lambda bi, ci: (bi, 0, ci)),
                  pl.BlockSpec((CF_KERNEL, LANE), lambda bi, ci: (0, ci)),
                  pl.BlockSpec((1, LANE), lambda bi, ci: (0, ci))],
        out_specs=pl.BlockSpec((1, s, LANE), lambda bi, ci: (bi, 0, ci)),
        scratch_shapes=[pltpu.VMEM((s + 2 * halo, LANE), F32)],
        compiler_params=_cparams(("parallel", "parallel")),
        name="conv31",
    )(u3, w, b.reshape(1, c))


def _route_kernel(x_ref, w_ref, sh_ref, sc_ref, rw_ref, rb_ref, h_ref, eid_ref, ew_ref):
    xf = x_ref[...]
    ms = jnp.mean(xf * xf, axis=-1, keepdims=True)
    h = xf * lax.rsqrt(ms + EPS) * w_ref[...]
    h = h * (1.0 + sc_ref[0]) + sh_ref[0]
    tm = xf.shape[0]
    nt = xf.shape[1] // LANE
    pitch = _pitch(nt)
    for j in range(nt):
        h_ref[pl.ds(j, tm, stride=pitch), :] = h[:, j * LANE:(j + 1) * LANE]
    for j in range(nt, pitch):
        h_ref[pl.ds(j, tm, stride=pitch), :] = jnp.zeros((tm, LANE), F32)
    logits = jnp.dot(h.astype(BF16), rw_ref[...], preferred_element_type=F32) + rb_ref[...]
    lane = lax.broadcasted_iota(jnp.int32, (tm, LANE), 1)
    lane_f = lane.astype(F32)
    ninf = -jnp.inf
    gl = jnp.where(lane < MOE_GROUPS, logits, ninf)
    gmax = jnp.max(gl, axis=-1, keepdims=True)
    gidx = jnp.min(jnp.where(gl == gmax, lane_f, float(LANE)), axis=-1, keepdims=True)
    gsum = jnp.sum(jnp.exp(gl - gmax), axis=-1, keepdims=True)
    g_p = 1.0 / gsum
    first = float(MOE_GROUPS) + gidx * float(EXPERTS_PER_GROUP)
    in_group = (lane_f >= first) & (lane_f < first + float(EXPERTS_PER_GROUP))
    el = jnp.where(in_group, logits, ninf)
    m1 = jnp.max(el, axis=-1, keepdims=True)
    i1 = jnp.min(jnp.where(el == m1, lane_f, float(LANE)), axis=-1, keepdims=True)
    el2 = jnp.where(lane_f == i1, ninf, el)
    m2 = jnp.max(el2, axis=-1, keepdims=True)
    i2 = jnp.min(jnp.where(el2 == m2, lane_f, float(LANE)), axis=-1, keepdims=True)
    e21 = jnp.exp(m2 - m1)
    den = 1.0 + e21
    w1 = (1.0 / den) * g_p
    w2 = (e21 / den) * g_p
    e1 = (i1 - float(MOE_GROUPS)).astype(jnp.int32)
    e2 = (i2 - float(MOE_GROUPS)).astype(jnp.int32)
    eid_ref[...] = jnp.where(lane == 0, e1, jnp.where(lane == 1, e2, 0))
    ew_ref[...] = jnp.where(lane == 0, w1, jnp.where(lane == 1, w2, 0.0))


def _route(x2, w, mod3, shift_chunk, scale_chunk, rows_per_batch, rw, rb, tm=512):
    m, d = x2.shape
    pitch = _pitch(d // LANE)
    tm = min(tm, rows_per_batch)
    tiles_per_batch = rows_per_batch // tm
    return pl.pallas_call(
        _route_kernel,
        out_shape=(jax.ShapeDtypeStruct((m * pitch, LANE), F32),
                   jax.ShapeDtypeStruct((m, LANE), jnp.int32),
                   jax.ShapeDtypeStruct((m, LANE), F32)),
        grid=(m // tm,),
        in_specs=[pl.BlockSpec((tm, d), lambda i: (i, 0)),
                  pl.BlockSpec((1, d), lambda i: (0, 0)),
                  pl.BlockSpec((1, 1, d), lambda i: (i // tiles_per_batch, 0, shift_chunk)),
                  pl.BlockSpec((1, 1, d), lambda i: (i // tiles_per_batch, 0, scale_chunk)),
                  pl.BlockSpec((d, LANE), lambda i: (0, 0)),
                  pl.BlockSpec((1, LANE), lambda i: (0, 0))],
        out_specs=(pl.BlockSpec((tm * pitch, LANE), lambda i: (i, 0)),
                   pl.BlockSpec((tm, LANE), lambda i: (i, 0)),
                   pl.BlockSpec((tm, LANE), lambda i: (i, 0))),
        compiler_params=_cparams(("parallel",)),
        name="route",
    )(x2, w.reshape(1, d), mod3, mod3, rw, rb)


_DMA_UNROLL = 8


def _rows_to_matrix(ref, tm, nt):
    return jnp.concatenate([ref[pl.ds(j, tm, stride=_pitch(nt)), :] for j in range(nt)], axis=1)


def _bulk_wait(src, dst, sem, total_rows):
    pltpu.make_async_copy(src.at[pl.ds(0, total_rows), :], dst.at[pl.ds(0, total_rows), :], sem).wait()


def _for_rows(n, body):
    groups = lax.shift_right_logical(n, _DMA_UNROLL.bit_length() - 1)

    def group(g, c):
        for u in range(_DMA_UNROLL):
            body(g * _DMA_UNROLL + u)
        return c

    def tail(r, c):
        body(r)
        return c

    lax.fori_loop(0, groups, group, 0)
    lax.fori_loop(groups * _DMA_UNROLL, n, tail, 0)


def _stream_expert_weights(b, be_ref, eord_ref, enext_ref, w_hbms, w_bufs, w_caches, wsem, both_queues):
    prev = jnp.maximum(b - 1, 0)

    def copies(e, slot):
        out = []
        for w, buf in zip(w_hbms, w_bufs):
            if both_queues:
                half = w.shape[1] // 2
                out.append((pltpu.make_async_copy(w.at[e, 0:half], buf.at[slot, 0:half], wsem.at[slot]), 1))
                out.append((pltpu.make_async_copy(w.at[e, half:], buf.at[slot, half:], wsem.at[slot]), 0))
            else:
                out.append((pltpu.make_async_copy(w.at[e], buf.at[slot], wsem.at[slot]), 1))
        return out

    @pl.when(b == 0)
    def _():
        for cp, prio in copies(be_ref[0], 0):
            cp.start(priority=prio)

    @pl.when((b == 0) | (be_ref[b] != be_ref[prev]))
    def _():
        for s in range(2):
            @pl.when((eord_ref[b] & 1) == s)
            def _(s=s):
                for cp, _ in copies(be_ref[b], s):
                    cp.wait()

                @pl.when(enext_ref[b] >= 0)
                def _():
                    for cp, prio in copies(enext_ref[b], 1 - s):
                        cp.start(priority=prio)

                for buf, cache in zip(w_bufs, w_caches):
                    cache[...] = buf[s].astype(BF16)


def _expert_up_kernel(be_ref, nused_ref, eord_ref, enext_ref, rowc_ref, rown_ref, h_hbm, wg_hbm, wu_hbm,
                      o_ref, xs0_ref, xs1_ref, wgs_ref, wus_ref, wgb_ref, wub_ref, sem, wsem, *, fchunk, nt):
    b = pl.program_id(0)
    n_used = nused_ref[0]
    dff = wgb_ref.shape[1]
    slots = (xs0_ref, xs1_ref)
    pitch = _pitch(nt)

    def gather_row(row_ref, r, slot):
        return pltpu.make_async_copy(h_hbm.at[pl.ds(row_ref[0, 0, r], nt), :],
                                     slots[slot].at[pl.ds(r * pitch, nt), :], sem.at[slot])

    @pl.when(b == 0)
    def _():
        _for_rows(MOE_BLOCK, lambda r: gather_row(rowc_ref, r, 0).start())

    for slot in range(2):
        @pl.when((b <= n_used) & (lax.rem(b, 2) == slot))
        def _(slot=slot):
            _bulk_wait(h_hbm, slots[slot], sem.at[slot], MOE_BLOCK * nt)

    @pl.when(b < n_used)
    def _():
        _stream_expert_weights(b, be_ref, eord_ref, enext_ref, (wg_hbm, wu_hbm), (wgs_ref, wus_ref),
                               (wgb_ref, wub_ref), wsem, both_queues=False)

        for slot in range(2):
            @pl.when(lax.rem(b, 2) == slot)
            def _(slot=slot):
                xb = _rows_to_matrix(slots[slot], MOE_BLOCK, nt).astype(BF16)
                nf = dff // fchunk
                per = MOE_BLOCK // (2 * nf)

                def request(part):
                    for r in range(part * per, (part + 1) * per):
                        gather_row(rown_ref, r, 1 - slot).start()

                for f in range(nf):
                    sl = slice(f * fchunk, (f + 1) * fchunk)
                    request(2 * f)
                    gate = jnp.dot(xb, wgb_ref[:, sl], preferred_element_type=F32)
                    request(2 * f + 1)
                    up = jnp.dot(xb, wub_ref[:, sl], preferred_element_type=F32)
                    o_ref[:, sl] = (_silu(gate) * up).astype(o_ref.dtype)

    @pl.when(b >= n_used)
    def _():
        o_ref[...] = jnp.zeros_like(o_ref)


def _expert_up(h2t, src_rows, w_gate, w_up, tables, fchunk=256):
    n_blocks = src_rows.shape[0]
    _, d, dff = w_gate.shape
    nt = d // LANE
    slot_rows = MOE_BLOCK * _pitch(nt)
    grid_spec = pltpu.PrefetchScalarGridSpec(
        num_scalar_prefetch=len(tables),
        grid=(n_blocks,),
        in_specs=[pl.BlockSpec((1, 1, MOE_BLOCK), lambda b, *_: (b, 0, 0), memory_space=pltpu.SMEM),
                  pl.BlockSpec((1, 1, MOE_BLOCK), lambda b, *_: (jnp.minimum(b + 1, n_blocks - 1), 0, 0),
                               memory_space=pltpu.SMEM),
                  pl.BlockSpec(memory_space=pl.ANY),
                  pl.BlockSpec(memory_space=pl.ANY),
                  pl.BlockSpec(memory_space=pl.ANY)],
        out_specs=pl.BlockSpec((MOE_BLOCK, dff), lambda b, *_: (b, 0)),
        scratch_shapes=[pltpu.VMEM((slot_rows, LANE), F32),
                        pltpu.VMEM((slot_rows, LANE), F32),
                        pltpu.VMEM((2, d, dff), F32),
                        pltpu.VMEM((2, d, dff), F32),
                        pltpu.VMEM((d, dff), BF16),
                        pltpu.VMEM((d, dff), BF16),
                        pltpu.SemaphoreType.DMA((2,)),
                        pltpu.SemaphoreType.DMA((2,))],
    )
    return pl.pallas_call(
        functools.partial(_expert_up_kernel, fchunk=fchunk, nt=nt),
        out_shape=jax.ShapeDtypeStruct((n_blocks * MOE_BLOCK, dff), BF16),
        grid_spec=grid_spec,
        compiler_params=_cparams(("arbitrary",)),
        name="expert_up",
    )(*tables, src_rows, src_rows, h2t, w_gate, w_up)


def _expert_down_kernel(be_ref, nused_ref, eord_ref, enext_ref, dst_ref, h_ref, wd_hbm, y_hbm,
                        ys0_ref, ys1_ref, wds_ref, wdb_ref, sem, wsem, *, nchunk, nt):
    b = pl.program_id(0)
    n_used = nused_ref[0]
    d = wdb_ref.shape[1]
    slots = (ys0_ref, ys1_ref)
    pitch = _pitch(nt)

    def scatter_row(r, slot):
        return pltpu.make_async_copy(slots[slot].at[pl.ds(r * pitch, pitch), :],
                                     y_hbm.at[pl.ds(dst_ref[0, 0, r], pitch), :], sem.at[slot])

    @pl.when(b == 0)
    def _():
        ys0_ref[...] = jnp.zeros_like(ys0_ref)
        ys1_ref[...] = jnp.zeros_like(ys1_ref)
        tail = pltpu.make_async_copy(ys1_ref, y_hbm.at[pl.ds(y_hbm.shape[0] - MOE_BLOCK * pitch, MOE_BLOCK * pitch), :],
                                     sem.at[1])
        tail.start()
        tail.wait()

    for slot in range(2):
        @pl.when((b >= 1) & (b <= n_used) & (lax.rem(b, 2) == slot))
        def _(slot=slot):
            _bulk_wait(slots[slot], y_hbm, sem.at[slot], MOE_BLOCK * pitch)

    @pl.when(b < n_used)
    def _():
        _stream_expert_weights(b, be_ref, eord_ref, enext_ref, (wd_hbm,), (wds_ref,), (wdb_ref,), wsem,
                               both_queues=True)
        hb = h_ref[...]
        for slot in range(2):
            @pl.when(lax.rem(b, 2) == slot)
            def _(slot=slot):
                nc = d // nchunk
                per = MOE_BLOCK // nc
                for c in range(nc):
                    for r in range(c * per, (c + 1) * per):
                        scatter_row(r, 1 - slot).start()
                    out = jnp.dot(hb, wdb_ref[:, c * nchunk:(c + 1) * nchunk], preferred_element_type=F32)
                    for j in range(nchunk // LANE):
                        slots[slot][pl.ds(c * (nchunk // LANE) + j, MOE_BLOCK, stride=pitch), :] = (
                            out[:, j * LANE:(j + 1) * LANE])

    for slot in range(2):
        @pl.when((b == n_used) & (lax.rem(b, 2) == slot))
        def _(slot=slot):
            _for_rows(MOE_BLOCK, lambda r: scatter_row(r, 1 - slot).start())
            _bulk_wait(slots[1 - slot], y_hbm, sem.at[1 - slot], MOE_BLOCK * pitch)


def _expert_down(hid, dst_rows, w_down, tables, y_slots, nchunk=256):
    n_rows, dff = hid.shape
    n_blocks = n_rows // MOE_BLOCK
    d = w_down.shape[2]
    nt = d // LANE
    pitch = _pitch(nt)
    grid_spec = pltpu.PrefetchScalarGridSpec(
        num_scalar_prefetch=len(tables),
        grid=(n_blocks,),
        in_specs=[pl.BlockSpec((1, 1, MOE_BLOCK), lambda b, *_: (b, 0, 0), memory_space=pltpu.SMEM),
                  pl.BlockSpec((MOE_BLOCK, dff), lambda b, be, n, *_: (jnp.minimum(b, n[0] - 1), 0)),
                  pl.BlockSpec(memory_space=pl.ANY)],
        out_specs=pl.BlockSpec(memory_space=pl.ANY),
        scratch_shapes=[pltpu.VMEM((MOE_BLOCK * pitch, LANE), F32),
                        pltpu.VMEM((MOE_BLOCK * pitch, LANE), F32),
                        pltpu.VMEM((2, dff, d), F32),
                        pltpu.VMEM((dff, d), BF16),
                        pltpu.SemaphoreType.DMA((2,)),
                        pltpu.SemaphoreType.DMA((2,))],
    )
    return pl.pallas_call(
        functools.partial(_expert_down_kernel, nchunk=nchunk, nt=nt),
        out_shape=jax.ShapeDtypeStruct((y_slots * pitch, LANE), F32),
        grid_spec=grid_spec,
        compiler_params=_cparams(("arbitrary",)),
        name="expert_down",
    )(*tables, dst_rows, hid, w_down)


def _combine_kernel(y0_ref, y1_ref, ew_ref, x_ref, g_ref, w_ref, o_ref):
    tm, d = x_ref.shape
    nt = d // LANE
    ew = ew_ref[...]
    moe = (_rows_to_matrix(y0_ref, tm, nt) * ew[:, 0:1]
           + _rows_to_matrix(y1_ref, tm, nt) * ew[:, 1:2])
    xo = x_ref[...] + g_ref[0] * moe
    ms = jnp.mean(xo * xo, axis=-1, keepdims=True)
    o_ref[...] = xo * lax.rsqrt(ms + EPS) * w_ref[...]


def _combine(y, ew, x2, mod3, gate_chunk, rows_per_batch, final_w, tm=512):
    m, d = x2.shape
    pitch = _pitch(d // LANE)
    tm = min(tm, rows_per_batch)
    tiles = m // tm
    tiles_per_batch = rows_per_batch // tm
    return pl.pallas_call(
        _combine_kernel,
        out_shape=jax.ShapeDtypeStruct((m, d), F32),
        grid=(tiles,),
        in_specs=[pl.BlockSpec((tm * pitch, LANE), lambda i: (i, 0)),
                  pl.BlockSpec((tm * pitch, LANE), lambda i: (tiles + i, 0)),
                  pl.BlockSpec((tm, LANE), lambda i: (i, 0)),
                  pl.BlockSpec((tm, d), lambda i: (i, 0)),
                  pl.BlockSpec((1, 1, d), lambda i: (i // tiles_per_batch, 0, gate_chunk)),
                  pl.BlockSpec((1, d), lambda i: (0, 0))],
        out_specs=pl.BlockSpec((tm, d), lambda i: (i, 0)),
        compiler_params=_cparams(("parallel",)),
        name="moe_combine",
    )(y, y, ew, x2, mod3, final_w.reshape(1, d))


def _dispatch_tables(eid, n_tok, pitch):
    top_k = eid.shape[1]
    n_assign = n_tok * top_k
    expert = eid.reshape(-1)
    key = jnp.sort(expert * n_assign + jnp.arange(n_assign, dtype=jnp.int32))
    sorted_assign = key % n_assign
    bounds = jnp.arange(N_EXPERTS + 1, dtype=jnp.int32) * n_assign
    start = jnp.sum((key[None, :] < bounds[:, None]).astype(jnp.int32), axis=1)
    counts = start[1:] - start[:-1]
    nblk = (counts + MOE_BLOCK - 1) // MOE_BLOCK
    blk_end = jnp.cumsum(nblk)
    blk_start = blk_end - nblk
    steps = -(-n_assign // MOE_BLOCK) + N_EXPERTS + 1
    bidx = jnp.arange(steps, dtype=jnp.int32)
    lane = jnp.arange(MOE_BLOCK, dtype=jnp.int32)[None, :]
    block_expert = jnp.minimum(jnp.sum((blk_end[None, :] <= bidx[:, None]).astype(jnp.int32), axis=1),
                               N_EXPERTS - 1)
    onehot = (block_expert[:, None] == jnp.arange(N_EXPERTS, dtype=jnp.int32)[None, :]).astype(jnp.int32)

    def lookup(table):
        return jnp.sum(onehot * table[None, :], axis=1)

    in_expert = (bidx - lookup(blk_start)) * MOE_BLOCK
    n_valid = jnp.clip(lookup(counts) - in_expert, 0, MOE_BLOCK)
    first_src = jnp.clip(lookup(start[:-1]) + in_expert, 0, n_assign)
    valid = lane < n_valid[:, None]
    assign = sorted_assign[jnp.minimum(first_src[:, None] + lane, n_assign - 1)]
    tok = assign // top_k
    src_rows = jnp.where(valid, tok, (bidx[:, None] * MOE_BLOCK + lane) % n_tok) * pitch
    dst_slot = jnp.where(valid, (assign % top_k) * n_tok + tok, top_k * n_tok + lane)
    dst_rows = jnp.concatenate([top_k * n_tok + lane, dst_slot[:-1]], axis=0) * pitch
    n_used = blk_end[-1].astype(jnp.int32)
    first = jnp.concatenate([jnp.ones((1,), jnp.int32),
                             (block_expert[1:] != block_expert[:-1]).astype(jnp.int32)])
    expert_ordinal = (jnp.cumsum(first) - 1).astype(jnp.int32)
    later = (jnp.arange(N_EXPERTS)[None, :] > block_expert[:, None]) & (counts[None, :] > 0)
    next_expert = jnp.min(jnp.where(later, jnp.arange(N_EXPERTS, dtype=jnp.int32)[None, :], N_EXPERTS), axis=1)
    next_expert = jnp.where(next_expert < N_EXPERTS, next_expert, -1).astype(jnp.int32)
    tables = (block_expert, n_used.reshape(1), expert_ordinal, next_expert)
    return (src_rows.astype(jnp.int32).reshape(steps, 1, MOE_BLOCK),
            dst_rows.astype(jnp.int32).reshape(steps, 1, MOE_BLOCK), tables)


def kernel(x, c, ctx, c_ctx, ada_w, ada_b, norm1_w, w_in, ssm_conv_w, ssm_conv_b, dt_bias, a_log, d_skip, ssm_norm_w, ssm_out_w, cf_dw_w, cf_dw_b, cf_ln_w, cf_ln_b, cf_out_w, cf_out_b, w_o, norm2_w, router_group_w, router_group_b, router_expert_w, router_expert_b, expert_w_gate, expert_w_up, expert_w_down, final_norm_w):
    bsz, seq, d = x.shape
    l_ctx = ctx.shape[1]
    n_tok = bsz * seq
    d_inner = ssm_norm_w.shape[1]
    gn = N_GROUPS * D_STATE
    xbc_dim = d_inner + 2 * gn
    off_dt = xbc_dim
    off_z = off_dt + N_HEADS
    off_glu = off_z + d_inner
    off_gate = off_glu + 2 * d

    ctx_row = bsz
    crows = jnp.zeros((8, d), F32).at[:bsz].set(c).at[ctx_row].set(c_ctx)
    mod = _ada(crows, ada_w[0], ada_b[0])
    mod3 = mod.reshape(8, 1, 6 * d)
    lat_rows = jnp.arange(bsz, dtype=jnp.int32)
    ctx_rows = jnp.full((bsz,), ctx_row, jnp.int32)

    h_lat = _normmod(x, norm1_w[0], mod3, lat_rows, 0, 1, BF16).reshape(n_tok, d)
    h_ctx = _normmod(ctx, norm1_w[0], mod3, ctx_rows, 0, 1, BF16).reshape(bsz * l_ctx, d)

    wt, (r_xbc, r_dt, r_z, r_glu, r_gate) = _pack_wt(
        jnp.transpose(w_in[0]),
        [(0, xbc_dim), (off_dt, off_z), (off_z, off_glu), (off_glu, off_gate), (off_gate, off_gate + 2 * d)])

    xbc_lat = _mm(h_lat, wt, tn=2048, name="in_xbc", rows=(r_xbc, xbc_dim)).reshape(bsz, seq, xbc_dim)
    xbc_ctx = _mm(h_ctx, wt, tm=512, name="in_xbc_ctx", rows=(r_xbc, xbc_dim)).reshape(bsz, l_ctx, xbc_dim)
    dt_lat = _mm(h_lat, wt, name="in_dt", rows=(r_dt, LANE)).reshape(bsz, seq, LANE)
    dt_ctx = _mm(h_ctx, wt, tm=512, name="in_dt_ctx", rows=(r_dt, LANE)).reshape(bsz, l_ctx, LANE)
    sz = _mm(h_lat, wt, act="silu", tn=2048, name="in_z", rows=(r_z, d_inner))
    u = _mm_glu(h_lat, wt, r_glu, d, tn=1024)

    xbc_act = _conv7(xbc_ctx, xbc_lat, ssm_conv_w[0], ssm_conv_b[0])

    def ssd_params(k):
        par = jnp.zeros((8, LANE), F32).at[0, :N_HEADS].set(dt_bias[0, k]).at[1, :N_HEADS].set(a_log[0, k])
        return par, jnp.repeat(d_skip[0, k], HEAD_DIM).reshape(N_HEADS // 2, 1, LANE)

    y_bwd = _ssd(xbc_act, dt_ctx, dt_lat, *ssd_params(1), reverse=True)
    gnorm = _ssd(xbc_act, dt_ctx, dt_lat, *ssd_params(0), reverse=False,
                 norm_with=(y_bwd, sz, ssm_norm_w[0]))
    y_ssd = _mm(gnorm, ssm_out_w[0].astype(BF16), tn=512, name="ssm_out")

    cv = _conv31(u.reshape(bsz, seq, d), cf_dw_w[0], cf_dw_b[0]).reshape(n_tok, d)
    merged = _mm_merge(cv, cf_ln_w[0], cf_ln_b[0], cf_out_w[0].astype(BF16), cf_out_b[0],
                       h_lat, wt, r_gate, y_ssd)
    x1 = _mm_resid(merged, w_o[0].astype(BF16), x.reshape(n_tok, d), mod3, 2, seq, tm=2048)

    n_r = MOE_GROUPS + N_EXPERTS
    rw = jnp.pad(jnp.concatenate([router_group_w[0], router_expert_w[0]], axis=1),
                 ((0, 0), (0, LANE - n_r))).astype(BF16)
    rb = jnp.pad(jnp.concatenate([router_group_b[0], router_expert_b[0]]), (0, LANE - n_r)).reshape(1, LANE)
    h2t, eid, ew = _route(x1, norm2_w[0], mod3, 3, 4, seq, rw, rb)

    src_rows, dst_rows, tables = _dispatch_tables(eid[:, :2], n_tok, _pitch(d // LANE))
    hid = _expert_up(h2t, src_rows, expert_w_gate[0], expert_w_up[0], tables)
    y = _expert_down(hid, dst_rows, expert_w_down[0], tables, 2 * n_tok + 2 * MOE_BLOCK)
    out = _combine(y, ew, x1, mod3, 5, seq, final_norm_w)
    return out.reshape(bsz, seq, d)
```

```python
import functools

import jax
import jax.numpy as jnp
from jax import lax
from jax.experimental import pallas as pl
from jax.experimental.pallas import tpu as pltpu

F32 = jnp.float32
BF16 = jnp.bfloat16

EPS = 1e-6
GRID_W = 64
HEAD_DIM = 64
N_HEADS = 64
N_GROUPS = 8
D_STATE = 128
CHUNK = 128
SSM_CONV = 7
CF_KERNEL = 31
MOE_GROUPS = 8
EXPERTS_PER_GROUP = 8
N_EXPERTS = 64
MOE_BLOCK = 256
LANE = 128
LOG2E = 1.4426950408889634
VMEM_LIMIT = 56 * 1024 * 1024


def _cparams(sem):
    return pltpu.CompilerParams(dimension_semantics=sem, vmem_limit_bytes=VMEM_LIMIT)


def _silu(v):
    return v * jax.nn.sigmoid(v)


def _pitch(nt):
    return nt + 1


def _ada_kernel(c_ref, w_ref, b_ref, o_ref):
    s = _silu(c_ref[...])
    o_ref[...] = jnp.dot(s.astype(BF16), w_ref[...].astype(BF16),
                         preferred_element_type=F32) + b_ref[...]


def _ada(crows, ada_w, ada_b, tn=1024):
    r, d = crows.shape
    n = ada_w.shape[1]
    return pl.pallas_call(
        _ada_kernel,
        out_shape=jax.ShapeDtypeStruct((r, n), F32),
        grid=(n // tn,),
        in_specs=[pl.BlockSpec((r, d), lambda j: (0, 0)),
                  pl.BlockSpec((d, tn), lambda j: (0, j)),
                  pl.BlockSpec((1, tn), lambda j: (0, j))],
        out_specs=pl.BlockSpec((r, tn), lambda j: (0, j)),
        compiler_params=_cparams(("parallel",)),
        name="ada",
    )(crows, ada_w, ada_b.reshape(1, n))


def _normmod_kernel(rows_ref, x_ref, w_ref, sh_ref, sc_ref, o_ref):
    del rows_ref
    xf = x_ref[0]
    ms = jnp.mean(xf * xf, axis=-1, keepdims=True)
    y = xf * lax.rsqrt(ms + EPS) * w_ref[...]
    o_ref[0] = (y * (1.0 + sc_ref[0]) + sh_ref[0]).astype(o_ref.dtype)


def _normmod(x3, w, mod3, rows, shift_chunk, scale_chunk, out_dtype, tm=512):
    bx, l, d = x3.shape
    tm = min(tm, l)
    grid_spec = pltpu.PrefetchScalarGridSpec(
        num_scalar_prefetch=1,
        grid=(bx, l // tm),
        in_specs=[pl.BlockSpec((1, tm, d), lambda b, i, r: (b, i, 0)),
                  pl.BlockSpec((1, d), lambda b, i, r: (0, 0)),
                  pl.BlockSpec((1, 1, d), lambda b, i, r: (r[b], 0, shift_chunk)),
                  pl.BlockSpec((1, 1, d), lambda b, i, r: (r[b], 0, scale_chunk))],
        out_specs=pl.BlockSpec((1, tm, d), lambda b, i, r: (b, i, 0)),
    )
    return pl.pallas_call(
        _normmod_kernel,
        out_shape=jax.ShapeDtypeStruct((bx, l, d), out_dtype),
        grid_spec=grid_spec,
        compiler_params=_cparams(("parallel", "parallel")),
        name="normmod",
    )(rows, x3, w.reshape(1, d), mod3, mod3)


def _dot_nt(a, wt):
    return lax.dot_general(a, wt, (((1,), (1,)), ((), ())), preferred_element_type=F32)


def _mm_kernel(a_ref, w_ref, *rest, act, has_bias, w_rows):
    o_ref = rest[-1]
    a = a_ref[...]
    acc = _dot_nt(a, w_ref[...]) if w_rows else jnp.dot(a, w_ref[...], preferred_element_type=F32)
    if has_bias:
        acc = acc + rest[0][...]
    if act == "silu":
        acc = _silu(acc)
    elif act == "sigmoid":
        acc = jax.nn.sigmoid(acc)
    o_ref[...] = acc.astype(o_ref.dtype)


def _mm(a, w, bias=None, act=None, out_dtype=F32, tm=1024, tn=1024, name="mm", rows=None):
    m, k = a.shape
    start, n = (0, w.shape[1]) if rows is None else rows
    tm, tn = min(tm, m), min(tn, n)
    j0 = start // tn
    w_spec = (pl.BlockSpec((k, tn), lambda i, j: (0, j)) if rows is None
              else pl.BlockSpec((tn, k), lambda i, j: (j0 + j, 0)))
    in_specs = [pl.BlockSpec((tm, k), lambda i, j: (i, 0)), w_spec]
    args = [a, w]
    if bias is not None:
        in_specs.append(pl.BlockSpec((1, tn), lambda i, j: (0, j)))
        args.append(bias.reshape(1, n))
    return pl.pallas_call(
        functools.partial(_mm_kernel, act=act, has_bias=bias is not None, w_rows=rows is not None),
        out_shape=jax.ShapeDtypeStruct((m, n), out_dtype),
        grid=(m // tm, n // tn),
        in_specs=in_specs,
        out_specs=pl.BlockSpec((tm, tn), lambda i, j: (i, j)),
        compiler_params=_cparams(("parallel", "parallel")),
        name=name,
    )(*args)


def _mm_glu_kernel(a_ref, wa_ref, wb_ref, o_ref):
    a = a_ref[...]
    va = _dot_nt(a, wa_ref[...])
    vb = _dot_nt(a, wb_ref[...])
    o_ref[...] = va * jax.nn.sigmoid(vb)


def _mm_glu(a, wt, start, n, tm=1024, tn=512):
    m, k = a.shape
    tm = min(tm, m)
    ja, jb = start // tn, (start + n) // tn
    return pl.pallas_call(
        _mm_glu_kernel,
        out_shape=jax.ShapeDtypeStruct((m, n), F32),
        grid=(m // tm, n // tn),
        in_specs=[pl.BlockSpec((tm, k), lambda i, j: (i, 0)),
                  pl.BlockSpec((tn, k), lambda i, j: (ja + j, 0)),
                  pl.BlockSpec((tn, k), lambda i, j: (jb + j, 0))],
        out_specs=pl.BlockSpec((tm, tn), lambda i, j: (i, j)),
        compiler_params=_cparams(("parallel", "parallel")),
        name="mm_glu",
    )(a, wt, wt)


def _mm_merge_kernel(cv_ref, lw_ref, lb_ref, w_ref, b_ref, h_ref, wga_ref, wgb_ref, ys_ref, o_ref, u_ref):
    @pl.when(pl.program_id(1) == 0)
    def _():
        xf = cv_ref[...]
        mu = jnp.mean(xf, axis=-1, keepdims=True)
        xc = xf - mu
        var = jnp.mean(xc * xc, axis=-1, keepdims=True)
        y = xc * lax.rsqrt(var + EPS) * lw_ref[...] + lb_ref[...]
        u_ref[...] = _silu(y).astype(u_ref.dtype)

    ycf = jnp.dot(u_ref[...], w_ref[...], preferred_element_type=F32) + b_ref[...]
    h = h_ref[...]
    gate_a = jax.nn.sigmoid(_dot_nt(h, wga_ref[...]))
    gate_b = jax.nn.sigmoid(_dot_nt(h, wgb_ref[...]))
    o_ref[...] = (gate_a * ys_ref[...] + gate_b * ycf).astype(o_ref.dtype)


def _mm_merge(cv, ln_w, ln_b, w, bias, h, wt, gate_start, y_ssd, tm=512, tn=1024):
    m, k = cv.shape
    n = w.shape[1]
    tm = min(tm, m)
    ja, jb = gate_start // tn, (gate_start + n) // tn
    return pl.pallas_call(
        _mm_merge_kernel,
        out_shape=jax.ShapeDtypeStruct((m, n), BF16),
        grid=(m // tm, n // tn),
        in_specs=[pl.BlockSpec((tm, k), lambda i, j: (i, 0)),
                  pl.BlockSpec((1, k), lambda i, j: (0, 0)),
                  pl.BlockSpec((1, k), lambda i, j: (0, 0)),
                  pl.BlockSpec((k, tn), lambda i, j: (0, j)),
                  pl.BlockSpec((1, tn), lambda i, j: (0, j)),
                  pl.BlockSpec((tm, k), lambda i, j: (i, 0)),
                  pl.BlockSpec((tn, k), lambda i, j: (ja + j, 0)),
                  pl.BlockSpec((tn, k), lambda i, j: (jb + j, 0)),
                  pl.BlockSpec((tm, tn), lambda i, j: (i, j))],
        out_specs=pl.BlockSpec((tm, tn), lambda i, j: (i, j)),
        scratch_shapes=[pltpu.VMEM((tm, k), BF16)],
        compiler_params=_cparams(("parallel", "arbitrary")),
        name="mm_merge",
    )(cv, ln_w.reshape(1, k), ln_b.reshape(1, k), w, bias.reshape(1, n), h, wt, wt, y_ssd)


def _mm_resid_kernel(a_ref, w_ref, x_ref, g_ref, o_ref):
    out = jnp.dot(a_ref[...], w_ref[...], preferred_element_type=F32)
    o_ref[...] = x_ref[...] + g_ref[0] * out


def _mm_resid(a, w, x2, mod3, gate_chunk, rows_per_batch, tm=1024, tn=512):
    m, k = a.shape
    n = w.shape[1]
    tm = min(tm, rows_per_batch)
    nj = n // tn
    tiles_per_batch = rows_per_batch // tm
    return pl.pallas_call(
        _mm_resid_kernel,
        out_shape=jax.ShapeDtypeStruct((m, n), F32),
        grid=(m // tm, nj),
        in_specs=[pl.BlockSpec((tm, k), lambda i, j: (i, 0)),
                  pl.BlockSpec((k, tn), lambda i, j: (0, j)),
                  pl.BlockSpec((tm, tn), lambda i, j: (i, j)),
                  pl.BlockSpec((1, 1, tn),
                               lambda i, j: (i // tiles_per_batch, 0, gate_chunk * nj + j))],
        out_specs=pl.BlockSpec((tm, tn), lambda i, j: (i, j)),
        compiler_params=_cparams(("parallel", "parallel")),
        name="mm_resid",
    )(a, w, x2, mod3)


W_ALIGN = 2048


def _pack_wt_kernel(valid_ref, off_ref, w_ref, o_ref):
    del off_ref
    nrow = valid_ref[pl.program_id(0)]
    row = lax.broadcasted_iota(jnp.int32, o_ref.shape, 0)
    o_ref[...] = jnp.where(row < nrow, w_ref[...], 0.0).astype(o_ref.dtype)


def _pack_wt(wt, segments, tr=1024):
    n, k = wt.shape
    starts, src_off, valid = [], [], []
    pos = 0
    for lo, hi in segments:
        pos = -(-pos // W_ALIGN) * W_ALIGN
        starts.append(pos)
        while len(src_off) < pos // tr:
            src_off.append(0)
            valid.append(0)
        for r in range(lo, hi, tr):
            src_off.append(min(r, n - tr))
            valid.append(min(tr, hi - r))
            assert r <= n - tr or hi - r == tr
        pos += -(-(hi - lo) // tr) * tr
    total = -(-pos // W_ALIGN) * W_ALIGN
    while len(src_off) < total // tr:
        src_off.append(0)
        valid.append(0)
    grid_spec = pltpu.PrefetchScalarGridSpec(
        num_scalar_prefetch=2,
        grid=(total // tr,),
        in_specs=[pl.BlockSpec((pl.Element(tr), pl.Element(k)), lambda t, v, off: (off[t] * 8, 0))],
        out_specs=pl.BlockSpec((tr, k), lambda t, v, off: (t, 0)),
    )
    packed = pl.pallas_call(
        _pack_wt_kernel,
        out_shape=jax.ShapeDtypeStruct((total, k), BF16),
        grid_spec=grid_spec,
        compiler_params=_cparams(("parallel",)),
        name="pack_wt",
    )(jnp.asarray(valid, jnp.int32), jnp.asarray(src_off, jnp.int32) // 8, wt)
    return packed, starts


_CONV_PAD = 8


def _conv7_kernel(ctx_ref, lat_ref, w_ref, b_ref, o_ref, pad_ref, *, l_ctx, l_lat):
    p = _CONV_PAD
    zeros = jnp.zeros((p, LANE), F32)
    off_ctx = p
    off_lat = 2 * p + l_ctx
    pad_ref[0:p, :] = zeros
    pad_ref[off_ctx + l_ctx:off_lat, :] = zeros
    pad_ref[off_lat + l_lat:off_lat + l_lat + p, :] = zeros
    pad_ref[off_ctx:off_ctx + l_ctx, :] = ctx_ref[0]
    pad_ref[off_lat:off_lat + l_lat, :] = lat_ref[0]
    reach = SSM_CONV // 2
    bias = b_ref[...]

    def chunk(pad_base, out_base):
        acc = jnp.broadcast_to(bias, (CHUNK, LANE))
        for k in range(SSM_CONV):
            tap = pad_ref[pl.ds(pad_base - reach + k, CHUNK), :]
            acc = acc + tap * w_ref[k:k + 1, :]
        o_ref[0, 0, pl.ds(out_base, CHUNK), :] = _silu(acc)

    def ctx_body(j, c):
        base = pl.multiple_of(j * CHUNK, CHUNK)
        chunk(off_ctx + base, base)
        return c

    def lat_body(j, c):
        base = pl.multiple_of(j * CHUNK, CHUNK)
        chunk(off_lat + base, l_ctx + base)
        return c

    lax.fori_loop(0, l_ctx // CHUNK, ctx_body, 0)
    lax.fori_loop(0, l_lat // CHUNK, lat_body, 0, unroll=2)


def _conv7(ctx_raw, lat_raw, w, b):
    bsz, l_ctx, c = ctx_raw.shape
    l_lat = lat_raw.shape[1]
    ltot = l_ctx + l_lat
    nct = c // LANE
    return pl.pallas_call(
        functools.partial(_conv7_kernel, l_ctx=l_ctx, l_lat=l_lat),
        out_shape=jax.ShapeDtypeStruct((bsz, nct, ltot, LANE), F32),
        grid=(bsz, nct),
        in_specs=[pl.BlockSpec((1, l_ctx, LANE), lambda bi, ci: (bi, 0, ci)),
                  pl.BlockSpec((1, l_lat, LANE), lambda bi, ci: (bi, 0, ci)),
                  pl.BlockSpec((SSM_CONV, LANE), lambda bi, ci: (0, ci)),
                  pl.BlockSpec((1, LANE), lambda bi, ci: (0, ci))],
        out_specs=pl.BlockSpec((1, 1, ltot, LANE), lambda bi, ci: (bi, ci, 0, 0)),
        scratch_shapes=[pltpu.VMEM((ltot + 3 * _CONV_PAD, LANE), F32)],
        compiler_params=_cparams(("parallel", "parallel")),
        name="conv7",
    )(ctx_raw, lat_raw, w, b.reshape(1, c))


def _ssd_kernel(xbc_ref, dtc_ref, dtl_ref, par_ref, dexp_ref, ex_ref, *rest, reverse, n_ctx, fuse_norm):
    if fuse_norm:
        yo_ref, sz_ref, nw_ref, o_ref, st_ref, cumt_ref, y_ref = rest
    else:
        y_ref, st_ref, cumt_ref = rest
    i = pl.program_id(1)

    @pl.when(i == 0)
    def _():
        st_ref[...] = jnp.zeros_like(st_ref)

    dt_raw = jnp.where(i < n_ctx, dtc_ref[0], dtl_ref[0])
    bias = par_ref[0:1, :]
    a = -jnp.exp(par_ref[1:2, :])
    dt = jax.nn.softplus(dt_raw + bias)
    cum = dt * a
    row = lax.broadcasted_iota(jnp.int32, (CHUNK, LANE), 0)
    k = 1
    while k < CHUNK:
        if reverse:
            cum = cum + jnp.where(row < CHUNK - k, pltpu.roll(cum, CHUNK - k, 0), 0.0)
        else:
            cum = cum + jnp.where(row >= k, pltpu.roll(cum, k, 0), 0.0)
        k *= 2
    last = 0 if reverse else CHUNK - 1
    cum = cum * LOG2E
    cumt_ref[...] = cum.T
    li = lax.broadcasted_iota(jnp.int32, (CHUNK, CHUNK), 0)
    si = lax.broadcasted_iota(jnp.int32, (CHUNK, CHUNK), 1)
    causal = (li <= si) if reverse else (li >= si)
    lo = lax.broadcasted_iota(jnp.int32, (CHUNK, LANE), 1) < HEAD_DIM
    heads_per_group = N_HEADS // N_GROUPS
    pairs = heads_per_group // 2
    x_tiles = N_HEADS // 2

    def group(g, carry):
        shift = (LANE - heads_per_group * g) & (LANE - 1)
        cum_g = pltpu.roll(cum, shift, 1)
        dt_g = pltpu.roll(dt, shift, 1)
        cum_t = cumt_ref[pl.ds(pl.multiple_of(heads_per_group * g, heads_per_group), heads_per_group), :]
        bb = xbc_ref[0, x_tiles + g].astype(BF16)
        cb = xbc_ref[0, x_tiles + N_GROUPS + g].astype(BF16)
        scores = lax.dot_general(cb, bb, (((1,), (1,)), ((), ())), preferred_element_type=F32)
        y_off = jnp.dot(cb, st_ref[g].astype(BF16), preferred_element_type=F32)
        d_hi = dt_g.astype(BF16)
        r_hi = dt_g - d_hi.astype(F32)
        d_mid = r_hi.astype(BF16)
        d_lo = (r_hi - d_mid.astype(F32)).astype(BF16)
        dt_x = (jnp.dot(jnp.concatenate([d_hi, d_mid], axis=1), ex_ref[...], preferred_element_type=F32)
                + jnp.dot(d_lo, ex_ref[0:LANE, :], preferred_element_type=F32))
        xw_parts, dec_parts = [], []
        for p in range(pairs):
            j0, j1 = 2 * p, 2 * p + 1
            x2 = xbc_ref[0, pairs * g + p]
            c0 = cum_g[:, j0:j0 + 1]
            c1 = cum_g[:, j1:j1 + 1]
            l0 = jnp.exp2(jnp.where(causal, c0 - cum_t[j0:j0 + 1, :], -jnp.inf))
            l1 = jnp.exp2(jnp.where(causal, c1 - cum_t[j1:j1 + 1, :], -jnp.inf))
            m0 = (scores * l0).astype(BF16)
            m1 = (scores * l1).astype(BF16)
            dt2 = dt_x[:, p * LANE:(p + 1) * LANE]
            c2 = jnp.where(lo, c0, c1)
            xdt = x2 * dt2
            xdt_b = xdt.astype(BF16)
            zero = jnp.zeros_like(xdt_b)
            y_diag = jnp.dot(jnp.concatenate([m0, m1], axis=1),
                             jnp.concatenate([jnp.where(lo, xdt_b, zero), jnp.where(lo, zero, xdt_b)], axis=0),
                             preferred_element_type=F32)
            e2 = jnp.exp2(c2)
            y = y_diag + y_off[:, p * LANE:(p + 1) * LANE] * e2
            y_ref[0, pairs * g + p] = y + dexp_ref[pairs * g + p] * x2
            to_end = jnp.exp2(c2[last:last + 1, :] - c2)
            xw_parts.append((xdt * to_end).astype(BF16))
            dec_parts.append(e2[last:last + 1, :])
        xw = jnp.concatenate(xw_parts, axis=1)
        dec = jnp.concatenate(dec_parts, axis=1)
        upd = lax.dot_general(bb, xw, (((0,), (0,)), ((), ())), preferred_element_type=F32)
        st_ref[g] = st_ref[g] * dec + upd
        return carry

    lax.fori_loop(0, N_GROUPS, group, 0, unroll=2)

    if fuse_norm:
        sq = jnp.zeros((CHUNK, LANE), F32)
        for j in range(x_tiles):
            gj = (y_ref[0, j] + yo_ref[0, j]) * sz_ref[:, j * LANE:(j + 1) * LANE]
            y_ref[0, j] = gj
            sq = sq + gj * gj
        r = lax.rsqrt(jnp.sum(sq, axis=-1, keepdims=True) / (x_tiles * LANE) + EPS)
        for j in range(x_tiles):
            sl = slice(j * LANE, (j + 1) * LANE)
            o_ref[:, sl] = (y_ref[0, j] * r * nw_ref[:, sl]).astype(o_ref.dtype)


def _ssd(xbc_act, dt_ctx, dt_lat, par, dexp, reverse, norm_with=None):
    bsz, ntile, ltot, _ = xbc_act.shape
    l_ctx = dt_ctx.shape[1]
    l_lat = dt_lat.shape[1]
    n_ctx = l_ctx // CHUNK
    n_lat = l_lat // CHUNK
    steps = n_ctx + n_lat
    x_tiles = N_HEADS // 2
    gw = (N_HEADS // N_GROUPS) * HEAD_DIM
    e1 = (jnp.arange(gw)[None, :] // HEAD_DIM == jnp.arange(LANE)[:, None]).astype(BF16)
    expand = jnp.concatenate([e1, e1], axis=0)

    if reverse:
        def cat_chunk(i):
            return jnp.where(i < n_ctx, n_ctx - 1 - i, n_ctx + steps - 1 - i)

        def ctx_chunk(i):
            return jnp.maximum(n_ctx - 1 - i, 0)

        def lat_chunk(i):
            return jnp.minimum(steps - 1 - i, n_lat - 1)
    else:
        def cat_chunk(i):
            return i

        def ctx_chunk(i):
            return jnp.minimum(i, n_ctx - 1)

        def lat_chunk(i):
            return jnp.maximum(i - n_ctx, 0)

    y_spec = pl.BlockSpec((1, x_tiles, CHUNK, LANE), lambda b, i: (b, 0, lat_chunk(i), 0))
    in_specs = [pl.BlockSpec((1, ntile, CHUNK, LANE), lambda b, i: (b, 0, cat_chunk(i), 0)),
                pl.BlockSpec((1, CHUNK, LANE), lambda b, i: (b, ctx_chunk(i), 0)),
                pl.BlockSpec((1, CHUNK, LANE), lambda b, i: (b, lat_chunk(i), 0)),
                pl.BlockSpec((8, LANE), lambda b, i: (0, 0)),
                pl.BlockSpec((x_tiles, 1, LANE), lambda b, i: (0, 0, 0)),
                pl.BlockSpec((2 * LANE, gw), lambda b, i: (0, 0))]
    args = [xbc_act, dt_ctx, dt_lat, par, dexp, expand]
    scratch = [pltpu.VMEM((N_GROUPS, D_STATE, gw), F32), pltpu.VMEM((LANE, CHUNK), F32)]
    if norm_with is None:
        out_shape = jax.ShapeDtypeStruct((bsz, x_tiles, l_lat, LANE), F32)
        out_spec = y_spec
    else:
        y_other, silu_z, norm_w = norm_with
        dn = x_tiles * LANE
        row_spec = pl.BlockSpec((CHUNK, dn), lambda b, i: (b * n_lat + lat_chunk(i), 0))
        in_specs += [y_spec, row_spec, pl.BlockSpec((1, dn), lambda b, i: (0, 0))]
        args += [y_other, silu_z, norm_w.reshape(1, dn)]
        out_shape = jax.ShapeDtypeStruct((bsz * l_lat, dn), BF16)
        out_spec = row_spec
        scratch.append(pltpu.VMEM((1, x_tiles, CHUNK, LANE), F32))
    return pl.pallas_call(
        functools.partial(_ssd_kernel, reverse=reverse, n_ctx=n_ctx, fuse_norm=norm_with is not None),
        out_shape=out_shape,
        grid=(bsz, steps),
        in_specs=in_specs,
        out_specs=out_spec,
        scratch_shapes=scratch,
        compiler_params=_cparams(("parallel", "arbitrary")),
        name="ssd_bwd" if reverse else "ssd_fwd",
    )(*args)


def _conv31_kernel(u_ref, w_ref, b_ref, o_ref, pad_ref, *, seq):
    halo = (CF_KERNEL // 2) * GRID_W
    zeros = jnp.zeros((halo, LANE), F32)
    pad_ref[0:halo, :] = zeros
    pad_ref[halo + seq:halo + seq + halo, :] = zeros
    pad_ref[halo:halo + seq, :] = u_ref[0]
    bias = b_ref[...]

    def body(j, c):
        base = pl.multiple_of(j * CHUNK, CHUNK)
        acc = jnp.broadcast_to(bias, (CHUNK, LANE))
        for k in range(CF_KERNEL):
            tap = pad_ref[pl.ds(pl.multiple_of(base + k * GRID_W, GRID_W), CHUNK), :]
            acc = acc + tap * w_ref[k:k + 1, :]
        o_ref[0, pl.ds(base, CHUNK), :] = acc
        return c

    lax.fori_loop(0, seq // CHUNK, body, 0, unroll=2)


def _conv31(u3, w, b):
    bsz, s, c = u3.shape
    halo = (CF_KERNEL // 2) * GRID_W
    return pl.pallas_call(
        functools.partial(_conv31_kernel, seq=s),
        out_shape=jax.ShapeDtypeStruct((bsz, s, c), F32),
        grid=(bsz, c // LANE),
        in_specs=[pl.BlockSpec((1, s, LANE), lambda bi, ci: (bi, 0, ci)),
                  pl.BlockSpec((CF_KERNEL, LANE), lambda bi, ci: (0, ci)),
                  pl.BlockSpec((1, LANE), lambda bi, ci: (0, ci))],
        out_specs=pl.BlockSpec((1, s, LANE), lambda bi, ci: (bi, 0, ci)),
        scratch_shapes=[pltpu.VMEM((s + 2 * halo, LANE), F32)],
        compiler_params=_cparams(("parallel", "parallel")),
        name="conv31",
    )(u3, w, b.reshape(1, c))


def _route_kernel(x_ref, w_ref, sh_ref, sc_ref, rw_ref, rb_ref, h_ref, eid_ref, ew_ref):
    xf = x_ref[...]
    ms = jnp.mean(xf * xf, axis=-1, keepdims=True)
    h = xf * lax.rsqrt(ms + EPS) * w_ref[...]
    h = h * (1.0 + sc_ref[0]) + sh_ref[0]
    tm = xf.shape[0]
    nt = xf.shape[1] // LANE
    pitch = _pitch(nt)
    for j in range(nt):
        h_ref[pl.ds(j, tm, stride=pitch), :] = h[:, j * LANE:(j + 1) * LANE]
    for j in range(nt, pitch):
        h_ref[pl.ds(j, tm, stride=pitch), :] = jnp.zeros((tm, LANE), F32)
    logits = jnp.dot(h.astype(BF16), rw_ref[...], preferred_element_type=F32) + rb_ref[...]
    lane = lax.broadcasted_iota(jnp.int32, (tm, LANE), 1)
    lane_f = lane.astype(F32)
    ninf = -jnp.inf
    gl = jnp.where(lane < MOE_GROUPS, logits, ninf)
    gmax = jnp.max(gl, axis=-1, keepdims=True)
    gidx = jnp.min(jnp.where(gl == gmax, lane_f, float(LANE)), axis=-1, keepdims=True)
    gsum = jnp.sum(jnp.exp(gl - gmax), axis=-1, keepdims=True)
    g_p = 1.0 / gsum
    first = float(MOE_GROUPS) + gidx * float(EXPERTS_PER_GROUP)
    in_group = (lane_f >= first) & (lane_f < first + float(EXPERTS_PER_GROUP))
    el = jnp.where(in_group, logits, ninf)
    m1 = jnp.max(el, axis=-1, keepdims=True)
    i1 = jnp.min(jnp.where(el == m1, lane_f, float(LANE)), axis=-1, keepdims=True)
    el2 = jnp.where(lane_f == i1, ninf, el)
    m2 = jnp.max(el2, axis=-1, keepdims=True)
    i2 = jnp.min(jnp.where(el2 == m2, lane_f, float(LANE)), axis=-1, keepdims=True)
    e21 = jnp.exp(m2 - m1)
    den = 1.0 + e21
    w1 = (1.0 / den) * g_p
    w2 = (e21 / den) * g_p
    e1 = (i1 - float(MOE_GROUPS)).astype(jnp.int32)
    e2 = (i2 - float(MOE_GROUPS)).astype(jnp.int32)
    eid_ref[...] = jnp.where(lane == 0, e1, jnp.where(lane == 1, e2, 0))
    ew_ref[...] = jnp.where(lane == 0, w1, jnp.where(lane == 1, w2, 0.0))


def _route(x2, w, mod3, shift_chunk, scale_chunk, rows_per_batch, rw, rb, tm=512):
    m, d = x2.shape
    pitch = _pitch(d // LANE)
    tm = min(tm, rows_per_batch)
    tiles_per_batch = rows_per_batch // tm
    return pl.pallas_call(
        _route_kernel,
        out_shape=(jax.ShapeDtypeStruct((m * pitch, LANE), F32),
                   jax.ShapeDtypeStruct((m, LANE), jnp.int32),
                   jax.ShapeDtypeStruct((m, LANE), F32)),
        grid=(m // tm,),
        in_specs=[pl.BlockSpec((tm, d), lambda i: (i, 0)),
                  pl.BlockSpec((1, d), lambda i: (0, 0)),
                  pl.BlockSpec((1, 1, d), lambda i: (i // tiles_per_batch, 0, shift_chunk)),
                  pl.BlockSpec((1, 1, d), lambda i: (i // tiles_per_batch, 0, scale_chunk)),
                  pl.BlockSpec((d, LANE), lambda i: (0, 0)),
                  pl.BlockSpec((1, LANE), lambda i: (0, 0))],
        out_specs=(pl.BlockSpec((tm * pitch, LANE), lambda i: (i, 0)),
                   pl.BlockSpec((tm, LANE), lambda i: (i, 0)),
                   pl.BlockSpec((tm, LANE), lambda i: (i, 0))),
        compiler_params=_cparams(("parallel",)),
        name="route",
    )(x2, w.reshape(1, d), mod3, mod3, rw, rb)


_DMA_UNROLL = 8


def _rows_to_matrix(ref, tm, nt):
    return jnp.concatenate([ref[pl.ds(j, tm, stride=_pitch(nt)), :] for j in range(nt)], axis=1)


def _bulk_wait(src, dst, sem, total_rows):
    pltpu.make_async_copy(src.at[pl.ds(0, total_rows), :], dst.at[pl.ds(0, total_rows), :], sem).wait()


def _for_rows(n, body):
    groups = lax.shift_right_logical(n, _DMA_UNROLL.bit_length() - 1)

    def group(g, c):
        for u in range(_DMA_UNROLL):
            body(g * _DMA_UNROLL + u)
        return c

    def tail(r, c):
        body(r)
        return c

    lax.fori_loop(0, groups, group, 0)
    lax.fori_loop(groups * _DMA_UNROLL, n, tail, 0)


def _stream_expert_weights(b, be_ref, eord_ref, enext_ref, w_hbms, w_bufs, w_caches, wsem, both_queues):
    prev = jnp.maximum(b - 1, 0)

    def copies(e, slot):
        out = []
        for w, buf in zip(w_hbms, w_bufs):
            if both_queues:
                half = w.shape[1] // 2
                out.append((pltpu.make_async_copy(w.at[e, 0:half], buf.at[slot, 0:half], wsem.at[slot]), 1))
                out.append((pltpu.make_async_copy(w.at[e, half:], buf.at[slot, half:], wsem.at[slot]), 0))
            else:
                out.append((pltpu.make_async_copy(w.at[e], buf.at[slot], wsem.at[slot]), 1))
        return out

    @pl.when(b == 0)
    def _():
        for cp, prio in copies(be_ref[0], 0):
            cp.start(priority=prio)

    @pl.when((b == 0) | (be_ref[b] != be_ref[prev]))
    def _():
        for s in range(2):
            @pl.when((eord_ref[b] & 1) == s)
            def _(s=s):
                for cp, _ in copies(be_ref[b], s):
                    cp.wait()

                @pl.when(enext_ref[b] >= 0)
                def _():
                    for cp, prio in copies(enext_ref[b], 1 - s):
                        cp.start(priority=prio)

                for buf, cache in zip(w_bufs, w_caches):
                    cache[...] = buf[s].astype(BF16)


def _expert_up_kernel(be_ref, nused_ref, eord_ref, enext_ref, rowc_ref, rown_ref, h_hbm, wg_hbm, wu_hbm,
                      o_ref, xs0_ref, xs1_ref, wgs_ref, wus_ref, wgb_ref, wub_ref, sem, wsem, *, fchunk, nt):
    b = pl.program_id(0)
    n_used = nused_ref[0]
    dff = wgb_ref.shape[1]
    slots = (xs0_ref, xs1_ref)
    pitch = _pitch(nt)

    def gather_row(row_ref, r, slot):
        return pltpu.make_async_copy(h_hbm.at[pl.ds(row_ref[0, 0, r], nt), :],
                                     slots[slot].at[pl.ds(r * pitch, nt), :], sem.at[slot])

    @pl.when(b == 0)
    def _():
        _for_rows(MOE_BLOCK, lambda r: gather_row(rowc_ref, r, 0).start())

    for slot in range(2):
        @pl.when((b <= n_used) & (lax.rem(b, 2) == slot))
        def _(slot=slot):
            _bulk_wait(h_hbm, slots[slot], sem.at[slot], MOE_BLOCK * nt)

    @pl.when(b < n_used)
    def _():
        _stream_expert_weights(b, be_ref, eord_ref, enext_ref, (wg_hbm, wu_hbm), (wgs_ref, wus_ref),
                               (wgb_ref, wub_ref), wsem, both_queues=False)

        for slot in range(2):
            @pl.when(lax.rem(b, 2) == slot)
            def _(slot=slot):
                xb = _rows_to_matrix(slots[slot], MOE_BLOCK, nt).astype(BF16)
                nf = dff // fchunk
                per = MOE_BLOCK // (2 * nf)

                def request(part):
                    for r in range(part * per, (part + 1) * per):
                        gather_row(rown_ref, r, 1 - slot).start()

                for f in range(nf):
                    sl = slice(f * fchunk, (f + 1) * fchunk)
                    request(2 * f)
                    gate = jnp.dot(xb, wgb_ref[:, sl], preferred_element_type=F32)
                    request(2 * f + 1)
                    up = jnp.dot(xb, wub_ref[:, sl], preferred_element_type=F32)
                    o_ref[:, sl] = (_silu(gate) * up).astype(o_ref.dtype)

    @pl.when(b >= n_used)
    def _():
        o_ref[...] = jnp.zeros_like(o_ref)


def _expert_up(h2t, src_rows, w_gate, w_up, tables, fchunk=256):
    n_blocks = src_rows.shape[0]
    _, d, dff = w_gate.shape
    nt = d // LANE
    slot_rows = MOE_BLOCK * _pitch(nt)
    grid_spec = pltpu.PrefetchScalarGridSpec(
        num_scalar_prefetch=len(tables),
        grid=(n_blocks,),
        in_specs=[pl.BlockSpec((1, 1, MOE_BLOCK), lambda b, *_: (b, 0, 0), memory_space=pltpu.SMEM),
                  pl.BlockSpec((1, 1, MOE_BLOCK), lambda b, *_: (jnp.minimum(b + 1, n_blocks - 1), 0, 0),
                               memory_space=pltpu.SMEM),
                  pl.BlockSpec(memory_space=pl.ANY),
                  pl.BlockSpec(memory_space=pl.ANY),
                  pl.BlockSpec(memory_space=pl.ANY)],
        out_specs=pl.BlockSpec((MOE_BLOCK, dff), lambda b, *_: (b, 0)),
        scratch_shapes=[pltpu.VMEM((slot_rows, LANE), F32),
                        pltpu.VMEM((slot_rows, LANE), F32),
                        pltpu.VMEM((2, d, dff), F32),
                        pltpu.VMEM((2, d, dff), F32),
                        pltpu.VMEM((d, dff), BF16),
                        pltpu.VMEM((d, dff), BF16),
                        pltpu.SemaphoreType.DMA((2,)),
                        pltpu.SemaphoreType.DMA((2,))],
    )
    return pl.pallas_call(
        functools.partial(_expert_up_kernel, fchunk=fchunk, nt=nt),
        out_shape=jax.ShapeDtypeStruct((n_blocks * MOE_BLOCK, dff), BF16),
        grid_spec=grid_spec,
        compiler_params=_cparams(("arbitrary",)),
        name="expert_up",
    )(*tables, src_rows, src_rows, h2t, w_gate, w_up)


def _expert_down_kernel(be_ref, nused_ref, eord_ref, enext_ref, dst_ref, h_ref, wd_hbm, y_hbm,
                        ys0_ref, ys1_ref, wds_ref, wdb_ref, sem, wsem, *, nchunk, nt):
    b = pl.program_id(0)
    n_used = nused_ref[0]
    d = wdb_ref.shape[1]
    slots = (ys0_ref, ys1_ref)
    pitch = _pitch(nt)

    def scatter_row(r, slot):
        return pltpu.make_async_copy(slots[slot].at[pl.ds(r * pitch, pitch), :],
                                     y_hbm.at[pl.ds(dst_ref[0, 0, r], pitch), :], sem.at[slot])

    @pl.when(b == 0)
    def _():
        ys0_ref[...] = jnp.zeros_like(ys0_ref)
        ys1_ref[...] = jnp.zeros_like(ys1_ref)
        tail = pltpu.make_async_copy(ys1_ref, y_hbm.at[pl.ds(y_hbm.shape[0] - MOE_BLOCK * pitch, MOE_BLOCK * pitch), :],
                                     sem.at[1])
        tail.start()
        tail.wait()

    for slot in range(2):
        @pl.when((b >= 1) & (b <= n_used) & (lax.rem(b, 2) == slot))
        def _(slot=slot):
            _bulk_wait(slots[slot], y_hbm, sem.at[slot], MOE_BLOCK * pitch)

    @pl.when(b < n_used)
    def _():
        _stream_expert_weights(b, be_ref, eord_ref, enext_ref, (wd_hbm,), (wds_ref,), (wdb_ref,), wsem,
                               both_queues=True)
        hb = h_ref[...]
        for slot in range(2):
            @pl.when(lax.rem(b, 2) == slot)
            def _(slot=slot):
                nc = d // nchunk
                per = MOE_BLOCK // nc
                for c in range(nc):
                    for r in range(c * per, (c + 1) * per):
                        scatter_row(r, 1 - slot).start()
                    out = jnp.dot(hb, wdb_ref[:, c * nchunk:(c + 1) * nchunk], preferred_element_type=F32)
                    for j in range(nchunk // LANE):
                        slots[slot][pl.ds(c * (nchunk // LANE) + j, MOE_BLOCK, stride=pitch), :] = (
                            out[:, j * LANE:(j + 1) * LANE])

    for slot in range(2):
        @pl.when((b == n_used) & (lax.rem(b, 2) == slot))
        def _(slot=slot):
            _for_rows(MOE_BLOCK, lambda r: scatter_row(r, 1 - slot).start())
            _bulk_wait(slots[1 - slot], y_hbm, sem.at[1 - slot], MOE_BLOCK * pitch)


def _expert_down(hid, dst_rows, w_down, tables, y_slots, nchunk=256):
    n_rows, dff = hid.shape
    n_blocks = n_rows // MOE_BLOCK
    d = w_down.shape[2]
    nt = d // LANE
    pitch = _pitch(nt)
    grid_spec = pltpu.PrefetchScalarGridSpec(
        num_scalar_prefetch=len(tables),
        grid=(n_blocks,),
        in_specs=[pl.BlockSpec((1, 1, MOE_BLOCK), lambda b, *_: (b, 0, 0), memory_space=pltpu.SMEM),
                  pl.BlockSpec((MOE_BLOCK, dff), lambda b, be, n, *_: (jnp.minimum(b, n[0] - 1), 0)),
                  pl.BlockSpec(memory_space=pl.ANY)],
        out_specs=pl.BlockSpec(memory_space=pl.ANY),
        scratch_shapes=[pltpu.VMEM((MOE_BLOCK * pitch, LANE), F32),
                        pltpu.VMEM((MOE_BLOCK * pitch, LANE), F32),
                        pltpu.VMEM((2, dff, d), F32),
                        pltpu.VMEM((dff, d), BF16),
                        pltpu.SemaphoreType.DMA((2,)),
                        pltpu.SemaphoreType.DMA((2,))],
    )
    return pl.pallas_call(
        functools.partial(_expert_down_kernel, nchunk=nchunk, nt=nt),
        out_shape=jax.ShapeDtypeStruct((y_slots * pitch, LANE), F32),
        grid_spec=grid_spec,
        compiler_params=_cparams(("arbitrary",)),
        name="expert_down",
    )(*tables, dst_rows, hid, w_down)


def _combine_kernel(y0_ref, y1_ref, ew_ref, x_ref, g_ref, w_ref, o_ref):
    tm, d = x_ref.shape
    nt = d // LANE
    ew = ew_ref[...]
    moe = (_rows_to_matrix(y0_ref, tm, nt) * ew[:, 0:1]
           + _rows_to_matrix(y1_ref, tm, nt) * ew[:, 1:2])
    xo = x_ref[...] + g_ref[0] * moe
    ms = jnp.mean(xo * xo, axis=-1, keepdims=True)
    o_ref[...] = xo * lax.rsqrt(ms + EPS) * w_ref[...]


def _combine(y, ew, x2, mod3, gate_chunk, rows_per_batch, final_w, tm=512):
    m, d = x2.shape
    pitch = _pitch(d // LANE)
    tm = min(tm, rows_per_batch)
    tiles = m // tm
    tiles_per_batch = rows_per_batch // tm
    return pl.pallas_call(
        _combine_kernel,
        out_shape=jax.ShapeDtypeStruct((m, d), F32),
        grid=(tiles,),
        in_specs=[pl.BlockSpec((tm * pitch, LANE), lambda i: (i, 0)),
                  pl.BlockSpec((tm * pitch, LANE), lambda i: (tiles + i, 0)),
                  pl.BlockSpec((tm, LANE), lambda i: (i, 0)),
                  pl.BlockSpec((tm, d), lambda i: (i, 0)),
                  pl.BlockSpec((1, 1, d), lambda i: (i // tiles_per_batch, 0, gate_chunk)),
                  pl.BlockSpec((1, d), lambda i: (0, 0))],
        out_specs=pl.BlockSpec((tm, d), lambda i: (i, 0)),
        compiler_params=_cparams(("parallel",)),
        name="moe_combine",
    )(y, y, ew, x2, mod3, final_w.reshape(1, d))


def _dispatch_tables(eid, n_tok, pitch):
    top_k = eid.shape[1]
    n_assign = n_tok * top_k
    expert = eid.reshape(-1)
    key = jnp.sort(expert * n_assign + jnp.arange(n_assign, dtype=jnp.int32))
    sorted_assign = key % n_assign
    bounds = jnp.arange(N_EXPERTS + 1, dtype=jnp.int32) * n_assign
    start = jnp.sum((key[None, :] < bounds[:, None]).astype(jnp.int32), axis=1)
    counts = start[1:] - start[:-1]
    nblk = (counts + MOE_BLOCK - 1) // MOE_BLOCK
    blk_end = jnp.cumsum(nblk)
    blk_start = blk_end - nblk
    steps = -(-n_assign // MOE_BLOCK) + N_EXPERTS + 1
    bidx = jnp.arange(steps, dtype=jnp.int32)
    lane = jnp.arange(MOE_BLOCK, dtype=jnp.int32)[None, :]
    block_expert = jnp.minimum(jnp.sum((blk_end[None, :] <= bidx[:, None]).astype(jnp.int32), axis=1),
                               N_EXPERTS - 1)
    onehot = (block_expert[:, None] == jnp.arange(N_EXPERTS, dtype=jnp.int32)[None, :]).astype(jnp.int32)

    def lookup(table):
        return jnp.sum(onehot * table[None, :], axis=1)

    in_expert = (bidx - lookup(blk_start)) * MOE_BLOCK
    n_valid = jnp.clip(lookup(counts) - in_expert, 0, MOE_BLOCK)
    first_src = jnp.clip(lookup(start[:-1]) + in_expert, 0, n_assign)
    valid = lane < n_valid[:, None]
    assign = sorted_assign[jnp.minimum(first_src[:, None] + lane, n_assign - 1)]
    tok = assign // top_k
    src_rows = jnp.where(valid, tok, (bidx[:, None] * MOE_BLOCK + lane) % n_tok) * pitch
    dst_slot = jnp.where(valid, (assign % top_k) * n_tok + tok, top_k * n_tok + lane)
    dst_rows = jnp.concatenate([top_k * n_tok + lane, dst_slot[:-1]], axis=0) * pitch
    n_used = blk_end[-1].astype(jnp.int32)
    first = jnp.concatenate([jnp.ones((1,), jnp.int32),
                             (block_expert[1:] != block_expert[:-1]).astype(jnp.int32)])
    expert_ordinal = (jnp.cumsum(first) - 1).astype(jnp.int32)
    later = (jnp.arange(N_EXPERTS)[None, :] > block_expert[:, None]) & (counts[None, :] > 0)
    next_expert = jnp.min(jnp.where(later, jnp.arange(N_EXPERTS, dtype=jnp.int32)[None, :], N_EXPERTS), axis=1)
    next_expert = jnp.where(next_expert < N_EXPERTS, next_expert, -1).astype(jnp.int32)
    tables = (block_expert, n_used.reshape(1), expert_ordinal, next_expert)
    return (src_rows.astype(jnp.int32).reshape(steps, 1, MOE_BLOCK),
            dst_rows.astype(jnp.int32).reshape(steps, 1, MOE_BLOCK), tables)


def kernel(x, c, ctx, c_ctx, ada_w, ada_b, norm1_w, w_in, ssm_conv_w, ssm_conv_b, dt_bias, a_log, d_skip, ssm_norm_w, ssm_out_w, cf_dw_w, cf_dw_b, cf_ln_w, cf_ln_b, cf_out_w, cf_out_b, w_o, norm2_w, router_group_w, router_group_b, router_expert_w, router_expert_b, expert_w_gate, expert_w_up, expert_w_down, final_norm_w):
    bsz, seq, d = x.shape
    l_ctx = ctx.shape[1]
    n_tok = bsz * seq
    d_inner = ssm_norm_w.shape[1]
    gn = N_GROUPS * D_STATE
    xbc_dim = d_inner + 2 * gn
    off_dt = xbc_dim
    off_z = off_dt + N_HEADS
    off_glu = off_z + d_inner
    off_gate = off_glu + 2 * d

    ctx_row = bsz
    crows = jnp.zeros((8, d), F32).at[:bsz].set(c).at[ctx_row].set(c_ctx)
    mod = _ada(crows, ada_w[0], ada_b[0])
    mod3 = mod.reshape(8, 1, 6 * d)
    lat_rows = jnp.arange(bsz, dtype=jnp.int32)
    ctx_rows = jnp.full((bsz,), ctx_row, jnp.int32)

    h_lat = _normmod(x, norm1_w[0], mod3, lat_rows, 0, 1, BF16).reshape(n_tok, d)
    h_ctx = _normmod(ctx, norm1_w[0], mod3, ctx_rows, 0, 1, BF16).reshape(bsz * l_ctx, d)

    wt, (r_xbc, r_dt, r_z, r_glu, r_gate) = _pack_wt(
        jnp.transpose(w_in[0]),
        [(0, xbc_dim), (off_dt, off_z), (off_z, off_glu), (off_glu, off_gate), (off_gate, off_gate + 2 * d)])

    xbc_lat = _mm(h_lat, wt, tn=2048, name="in_xbc", rows=(r_xbc, xbc_dim)).reshape(bsz, seq, xbc_dim)
    xbc_ctx = _mm(h_ctx, wt, tm=512, name="in_xbc_ctx", rows=(r_xbc, xbc_dim)).reshape(bsz, l_ctx, xbc_dim)
    dt_lat = _mm(h_lat, wt, name="in_dt", rows=(r_dt, LANE)).reshape(bsz, seq, LANE)
    dt_ctx = _mm(h_ctx, wt, tm=512, name="in_dt_ctx", rows=(r_dt, LANE)).reshape(bsz, l_ctx, LANE)
    sz = _mm(h_lat, wt, act="silu", tn=2048, name="in_z", rows=(r_z, d_inner))
    u = _mm_glu(h_lat, wt, r_glu, d, tn=1024)

    xbc_act = _conv7(xbc_ctx, xbc_lat, ssm_conv_w[0], ssm_conv_b[0])

    def ssd_params(k):
        par = jnp.zeros((8, LANE), F32).at[0, :N_HEADS].set(dt_bias[0, k]).at[1, :N_HEADS].set(a_log[0, k])
        return par, jnp.repeat(d_skip[0, k], HEAD_DIM).reshape(N_HEADS // 2, 1, LANE)

    y_bwd = _ssd(xbc_act, dt_ctx, dt_lat, *ssd_params(1), reverse=True)
    gnorm = _ssd(xbc_act, dt_ctx, dt_lat, *ssd_params(0), reverse=False,
                 norm_with=(y_bwd, sz, ssm_norm_w[0]))
    y_ssd = _mm(gnorm, ssm_out_w[0].astype(BF16), tn=1024, name="ssm_out")

    cv = _conv31(u.reshape(bsz, seq, d), cf_dw_w[0], cf_dw_b[0]).reshape(n_tok, d)
    merged = _mm_merge(cv, cf_ln_w[0], cf_ln_b[0], cf_out_w[0].astype(BF16), cf_out_b[0],
                       h_lat, wt, r_gate, y_ssd)
    x1 = _mm_resid(merged, w_o[0].astype(BF16), x.reshape(n_tok, d), mod3, 2, seq, tm=2048)

    n_r = MOE_GROUPS + N_EXPERTS
    rw = jnp.pad(jnp.concatenate([router_group_w[0], router_expert_w[0]], axis=1),
                 ((0, 0), (0, LANE - n_r))).astype(BF16)
    rb = jnp.pad(jnp.concatenate([router_group_b[0], router_expert_b[0]]), (0, LANE - n_r)).reshape(1, LANE)
    h2t, eid, ew = _route(x1, norm2_w[0], mod3, 3, 4, seq, rw, rb)

    src_rows, dst_rows, tables = _dispatch_tables(eid[:, :2], n_tok, _pitch(d // LANE))
    hid = _expert_up(h2t, src_rows, expert_w_gate[0], expert_w_up[0], tables)
    y = _expert_down(hid, dst_rows, expert_w_down[0], tables, 2 * n_tok + 2 * MOE_BLOCK)
    out = _combine(y, ew, x1, mod3, 5, seq, final_norm_w)
    return out.reshape(bsz, seq, d)
```

```python
import functools

import jax
import jax.numpy as jnp
from jax import lax
from jax.experimental import pallas as pl
from jax.experimental.pallas import tpu as pltpu

F32 = jnp.float32
BF16 = jnp.bfloat16

EPS = 1e-6
GRID_W = 64
HEAD_DIM = 64
N_HEADS = 64
N_GROUPS = 8
D_STATE = 128
CHUNK = 128
SSM_CONV = 7
CF_KERNEL = 31
MOE_GROUPS = 8
EXPERTS_PER_GROUP = 8
N_EXPERTS = 64
MOE_BLOCK = 256
LANE = 128
LOG2E = 1.4426950408889634
VMEM_LIMIT = 56 * 1024 * 1024


def _cparams(sem):
    return pltpu.CompilerParams(dimension_semantics=sem, vmem_limit_bytes=VMEM_LIMIT)


def _silu(v):
    return v * jax.nn.sigmoid(v)


def _pitch(nt):
    return nt + 1


def _ada_kernel(c_ref, w_ref, b_ref, o_ref):
    s = _silu(c_ref[...])
    o_ref[...] = jnp.dot(s.astype(BF16), w_ref[...].astype(BF16),
                         preferred_element_type=F32) + b_ref[...]


def _ada(crows, ada_w, ada_b, tn=1024):
    r, d = crows.shape
    n = ada_w.shape[1]
    return pl.pallas_call(
        _ada_kernel,
        out_shape=jax.ShapeDtypeStruct((r, n), F32),
        grid=(n // tn,),
        in_specs=[pl.BlockSpec((r, d), lambda j: (0, 0)),
                  pl.BlockSpec((d, tn), lambda j: (0, j)),
                  pl.BlockSpec((1, tn), lambda j: (0, j))],
        out_specs=pl.BlockSpec((r, tn), lambda j: (0, j)),
        compiler_params=_cparams(("parallel",)),
        name="ada",
    )(crows, ada_w, ada_b.reshape(1, n))


def _normmod_kernel(rows_ref, x_ref, w_ref, sh_ref, sc_ref, o_ref):
    del rows_ref
    xf = x_ref[0]
    ms = jnp.mean(xf * xf, axis=-1, keepdims=True)
    y = xf * lax.rsqrt(ms + EPS) * w_ref[...]
    o_ref[0] = (y * (1.0 + sc_ref[0]) + sh_ref[0]).astype(o_ref.dtype)


def _normmod(x3, w, mod3, rows, shift_chunk, scale_chunk, out_dtype, tm=512):
    bx, l, d = x3.shape
    tm = min(tm, l)
    grid_spec = pltpu.PrefetchScalarGridSpec(
        num_scalar_prefetch=1,
        grid=(bx, l // tm),
        in_specs=[pl.BlockSpec((1, tm, d), lambda b, i, r: (b, i, 0)),
                  pl.BlockSpec((1, d), lambda b, i, r: (0, 0)),
                  pl.BlockSpec((1, 1, d), lambda b, i, r: (r[b], 0, shift_chunk)),
                  pl.BlockSpec((1, 1, d), lambda b, i, r: (r[b], 0, scale_chunk))],
        out_specs=pl.BlockSpec((1, tm, d), lambda b, i, r: (b, i, 0)),
    )
    return pl.pallas_call(
        _normmod_kernel,
        out_shape=jax.ShapeDtypeStruct((bx, l, d), out_dtype),
        grid_spec=grid_spec,
        compiler_params=_cparams(("parallel", "parallel")),
        name="normmod",
    )(rows, x3, w.reshape(1, d), mod3, mod3)


def _dot_nt(a, wt):
    return lax.dot_general(a, wt, (((1,), (1,)), ((), ())), preferred_element_type=F32)


def _mm_kernel(a_ref, w_ref, *rest, act, has_bias, w_rows):
    o_ref = rest[-1]
    a = a_ref[...]
    acc = _dot_nt(a, w_ref[...]) if w_rows else jnp.dot(a, w_ref[...], preferred_element_type=F32)
    if has_bias:
        acc = acc + rest[0][...]
    if act == "silu":
        acc = _silu(acc)
    elif act == "sigmoid":
        acc = jax.nn.sigmoid(acc)
    o_ref[...] = acc.astype(o_ref.dtype)


def _mm(a, w, bias=None, act=None, out_dtype=F32, tm=1024, tn=1024, name="mm", rows=None):
    m, k = a.shape
    start, n = (0, w.shape[1]) if rows is None else rows
    tm, tn = min(tm, m), min(tn, n)
    j0 = start // tn
    w_spec = (pl.BlockSpec((k, tn), lambda i, j: (0, j)) if rows is None
              else pl.BlockSpec((tn, k), lambda i, j: (j0 + j, 0)))
    in_specs = [pl.BlockSpec((tm, k), lambda i, j: (i, 0)), w_spec]
    args = [a, w]
    if bias is not None:
        in_specs.append(pl.BlockSpec((1, tn), lambda i, j: (0, j)))
        args.append(bias.reshape(1, n))
    return pl.pallas_call(
        functools.partial(_mm_kernel, act=act, has_bias=bias is not None, w_rows=rows is not None),
        out_shape=jax.ShapeDtypeStruct((m, n), out_dtype),
        grid=(m // tm, n // tn),
        in_specs=in_specs,
        out_specs=pl.BlockSpec((tm, tn), lambda i, j: (i, j)),
        compiler_params=_cparams(("parallel", "parallel")),
        name=name,
    )(*args)


def _mm_glu_kernel(a_ref, wa_ref, wb_ref, o_ref):
    a = a_ref[...]
    va = _dot_nt(a, wa_ref[...])
    vb = _dot_nt(a, wb_ref[...])
    o_ref[...] = va * jax.nn.sigmoid(vb)


def _mm_glu(a, wt, start, n, tm=1024, tn=512):
    m, k = a.shape
    tm = min(tm, m)
    ja, jb = start // tn, (start + n) // tn
    return pl.pallas_call(
        _mm_glu_kernel,
        out_shape=jax.ShapeDtypeStruct((m, n), F32),
        grid=(m // tm, n // tn),
        in_specs=[pl.BlockSpec((tm, k), lambda i, j: (i, 0)),
                  pl.BlockSpec((tn, k), lambda i, j: (ja + j, 0)),
                  pl.BlockSpec((tn, k), lambda i, j: (jb + j, 0))],
        out_specs=pl.BlockSpec((tm, tn), lambda i, j: (i, j)),
        compiler_params=_cparams(("parallel", "parallel")),
        name="mm_glu",
    )(a, wt, wt)


def _mm_merge_kernel(cv_ref, lw_ref, lb_ref, w_ref, b_ref, h_ref, wga_ref, wgb_ref, ys_ref, o_ref):
    h = h_ref[...]
    gate_a = jax.nn.sigmoid(_dot_nt(h, wga_ref[...]))
    gate_b = jax.nn.sigmoid(_dot_nt(h, wgb_ref[...]))
    xf = cv_ref[...]
    mu = jnp.mean(xf, axis=-1, keepdims=True)
    xc = xf - mu
    var = jnp.mean(xc * xc, axis=-1, keepdims=True)
    u = _silu(xc * lax.rsqrt(var + EPS) * lw_ref[...] + lb_ref[...]).astype(BF16)
    ycf = jnp.dot(u, w_ref[...], preferred_element_type=F32) + b_ref[...]
    o_ref[...] = (gate_a * ys_ref[...] + gate_b * ycf).astype(o_ref.dtype)


def _mm_merge(cv, ln_w, ln_b, w, bias, h, wt, gate_start, y_ssd, tm=512, tn=1024):
    m, k = cv.shape
    n = w.shape[1]
    tm = min(tm, m)
    ja, jb = gate_start // tn, (gate_start + n) // tn
    return pl.pallas_call(
        _mm_merge_kernel,
        out_shape=jax.ShapeDtypeStruct((m, n), BF16),
        grid=(m // tm, n // tn),
        in_specs=[pl.BlockSpec((tm, k), lambda i, j: (i, 0)),
                  pl.BlockSpec((1, k), lambda i, j: (0, 0)),
                  pl.BlockSpec((1, k), lambda i, j: (0, 0)),
                  pl.BlockSpec((k, tn), lambda i, j: (0, j)),
                  pl.BlockSpec((1, tn), lambda i, j: (0, j)),
                  pl.BlockSpec((tm, k), lambda i, j: (i, 0)),
                  pl.BlockSpec((tn, k), lambda i, j: (ja + j, 0)),
                  pl.BlockSpec((tn, k), lambda i, j: (jb + j, 0)),
                  pl.BlockSpec((tm, tn), lambda i, j: (i, j))],
        out_specs=pl.BlockSpec((tm, tn), lambda i, j: (i, j)),
        compiler_params=_cparams(("parallel", "parallel")),
        name="mm_merge",
    )(cv, ln_w.reshape(1, k), ln_b.reshape(1, k), w, bias.reshape(1, n), h, wt, wt, y_ssd)


def _mm_resid_kernel(a_ref, w_ref, x_ref, g_ref, o_ref):
    out = jnp.dot(a_ref[...], w_ref[...], preferred_element_type=F32)
    o_ref[...] = x_ref[...] + g_ref[0] * out


def _mm_resid(a, w, x2, mod3, gate_chunk, rows_per_batch, tm=1024, tn=512):
    m, k = a.shape
    n = w.shape[1]
    tm = min(tm, rows_per_batch)
    nj = n // tn
    tiles_per_batch = rows_per_batch // tm
    return pl.pallas_call(
        _mm_resid_kernel,
        out_shape=jax.ShapeDtypeStruct((m, n), F32),
        grid=(m // tm, nj),
        in_specs=[pl.BlockSpec((tm, k), lambda i, j: (i, 0)),
                  pl.BlockSpec((k, tn), lambda i, j: (0, j)),
                  pl.BlockSpec((tm, tn), lambda i, j: (i, j)),
                  pl.BlockSpec((1, 1, tn),
                               lambda i, j: (i // tiles_per_batch, 0, gate_chunk * nj + j))],
        out_specs=pl.BlockSpec((tm, tn), lambda i, j: (i, j)),
        compiler_params=_cparams(("parallel", "parallel")),
        name="mm_resid",
    )(a, w, x2, mod3)


W_ALIGN = 2048


def _pack_wt_kernel(valid_ref, off_ref, w_ref, o_ref):
    del off_ref
    nrow = valid_ref[pl.program_id(0)]
    row = lax.broadcasted_iota(jnp.int32, o_ref.shape, 0)
    o_ref[...] = jnp.where(row < nrow, w_ref[...], 0.0).astype(o_ref.dtype)


def _pack_wt(wt, segments, tr=1024):
    n, k = wt.shape
    starts, src_off, valid = [], [], []
    pos = 0
    for lo, hi in segments:
        pos = -(-pos // W_ALIGN) * W_ALIGN
        starts.append(pos)
        while len(src_off) < pos // tr:
            src_off.append(0)
            valid.append(0)
        for r in range(lo, hi, tr):
            src_off.append(min(r, n - tr))
            valid.append(min(tr, hi - r))
            assert r <= n - tr or hi - r == tr
        pos += -(-(hi - lo) // tr) * tr
    total = -(-pos // W_ALIGN) * W_ALIGN
    while len(src_off) < total // tr:
        src_off.append(0)
        valid.append(0)
    grid_spec = pltpu.PrefetchScalarGridSpec(
        num_scalar_prefetch=2,
        grid=(total // tr,),
        in_specs=[pl.BlockSpec((pl.Element(tr), pl.Element(k)), lambda t, v, off: (off[t] * 8, 0))],
        out_specs=pl.BlockSpec((tr, k), lambda t, v, off: (t, 0)),
    )
    packed = pl.pallas_call(
        _pack_wt_kernel,
        out_shape=jax.ShapeDtypeStruct((total, k), BF16),
        grid_spec=grid_spec,
        compiler_params=_cparams(("parallel",)),
        name="pack_wt",
    )(jnp.asarray(valid, jnp.int32), jnp.asarray(src_off, jnp.int32) // 8, wt)
    return packed, starts


_CONV_PAD = 8


def _conv7_kernel(ctx_ref, lat_ref, w_ref, b_ref, o_ref, pad_ref, *, l_ctx, l_lat):
    p = _CONV_PAD
    zeros = jnp.zeros((p, LANE), F32)
    off_ctx = p
    off_lat = 2 * p + l_ctx
    pad_ref[0:p, :] = zeros
    pad_ref[off_ctx + l_ctx:off_lat, :] = zeros
    pad_ref[off_lat + l_lat:off_lat + l_lat + p, :] = zeros
    pad_ref[off_ctx:off_ctx + l_ctx, :] = ctx_ref[0]
    pad_ref[off_lat:off_lat + l_lat, :] = lat_ref[0]
    reach = SSM_CONV // 2
    bias = b_ref[...]

    def chunk(pad_base, out_base):
        acc = jnp.broadcast_to(bias, (CHUNK, LANE))
        for k in range(SSM_CONV):
            tap = pad_ref[pl.ds(pad_base - reach + k, CHUNK), :]
            acc = acc + tap * w_ref[k:k + 1, :]
        o_ref[0, 0, pl.ds(out_base, CHUNK), :] = _silu(acc)

    def ctx_body(j, c):
        base = pl.multiple_of(j * CHUNK, CHUNK)
        chunk(off_ctx + base, base)
        return c

    def lat_body(j, c):
        base = pl.multiple_of(j * CHUNK, CHUNK)
        chunk(off_lat + base, l_ctx + base)
        return c

    lax.fori_loop(0, l_ctx // CHUNK, ctx_body, 0)
    lax.fori_loop(0, l_lat // CHUNK, lat_body, 0, unroll=2)


def _conv7(ctx_raw, lat_raw, w, b):
    bsz, l_ctx, c = ctx_raw.shape
    l_lat = lat_raw.shape[1]
    ltot = l_ctx + l_lat
    nct = c // LANE
    return pl.pallas_call(
        functools.partial(_conv7_kernel, l_ctx=l_ctx, l_lat=l_lat),
        out_shape=jax.ShapeDtypeStruct((bsz, nct, ltot, LANE), F32),
        grid=(bsz, nct),
        in_specs=[pl.BlockSpec((1, l_ctx, LANE), lambda bi, ci: (bi, 0, ci)),
                  pl.BlockSpec((1, l_lat, LANE), lambda bi, ci: (bi, 0, ci)),
                  pl.BlockSpec((SSM_CONV, LANE), lambda bi, ci: (0, ci)),
                  pl.BlockSpec((1, LANE), lambda bi, ci: (0, ci))],
        out_specs=pl.BlockSpec((1, 1, ltot, LANE), lambda bi, ci: (bi, ci, 0, 0)),
        scratch_shapes=[pltpu.VMEM((ltot + 3 * _CONV_PAD, LANE), F32)],
        compiler_params=_cparams(("parallel", "parallel")),
        name="conv7",
    )(ctx_raw, lat_raw, w, b.reshape(1, c))


def _ssd_kernel(xbc_ref, dtc_ref, dtl_ref, par_ref, dexp_ref, ex_ref, *rest, reverse, n_ctx, fuse_norm):
    if fuse_norm:
        yo_ref, sz_ref, nw_ref, o_ref, st_ref, cumt_ref, y_ref = rest
    else:
        y_ref, st_ref, cumt_ref = rest
    i = pl.program_id(1)

    @pl.when(i == 0)
    def _():
        st_ref[...] = jnp.zeros_like(st_ref)

    dt_raw = jnp.where(i < n_ctx, dtc_ref[0], dtl_ref[0])
    bias = par_ref[0:1, :]
    a = -jnp.exp(par_ref[1:2, :])
    dt = jax.nn.softplus(dt_raw + bias)
    cum = dt * a
    row = lax.broadcasted_iota(jnp.int32, (CHUNK, LANE), 0)
    k = 1
    while k < CHUNK:
        if reverse:
            cum = cum + jnp.where(row < CHUNK - k, pltpu.roll(cum, CHUNK - k, 0), 0.0)
        else:
            cum = cum + jnp.where(row >= k, pltpu.roll(cum, k, 0), 0.0)
        k *= 2
    last = 0 if reverse else CHUNK - 1
    cum = cum * LOG2E
    cumt_ref[...] = cum.T
    li = lax.broadcasted_iota(jnp.int32, (CHUNK, CHUNK), 0)
    si = lax.broadcasted_iota(jnp.int32, (CHUNK, CHUNK), 1)
    causal = (li <= si) if reverse else (li >= si)
    lo = lax.broadcasted_iota(jnp.int32, (CHUNK, LANE), 1) < HEAD_DIM
    heads_per_group = N_HEADS // N_GROUPS
    pairs = heads_per_group // 2
    x_tiles = N_HEADS // 2

    def group(g, carry):
        shift = (LANE - heads_per_group * g) & (LANE - 1)
        cum_g = pltpu.roll(cum, shift, 1)
        dt_g = pltpu.roll(dt, shift, 1)
        cum_t = cumt_ref[pl.ds(pl.multiple_of(heads_per_group * g, heads_per_group), heads_per_group), :]
        bb = xbc_ref[0, x_tiles + g].astype(BF16)
        cb = xbc_ref[0, x_tiles + N_GROUPS + g].astype(BF16)
        scores = lax.dot_general(cb, bb, (((1,), (1,)), ((), ())), preferred_element_type=F32)
        y_off = jnp.dot(cb, st_ref[g].astype(BF16), preferred_element_type=F32)
        d_hi = dt_g.astype(BF16)
        r_hi = dt_g - d_hi.astype(F32)
        d_mid = r_hi.astype(BF16)
        d_lo = (r_hi - d_mid.astype(F32)).astype(BF16)
        dt_x = (jnp.dot(jnp.concatenate([d_hi, d_mid], axis=1), ex_ref[...], preferred_element_type=F32)
                + jnp.dot(d_lo, ex_ref[0:LANE, :], preferred_element_type=F32))
        xw_parts, dec_parts = [], []
        for p in range(pairs):
            j0, j1 = 2 * p, 2 * p + 1
            x2 = xbc_ref[0, pairs * g + p]
            c0 = cum_g[:, j0:j0 + 1]
            c1 = cum_g[:, j1:j1 + 1]
            l0 = jnp.exp2(jnp.where(causal, c0 - cum_t[j0:j0 + 1, :], -jnp.inf))
            l1 = jnp.exp2(jnp.where(causal, c1 - cum_t[j1:j1 + 1, :], -jnp.inf))
            m0 = (scores * l0).astype(BF16)
            m1 = (scores * l1).astype(BF16)
            dt2 = dt_x[:, p * LANE:(p + 1) * LANE]
            c2 = jnp.where(lo, c0, c1)
            xdt = x2 * dt2
            xdt_b = xdt.astype(BF16)
            zero = jnp.zeros_like(xdt_b)
            y_diag = jnp.dot(jnp.concatenate([m0, m1], axis=1),
                             jnp.concatenate([jnp.where(lo, xdt_b, zero), jnp.where(lo, zero, xdt_b)], axis=0),
                             preferred_element_type=F32)
            e2 = jnp.exp2(c2)
            y = y_diag + y_off[:, p * LANE:(p + 1) * LANE] * e2
            y_ref[0, pairs * g + p] = y + dexp_ref[pairs * g + p] * x2
            to_end = jnp.exp2(c2[last:last + 1, :] - c2)
            xw_parts.append((xdt * to_end).astype(BF16))
            dec_parts.append(e2[last:last + 1, :])
        xw = jnp.concatenate(xw_parts, axis=1)
        dec = jnp.concatenate(dec_parts, axis=1)
        upd = lax.dot_general(bb, xw, (((0,), (0,)), ((), ())), preferred_element_type=F32)
        st_ref[g] = st_ref[g] * dec + upd
        return carry

    lax.fori_loop(0, N_GROUPS, group, 0, unroll=2)

    if fuse_norm:
        sq = jnp.zeros((CHUNK, LANE), F32)
        for j in range(x_tiles):
            gj = (y_ref[0, j] + yo_ref[0, j]) * sz_ref[:, j * LANE:(j + 1) * LANE]
            y_ref[0, j] = gj
            sq = sq + gj * gj
        r = lax.rsqrt(jnp.sum(sq, axis=-1, keepdims=True) / (x_tiles * LANE) + EPS)
        for j in range(x_tiles):
            sl = slice(j * LANE, (j + 1) * LANE)
            o_ref[:, sl] = (y_ref[0, j] * r * nw_ref[:, sl]).astype(o_ref.dtype)


def _ssd(xbc_act, dt_ctx, dt_lat, par, dexp, reverse, norm_with=None):
    bsz, ntile, ltot, _ = xbc_act.shape
    l_ctx = dt_ctx.shape[1]
    l_lat = dt_lat.shape[1]
    n_ctx = l_ctx // CHUNK
    n_lat = l_lat // CHUNK
    steps = n_ctx + n_lat
    x_tiles = N_HEADS // 2
    gw = (N_HEADS // N_GROUPS) * HEAD_DIM
    e1 = (jnp.arange(gw)[None, :] // HEAD_DIM == jnp.arange(LANE)[:, None]).astype(BF16)
    expand = jnp.concatenate([e1, e1], axis=0)

    if reverse:
        def cat_chunk(i):
            return jnp.where(i < n_ctx, n_ctx - 1 - i, n_ctx + steps - 1 - i)

        def ctx_chunk(i):
            return jnp.maximum(n_ctx - 1 - i, 0)

        def lat_chunk(i):
            return jnp.minimum(steps - 1 - i, n_lat - 1)
    else:
        def cat_chunk(i):
            return i

        def ctx_chunk(i):
            return jnp.minimum(i, n_ctx - 1)

        def lat_chunk(i):
            return jnp.maximum(i - n_ctx, 0)

    y_spec = pl.BlockSpec((1, x_tiles, CHUNK, LANE), lambda b, i: (b, 0, lat_chunk(i), 0))
    in_specs = [pl.BlockSpec((1, ntile, CHUNK, LANE), lambda b, i: (b, 0, cat_chunk(i), 0)),
                pl.BlockSpec((1, CHUNK, LANE), lambda b, i: (b, ctx_chunk(i), 0)),
                pl.BlockSpec((1, CHUNK, LANE), lambda b, i: (b, lat_chunk(i), 0)),
                pl.BlockSpec((8, LANE), lambda b, i: (0, 0)),
                pl.BlockSpec((x_tiles, 1, LANE), lambda b, i: (0, 0, 0)),
                pl.BlockSpec((2 * LANE, gw), lambda b, i: (0, 0))]
    args = [xbc_act, dt_ctx, dt_lat, par, dexp, expand]
    scratch = [pltpu.VMEM((N_GROUPS, D_STATE, gw), F32), pltpu.VMEM((LANE, CHUNK), F32)]
    if norm_with is None:
        out_shape = jax.ShapeDtypeStruct((bsz, x_tiles, l_lat, LANE), F32)
        out_spec = y_spec
    else:
        y_other, silu_z, norm_w = norm_with
        dn = x_tiles * LANE
        row_spec = pl.BlockSpec((CHUNK, dn), lambda b, i: (b * n_lat + lat_chunk(i), 0))
        in_specs += [y_spec, row_spec, pl.BlockSpec((1, dn), lambda b, i: (0, 0))]
        args += [y_other, silu_z, norm_w.reshape(1, dn)]
        out_shape = jax.ShapeDtypeStruct((bsz * l_lat, dn), BF16)
        out_spec = row_spec
        scratch.append(pltpu.VMEM((1, x_tiles, CHUNK, LANE), F32))
    return pl.pallas_call(
        functools.partial(_ssd_kernel, reverse=reverse, n_ctx=n_ctx, fuse_norm=norm_with is not None),
        out_shape=out_shape,
        grid=(bsz, steps),
        in_specs=in_specs,
        out_specs=out_spec,
        scratch_shapes=scratch,
        compiler_params=_cparams(("parallel", "arbitrary")),
        name="ssd_bwd" if reverse else "ssd_fwd",
    )(*args)


def _conv31_kernel(u_ref, w_ref, b_ref, o_ref, pad_ref, *, seq):
    halo = (CF_KERNEL // 2) * GRID_W
    zeros = jnp.zeros((halo, LANE), F32)
    pad_ref[0:halo, :] = zeros
    pad_ref[halo + seq:halo + seq + halo, :] = zeros
    pad_ref[halo:halo + seq, :] = u_ref[0]
    bias = b_ref[...]

    def body(j, c):
        base = pl.multiple_of(j * CHUNK, CHUNK)
        acc = jnp.broadcast_to(bias, (CHUNK, LANE))
        for k in range(CF_KERNEL):
            tap = pad_ref[pl.ds(pl.multiple_of(base + k * GRID_W, GRID_W), CHUNK), :]
            acc = acc + tap * w_ref[k:k + 1, :]
        o_ref[0, pl.ds(base, CHUNK), :] = acc
        return c

    lax.fori_loop(0, seq // CHUNK, body, 0, unroll=2)


def _conv31(u3, w, b):
    bsz, s, c = u3.shape
    halo = (CF_KERNEL // 2) * GRID_W
    return pl.pallas_call(
        functools.partial(_conv31_kernel, seq=s),
        out_shape=jax.ShapeDtypeStruct((bsz, s, c), F32),
        grid=(bsz, c // LANE),
        in_specs=[pl.BlockSpec((1, s, LANE), lambda bi, ci: (bi, 0, ci)),
                  pl.BlockSpec((CF_KERNEL, LANE), lambda bi, ci: (0, ci)),
                  pl.BlockSpec((1, LANE), lambda bi, ci: (0, ci))],
        out_specs=pl.BlockSpec((1, s, LANE), lambda bi, ci: (bi, 0, ci)),
        scratch_shapes=[pltpu.VMEM((s + 2 * halo, LANE), F32)],
        compiler_params=_cparams(("parallel", "parallel")),
        name="conv31",
    )(u3, w, b.reshape(1, c))


def _route_kernel(x_ref, w_ref, sh_ref, sc_ref, rw_ref, rb_ref, h_ref, eid_ref, ew_ref):
    xf = x_ref[...]
    ms = jnp.mean(xf * xf, axis=-1, keepdims=True)
    h = xf * lax.rsqrt(ms + EPS) * w_ref[...]
    h = h * (1.0 + sc_ref[0]) + sh_ref[0]
    tm = xf.shape[0]
    nt = xf.shape[1] // LANE
    pitch = _pitch(nt)
    for j in range(nt):
        h_ref[pl.ds(j, tm, stride=pitch), :] = h[:, j * LANE:(j + 1) * LANE]
    for j in range(nt, pitch):
        h_ref[pl.ds(j, tm, stride=pitch), :] = jnp.zeros((tm, LANE), F32)
    logits = jnp.dot(h.astype(BF16), rw_ref[...], preferred_element_type=F32) + rb_ref[...]
    lane = lax.broadcasted_iota(jnp.int32, (tm, LANE), 1)
    lane_f = lane.astype(F32)
    ninf = -jnp.inf
    gl = jnp.where(lane < MOE_GROUPS, logits, ninf)
    gmax = jnp.max(gl, axis=-1, keepdims=True)
    gidx = jnp.min(jnp.where(gl == gmax, lane_f, float(LANE)), axis=-1, keepdims=True)
    gsum = jnp.sum(jnp.exp(gl - gmax), axis=-1, keepdims=True)
    g_p = 1.0 / gsum
    first = float(MOE_GROUPS) + gidx * float(EXPERTS_PER_GROUP)
    in_group = (lane_f >= first) & (lane_f < first + float(EXPERTS_PER_GROUP))
    el = jnp.where(in_group, logits, ninf)
    m1 = jnp.max(el, axis=-1, keepdims=True)
    i1 = jnp.min(jnp.where(el == m1, lane_f, float(LANE)), axis=-1, keepdims=True)
    el2 = jnp.where(lane_f == i1, ninf, el)
    m2 = jnp.max(el2, axis=-1, keepdims=True)
    i2 = jnp.min(jnp.where(el2 == m2, lane_f, float(LANE)), axis=-1, keepdims=True)
    e21 = jnp.exp(m2 - m1)
    den = 1.0 + e21
    w1 = (1.0 / den) * g_p
    w2 = (e21 / den) * g_p
    e1 = (i1 - float(MOE_GROUPS)).astype(jnp.int32)
    e2 = (i2 - float(MOE_GROUPS)).astype(jnp.int32)
    eid_ref[...] = jnp.where(lane == 0, e1, jnp.where(lane == 1, e2, 0))
    ew_ref[...] = jnp.where(lane == 0, w1, jnp.where(lane == 1, w2, 0.0))


def _route(x2, w, mod3, shift_chunk, scale_chunk, rows_per_batch, rw, rb, tm=512):
    m, d = x2.shape
    pitch = _pitch(d // LANE)
    tm = min(tm, rows_per_batch)
    tiles_per_batch = rows_per_batch // tm
    return pl.pallas_call(
        _route_kernel,
        out_shape=(jax.ShapeDtypeStruct((m * pitch, LANE), F32),
                   jax.ShapeDtypeStruct((m, LANE), jnp.int32),
                   jax.ShapeDtypeStruct((m, LANE), F32)),
        grid=(m // tm,),
        in_specs=[pl.BlockSpec((tm, d), lambda i: (i, 0)),
                  pl.BlockSpec((1, d), lambda i: (0, 0)),
                  pl.BlockSpec((1, 1, d), lambda i: (i // tiles_per_batch, 0, shift_chunk)),
                  pl.BlockSpec((1, 1, d), lambda i: (i // tiles_per_batch, 0, scale_chunk)),
                  pl.BlockSpec((d, LANE), lambda i: (0, 0)),
                  pl.BlockSpec((1, LANE), lambda i: (0, 0))],
        out_specs=(pl.BlockSpec((tm * pitch, LANE), lambda i: (i, 0)),
                   pl.BlockSpec((tm, LANE), lambda i: (i, 0)),
                   pl.BlockSpec((tm, LANE), lambda i: (i, 0))),
        compiler_params=_cparams(("parallel",)),
        name="route",
    )(x2, w.reshape(1, d), mod3, mod3, rw, rb)


_DMA_UNROLL = 8


def _rows_to_matrix(ref, tm, nt):
    return jnp.concatenate([ref[pl.ds(j, tm, stride=_pitch(nt)), :] for j in range(nt)], axis=1)


def _bulk_wait(src, dst, sem, total_rows):
    pltpu.make_async_copy(src.at[pl.ds(0, total_rows), :], dst.at[pl.ds(0, total_rows), :], sem).wait()


def _for_rows(n, body):
    groups = lax.shift_right_logical(n, _DMA_UNROLL.bit_length() - 1)

    def group(g, c):
        for u in range(_DMA_UNROLL):
            body(g * _DMA_UNROLL + u)
        return c

    def tail(r, c):
        body(r)
        return c

    lax.fori_loop(0, groups, group, 0)
    lax.fori_loop(groups * _DMA_UNROLL, n, tail, 0)


def _stream_expert_weights(b, be_ref, eord_ref, enext_ref, w_hbms, w_bufs, w_caches, wsem, both_queues):
    prev = jnp.maximum(b - 1, 0)

    def copies(e, slot):
        out = []
        for w, buf in zip(w_hbms, w_bufs):
            if both_queues:
                half = w.shape[1] // 2
                out.append((pltpu.make_async_copy(w.at[e, 0:half], buf.at[slot, 0:half], wsem.at[slot]), 1))
                out.append((pltpu.make_async_copy(w.at[e, half:], buf.at[slot, half:], wsem.at[slot]), 0))
            else:
                out.append((pltpu.make_async_copy(w.at[e], buf.at[slot], wsem.at[slot]), 1))
        return out

    @pl.when(b == 0)
    def _():
        for cp, prio in copies(be_ref[0], 0):
            cp.start(priority=prio)

    @pl.when((b == 0) | (be_ref[b] != be_ref[prev]))
    def _():
        for s in range(2):
            @pl.when((eord_ref[b] & 1) == s)
            def _(s=s):
                for cp, _ in copies(be_ref[b], s):
                    cp.wait()

                @pl.when(enext_ref[b] >= 0)
                def _():
                    for cp, prio in copies(enext_ref[b], 1 - s):
                        cp.start(priority=prio)

                for buf, cache in zip(w_bufs, w_caches):
                    cache[...] = buf[s].astype(BF16)


def _expert_up_kernel(be_ref, nused_ref, eord_ref, enext_ref, rowc_ref, rown_ref, h_hbm, wg_hbm, wu_hbm,
                      o_ref, xs0_ref, xs1_ref, wgs_ref, wus_ref, wgb_ref, wub_ref, sem, wsem, *, fchunk, nt):
    b = pl.program_id(0)
    n_used = nused_ref[0]
    dff = wgb_ref.shape[1]
    slots = (xs0_ref, xs1_ref)
    pitch = _pitch(nt)

    def gather_row(row_ref, r, slot):
        return pltpu.make_async_copy(h_hbm.at[pl.ds(row_ref[0, 0, r], nt), :],
                                     slots[slot].at[pl.ds(r * pitch, nt), :], sem.at[slot])

    @pl.when(b == 0)
    def _():
        _for_rows(MOE_BLOCK, lambda r: gather_row(rowc_ref, r, 0).start())

    for slot in range(2):
        @pl.when((b <= n_used) & (lax.rem(b, 2) == slot))
        def _(slot=slot):
            _bulk_wait(h_hbm, slots[slot], sem.at[slot], MOE_BLOCK * nt)

    @pl.when(b < n_used)
    def _():
        _stream_expert_weights(b, be_ref, eord_ref, enext_ref, (wg_hbm, wu_hbm), (wgs_ref, wus_ref),
                               (wgb_ref, wub_ref), wsem, both_queues=False)

        for slot in range(2):
            @pl.when(lax.rem(b, 2) == slot)
            def _(slot=slot):
                xb = _rows_to_matrix(slots[slot], MOE_BLOCK, nt).astype(BF16)
                nf = dff // fchunk
                per = MOE_BLOCK // (2 * nf)

                def request(part):
                    for r in range(part * per, (part + 1) * per):
                        gather_row(rown_ref, r, 1 - slot).start()

                for f in range(nf):
                    sl = slice(f * fchunk, (f + 1) * fchunk)
                    request(2 * f)
                    gate = jnp.dot(xb, wgb_ref[:, sl], preferred_element_type=F32)
                    request(2 * f + 1)
                    up = jnp.dot(xb, wub_ref[:, sl], preferred_element_type=F32)
                    o_ref[:, sl] = (_silu(gate) * up).astype(o_ref.dtype)

    @pl.when(b >= n_used)
    def _():
        o_ref[...] = jnp.zeros_like(o_ref)


def _expert_up(h2t, src_rows, w_gate, w_up, tables, fchunk=256):
    n_blocks = src_rows.shape[0]
    _, d, dff = w_gate.shape
    nt = d // LANE
    slot_rows = MOE_BLOCK * _pitch(nt)
    grid_spec = pltpu.PrefetchScalarGridSpec(
        num_scalar_prefetch=len(tables),
        grid=(n_blocks,),
        in_specs=[pl.BlockSpec((1, 1, MOE_BLOCK), lambda b, *_: (b, 0, 0), memory_space=pltpu.SMEM),
                  pl.BlockSpec((1, 1, MOE_BLOCK), lambda b, *_: (jnp.minimum(b + 1, n_blocks - 1), 0, 0),
                               memory_space=pltpu.SMEM),
                  pl.BlockSpec(memory_space=pl.ANY),
                  pl.BlockSpec(memory_space=pl.ANY),
                  pl.BlockSpec(memory_space=pl.ANY)],
        out_specs=pl.BlockSpec((MOE_BLOCK, dff), lambda b, *_: (b, 0)),
        scratch_shapes=[pltpu.VMEM((slot_rows, LANE), F32),
                        pltpu.VMEM((slot_rows, LANE), F32),
                        pltpu.VMEM((2, d, dff), F32),
                        pltpu.VMEM((2, d, dff), F32),
                        pltpu.VMEM((d, dff), BF16),
                        pltpu.VMEM((d, dff), BF16),
                        pltpu.SemaphoreType.DMA((2,)),
                        pltpu.SemaphoreType.DMA((2,))],
    )
    return pl.pallas_call(
        functools.partial(_expert_up_kernel, fchunk=fchunk, nt=nt),
        out_shape=jax.ShapeDtypeStruct((n_blocks * MOE_BLOCK, dff), BF16),
        grid_spec=grid_spec,
        compiler_params=_cparams(("arbitrary",)),
        name="expert_up",
    )(*tables, src_rows, src_rows, h2t, w_gate, w_up)


def _expert_down_kernel(be_ref, nused_ref, eord_ref, enext_ref, dst_ref, h_ref, wd_hbm, y_hbm,
                        ys0_ref, ys1_ref, wds_ref, wdb_ref, sem, wsem, *, nchunk, nt):
    b = pl.program_id(0)
    n_used = nused_ref[0]
    d = wdb_ref.shape[1]
    slots = (ys0_ref, ys1_ref)
    pitch = _pitch(nt)

    def scatter_row(r, slot):
        return pltpu.make_async_copy(slots[slot].at[pl.ds(r * pitch, pitch), :],
                                     y_hbm.at[pl.ds(dst_ref[0, 0, r], pitch), :], sem.at[slot])

    @pl.when(b == 0)
    def _():
        ys0_ref[...] = jnp.zeros_like(ys0_ref)
        ys1_ref[...] = jnp.zeros_like(ys1_ref)
        tail = pltpu.make_async_copy(ys1_ref, y_hbm.at[pl.ds(y_hbm.shape[0] - MOE_BLOCK * pitch, MOE_BLOCK * pitch), :],
                                     sem.at[1])
        tail.start()
        tail.wait()

    for slot in range(2):
        @pl.when((b >= 1) & (b <= n_used) & (lax.rem(b, 2) == slot))
        def _(slot=slot):
            _bulk_wait(slots[slot], y_hbm, sem.at[slot], MOE_BLOCK * pitch)

    @pl.when(b < n_used)
    def _():
        _stream_expert_weights(b, be_ref, eord_ref, enext_ref, (wd_hbm,), (wds_ref,), (wdb_ref,), wsem,
                               both_queues=True)
        hb = h_ref[...]
        for slot in range(2):
            @pl.when(lax.rem(b, 2) == slot)
            def _(slot=slot):
                nc = d // nchunk
                per = MOE_BLOCK // nc
                for c in range(nc):
                    for r in range(c * per, (c + 1) * per):
                        scatter_row(r, 1 - slot).start()
                    out = jnp.dot(hb, wdb_ref[:, c * nchunk:(c + 1) * nchunk], preferred_element_type=F32)
                    for j in range(nchunk // LANE):
                        slots[slot][pl.ds(c * (nchunk // LANE) + j, MOE_BLOCK, stride=pitch), :] = (
                            out[:, j * LANE:(j + 1) * LANE])

    for slot in range(2):
        @pl.when((b == n_used) & (lax.rem(b, 2) == slot))
        def _(slot=slot):
            _for_rows(MOE_BLOCK, lambda r: scatter_row(r, 1 - slot).start())
            _bulk_wait(slots[1 - slot], y_hbm, sem.at[1 - slot], MOE_BLOCK * pitch)


def _expert_down(hid, dst_rows, w_down, tables, y_slots, nchunk=256):
    n_rows, dff = hid.shape
    n_blocks = n_rows // MOE_BLOCK
    d = w_down.shape[2]
    nt = d // LANE
    pitch = _pitch(nt)
    grid_spec = pltpu.PrefetchScalarGridSpec(
        num_scalar_prefetch=len(tables),
        grid=(n_blocks,),
        in_specs=[pl.BlockSpec((1, 1, MOE_BLOCK), lambda b, *_: (b, 0, 0), memory_space=pltpu.SMEM),
                  pl.BlockSpec((MOE_BLOCK, dff), lambda b, be, n, *_: (jnp.minimum(b, n[0] - 1), 0)),
                  pl.BlockSpec(memory_space=pl.ANY)],
        out_specs=pl.BlockSpec(memory_space=pl.ANY),
        scratch_shapes=[pltpu.VMEM((MOE_BLOCK * pitch, LANE), F32),
                        pltpu.VMEM((MOE_BLOCK * pitch, LANE), F32),
                        pltpu.VMEM((2, dff, d), F32),
                        pltpu.VMEM((dff, d), BF16),
                        pltpu.SemaphoreType.DMA((2,)),
                        pltpu.SemaphoreType.DMA((2,))],
    )
    return pl.pallas_call(
        functools.partial(_expert_down_kernel, nchunk=nchunk, nt=nt),
        out_shape=jax.ShapeDtypeStruct((y_slots * pitch, LANE), F32),
        grid_spec=grid_spec,
        compiler_params=_cparams(("arbitrary",)),
        name="expert_down",
    )(*tables, dst_rows, hid, w_down)


def _combine_kernel(y0_ref, y1_ref, ew_ref, x_ref, g_ref, w_ref, o_ref):
    tm, d = x_ref.shape
    nt = d // LANE
    ew = ew_ref[...]
    moe = (_rows_to_matrix(y0_ref, tm, nt) * ew[:, 0:1]
           + _rows_to_matrix(y1_ref, tm, nt) * ew[:, 1:2])
    xo = x_ref[...] + g_ref[0] * moe
    ms = jnp.mean(xo * xo, axis=-1, keepdims=True)
    o_ref[...] = xo * lax.rsqrt(ms + EPS) * w_ref[...]


def _combine(y, ew, x2, mod3, gate_chunk, rows_per_batch, final_w, tm=512):
    m, d = x2.shape
    pitch = _pitch(d // LANE)
    tm = min(tm, rows_per_batch)
    tiles = m // tm
    tiles_per_batch = rows_per_batch // tm
    return pl.pallas_call(
        _combine_kernel,
        out_shape=jax.ShapeDtypeStruct((m, d), F32),
        grid=(tiles,),
        in_specs=[pl.BlockSpec((tm * pitch, LANE), lambda i: (i, 0)),
                  pl.BlockSpec((tm * pitch, LANE), lambda i: (tiles + i, 0)),
                  pl.BlockSpec((tm, LANE), lambda i: (i, 0)),
                  pl.BlockSpec((tm, d), lambda i: (i, 0)),
                  pl.BlockSpec((1, 1, d), lambda i: (i // tiles_per_batch, 0, gate_chunk)),
                  pl.BlockSpec((1, d), lambda i: (0, 0))],
        out_specs=pl.BlockSpec((tm, d), lambda i: (i, 0)),
        compiler_params=_cparams(("parallel",)),
        name="moe_combine",
    )(y, y, ew, x2, mod3, final_w.reshape(1, d))


def _dispatch_tables(eid, n_tok, pitch):
    top_k = eid.shape[1]
    n_assign = n_tok * top_k
    expert = eid.reshape(-1)
    key = jnp.sort(expert * n_assign + jnp.arange(n_assign, dtype=jnp.int32))
    sorted_assign = key % n_assign
    bounds = jnp.arange(N_EXPERTS + 1, dtype=jnp.int32) * n_assign
    start = jnp.sum((key[None, :] < bounds[:, None]).astype(jnp.int32), axis=1)
    counts = start[1:] - start[:-1]
    nblk = (counts + MOE_BLOCK - 1) // MOE_BLOCK
    blk_end = jnp.cumsum(nblk)
    blk_start = blk_end - nblk
    steps = -(-n_assign // MOE_BLOCK) + N_EXPERTS + 1
    bidx = jnp.arange(steps, dtype=jnp.int32)
    lane = jnp.arange(MOE_BLOCK, dtype=jnp.int32)[None, :]
    block_expert = jnp.minimum(jnp.sum((blk_end[None, :] <= bidx[:, None]).astype(jnp.int32), axis=1),
                               N_EXPERTS - 1)
    onehot = (block_expert[:, None] == jnp.arange(N_EXPERTS, dtype=jnp.int32)[None, :]).astype(jnp.int32)

    def lookup(table):
        return jnp.sum(onehot * table[None, :], axis=1)

    in_expert = (bidx - lookup(blk_start)) * MOE_BLOCK
    n_valid = jnp.clip(lookup(counts) - in_expert, 0, MOE_BLOCK)
    first_src = jnp.clip(lookup(start[:-1]) + in_expert, 0, n_assign)
    valid = lane < n_valid[:, None]
    assign = sorted_assign[jnp.minimum(first_src[:, None] + lane, n_assign - 1)]
    tok = assign // top_k
    src_rows = jnp.where(valid, tok, (bidx[:, None] * MOE_BLOCK + lane) % n_tok) * pitch
    dst_slot = jnp.where(valid, (assign % top_k) * n_tok + tok, top_k * n_tok + lane)
    dst_rows = jnp.concatenate([top_k * n_tok + lane, dst_slot[:-1]], axis=0) * pitch
    n_used = blk_end[-1].astype(jnp.int32)
    first = jnp.concatenate([jnp.ones((1,), jnp.int32),
                             (block_expert[1:] != block_expert[:-1]).astype(jnp.int32)])
    expert_ordinal = (jnp.cumsum(first) - 1).astype(jnp.int32)
    later = (jnp.arange(N_EXPERTS)[None, :] > block_expert[:, None]) & (counts[None, :] > 0)
    next_expert = jnp.min(jnp.where(later, jnp.arange(N_EXPERTS, dtype=jnp.int32)[None, :], N_EXPERTS), axis=1)
    next_expert = jnp.where(next_expert < N_EXPERTS, next_expert, -1).astype(jnp.int32)
    tables = (block_expert, n_used.reshape(1), expert_ordinal, next_expert)
    return (src_rows.astype(jnp.int32).reshape(steps, 1, MOE_BLOCK),
            dst_rows.astype(jnp.int32).reshape(steps, 1, MOE_BLOCK), tables)


def kernel(x, c, ctx, c_ctx, ada_w, ada_b, norm1_w, w_in, ssm_conv_w, ssm_conv_b, dt_bias, a_log, d_skip, ssm_norm_w, ssm_out_w, cf_dw_w, cf_dw_b, cf_ln_w, cf_ln_b, cf_out_w, cf_out_b, w_o, norm2_w, router_group_w, router_group_b, router_expert_w, router_expert_b, expert_w_gate, expert_w_up, expert_w_down, final_norm_w):
    bsz, seq, d = x.shape
    l_ctx = ctx.shape[1]
    n_tok = bsz * seq
    d_inner = ssm_norm_w.shape[1]
    gn = N_GROUPS * D_STATE
    xbc_dim = d_inner + 2 * gn
    off_dt = xbc_dim
    off_z = off_dt + N_HEADS
    off_glu = off_z + d_inner
    off_gate = off_glu + 2 * d

    ctx_row = bsz
    crows = jnp.zeros((8, d), F32).at[:bsz].set(c).at[ctx_row].set(c_ctx)
    mod = _ada(crows, ada_w[0], ada_b[0])
    mod3 = mod.reshape(8, 1, 6 * d)
    lat_rows = jnp.arange(bsz, dtype=jnp.int32)
    ctx_rows = jnp.full((bsz,), ctx_row, jnp.int32)

    h_lat = _normmod(x, norm1_w[0], mod3, lat_rows, 0, 1, BF16).reshape(n_tok, d)
    h_ctx = _normmod(ctx, norm1_w[0], mod3, ctx_rows, 0, 1, BF16).reshape(bsz * l_ctx, d)

    wt, (r_xbc, r_dt, r_z, r_glu, r_gate) = _pack_wt(
        jnp.transpose(w_in[0]),
        [(0, xbc_dim), (off_dt, off_z), (off_z, off_glu), (off_glu, off_gate), (off_gate, off_gate + 2 * d)])

    xbc_lat = _mm(h_lat, wt, tn=2048, name="in_xbc", rows=(r_xbc, xbc_dim)).reshape(bsz, seq, xbc_dim)
    xbc_ctx = _mm(h_ctx, wt, tm=512, name="in_xbc_ctx", rows=(r_xbc, xbc_dim)).reshape(bsz, l_ctx, xbc_dim)
    dt_lat = _mm(h_lat, wt, name="in_dt", rows=(r_dt, LANE)).reshape(bsz, seq, LANE)
    dt_ctx = _mm(h_ctx, wt, tm=512, name="in_dt_ctx", rows=(r_dt, LANE)).reshape(bsz, l_ctx, LANE)
    sz = _mm(h_lat, wt, act="silu", tn=2048, name="in_z", rows=(r_z, d_inner))
    u = _mm_glu(h_lat, wt, r_glu, d, tn=1024)

    xbc_act = _conv7(xbc_ctx, xbc_lat, ssm_conv_w[0], ssm_conv_b[0])

    def ssd_params(k):
        par = jnp.zeros((8, LANE), F32).at[0, :N_HEADS].set(dt_bias[0, k]).at[1, :N_HEADS].set(a_log[0, k])
        return par, jnp.repeat(d_skip[0, k], HEAD_DIM).reshape(N_HEADS // 2, 1, LANE)

    y_bwd = _ssd(xbc_act, dt_ctx, dt_lat, *ssd_params(1), reverse=True)
    gnorm = _ssd(xbc_act, dt_ctx, dt_lat, *ssd_params(0), reverse=False,
                 norm_with=(y_bwd, sz, ssm_norm_w[0]))
    y_ssd = _mm(gnorm, ssm_out_w[0].astype(BF16), tn=1024, name="ssm_out")

    cv = _conv31(u.reshape(bsz, seq, d), cf_dw_w[0], cf_dw_b[0]).reshape(n_tok, d)
    merged = _mm_merge(cv, cf_ln_w[0], cf_ln_b[0], cf_out_w[0].astype(BF16), cf_out_b[0],
                       h_lat, wt, r_gate, y_ssd)
    x1 = _mm_resid(merged, w_o[0].astype(BF16), x.reshape(n_tok, d), mod3, 2, seq, tm=2048)

    n_r = MOE_GROUPS + N_EXPERTS
    rw = jnp.pad(jnp.concatenate([router_group_w[0], router_expert_w[0]], axis=1),
                 ((0, 0), (0, LANE - n_r))).astype(BF16)
    rb = jnp.pad(jnp.concatenate([router_group_b[0], router_expert_b[0]]), (0, LANE - n_r)).reshape(1, LANE)
    h2t, eid, ew = _route(x1, norm2_w[0], mod3, 3, 4, seq, rw, rb)

    src_rows, dst_rows, tables = _dispatch_tables(eid[:, :2], n_tok, _pitch(d // LANE))
    hid = _expert_up(h2t, src_rows, expert_w_gate[0], expert_w_up[0], tables)
    y = _expert_down(hid, dst_rows, expert_w_down[0], tables, 2 * n_tok + 2 * MOE_BLOCK)
    out = _combine(y, ew, x1, mod3, 5, seq, final_norm_w)
    return out.reshape(bsz, seq, d)
```
